```python
import math
import jax, jax.numpy as jnp
from jax import lax
import numpy as np

D_MODEL = 1024
BATCH = 8
SEQ = 4096
DEPTH = 4
DEC_BATCH = 32
DEC_SEQ = 32
PAST_LEN = 1024

CHUNK = 64
Q_BLOCK = 128
CONV_W = 512
CONV_K = 3
POOL_W = 512
POOL_GROUPS = 4
POOL_GC = POOL_W // POOL_GROUPS
POOL_WINDOWS = (2, 4, 8, 16)
POOL_HIST = 15
POOL_OUT = D_MODEL // POOL_GROUPS
HEADS_C = 4
HEAD_DIM_C = 64
ATT_W = HEADS_C * 2 * HEAD_DIM_C
N_BRANCH = 3
IN_COLS = 3 * CONV_W + POOL_W + 3 * ATT_W + N_BRANCH * D_MODEL
ROPE_THETA = 10000.0
D_FF = 2816
N_EXPERTS = 8
TOP_K = 2
D_FF_EXPERT = 1408
N_DENSE = (DEPTH + 1) // 2
N_MOE = DEPTH // 2
EPS = 1e-6
SUBLN_EPS = 1e-5

kernel_name = 'hybrid_streaming_encoder_step'


def rmsnorm(x, g, eps=EPS):
    xf = x.astype(jnp.float32)
    y = xf * lax.rsqrt(jnp.mean(xf * xf, axis=-1, keepdims=True) + eps)
    return (y * g.astype(jnp.float32)).astype(x.dtype)


def rope(x, pos):
    half = HEAD_DIM_C // 2
    inv_freq = ROPE_THETA ** (-jnp.arange(half, dtype=jnp.float32) / half)
    ang = pos.astype(jnp.float32)[:, None] * inv_freq[None, :]
    cos = jnp.cos(ang)[None, :, None, None, :]
    sin = jnp.sin(ang)[None, :, None, None, :]
    xf = x.astype(jnp.float32)
    x1, x2 = xf[..., :half], xf[..., half:]
    return jnp.concatenate([x1 * cos - x2 * sin, x2 * cos + x1 * sin], axis=-1).astype(x.dtype)


def causal_conv(ext, w, T):
    y = ext[:, 0:T] * w[0]
    for j in range(1, CONV_K):
        y = y + ext[:, j:j + T] * w[j]
    return y


def multiscale_pool(u_ext, pos):
    T = pos.shape[0]
    uf = u_ext.astype(jnp.float32)
    cs = jnp.cumsum(uf, axis=1)
    cs = jnp.concatenate([jnp.zeros_like(cs[:, :1]), cs], axis=1)
    cur = uf[:, POOL_HIST:]
    outs = []
    for gi, win in enumerate(POOL_WINDOWS):
        lo, hi = gi * POOL_GC, (gi + 1) * POOL_GC
        s = cs[:, POOL_HIST + 1:POOL_HIST + 1 + T, lo:hi] - cs[:, POOL_HIST + 1 - win:POOL_HIST + 1 - win + T, lo:hi]
        cnt = jnp.minimum(pos + 1, win).astype(jnp.float32)[None, :, None]
        outs.append(s / cnt - cur[..., lo:hi])
    return jnp.concatenate(outs, axis=-1).astype(u_ext.dtype)


def diff_attn_block(q, k, v, q_chunk, k_chunk, lam):
    scale = HEAD_DIM_C ** -0.5
    s = jnp.einsum('bqhcd,bkhcd->bhcqk', q.astype(jnp.float32), k.astype(jnp.float32)) * scale
    visible = k_chunk[None, :] <= q_chunk[:, None]
    s = jnp.where(visible, s, -jnp.inf)
    p = jax.nn.softmax(s, axis=-1)
    a = p[:, :, 0] - lam * p[:, :, 1]
    return jnp.einsum('bhqk,bkhe->bqhe', a, v.astype(jnp.float32))


def token_mixer(x, pos, conv_prev, pool_prev, k_past, v_past, layer,
                norm_g, w_in, conv_w, w_a, w_pool, pool_scale,
                lq1, lk1, lq2, lk2, subln_g, w_c, w_o):
    B, T, _ = x.shape
    h = rmsnorm(x, norm_g)
    z = jnp.einsum('btd,de->bte', h, w_in)
    cuts = [int(c) for c in np.cumsum([CONV_W, CONV_W, CONV_W, POOL_W, ATT_W, ATT_W, ATT_W])]
    xa, ba, ca, u, q, k, v, g = jnp.split(z, cuts, axis=-1)
    cin = ca * xa
    cext = jnp.concatenate([conv_prev.astype(cin.dtype), cin], axis=1)
    y_a = jnp.einsum('btc,cd->btd', ba * causal_conv(cext, conv_w, T), w_a)
    new_conv = cext[:, -(CONV_K - 1):]
    uext = jnp.concatenate([pool_prev.astype(u.dtype), u], axis=1)
    pooled = multiscale_pool(uext, pos).reshape(B, T, POOL_GROUPS, POOL_GC)
    y_b = jnp.einsum('btgc,gce->btge', pooled, w_pool).reshape(B, T, D_MODEL) * pool_scale
    new_pool = uext[:, -POOL_HIST:]
    q = rope(q.reshape(B, T, HEADS_C, 2, HEAD_DIM_C), pos)
    k = rope(k.reshape(B, T, HEADS_C, 2, HEAD_DIM_C), pos)
    v = v.reshape(B, T, HEADS_C, 2 * HEAD_DIM_C)
    k_new = k.reshape(B, T, HEADS_C, 2 * HEAD_DIM_C)
    q_chunk = pos // CHUNK
    lam_init = 0.8 - 0.6 * math.exp(-0.3 * layer)
    lam = (jnp.exp(jnp.sum(lq1.astype(jnp.float32) * lk1.astype(jnp.float32)))
           - jnp.exp(jnp.sum(lq2.astype(jnp.float32) * lk2.astype(jnp.float32))) + lam_init)
    if k_past is None:
        nb = T // Q_BLOCK
        qb = jnp.moveaxis(q.reshape(B, nb, Q_BLOCK, HEADS_C, 2, HEAD_DIM_C), 1, 0)
        cb = q_chunk.reshape(nb, Q_BLOCK)
        o = lax.map(lambda qc: diff_attn_block(qc[0], k, v, qc[1], q_chunk, lam), (qb, cb))
        o = jnp.moveaxis(o, 0, 1).reshape(B, T, HEADS_C, 2 * HEAD_DIM_C)
    else:
        P = k_past.shape[1]
        keys = jnp.concatenate([k_past.astype(k.dtype).reshape(B, P, HEADS_C, 2, HEAD_DIM_C), k], axis=1)
        vals = jnp.concatenate([v_past.astype(v.dtype), v], axis=1)
        k_chunk = jnp.concatenate([jnp.arange(P) // CHUNK, q_chunk])
        o = diff_attn_block(q, keys, vals, q_chunk, k_chunk, lam)
    o = rmsnorm(o, subln_g, SUBLN_EPS) * (1.0 - lam_init)
    y_c = jnp.einsum('bte,ed->btd', o.reshape(B, T, ATT_W).astype(x.dtype), w_c)
    gates = jax.nn.sigmoid(g.astype(jnp.float32)).astype(x.dtype).reshape(B, T, N_BRANCH, D_MODEL)
    merged = gates[:, :, 0] * y_a + gates[:, :, 1] * y_b + gates[:, :, 2] * y_c
    out = x + jnp.einsum('btd,de->bte', merged, w_o)
    return out, new_conv, new_pool, k_new, v


def swiglu(h, wg, wu, wd):
    a = jax.nn.silu(jnp.einsum('btd,df->btf', h, wg)) * jnp.einsum('btd,df->btf', h, wu)
    return jnp.einsum('btf,fd->btd', a, wd)


def moe_swiglu(h, w_r, b_r, wg, wu, wd):
    logits = jnp.einsum('btd,de->bte', h.astype(jnp.float32), w_r.astype(jnp.float32)) + b_r.astype(jnp.float32)
    top_val, top_idx = lax.top_k(logits, TOP_K)
    top_w = jax.nn.softmax(top_val, axis=-1)
    y = jnp.zeros_like(h)
    for e in range(N_EXPERTS):
        w_e = jnp.sum(jnp.where(top_idx == e, top_w, 0.0), axis=-1, keepdims=True).astype(h.dtype)
        y = y + w_e * swiglu(h, wg[e], wu[e], wd[e])
    return y


def channel_mixer(x, layer, norm_g, w_gate_d, w_up_d, w_down_d,
                  w_router, b_router, w_gate_e, w_up_e, w_down_e):
    h = rmsnorm(x, norm_g)
    if layer % 2 == 0:
        d = layer // 2
        return x + swiglu(h, w_gate_d[d], w_up_d[d], w_down_d[d])
    m = layer // 2
    return x + moe_swiglu(h, w_router[m], b_router[m], w_gate_e[m], w_up_e[m], w_down_e[m])


def setup_inputs(seed: int = 0) -> dict:
    key = jax.random.key(seed)
    ks = jax.random.split(key, 32)

    def nrm(k, shape, scale):
        return jax.random.normal(k, shape, jnp.float32) * scale

    return {
        'x_prompt': nrm(ks[0], (BATCH, SEQ, D_MODEL), 1.0),
        'x_sample': nrm(ks[1], (DEC_BATCH, DEC_SEQ, D_MODEL), 1.0),
        'cache_k': nrm(ks[2], (DEPTH, DEC_BATCH, PAST_LEN, HEADS_C, 2 * HEAD_DIM_C), 1.0),
        'cache_v': nrm(ks[3], (DEPTH, DEC_BATCH, PAST_LEN, HEADS_C, 2 * HEAD_DIM_C), 1.0),
        'state_conv': nrm(ks[4], (DEPTH, DEC_BATCH, CONV_K - 1, CONV_W), 1.0),
        'state_pool': nrm(ks[5], (DEPTH, DEC_BATCH, POOL_HIST, POOL_W), 1.0),
        'norm_mix': 1.0 + nrm(ks[6], (DEPTH, D_MODEL), 0.05),
        'w_in': nrm(ks[7], (DEPTH, D_MODEL, IN_COLS), D_MODEL ** -0.5),
        'conv_w': nrm(ks[8], (DEPTH, CONV_K, CONV_W), CONV_K ** -0.5),
        'w_conv_out': nrm(ks[9], (DEPTH, CONV_W, D_MODEL), CONV_W ** -0.5),
        'w_pool': nrm(ks[10], (DEPTH, POOL_GROUPS, POOL_GC, POOL_OUT), POOL_GC ** -0.5),
        'pool_scale': 1.0 + nrm(ks[11], (DEPTH, D_MODEL), 0.05),
        'lambda_q1': nrm(ks[12], (DEPTH, HEAD_DIM_C), 0.1),
        'lambda_k1': nrm(ks[13], (DEPTH, HEAD_DIM_C), 0.1),
        'lambda_q2': nrm(ks[14], (DEPTH, HEAD_DIM_C), 0.1),
        'lambda_k2': nrm(ks[15], (DEPTH, HEAD_DIM_C), 0.1),
        'subln_g': 1.0 + nrm(ks[16], (DEPTH, 2 * HEAD_DIM_C), 0.05),
        'w_attn_out': nrm(ks[17], (DEPTH, ATT_W, D_MODEL), ATT_W ** -0.5),
        'w_o': nrm(ks[18], (DEPTH, D_MODEL, D_MODEL), D_MODEL ** -0.5),
        'norm_ffn': 1.0 + nrm(ks[19], (DEPTH, D_MODEL), 0.05),
        'w_gate_d': nrm(ks[20], (N_DENSE, D_MODEL, D_FF), D_MODEL ** -0.5),
        'w_up_d': nrm(ks[21], (N_DENSE, D_MODEL, D_FF), D_MODEL ** -0.5),
        'w_down_d': nrm(ks[22], (N_DENSE, D_FF, D_MODEL), D_FF ** -0.5),
        'w_router': nrm(ks[23], (N_MOE, D_MODEL, N_EXPERTS), D_MODEL ** -0.5),
        'b_router': nrm(ks[24], (N_MOE, N_EXPERTS), 0.01),
        'w_gate_e': nrm(ks[25], (N_MOE, N_EXPERTS, D_MODEL, D_FF_EXPERT), D_MODEL ** -0.5),
        'w_up_e': nrm(ks[26], (N_MOE, N_EXPERTS, D_MODEL, D_FF_EXPERT), D_MODEL ** -0.5),
        'w_down_e': nrm(ks[27], (N_MOE, N_EXPERTS, D_FF_EXPERT, D_MODEL), D_FF_EXPERT ** -0.5),
        'norm_final': 1.0 + nrm(ks[28], (D_MODEL,), 0.05),
    }


def reference(x_prompt, x_sample, cache_k, cache_v, state_conv, state_pool,
              norm_mix, w_in, conv_w, w_conv_out, w_pool, pool_scale,
              lambda_q1, lambda_k1, lambda_q2, lambda_k2, subln_g, w_attn_out, w_o,
              norm_ffn, w_gate_d, w_up_d, w_down_d,
              w_router, b_router, w_gate_e, w_up_e, w_down_e, norm_final):
    B, T = x_prompt.shape[0], x_prompt.shape[1]
    DT = x_sample.shape[1]
    P = cache_k.shape[2]
    pos_p = jnp.arange(T)
    pos_s = P + jnp.arange(DT)
    zero_conv = jnp.zeros((B, CONV_K - 1, CONV_W), x_prompt.dtype)
    zero_pool = jnp.zeros((B, POOL_HIST, POOL_W), x_prompt.dtype)
    xp, xs = x_prompt, x_sample
    kp_l, vp_l, cp_l, pp_l = [], [], [], []
    ks_l, vs_l, cs_l, ps_l = [], [], [], []
    for l in range(DEPTH):
        mix_w = (norm_mix[l], w_in[l], conv_w[l], w_conv_out[l], w_pool[l], pool_scale[l],
                 lambda_q1[l], lambda_k1[l], lambda_q2[l], lambda_k2[l], subln_g[l],
                 w_attn_out[l], w_o[l])
        xp, cp, pp, kp, vp = token_mixer(xp, pos_p, zero_conv, zero_pool, None, None, l, *mix_w)
        xs, cs, ps, kss, vss = token_mixer(xs, pos_s, state_conv[l], state_pool[l],
                                           cache_k[l], cache_v[l], l, *mix_w)
        kp_l.append(kp); vp_l.append(vp); cp_l.append(cp); pp_l.append(pp)
        ks_l.append(kss); vs_l.append(vss); cs_l.append(cs); ps_l.append(ps)
        ffn_w = (norm_ffn[l], w_gate_d, w_up_d, w_down_d, w_router, b_router, w_gate_e, w_up_e, w_down_e)
        xp = channel_mixer(xp, l, *ffn_w)
        xs = channel_mixer(xs, l, *ffn_w)
    y_prompt = rmsnorm(xp, norm_final)
    y_sample = rmsnorm(xs, norm_final)
    k_prompt = jnp.stack(kp_l)
    v_prompt = jnp.stack(vp_l)
    conv_prompt = jnp.stack(cp_l)
    pool_prompt = jnp.stack(pp_l)
    k_sample = jnp.stack(ks_l)
    v_sample = jnp.stack(vs_l)
    conv_sample = jnp.stack(cs_l)
    pool_sample = jnp.stack(ps_l)
    return (y_prompt, y_sample, k_prompt, v_prompt, conv_prompt, pool_prompt, k_sample, v_sample, conv_sample, pool_sample)
```

```python
import functools
import math

import jax
import jax.numpy as jnp
from jax import lax
from jax.experimental import pallas as pl
from jax.experimental.pallas import tpu as pltpu

D_MODEL = 1024
CHUNK = 64
CONV_W = 512
CONV_K = 3
POOL_W = 512
POOL_GC = 128
POOL_WINDOWS = (2, 4, 8, 16)
POOL_HIST = 15
HEADS = 4
HEAD_DIM = 64
HEAD_W = 2 * HEAD_DIM
ATT_W = HEADS * HEAD_W
ROPE_THETA = 10000.0
N_EXPERTS = 8
EPS = 1e-6
SUBLN_EPS = 1e-5

C_XA, C_BA, C_CA, C_U, C_Q, C_K, C_V, C_GA, C_GB, C_GC, C_END = (
    0, 512, 1024, 1536, 2048, 2560, 3072, 3584, 4608, 5632, 6656)

HIST_PAD = 16
VMEM_LIMIT = 56 * 1024 * 1024

F32 = jnp.float32
BF16 = jnp.bfloat16


def _const_spec(shape):
    nd = len(shape)
    return pl.BlockSpec(tuple(shape), lambda *_: (0,) * nd, pipeline_mode=pl.Buffered(1))


def _layer_spec(arr, layer):
    nd = arr.ndim
    return pl.BlockSpec((None,) + tuple(arr.shape[1:]), lambda *_: (layer,) + (0,) * (nd - 1),
                        pipeline_mode=pl.Buffered(1))


def _rms(x, g, eps):
    return x * lax.rsqrt(jnp.mean(x * x, axis=-1, keepdims=True) + eps) * g


def _mixer_in_kernel(x_ref, hc_ref, hp_ref, cos_ref, sin_ref, ng_ref, win_ref, cw_ref, wa_ref,
                     wp_ref, ps_ref,
                     mab_ref, gc_ref, q_ref, k32_ref, v32_ref, k16_ref, v16_ref, co_ref, po_ref,
                     cbuf, ubuf, *, bb, tt, pos0):
    t = pl.program_id(1)
    m = bb * tt
    x = x_ref[...].reshape(m, D_MODEL)
    h = _rms(x, ng_ref[...], EPS).astype(BF16)

    def seg(lo, hi):
        return jnp.dot(h, win_ref[:, lo:hi], preferred_element_type=F32)

    @pl.when(t == 0)
    def _():
        cbuf[:, 0:HIST_PAD, :] = hc_ref[...]
        ubuf[:, 0:HIST_PAD, :] = hp_ref[...]

    cin = (seg(C_CA, C_U) * seg(C_XA, C_BA)).reshape(bb, tt, CONV_W)
    cbuf[:, HIST_PAD:, :] = cin
    conv = cbuf[:, HIST_PAD - 2:HIST_PAD - 2 + tt, :] * cw_ref[0:1, :]
    conv = conv + cbuf[:, HIST_PAD - 1:HIST_PAD - 1 + tt, :] * cw_ref[1:2, :]
    conv = conv + cin * cw_ref[2:3, :]
    ya = jnp.dot((seg(C_BA, C_CA) * conv.reshape(m, CONV_W)).astype(BF16), wa_ref[...],
                 preferred_element_type=F32)

    ubuf[:, HIST_PAD:, :] = seg(C_U, C_Q).reshape(bb, tt, POOL_W)
    pos = pos0 + t * tt + lax.broadcasted_iota(jnp.int32, (bb, tt, POOL_GC), 1)
    yb_parts = []
    for gi, win in enumerate(POOL_WINDOWS):
        lo, hi = gi * POOL_GC, (gi + 1) * POOL_GC
        cur = ubuf[:, HIST_PAD:HIST_PAD + tt, lo:hi]
        s = cur
        for k in range(1, win):
            s = s + ubuf[:, HIST_PAD - k:HIST_PAD - k + tt, lo:hi]
        cnt = jnp.minimum(pos + 1, win).astype(F32)
        pooled = s / cnt - cur
        yb_parts.append(jnp.dot(pooled.reshape(m, POOL_GC).astype(BF16), wp_ref[gi],
                                preferred_element_type=F32))
    yb = jnp.concatenate(yb_parts, axis=-1) * ps_ref[...]

    cos4 = jnp.concatenate([cos_ref[...]] * HEADS, axis=-1)
    sin4 = jnp.concatenate([sin_ref[...]] * HEADS, axis=-1)
    lane = lax.broadcasted_iota(jnp.int32, (m, ATT_W), 1)
    first_half = (lane & (HEAD_DIM // 2)) == 0

    def rope(z):
        swapped = jnp.where(first_half, pltpu.roll(z, ATT_W - HEAD_DIM // 2, axis=1),
                            pltpu.roll(z, HEAD_DIM // 2, axis=1))
        return z * cos4 + swapped * sin4

    q = rope(seg(C_Q, C_K)) * (HEAD_DIM ** -0.5)
    k = rope(seg(C_K, C_V))
    v = seg(C_V, C_GA)
    q_ref[...] = q.astype(BF16).reshape(bb, tt, ATT_W)
    k32_ref[...] = k.reshape(bb, tt, ATT_W)
    v32_ref[...] = v.reshape(bb, tt, ATT_W)
    k16_ref[...] = k.astype(BF16).reshape(bb, tt, ATT_W)
    v16_ref[...] = v.astype(BF16).reshape(bb, tt, ATT_W)

    ga = jax.nn.sigmoid(seg(C_GA, C_GB))
    gb = jax.nn.sigmoid(seg(C_GB, C_GC))
    mab_ref[...] = (ga * ya + gb * yb).astype(BF16).reshape(bb, tt, D_MODEL)
    gc_ref[...] = jax.nn.sigmoid(seg(C_GC, C_END)).astype(BF16).reshape(bb, tt, D_MODEL)

    last_c = cbuf[:, tt:tt + HIST_PAD, :]
    last_p = ubuf[:, tt:tt + HIST_PAD, :]
    co_ref[...] = last_c
    po_ref[...] = last_p
    cbuf[:, 0:HIST_PAD, :] = last_c
    ubuf[:, 0:HIST_PAD, :] = last_p


def _mixer_in(x, hist_c, hist_p, cos, sin, layer, p, *, bb, tt, pos0):
    B, T, _ = x.shape
    assert B % bb == 0 and T % tt == 0 and tt >= HIST_PAD and tt % 8 == 0
    grid = (B // bb, T // tt)
    m = bb * tt

    def seq_spec(w):
        return pl.BlockSpec((bb, tt, w), lambda b, t: (b, t, 0))

    if hist_c.ndim == 4:
        hist_spec = pl.BlockSpec((None, bb, HIST_PAD, CONV_W), lambda b, t: (layer, b, 0, 0))
    else:
        hist_spec = pl.BlockSpec((bb, HIST_PAD, CONV_W), lambda b, t: (b, 0, 0))
    state_spec = pl.BlockSpec((bb, HIST_PAD, CONV_W), lambda b, t: (b, 0, 0))
    rope_spec = pl.BlockSpec((None, m, HEAD_W), lambda b, t: (t, 0, 0))

    out_shape = (
        jax.ShapeDtypeStruct((B, T, D_MODEL), BF16),
        jax.ShapeDtypeStruct((B, T, D_MODEL), BF16),
        jax.ShapeDtypeStruct((B, T, ATT_W), BF16),
        jax.ShapeDtypeStruct((B, T, ATT_W), F32),
        jax.ShapeDtypeStruct((B, T, ATT_W), F32),
        jax.ShapeDtypeStruct((B, T, ATT_W), BF16),
        jax.ShapeDtypeStruct((B, T, ATT_W), BF16),
        jax.ShapeDtypeStruct((B, HIST_PAD, CONV_W), F32),
        jax.ShapeDtypeStruct((B, HIST_PAD, POOL_W), F32),
    )
    out_specs = (seq_spec(D_MODEL), seq_spec(D_MODEL), seq_spec(ATT_W), seq_spec(ATT_W),
                 seq_spec(ATT_W), seq_spec(ATT_W), seq_spec(ATT_W), state_spec, state_spec)
    in_specs = [seq_spec(D_MODEL), hist_spec, hist_spec, rope_spec, rope_spec,
                _layer_spec(p['norm_mix'], layer), _layer_spec(p['w_in'], layer),
                _layer_spec(p['conv_w'], layer), _layer_spec(p['w_conv_out'], layer),
                _layer_spec(p['w_pool'], layer), _layer_spec(p['pool_scale'], layer)]
    return pl.pallas_call(
        functools.partial(_mixer_in_kernel, bb=bb, tt=tt, pos0=pos0),
        grid=grid, in_specs=in_specs, out_specs=out_specs, out_shape=out_shape,
        scratch_shapes=[pltpu.VMEM((bb, HIST_PAD + tt, CONV_W), F32),
                        pltpu.VMEM((bb, HIST_PAD + tt, POOL_W), F32)],
        compiler_params=pltpu.CompilerParams(dimension_semantics=("parallel", "arbitrary"),
                                             vmem_limit_bytes=VMEM_LIMIT),
        name="mixer_in",
    )(x, hist_c, hist_p, cos, sin, p['norm_mix'], p['w_in'], p['conv_w'], p['w_conv_out'],
      p['w_pool'], p['pool_scale'])


def _lambda(lq1_ref, lk1_ref, lq2_ref, lk2_ref, lam_init):
    a = jnp.exp(jnp.sum(lq1_ref[...] * lk1_ref[...], axis=-1, keepdims=True))
    b = jnp.exp(jnp.sum(lq2_ref[...] * lk2_ref[...], axis=-1, keepdims=True))
    return a - b + lam_init


def _stack_components(qh):
    lane = lax.broadcasted_iota(jnp.int32, qh.shape, 1)
    zero = jnp.zeros_like(qh)
    return jnp.concatenate([jnp.where(lane < HEAD_DIM, qh, zero),
                            jnp.where(lane < HEAD_DIM, zero, qh)], axis=0)


def _softmax_step(carry, s, vj):
    m, l, acc = carry
    m_new = jnp.maximum(m, jnp.max(s, axis=-1, keepdims=True))
    p = jnp.exp(s - m_new)
    alpha = jnp.exp(m - m_new)
    l = alpha * l + jnp.sum(p, axis=-1, keepdims=True)
    acc = alpha * acc + jnp.dot(p.astype(BF16), vj, preferred_element_type=F32)
    return m_new, l, acc


def _scores(qs, kj):
    return lax.dot_general(qs, kj, (((1,), (1,)), ((), ())), preferred_element_type=F32)


def _head_out(carry, tq, lam, sg, lam_init):
    _, l, acc = carry
    o = acc / l
    o = o[:tq] - lam * o[tq:]
    return _rms(o, sg, SUBLN_EPS) * (1.0 - lam_init)


def _merge_out(o, mab, gc, x, wc_ref, wo_ref):
    yc = jnp.dot(o.astype(BF16), wc_ref[...], preferred_element_type=F32)
    merged = mab.astype(F32) + gc.astype(F32) * yc
    return x + jnp.dot(merged.astype(BF16), wo_ref[...], preferred_element_type=F32)


def _attn_prompt_kernel(q_ref, k_ref, v_ref, mab_ref, gc_ref, x_ref, lq1_ref, lk1_ref, lq2_ref,
                        lk2_ref, sg_ref, wc_ref, wo_ref, o_ref, *, tq, lam_init):
    i = pl.program_id(1)
    lam = _lambda(lq1_ref, lk1_ref, lq2_ref, lk2_ref, lam_init)
    q_chunk = (lax.broadcasted_iota(jnp.int32, (2 * tq, tq), 0) % tq) // CHUNK
    k_chunk = lax.broadcasted_iota(jnp.int32, (2 * tq, tq), 1) // CHUNK
    visible = k_chunk <= q_chunk
    heads = []
    for hd in range(HEADS):
        lo, hi = hd * HEAD_W, (hd + 1) * HEAD_W
        qs = _stack_components(q_ref[0, :, lo:hi])

        def body(j, carry):
            off = pl.multiple_of(j * tq, tq)
            s = _scores(qs, k_ref[0, pl.ds(off, tq), lo:hi])
            return _softmax_step(carry, s, v_ref[0, pl.ds(off, tq), lo:hi])

        init = (jnp.full((2 * tq, 1), -jnp.inf, F32), jnp.zeros((2 * tq, 1), F32),
                jnp.zeros((2 * tq, HEAD_W), F32))
        carry = lax.fori_loop(0, i, body, init)
        off = pl.multiple_of(i * tq, tq)
        s = jnp.where(visible, _scores(qs, k_ref[0, pl.ds(off, tq), lo:hi]), -jnp.inf)
        carry = _softmax_step(carry, s, v_ref[0, pl.ds(off, tq), lo:hi])
        heads.append(_head_out(carry, tq, lam, sg_ref[...], lam_init))
    o = jnp.concatenate(heads, axis=-1)
    o_ref[0] = _merge_out(o, mab_ref[0], gc_ref[0], x_ref[0], wc_ref, wo_ref)


def _attn_prompt(q, k16, v16, mab, gc, x, layer, p, *, tq):
    B, T, _ = x.shape
    assert T % tq == 0 and tq % CHUNK == 0
    lam_init = 0.8 - 0.6 * math.exp(-0.3 * layer)

    def tile_spec(w):
        return pl.BlockSpec((1, tq, w), lambda b, i: (b, i, 0))

    kv_spec = pl.BlockSpec((1, T, ATT_W), lambda b, i: (b, 0, 0))
    lam_specs = [_layer_spec(p[n], layer) for n in ('lambda_q1', 'lambda_k1', 'lambda_q2', 'lambda_k2')]
    return pl.pallas_call(
        functools.partial(_attn_prompt_kernel, tq=tq, lam_init=lam_init),
        grid=(B, T // tq),
        in_specs=[tile_spec(ATT_W), kv_spec, kv_spec, tile_spec(D_MODEL), tile_spec(D_MODEL),
                  tile_spec(D_MODEL)] + lam_specs +
                 [_layer_spec(p['subln_g'], layer), _layer_spec(p['w_attn_out'], layer),
                  _layer_spec(p['w_o'], layer)],
        out_specs=tile_spec(D_MODEL),
        out_shape=jax.ShapeDtypeStruct((B, T, D_MODEL), F32),
        compiler_params=pltpu.CompilerParams(dimension_semantics=("parallel", "arbitrary"),
                                             vmem_limit_bytes=VMEM_LIMIT),
        name="attn_prompt",
    )(q, k16, v16, mab, gc, x, p['lambda_q1'], p['lambda_k1'], p['lambda_q2'], p['lambda_k2'],
      p['subln_g'], p['w_attn_out'], p['w_o'])


def _attn_sample_kernel(q_ref, kp_ref, vp_ref, kn_ref, vn_ref, mab_ref, gc_ref, x_ref, lq1_ref,
                        lk1_ref, lq2_ref, lk2_ref, sg_ref, wc_ref, wo_ref, o_ref, *, tq, past,
                        lam_init):
    lam = _lambda(lq1_ref, lk1_ref, lq2_ref, lk2_ref, lam_init)
    q_chunk = (past + lax.broadcasted_iota(jnp.int32, (2 * tq, 1), 0) % tq) // CHUNK
    vis_past = (lax.broadcasted_iota(jnp.int32, (2 * tq, past), 1) // CHUNK) <= q_chunk
    vis_new = ((past + lax.broadcasted_iota(jnp.int32, (2 * tq, tq), 1)) // CHUNK) <= q_chunk
    heads = []
    for hd in range(HEADS):
        lo, hi = hd * HEAD_W, (hd + 1) * HEAD_W
        qs = _stack_components(q_ref[0, :, lo:hi])
        carry = (jnp.full((2 * tq, 1), -jnp.inf, F32), jnp.zeros((2 * tq, 1), F32),
                 jnp.zeros((2 * tq, HEAD_W), F32))
        s = jnp.where(vis_new, _scores(qs, kn_ref[0, :, lo:hi]), -jnp.inf)
        carry = _softmax_step(carry, s, vn_ref[0, :, lo:hi])
        s = jnp.where(vis_past, _scores(qs, kp_ref[0, :, lo:hi].astype(BF16)), -jnp.inf)
        carry = _softmax_step(carry, s, vp_ref[0, :, lo:hi].astype(BF16))
        heads.append(_head_out(carry, tq, lam, sg_ref[...], lam_init))
    o = jnp.concatenate(heads, axis=-1)
    o_ref[0] = _merge_out(o, mab_ref[0], gc_ref[0], x_ref[0], wc_ref, wo_ref)


def _attn_sample(q, cache_k, cache_v, k16, v16, mab, gc, x, layer, p):
    B, T, _ = x.shape
    past = cache_k.shape[2]
    lam_init = 0.8 - 0.6 * math.exp(-0.3 * layer)

    def tile_spec(w):
        return pl.BlockSpec((1, T, w), lambda b: (b, 0, 0))

    past_spec = pl.BlockSpec((None, 1, past, ATT_W), lambda b: (layer, b, 0, 0))
    lam_specs = [_layer_spec(p[n], layer) for n in ('lambda_q1', 'lambda_k1', 'lambda_q2', 'lambda_k2')]
    return pl.pallas_call(
        functools.partial(_attn_sample_kernel, tq=T, past=past, lam_init=lam_init),
        grid=(B,),
        in_specs=[tile_spec(ATT_W), past_spec, past_spec, tile_spec(ATT_W), tile_spec(ATT_W),
                  tile_spec(D_MODEL), tile_spec(D_MODEL), tile_spec(D_MODEL)] + lam_specs +
                 [_layer_spec(p['subln_g'], layer), _layer_spec(p['w_attn_out'], layer),
                  _layer_spec(p['w_o'], layer)],
        out_specs=tile_spec(D_MODEL),
        out_shape=jax.ShapeDtypeStruct((B, T, D_MODEL), F32),
        compiler_params=pltpu.CompilerParams(dimension_semantics=("parallel",),
                                             vmem_limit_bytes=VMEM_LIMIT),
        name="attn_sample",
    )(q, cache_k, cache_v, k16, v16, mab, gc, x, p['lambda_q1'], p['lambda_k1'], p['lambda_q2'],
      p['lambda_k2'], p['subln_g'], p['w_attn_out'], p['w_o'])


def _swiglu_acc(acc, h, wg_ref, wu_ref, wd_ref, chunks, scale=None):
    for lo, hi in chunks:
        g = jnp.dot(h, wg_ref[:, lo:hi], preferred_element_type=F32)
        u = jnp.dot(h, wu_ref[:, lo:hi], preferred_element_type=F32)
        a = (g * jax.nn.sigmoid(g) * u).astype(BF16)
        y = jnp.dot(a, wd_ref[lo:hi, :], preferred_element_type=F32)
        acc = acc + (y if scale is None else scale * y)
    return acc


def _chunks(n, step):
    return [(lo, min(lo + step, n)) for lo in range(0, n, step)]


def _ffn_dense_kernel(x_ref, ng_ref, wg_ref, wu_ref, wd_ref, nf_ref, o_ref, *, final):
    x = x_ref[...]
    h = _rms(x, ng_ref[...], EPS).astype(BF16)
    y = _swiglu_acc(x, h, wg_ref, wu_ref, wd_ref, _chunks(wg_ref.shape[1], 1024))
    if final:
        y = _rms(y, nf_ref[...], EPS)
    o_ref[...] = y


def _ffn_dense(x2d, layer, p, *, tm, final):
    N = x2d.shape[0]
    assert N % tm == 0
    d = layer // 2
    row_spec = pl.BlockSpec((tm, D_MODEL), lambda i: (i, 0))
    return pl.pallas_call(
        functools.partial(_ffn_dense_kernel, final=final),
        grid=(N // tm,),
        in_specs=[row_spec, _layer_spec(p['norm_ffn'], layer), _layer_spec(p['w_gate_d'], d),
                  _layer_spec(p['w_up_d'], d), _layer_spec(p['w_down_d'], d),
                  _const_spec(p['norm_final'].shape)],
        out_specs=row_spec,
        out_shape=jax.ShapeDtypeStruct((N, D_MODEL), F32),
        compiler_params=pltpu.CompilerParams(dimension_semantics=("parallel",),
                                             vmem_limit_bytes=VMEM_LIMIT),
        name="ffn_dense",
    )(x2d, p['norm_ffn'], p['w_gate_d'], p['w_up_d'], p['w_down_d'], p['norm_final'])


def _route_top2(logits):
    lane = lax.broadcasted_iota(jnp.int32, logits.shape, 1).astype(F32)
    big = float(N_EXPERTS)
    m1 = jnp.max(logits, axis=-1, keepdims=True)
    i1 = jnp.min(jnp.where(logits == m1, lane, big), axis=-1, keepdims=True)
    rest = jnp.where(lane == i1, -jnp.inf, logits)
    m2 = jnp.max(rest, axis=-1, keepdims=True)
    i2 = jnp.min(jnp.where(rest == m2, lane, big), axis=-1, keepdims=True)
    e2 = jnp.exp(m2 - m1)
    w1 = 1.0 / (1.0 + e2)
    w2 = e2 / (1.0 + e2)
    return jnp.where(lane == i1, w1, 0.0) + jnp.where(lane == i2, w2, 0.0)


def _ffn_moe_kernel(x_ref, ng_ref, wr_ref, br_ref, wg_ref, wu_ref, wd_ref, nf_ref, o_ref,
                    h_scr, wts_scr, acc_scr, *, final):
    e = pl.program_id(1)

    @pl.when(e == 0)
    def _():
        x = x_ref[...]
        h = _rms(x, ng_ref[...], EPS)
        logits = jnp.dot(h, wr_ref[...], preferred_element_type=F32,
                         precision=lax.Precision.HIGHEST) + br_ref[...]
        wts_scr[...] = _route_top2(logits)
        h_scr[...] = h.astype(BF16)
        acc_scr[...] = x

    lane = lax.broadcasted_iota(jnp.int32, wts_scr.shape, 1)
    w_e = jnp.sum(jnp.where(lane == e, wts_scr[...], 0.0), axis=-1, keepdims=True)
    acc_scr[...] = _swiglu_acc(acc_scr[...], h_scr[...], wg_ref, wu_ref, wd_ref,
                               _chunks(wg_ref.shape[1], 768), scale=w_e)

    @pl.when(e == N_EXPERTS - 1)
    def _():
        y = acc_scr[...]
        if final:
            y = _rms(y, nf_ref[...], EPS)
        o_ref[...] = y


def _ffn_moe(x2d, layer, p, *, tm, final):
    N = x2d.shape[0]
    assert N % tm == 0
    mo = layer // 2
    row_spec = pl.BlockSpec((tm, D_MODEL), lambda i, e: (i, 0))

    def expert_spec(arr):
        return pl.BlockSpec((None, None) + tuple(arr.shape[2:]), lambda i, e: (mo, e, 0, 0))

    return pl.pallas_call(
        functools.partial(_ffn_moe_kernel, final=final),
        grid=(N // tm, N_EXPERTS),
        in_specs=[row_spec, _layer_spec(p['norm_ffn'], layer), _layer_spec(p['w_router'], mo),
                  _layer_spec(p['b_router'], mo), expert_spec(p['w_gate_e']),
                  expert_spec(p['w_up_e']), expert_spec(p['w_down_e']),
                  _const_spec(p['norm_final'].shape)],
        out_specs=row_spec,
        out_shape=jax.ShapeDtypeStruct((N, D_MODEL), F32),
        scratch_shapes=[pltpu.VMEM((tm, D_MODEL), BF16), pltpu.VMEM((tm, N_EXPERTS), F32),
                        pltpu.VMEM((tm, D_MODEL), F32)],
        compiler_params=pltpu.CompilerParams(dimension_semantics=("parallel", "arbitrary"),
                                             vmem_limit_bytes=VMEM_LIMIT),
        name="ffn_moe",
    )(x2d, p['norm_ffn'], p['w_router'], p['b_router'], p['w_gate_e'], p['w_up_e'],
      p['w_down_e'], p['norm_final'])


def _rope_tables(pos0, T, bb, tt):
    half = HEAD_DIM // 2
    inv_freq = ROPE_THETA ** (-jnp.arange(half, dtype=F32) / half)
    ang = (pos0 + jnp.arange(T)).astype(F32)[:, None] * inv_freq[None, :]
    cos, sin = jnp.cos(ang), jnp.sin(ang)
    cos = jnp.concatenate([cos, cos, cos, cos], axis=-1)
    sin = jnp.concatenate([-sin, sin, -sin, sin], axis=-1)

    def lay(a):
        a = a.reshape(T // tt, 1, tt, HEAD_W)
        return jnp.broadcast_to(a, (T // tt, bb, tt, HEAD_W)).reshape(T // tt, bb * tt, HEAD_W)

    return lay(cos), lay(sin)


def _pad_hist(state):
    return jnp.pad(state, ((0, 0), (0, 0), (HIST_PAD - state.shape[2], 0), (0, 0)))


_MATMUL_WEIGHTS = ('w_in', 'w_conv_out', 'w_pool', 'w_attn_out', 'w_o', 'w_gate_d', 'w_up_d',
                   'w_down_d', 'w_gate_e', 'w_up_e', 'w_down_e')
_ROW_VECTORS = ('norm_mix', 'pool_scale', 'lambda_q1', 'lambda_k1', 'lambda_q2', 'lambda_k2',
                'subln_g', 'norm_ffn', 'b_router')


def _params(w):
    p = dict(w)
    for n in _MATMUL_WEIGHTS:
        p[n] = w[n].astype(BF16)
    for n in _ROW_VECTORS:
        p[n] = w[n][:, None, :]
    p['norm_final'] = w['norm_final'][None, :]
    return p


def _token_mixer_prompt(x, layer, p, *, tt=512, tq=256):
    B, T, _ = x.shape
    cos, sin = _rope_tables(0, T, 1, tt)
    zero_hist = jnp.zeros((B, HIST_PAD, CONV_W), F32)
    mab, gc, q, k32, v32, k16, v16, co, po = _mixer_in(
        x, zero_hist, zero_hist, cos, sin, layer, p, bb=1, tt=tt, pos0=0)
    x = _attn_prompt(q, k16, v16, mab, gc, x, layer, p, tq=tq)
    return x, co[:, HIST_PAD - (CONV_K - 1):], po[:, HIST_PAD - POOL_HIST:], k32, v32


def _token_mixer_sample(x, cache_k, cache_v, state_conv, state_pool, layer, p, *, bb):
    depth, B, past = cache_k.shape[:3]
    T = x.shape[1]
    cos, sin = _rope_tables(past, T, bb, T)
    mab, gc, q, k32, v32, k16, v16, co, po = _mixer_in(
        x, _pad_hist(state_conv), _pad_hist(state_pool), cos, sin, layer, p, bb=bb, tt=T, pos0=past)
    x = _attn_sample(q, cache_k.reshape(depth, B, past, ATT_W), cache_v.reshape(depth, B, past, ATT_W),
                     k16, v16, mab, gc, x, layer, p)
    return x, co[:, HIST_PAD - (CONV_K - 1):], po[:, HIST_PAD - POOL_HIST:], k32, v32


def _channel_mixer(x, layer, p, *, final):
    B, T, _ = x.shape
    x2d = x.reshape(B * T, D_MODEL)
    if layer % 2 == 0:
        y = _ffn_dense(x2d, layer, p, tm=512, final=final)
    else:
        y = _ffn_moe(x2d, layer, p, tm=min(1024, B * T), final=final)
    return y.reshape(B, T, D_MODEL)


def kernel(x_prompt, x_sample, cache_k, cache_v, state_conv, state_pool, norm_mix, w_in, conv_w,
           w_conv_out, w_pool, pool_scale, lambda_q1, lambda_k1, lambda_q2, lambda_k2, subln_g,
           w_attn_out, w_o, norm_ffn, w_gate_d, w_up_d, w_down_d, w_router, b_router, w_gate_e,
           w_up_e, w_down_e, norm_final):
    depth = w_in.shape[0]
    B, T, _ = x_prompt.shape
    BS, TS, _ = x_sample.shape
    p = _params(dict(
        norm_mix=norm_mix, w_in=w_in, conv_w=conv_w, w_conv_out=w_conv_out, w_pool=w_pool,
        pool_scale=pool_scale, lambda_q1=lambda_q1, lambda_k1=lambda_k1, lambda_q2=lambda_q2,
        lambda_k2=lambda_k2, subln_g=subln_g, w_attn_out=w_attn_out, w_o=w_o, norm_ffn=norm_ffn,
        w_gate_d=w_gate_d, w_up_d=w_up_d, w_down_d=w_down_d, w_router=w_router,
        b_router=b_router, w_gate_e=w_gate_e, w_up_e=w_up_e, w_down_e=w_down_e,
        norm_final=norm_final))

    xp, xs = x_prompt, x_sample
    outs = [[] for _ in range(8)]
    for l in range(depth):
        final = l == depth - 1
        xp, cp, pp, kp, vp = _token_mixer_prompt(xp, l, p)
        xs, cs, ps, ks, vs = _token_mixer_sample(xs, cache_k, cache_v, state_conv, state_pool, l, p,
                                                 bb=16)
        for lst, a in zip(outs, (kp, vp, cp, pp, ks, vs, cs, ps)):
            lst.append(a)
        xp = _channel_mixer(xp, l, p, final=final)
        xs = _channel_mixer(xs, l, p, final=final)

    def stack_kv(lst, b, t):
        return jnp.stack(lst).reshape(depth, b, t, HEADS, HEAD_W)

    return (xp, xs, stack_kv(outs[0], B, T), stack_kv(outs[1], B, T),
            jnp.stack(outs[2]), jnp.stack(outs[3]),
            stack_kv(outs[4], BS, TS), stack_kv(outs[5], BS, TS),
            jnp.stack(outs[6]), jnp.stack(outs[7]))
```

```python
import functools
import math

import jax
import jax.numpy as jnp
from jax import lax
from jax.experimental import pallas as pl
from jax.experimental.pallas import tpu as pltpu

D_MODEL = 1024
CHUNK = 64
CONV_W = 512
CONV_K = 3
POOL_W = 512
POOL_GC = 128
POOL_WINDOWS = (2, 4, 8, 16)
POOL_HIST = 15
HEADS = 4
HEAD_DIM = 64
HEAD_W = 2 * HEAD_DIM
ATT_W = HEADS * HEAD_W
ROPE_THETA = 10000.0
N_EXPERTS = 8
EPS = 1e-6
SUBLN_EPS = 1e-5

C_XA, C_BA, C_CA, C_U, C_Q, C_K, C_V, C_GA, C_GB, C_GC, C_END = (
    0, 512, 1024, 1536, 2048, 2560, 3072, 3584, 4608, 5632, 6656)

HIST_PAD = 16
VMEM_LIMIT = 56 * 1024 * 1024
Q_SCALE = HEAD_DIM ** -0.5 * math.log2(math.e)

F32 = jnp.float32
BF16 = jnp.bfloat16
NT_DIMS = (((1,), (1,)), ((), ()))
TN_DIMS = (((0,), (0,)), ((), ()))


def _const_spec(shape):
    nd = len(shape)
    return pl.BlockSpec(tuple(shape), lambda *_: (0,) * nd, pipeline_mode=pl.Buffered(1))


def _layer_spec(arr, layer):
    nd = arr.ndim
    return pl.BlockSpec((None,) + tuple(arr.shape[1:]), lambda *_: (layer,) + (0,) * (nd - 1),
                        pipeline_mode=pl.Buffered(1))


def _rms(x, g, eps):
    return x * lax.rsqrt(jnp.mean(x * x, axis=-1, keepdims=True) + eps) * g


def _mixer_in_kernel(*refs, bb, tt, pos0, tk, aliased):
    (x_ref, hc_ref, hp_ref, cos_ref, sin_ref, ng_ref, win_ref, cw_ref, wa_ref, wp_ref,
     ps_ref) = refs[:11]
    refs = refs[11 + (2 if aliased else 0):]
    (mab_ref, gc_ref, q_ref, k16_ref, v16_ref, k32_ref, v32_ref, co_ref, po_ref, cbuf,
     ubuf) = refs
    t = pl.program_id(1)
    m = bb * tt
    x = x_ref[...].reshape(m, D_MODEL)
    h = _rms(x, ng_ref[...], EPS).astype(BF16)

    def seg(lo, hi):
        return jnp.dot(h, win_ref[:, lo:hi], preferred_element_type=F32)

    @pl.when(t == 0)
    def _():
        cbuf[:, 0:HIST_PAD, :] = hc_ref[...]
        ubuf[:, 0:HIST_PAD, :] = hp_ref[...]

    cin = (seg(C_CA, C_U) * seg(C_XA, C_BA)).reshape(bb, tt, CONV_W)
    cbuf[:, HIST_PAD:, :] = cin
    conv = cbuf[:, HIST_PAD - 2:HIST_PAD - 2 + tt, :] * cw_ref[0:1, :]
    conv = conv + cbuf[:, HIST_PAD - 1:HIST_PAD - 1 + tt, :] * cw_ref[1:2, :]
    conv = conv + cin * cw_ref[2:3, :]
    ya = jnp.dot((seg(C_BA, C_CA) * conv.reshape(m, CONV_W)).astype(BF16), wa_ref[...],
                 preferred_element_type=F32)

    ubuf[:, HIST_PAD:, :] = seg(C_U, C_Q).reshape(bb, tt, POOL_W)
    pos = pos0 + t * tt + lax.broadcasted_iota(jnp.int32, (bb, tt, POOL_GC), 1)
    yb_parts = []
    for gi, win in enumerate(POOL_WINDOWS):
        lo, hi = gi * POOL_GC, (gi + 1) * POOL_GC
        cur = ubuf[:, HIST_PAD:HIST_PAD + tt, lo:hi]
        s = cur
        for k in range(1, win):
            s = s + ubuf[:, HIST_PAD - k:HIST_PAD - k + tt, lo:hi]
        cnt = jnp.minimum(pos + 1, win).astype(F32)
        pooled = s / cnt - cur
        yb_parts.append(jnp.dot(pooled.reshape(m, POOL_GC).astype(BF16), wp_ref[gi],
                                preferred_element_type=F32))
    yb = jnp.concatenate(yb_parts, axis=-1) * ps_ref[...]

    cos4 = jnp.concatenate([cos_ref[...]] * HEADS, axis=-1)
    sin4 = jnp.concatenate([sin_ref[...]] * HEADS, axis=-1)
    lane = lax.broadcasted_iota(jnp.int32, (m, ATT_W), 1)
    first_half = (lane & (HEAD_DIM // 2)) == 0

    def rope(z):
        swapped = jnp.where(first_half, pltpu.roll(z, ATT_W - HEAD_DIM // 2, axis=1),
                            pltpu.roll(z, HEAD_DIM // 2, axis=1))
        return z * cos4 + swapped * sin4

    q = rope(seg(C_Q, C_K)) * Q_SCALE
    k = rope(seg(C_K, C_V))
    v = seg(C_V, C_GA)
    for hd in range(HEADS):
        lo, hi = hd * HEAD_W, (hd + 1) * HEAD_W
        k32_ref[:, :, hd, :] = k[:, lo:hi].reshape(bb, tt, HEAD_W)
        v32_ref[:, :, hd, :] = v[:, lo:hi].reshape(bb, tt, HEAD_W)
    k16_ref[...] = k.astype(BF16).reshape(bb, tt, ATT_W)
    if tk is None:
        q_ref[...] = q.astype(BF16).reshape(bb, tt, ATT_W)
        v16_ref[...] = v.astype(BF16).reshape(bb, tt, ATT_W)
    else:
        q_ref[0] = q.T.astype(BF16)
        for c in range(tt // tk):
            v16_ref[0, c] = v[c * tk:(c + 1) * tk, :].T.astype(BF16)

    ga = jax.nn.sigmoid(seg(C_GA, C_GB))
    gb = jax.nn.sigmoid(seg(C_GB, C_GC))
    mab_ref[...] = (ga * ya + gb * yb).astype(BF16).reshape(bb, tt, D_MODEL)
    gc_ref[...] = jax.nn.sigmoid(seg(C_GC, C_END)).astype(BF16).reshape(bb, tt, D_MODEL)

    last_c = cbuf[:, tt:tt + HIST_PAD, :]
    last_p = ubuf[:, tt:tt + HIST_PAD, :]
    co_ref[...] = last_c
    po_ref[...] = last_p
    cbuf[:, 0:HIST_PAD, :] = last_c
    ubuf[:, 0:HIST_PAD, :] = last_p


def _mixer_in(x, hist_c, hist_p, cos, sin, layer, p, kv_bufs, *, bb, tt, pos0, tk):
    B, T, _ = x.shape
    depth = p['w_in'].shape[0]
    assert B % bb == 0 and T % tt == 0 and tt >= HIST_PAD and tt % 8 == 0
    assert tk is None or (bb == 1 and tt % tk == 0)
    grid = (B // bb, T // tt)
    m = bb * tt

    def seq_spec(w):
        return pl.BlockSpec((bb, tt, w), lambda b, t: (b, t, 0))

    if hist_c.ndim == 4:
        hist_spec = pl.BlockSpec((None, bb, HIST_PAD, CONV_W), lambda b, t: (layer, b, 0, 0))
    else:
        hist_spec = pl.BlockSpec((bb, HIST_PAD, CONV_W), lambda b, t: (b, 0, 0))
    state_spec = pl.BlockSpec((bb, HIST_PAD, CONV_W), lambda b, t: (b, 0, 0))
    rope_spec = pl.BlockSpec((None, m, HEAD_W), lambda b, t: (t, 0, 0))
    kv_spec = pl.BlockSpec((None, bb, tt, HEADS, HEAD_W), lambda b, t: (layer, b, t, 0, 0))
    kv_shape = jax.ShapeDtypeStruct((depth, B, T, HEADS, HEAD_W), F32)
    if tk is None:
        q_shape, q_spec = jax.ShapeDtypeStruct((B, T, ATT_W), BF16), seq_spec(ATT_W)
        v_shape, v_spec = q_shape, q_spec
    else:
        q_shape = jax.ShapeDtypeStruct((B, ATT_W, T), BF16)
        q_spec = pl.BlockSpec((1, ATT_W, tt), lambda b, t: (b, 0, t))
        v_shape = jax.ShapeDtypeStruct((B, T // tk, ATT_W, tk), BF16)
        v_spec = pl.BlockSpec((1, tt // tk, ATT_W, tk), lambda b, t: (b, t, 0, 0))

    out_shape = (
        jax.ShapeDtypeStruct((B, T, D_MODEL), BF16),
        jax.ShapeDtypeStruct((B, T, D_MODEL), BF16),
        q_shape,
        jax.ShapeDtypeStruct((B, T, ATT_W), BF16),
        v_shape,
        kv_shape, kv_shape,
        jax.ShapeDtypeStruct((B, HIST_PAD, CONV_W), F32),
        jax.ShapeDtypeStruct((B, HIST_PAD, POOL_W), F32),
    )
    out_specs = (seq_spec(D_MODEL), seq_spec(D_MODEL), q_spec, seq_spec(ATT_W), v_spec,
                 kv_spec, kv_spec, state_spec, state_spec)
    in_specs = [seq_spec(D_MODEL), hist_spec, hist_spec, rope_spec, rope_spec,
                _layer_spec(p['norm_mix'], layer), _layer_spec(p['w_in'], layer),
                _layer_spec(p['conv_w'], layer), _layer_spec(p['w_conv_out'], layer),
                _layer_spec(p['w_pool'], layer), _layer_spec(p['pool_scale'], layer)]
    args = [x, hist_c, hist_p, cos, sin, p['norm_mix'], p['w_in'], p['conv_w'], p['w_conv_out'],
            p['w_pool'], p['pool_scale']]
    aliases = {}
    if kv_bufs is not None:
        in_specs += [pl.BlockSpec(memory_space=pl.ANY)] * 2
        aliases = {len(args): 5, len(args) + 1: 6}
        args += list(kv_bufs)
    return pl.pallas_call(
        functools.partial(_mixer_in_kernel, bb=bb, tt=tt, pos0=pos0, tk=tk,
                          aliased=kv_bufs is not None),
        grid=grid, in_specs=in_specs, out_specs=out_specs, out_shape=out_shape,
        scratch_shapes=[pltpu.VMEM((bb, HIST_PAD + tt, CONV_W), F32),
                        pltpu.VMEM((bb, HIST_PAD + tt, POOL_W), F32)],
        input_output_aliases=aliases,
        compiler_params=pltpu.CompilerParams(dimension_semantics=("parallel", "arbitrary"),
                                             vmem_limit_bytes=VMEM_LIMIT),
        name="mixer_in",
    )(*args)


def _lambda(lq1_ref, lk1_ref, lq2_ref, lk2_ref, lam_init):
    a = jnp.exp(jnp.sum(lq1_ref[...] * lk1_ref[...], axis=-1, keepdims=True))
    b = jnp.exp(jnp.sum(lq2_ref[...] * lk2_ref[...], axis=-1, keepdims=True))
    return a - b + lam_init


def _merge_out(o, mab, gc, x, wc_ref, wo_ref):
    yc = jnp.dot(o.astype(BF16), wc_ref[...], preferred_element_type=F32)
    merged = mab.astype(F32) + gc.astype(F32) * yc
    return x + jnp.dot(merged.astype(BF16), wo_ref[...], preferred_element_type=F32)


def _attn_prompt_kernel(qt_ref, k_ref, vt_ref, mab_ref, gc_ref, x_ref, lq1_ref, lk1_ref, lq2_ref,
                        lk2_ref, sg_ref, wc_ref, wo_ref, o_ref, *, tq, lam_init):
    i = pl.program_id(1)
    lam = _lambda(lq1_ref, lk1_ref, lq2_ref, lk2_ref, lam_init)
    k_chunk = lax.broadcasted_iota(jnp.int32, (tq, 2 * tq), 0) // CHUNK
    q_chunk = (lax.broadcasted_iota(jnp.int32, (tq, 2 * tq), 1) % tq) // CHUNK
    visible = k_chunk <= q_chunk
    feat = lax.broadcasted_iota(jnp.int32, (HEAD_W, tq), 0)
    qs = []
    for hd in range(HEADS):
        qt = qt_ref[0, hd * HEAD_W:(hd + 1) * HEAD_W, :]
        zero = jnp.zeros_like(qt)
        qs.append(jnp.concatenate([jnp.where(feat < HEAD_DIM, qt, zero),
                                   jnp.where(feat < HEAD_DIM, zero, qt)], axis=1))

    def step(j, carries, masked):
        off = pl.multiple_of(j * tq, tq)
        scores = [jnp.dot(k_ref[0, pl.ds(off, tq), hd * HEAD_W:(hd + 1) * HEAD_W], qs[hd],
                          preferred_element_type=F32) for hd in range(HEADS)]
        stats = []
        for hd in range(HEADS):
            m, l, _ = carries[hd]
            s = scores[hd]
            if masked:
                s = jnp.where(visible, s, -jnp.inf)
            m_new = jnp.maximum(m, jnp.max(s, axis=0, keepdims=True))
            pr = jnp.exp2(s - m_new)
            alpha = jnp.exp2(m - m_new)
            l = alpha * l + jnp.sum(pr, axis=0, keepdims=True)
            stats.append((m_new, l, alpha, pr.astype(BF16)))
        out = []
        for hd in range(HEADS):
            m_new, l, alpha, pr = stats[hd]
            pv = jnp.dot(vt_ref[0, j, hd * HEAD_W:(hd + 1) * HEAD_W, :], pr,
                         preferred_element_type=F32)
            out.append((m_new, l, alpha * carries[hd][2] + pv))
        return tuple(out)

    init = tuple((jnp.full((1, 2 * tq), -jnp.inf, F32), jnp.zeros((1, 2 * tq), F32),
                  jnp.zeros((HEAD_W, 2 * tq), F32)) for _ in range(HEADS))
    carries = lax.fori_loop(0, i, lambda j, c: step(j, c, False), init)
    carries = step(i, carries, True)
    heads = []
    for hd in range(HEADS):
        _, l, acc = carries[hd]
        ot = acc / l
        ot = ot[:, :tq] - lam * ot[:, tq:]
        ot = ot * lax.rsqrt(jnp.mean(ot * ot, axis=0, keepdims=True) + SUBLN_EPS)
        heads.append(ot.T * sg_ref[...] * (1.0 - lam_init))
    o = jnp.concatenate(heads, axis=-1)
    o_ref[0] = _merge_out(o, mab_ref[0], gc_ref[0], x_ref[0], wc_ref, wo_ref)


def _attn_prompt(qt, k16, vt, mab, gc, x, layer, p, *, tq):
    B, T, _ = x.shape
    assert T % tq == 0 and tq % CHUNK == 0 and vt.shape[3] == tq
    lam_init = 0.8 - 0.6 * math.exp(-0.3 * layer)

    def tile_spec(w):
        return pl.BlockSpec((1, tq, w), lambda b, i: (b, i, 0))

    lam_specs = [_layer_spec(p[n], layer) for n in ('lambda_q1', 'lambda_k1', 'lambda_q2', 'lambda_k2')]
    return pl.pallas_call(
        functools.partial(_attn_prompt_kernel, tq=tq, lam_init=lam_init),
        grid=(B, T // tq),
        in_specs=[pl.BlockSpec((1, ATT_W, tq), lambda b, i: (b, 0, i)),
                  pl.BlockSpec((1, T, ATT_W), lambda b, i: (b, 0, 0)),
                  pl.BlockSpec((1, T // tq, ATT_W, tq), lambda b, i: (b, 0, 0, 0)),
                  tile_spec(D_MODEL), tile_spec(D_MODEL), tile_spec(D_MODEL)] + lam_specs +
                 [_layer_spec(p['subln_g'], layer), _layer_spec(p['w_attn_out'], layer),
                  _layer_spec(p['w_o'], layer)],
        out_specs=tile_spec(D_MODEL),
        out_shape=jax.ShapeDtypeStruct((B, T, D_MODEL), F32),
        compiler_params=pltpu.CompilerParams(dimension_semantics=("parallel", "arbitrary"),
                                             vmem_limit_bytes=VMEM_LIMIT),
        name="attn_prompt",
    )(qt, k16, vt, mab, gc, x, p['lambda_q1'], p['lambda_k1'], p['lambda_q2'], p['lambda_k2'],
      p['subln_g'], p['w_attn_out'], p['w_o'])


def _stack_components(qh):
    lane = lax.broadcasted_iota(jnp.int32, qh.shape, 1)
    zero = jnp.zeros_like(qh)
    return jnp.concatenate([jnp.where(lane < HEAD_DIM, qh, zero),
                            jnp.where(lane < HEAD_DIM, zero, qh)], axis=0)


def _softmax_step(carry, s, vj):
    m, l, acc = carry
    m_new = jnp.maximum(m, jnp.max(s, axis=-1, keepdims=True))
    pr = jnp.exp2(s - m_new)
    alpha = jnp.exp2(m - m_new)
    l = alpha * l + jnp.sum(pr, axis=-1, keepdims=True)
    acc = alpha * acc + jnp.dot(pr.astype(BF16), vj, preferred_element_type=F32)
    return m_new, l, acc


def _attn_sample_kernel(q_ref, kp_ref, vp_ref, kn_ref, vn_ref, mab_ref, gc_ref, x_ref, lq1_ref,
                        lk1_ref, lq2_ref, lk2_ref, sg_ref, wc_ref, wo_ref, o_ref, *, tq, past,
                        lam_init):
    lam = _lambda(lq1_ref, lk1_ref, lq2_ref, lk2_ref, lam_init)
    q_chunk = (past + lax.broadcasted_iota(jnp.int32, (2 * tq, 1), 0) % tq) // CHUNK
    vis_past = (lax.broadcasted_iota(jnp.int32, (2 * tq, past), 1) // CHUNK) <= q_chunk
    vis_new = ((past + lax.broadcasted_iota(jnp.int32, (2 * tq, tq), 1)) // CHUNK) <= q_chunk
    heads = []
    for hd in range(HEADS):
        lo, hi = hd * HEAD_W, (hd + 1) * HEAD_W
        qs = _stack_components(q_ref[0, :, lo:hi])
        carry = (jnp.full((2 * tq, 1), -jnp.inf, F32), jnp.zeros((2 * tq, 1), F32),
                 jnp.zeros((2 * tq, HEAD_W), F32))
        s = lax.dot_general(qs, kn_ref[0, :, lo:hi], NT_DIMS, preferred_element_type=F32)
        carry = _softmax_step(carry, jnp.where(vis_new, s, -jnp.inf), vn_ref[0, :, lo:hi])
        s = lax.dot_general(qs, kp_ref[0, :, hd, :].astype(BF16), NT_DIMS,
                            preferred_element_type=F32)
        _, l, acc = _softmax_step(carry, jnp.where(vis_past, s, -jnp.inf),
                                  vp_ref[0, :, hd, :].astype(BF16))
        o = acc / l
        o = o[:tq] - lam * o[tq:]
        heads.append(_rms(o, sg_ref[...], SUBLN_EPS) * (1.0 - lam_init))
    o = jnp.concatenate(heads, axis=-1)
    o_ref[0] = _merge_out(o, mab_ref[0], gc_ref[0], x_ref[0], wc_ref, wo_ref)


def _attn_sample(q, cache_k, cache_v, k16, v16, mab, gc, x, layer, p):
    B, T, _ = x.shape
    past = cache_k.shape[2]
    lam_init = 0.8 - 0.6 * math.exp(-0.3 * layer)

    def tile_spec(w):
        return pl.BlockSpec((1, T, w), lambda b: (b, 0, 0))

    past_spec = pl.BlockSpec((None, 1, past, HEADS, HEAD_W), lambda b: (layer, b, 0, 0, 0))
    lam_specs = [_layer_spec(p[n], layer) for n in ('lambda_q1', 'lambda_k1', 'lambda_q2', 'lambda_k2')]
    return pl.pallas_call(
        functools.partial(_attn_sample_kernel, tq=T, past=past, lam_init=lam_init),
        grid=(B,),
        in_specs=[tile_spec(ATT_W), past_spec, past_spec, tile_spec(ATT_W), tile_spec(ATT_W),
                  tile_spec(D_MODEL), tile_spec(D_MODEL), tile_spec(D_MODEL)] + lam_specs +
                 [_layer_spec(p['subln_g'], layer), _layer_spec(p['w_attn_out'], layer),
                  _layer_spec(p['w_o'], layer)],
        out_specs=tile_spec(D_MODEL),
        out_shape=jax.ShapeDtypeStruct((B, T, D_MODEL), F32),
        compiler_params=pltpu.CompilerParams(dimension_semantics=("parallel",),
                                             vmem_limit_bytes=VMEM_LIMIT),
        name="attn_sample",
    )(q, cache_k, cache_v, k16, v16, mab, gc, x, p['lambda_q1'], p['lambda_k1'], p['lambda_q2'],
      p['lambda_k2'], p['subln_g'], p['w_attn_out'], p['w_o'])


def _chunks(n, step):
    return [(lo, min(lo + step, n)) for lo in range(0, n, step)]


def _ffn_dense_kernel(x_ref, ng_ref, wg_ref, wu_ref, wd_ref, nf_ref, o_ref, *, final):
    bb, tt, _ = x_ref.shape
    x = x_ref[...].reshape(bb * tt, D_MODEL)
    h = _rms(x, ng_ref[...], EPS).astype(BF16)
    y = x
    for lo, hi in _chunks(wg_ref.shape[1], 1024):
        g = jnp.dot(h, wg_ref[:, lo:hi], preferred_element_type=F32)
        u = jnp.dot(h, wu_ref[:, lo:hi], preferred_element_type=F32)
        a = (g * jax.nn.sigmoid(g) * u).astype(BF16)
        y = y + jnp.dot(a, wd_ref[lo:hi, :], preferred_element_type=F32)
    if final:
        y = _rms(y, nf_ref[...], EPS)
    o_ref[...] = y.reshape(bb, tt, D_MODEL)


def _token_blocks(B, T, tm):
    if T >= tm:
        assert T % tm == 0
        return 1, tm
    assert tm % T == 0 and B % (tm // T) == 0
    return tm // T, T


def _ffn_dense(x, layer, p, *, tm, final):
    B, T, _ = x.shape
    bb, tt = _token_blocks(B, T, tm)
    d = layer // 2
    row_spec = pl.BlockSpec((bb, tt, D_MODEL), lambda b, t: (b, t, 0))
    return pl.pallas_call(
        functools.partial(_ffn_dense_kernel, final=final),
        grid=(B // bb, T // tt),
        in_specs=[row_spec, _layer_spec(p['norm_ffn'], layer), _layer_spec(p['w_gate_d'], d),
                  _layer_spec(p['w_up_d'], d), _layer_spec(p['w_down_d'], d),
                  _const_spec(p['norm_final'].shape)],
        out_specs=row_spec,
        out_shape=jax.ShapeDtypeStruct(x.shape, F32),
        compiler_params=pltpu.CompilerParams(dimension_semantics=("parallel", "parallel"),
                                             vmem_limit_bytes=VMEM_LIMIT),
        name="ffn_dense",
    )(x, p['norm_ffn'], p['w_gate_d'], p['w_up_d'], p['w_down_d'], p['norm_final'])


def _route_top2(logits):
    row = lax.broadcasted_iota(jnp.int32, logits.shape, 0).astype(F32)
    big = float(N_EXPERTS)
    m1 = jnp.max(logits, axis=0, keepdims=True)
    i1 = jnp.min(jnp.where(logits == m1, row, big), axis=0, keepdims=True)
    rest = jnp.where(row == i1, -jnp.inf, logits)
    m2 = jnp.max(rest, axis=0, keepdims=True)
    i2 = jnp.min(jnp.where(rest == m2, row, big), axis=0, keepdims=True)
    e2 = jnp.exp(m2 - m1)
    w1 = 1.0 / (1.0 + e2)
    w2 = e2 / (1.0 + e2)
    return jnp.where(row == i1, w1, 0.0) + jnp.where(row == i2, w2, 0.0)


def _ffn_moe_kernel(x_ref, ng_ref, wrt_ref, brt_ref, wg_ref, wu_ref, wd_ref, nf_ref, o_ref,
                    h_scr, wts_scr, rank_scr, *, final, tb, cm):
    e = pl.program_id(2)
    bb, tt, _ = x_ref.shape
    n_tok = bb * tt
    sub_blocks = _chunks(n_tok, tb)

    @pl.when(e == 0)
    def _():
        x = x_ref[...].reshape(n_tok, D_MODEL)
        h = _rms(x, ng_ref[...], EPS)
        logits = lax.dot_general(wrt_ref[...], h, NT_DIMS, preferred_element_type=F32,
                                 precision=lax.Precision.HIGHEST) + brt_ref[...]
        wts = _route_top2(logits)
        sel = (wts > 0.0).astype(BF16)
        before = (lax.broadcasted_iota(jnp.int32, (tb, tb), 0)
                  < lax.broadcasted_iota(jnp.int32, (tb, tb), 1)).astype(BF16)
        rank = jnp.concatenate(
            [jnp.dot(sel[:, lo:hi], before, preferred_element_type=F32) for lo, hi in sub_blocks],
            axis=1)
        for ee in range(N_EXPERTS):
            wts_scr[ee] = wts[ee:ee + 1]
            rank_scr[ee] = rank[ee:ee + 1]
        h_scr[...] = h.astype(BF16)
        o_ref[...] = x_ref[...]

    w_all = wts_scr[e]
    r_all = rank_scr[e]
    for lo, hi in sub_blocks:
        w_row = w_all[:, lo:hi]
        r_row = r_all[:, lo:hi]
        sel = w_row > 0.0
        n_sel = jnp.sum(sel.astype(F32)).astype(jnp.int32)
        n_chunks = sum((n_sel > c * cm).astype(jnp.int32) for c in range(-(-tb // cm)))

        def chunk(c, carry, lo=lo, hi=hi, w_row=w_row, r_row=r_row, sel=sel):
            rows = (lax.broadcasted_iota(jnp.int32, (cm, tb), 0) + c * cm).astype(F32)
            hit = (r_row == rows) & sel
            gather = jnp.where(hit, 1.0, 0.0).astype(BF16)
            xg = jnp.dot(gather, h_scr[lo:hi, :], preferred_element_type=F32).astype(BF16)
            g = jnp.dot(xg, wg_ref[...], preferred_element_type=F32)
            u = jnp.dot(xg, wu_ref[...], preferred_element_type=F32)
            a = (g * jax.nn.sigmoid(g) * u).astype(BF16)
            y = jnp.dot(a, wd_ref[...], preferred_element_type=F32).astype(BF16)
            scatter = jnp.where(hit, w_row, 0.0).astype(BF16)
            upd = lax.dot_general(scatter, y, TN_DIMS, preferred_element_type=F32)
            if bb == 1:
                o_ref[0, lo:hi, :] += upd
            else:
                o_ref[lo // tt:hi // tt] += upd.reshape((hi - lo) // tt, tt, D_MODEL)
            return carry

        lax.fori_loop(0, n_chunks, chunk, 0)

    if final:
        @pl.when(e == N_EXPERTS - 1)
        def _():
            y = o_ref[...].reshape(n_tok, D_MODEL)
            o_ref[...] = _rms(y, nf_ref[...], EPS).reshape(bb, tt, D_MODEL)


def _ffn_moe(x, layer, p, *, tm, tb, cm, final):
    B, T, _ = x.shape
    bb, tt = _token_blocks(B, T, tm)
    assert tm % tb == 0 and (bb == 1 or tb % tt == 0)
    mo = layer // 2
    row_spec = pl.BlockSpec((bb, tt, D_MODEL), lambda b, t, e: (b, t, 0))

    def expert_spec(arr):
        return pl.BlockSpec((None, None) + tuple(arr.shape[2:]), lambda b, t, e: (mo, e, 0, 0))

    return pl.pallas_call(
        functools.partial(_ffn_moe_kernel, final=final, tb=tb, cm=cm),
        grid=(B // bb, T // tt, N_EXPERTS),
        in_specs=[row_spec, _layer_spec(p['norm_ffn'], layer), _layer_spec(p['w_router_t'], mo),
                  _layer_spec(p['b_router_t'], mo), expert_spec(p['w_gate_e']),
                  expert_spec(p['w_up_e']), expert_spec(p['w_down_e']),
                  _const_spec(p['norm_final'].shape)],
        out_specs=row_spec,
        out_shape=jax.ShapeDtypeStruct(x.shape, F32),
        scratch_shapes=[pltpu.VMEM((tm, D_MODEL), BF16), pltpu.VMEM((N_EXPERTS, 1, tm), F32),
                        pltpu.VMEM((N_EXPERTS, 1, tm), F32)],
        compiler_params=pltpu.CompilerParams(
            dimension_semantics=("parallel", "parallel", "arbitrary"),
            vmem_limit_bytes=VMEM_LIMIT),
        name="ffn_moe",
    )(x, p['norm_ffn'], p['w_router_t'], p['b_router_t'], p['w_gate_e'], p['w_up_e'],
      p['w_down_e'], p['norm_final'])


def _rope_tables(pos0, T, bb, tt):
    half = HEAD_DIM // 2
    inv_freq = ROPE_THETA ** (-jnp.arange(half, dtype=F32) / half)
    ang = (pos0 + jnp.arange(T)).astype(F32)[:, None] * inv_freq[None, :]
    cos, sin = jnp.cos(ang), jnp.sin(ang)
    cos = jnp.concatenate([cos, cos, cos, cos], axis=-1)
    sin = jnp.concatenate([-sin, sin, -sin, sin], axis=-1)

    def lay(a):
        a = a.reshape(T // tt, 1, tt, HEAD_W)
        return jnp.broadcast_to(a, (T // tt, bb, tt, HEAD_W)).reshape(T // tt, bb * tt, HEAD_W)

    return lay(cos), lay(sin)


def _pad_hist(state):
    return jnp.pad(state, ((0, 0), (0, 0), (HIST_PAD - state.shape[2], 0), (0, 0)))


_MATMUL_WEIGHTS = ('w_in', 'w_conv_out', 'w_pool', 'w_attn_out', 'w_o', 'w_gate_d', 'w_up_d',
                   'w_down_d', 'w_gate_e', 'w_up_e', 'w_down_e')
_ROW_VECTORS = ('norm_mix', 'pool_scale', 'lambda_q1', 'lambda_k1', 'lambda_q2', 'lambda_k2',
                'subln_g', 'norm_ffn')


def _params(w):
    p = dict(w)
    for n in _MATMUL_WEIGHTS:
        p[n] = w[n].astype(BF16)
    for n in _ROW_VECTORS:
        p[n] = w[n][:, None, :]
    p['w_router_t'] = jnp.swapaxes(w['w_router'], 1, 2)
    p['b_router_t'] = w['b_router'][:, :, None]
    p['norm_final'] = w['norm_final'][None, :]
    return p


def _token_mixer_prompt(x, layer, p, kv_bufs, *, tt=512, tq=256):
    B, T, _ = x.shape
    cos, sin = _rope_tables(0, T, 1, tt)
    zero_hist = jnp.zeros((B, HIST_PAD, CONV_W), F32)
    mab, gc, qt, k16, vt, k32, v32, co, po = _mixer_in(
        x, zero_hist, zero_hist, cos, sin, layer, p, kv_bufs, bb=1, tt=tt, pos0=0, tk=tq)
    x = _attn_prompt(qt, k16, vt, mab, gc, x, layer, p, tq=tq)
    return x, co[:, HIST_PAD - (CONV_K - 1):], po[:, HIST_PAD - POOL_HIST:], (k32, v32)


def _token_mixer_sample(x, cache_k, cache_v, state_conv, state_pool, layer, p, kv_bufs, *, bb):
    past = cache_k.shape[2]
    T = x.shape[1]
    cos, sin = _rope_tables(past, T, bb, T)
    mab, gc, q, k16, v16, k32, v32, co, po = _mixer_in(
        x, _pad_hist(state_conv), _pad_hist(state_pool), cos, sin, layer, p, kv_bufs,
        bb=bb, tt=T, pos0=past, tk=None)
    x = _attn_sample(q, cache_k, cache_v, k16, v16, mab, gc, x, layer, p)
    return x, co[:, HIST_PAD - (CONV_K - 1):], po[:, HIST_PAD - POOL_HIST:], (k32, v32)


def _channel_mixer(x, layer, p, *, final):
    if layer % 2 == 0:
        return _ffn_dense(x, layer, p, tm=512, final=final)
    return _ffn_moe(x, layer, p, tm=1024, tb=512, cm=160, final=final)


def kernel(x_prompt, x_sample, cache_k, cache_v, state_conv, state_pool, norm_mix, w_in, conv_w,
           w_conv_out, w_pool, pool_scale, lambda_q1, lambda_k1, lambda_q2, lambda_k2, subln_g,
           w_attn_out, w_o, norm_ffn, w_gate_d, w_up_d, w_down_d, w_router, b_router, w_gate_e,
           w_up_e, w_down_e, norm_final):
    depth = w_in.shape[0]
    p = _params(dict(
        norm_mix=norm_mix, w_in=w_in, conv_w=conv_w, w_conv_out=w_conv_out, w_pool=w_pool,
        pool_scale=pool_scale, lambda_q1=lambda_q1, lambda_k1=lambda_k1, lambda_q2=lambda_q2,
        lambda_k2=lambda_k2, subln_g=subln_g, w_attn_out=w_attn_out, w_o=w_o, norm_ffn=norm_ffn,
        w_gate_d=w_gate_d, w_up_d=w_up_d, w_down_d=w_down_d, w_router=w_router,
        b_router=b_router, w_gate_e=w_gate_e, w_up_e=w_up_e, w_down_e=w_down_e,
        norm_final=norm_final))

    xp, xs = x_prompt, x_sample
    kv_p = kv_s = None
    states = [[] for _ in range(4)]
    for l in range(depth):
        final = l == depth - 1
        xp, cp, pp, kv_p = _token_mixer_prompt(xp, l, p, kv_p)
        xs, cs, ps, kv_s = _token_mixer_sample(xs, cache_k, cache_v, state_conv, state_pool, l, p,
                                               kv_s, bb=16)
        for lst, a in zip(states, (cp, pp, cs, ps)):
            lst.append(a)
        xp = _channel_mixer(xp, l, p, final=final)
        xs = _channel_mixer(xs, l, p, final=final)

    return (xp, xs, kv_p[0], kv_p[1], jnp.stack(states[0]), jnp.stack(states[1]),
            kv_s[0], kv_s[1], jnp.stack(states[2]), jnp.stack(states[3]))
```

```python
import functools
import math

import jax
import jax.numpy as jnp
from jax import lax
from jax.experimental import pallas as pl
from jax.experimental.pallas import tpu as pltpu

D_MODEL = 1024
CHUNK = 64
CONV_W = 512
CONV_K = 3
POOL_W = 512
POOL_GC = 128
POOL_WINDOWS = (2, 4, 8, 16)
POOL_HIST = 15
HEADS = 4
HEAD_DIM = 64
HEAD_W = 2 * HEAD_DIM
ATT_W = HEADS * HEAD_W
ROPE_THETA = 10000.0
N_EXPERTS = 8
EPS = 1e-6
SUBLN_EPS = 1e-5

C_XA, C_BA, C_CA, C_U, C_Q, C_K, C_V, C_GA, C_GB, C_GC, C_END = (
    0, 512, 1024, 1536, 2048, 2560, 3072, 3584, 4608, 5632, 6656)

HIST_PAD = 16
VMEM_LIMIT = 56 * 1024 * 1024
Q_SCALE = HEAD_DIM ** -0.5 * math.log2(math.e)

F32 = jnp.float32
BF16 = jnp.bfloat16
NT_DIMS = (((1,), (1,)), ((), ()))
TN_DIMS = (((0,), (0,)), ((), ()))


def _const_spec(shape):
    nd = len(shape)
    return pl.BlockSpec(tuple(shape), lambda *_: (0,) * nd, pipeline_mode=pl.Buffered(1))


def _layer_spec(arr, layer):
    nd = arr.ndim
    return pl.BlockSpec((None,) + tuple(arr.shape[1:]), lambda *_: (layer,) + (0,) * (nd - 1),
                        pipeline_mode=pl.Buffered(1))


def _rms(x, g, eps):
    return x * lax.rsqrt(jnp.mean(x * x, axis=-1, keepdims=True) + eps) * g


def _mixer_in_kernel(*refs, bb, tt, pos0, tk, aliased):
    (x_ref, hc_ref, hp_ref, cos_ref, sin_ref, ng_ref, win_ref, cw_ref, wa_ref, wp_ref,
     ps_ref) = refs[:11]
    refs = refs[11 + (2 if aliased else 0):]
    (mab_ref, gc_ref, q_ref, k16_ref, v16_ref, k32_ref, v32_ref, co_ref, po_ref, cbuf,
     ubuf) = refs
    t = pl.program_id(1)
    m = bb * tt
    x = x_ref[...].reshape(m, D_MODEL)
    h = _rms(x, ng_ref[...], EPS).astype(BF16)

    def seg(lo, hi):
        return jnp.dot(h, win_ref[:, lo:hi], preferred_element_type=F32)

    @pl.when(t == 0)
    def _():
        cbuf[:, 0:HIST_PAD, :] = hc_ref[...]
        ubuf[:, 0:HIST_PAD, :] = hp_ref[...]

    cin = (seg(C_CA, C_U) * seg(C_XA, C_BA)).reshape(bb, tt, CONV_W)
    cbuf[:, HIST_PAD:, :] = cin
    conv = cbuf[:, HIST_PAD - 2:HIST_PAD - 2 + tt, :] * cw_ref[0:1, :]
    conv = conv + cbuf[:, HIST_PAD - 1:HIST_PAD - 1 + tt, :] * cw_ref[1:2, :]
    conv = conv + cin * cw_ref[2:3, :]
    ya = jnp.dot((seg(C_BA, C_CA) * conv.reshape(m, CONV_W)).astype(BF16), wa_ref[...],
                 preferred_element_type=F32)

    ubuf[:, HIST_PAD:, :] = seg(C_U, C_Q).reshape(bb, tt, POOL_W)
    pos = pos0 + t * tt + lax.broadcasted_iota(jnp.int32, (bb, tt, POOL_GC), 1)
    yb_parts = []
    for gi, win in enumerate(POOL_WINDOWS):
        lo, hi = gi * POOL_GC, (gi + 1) * POOL_GC
        cur = ubuf[:, HIST_PAD:HIST_PAD + tt, lo:hi]
        s = cur
        for k in range(1, win):
            s = s + ubuf[:, HIST_PAD - k:HIST_PAD - k + tt, lo:hi]
        cnt = jnp.minimum(pos + 1, win).astype(F32)
        pooled = s / cnt - cur
        yb_parts.append(jnp.dot(pooled.reshape(m, POOL_GC).astype(BF16), wp_ref[gi],
                                preferred_element_type=F32))
    yb = jnp.concatenate(yb_parts, axis=-1) * ps_ref[...]

    cos4 = jnp.concatenate([cos_ref[...]] * HEADS, axis=-1)
    sin4 = jnp.concatenate([sin_ref[...]] * HEADS, axis=-1)
    lane = lax.broadcasted_iota(jnp.int32, (m, ATT_W), 1)
    first_half = (lane & (HEAD_DIM // 2)) == 0

    def rope(z):
        swapped = jnp.where(first_half, pltpu.roll(z, ATT_W - HEAD_DIM // 2, axis=1),
                            pltpu.roll(z, HEAD_DIM // 2, axis=1))
        return z * cos4 + swapped * sin4

    q = rope(seg(C_Q, C_K)) * Q_SCALE
    k = rope(seg(C_K, C_V))
    v = seg(C_V, C_GA)
    for hd in range(HEADS):
        lo, hi = hd * HEAD_W, (hd + 1) * HEAD_W
        k32_ref[:, :, hd, :] = k[:, lo:hi].reshape(bb, tt, HEAD_W)
        v32_ref[:, :, hd, :] = v[:, lo:hi].reshape(bb, tt, HEAD_W)
    k16_ref[...] = k.astype(BF16).reshape(bb, tt, ATT_W)
    if tk is None:
        q_ref[...] = q.astype(BF16).reshape(bb, tt, ATT_W)
        v16_ref[...] = v.astype(BF16).reshape(bb, tt, ATT_W)
    else:
        q_ref[0] = q.T.astype(BF16)
        for c in range(tt // tk):
            v16_ref[0, c] = v[c * tk:(c + 1) * tk, :].T.astype(BF16)

    ga = jax.nn.sigmoid(seg(C_GA, C_GB))
    gb = jax.nn.sigmoid(seg(C_GB, C_GC))
    mab_ref[...] = (ga * ya + gb * yb).astype(BF16).reshape(bb, tt, D_MODEL)
    gc_ref[...] = jax.nn.sigmoid(seg(C_GC, C_END)).astype(BF16).reshape(bb, tt, D_MODEL)

    last_c = cbuf[:, tt:tt + HIST_PAD, :]
    last_p = ubuf[:, tt:tt + HIST_PAD, :]
    co_ref[...] = last_c
    po_ref[...] = last_p
    cbuf[:, 0:HIST_PAD, :] = last_c
    ubuf[:, 0:HIST_PAD, :] = last_p


def _mixer_in(x, hist_c, hist_p, cos, sin, layer, p, kv_bufs, *, bb, tt, pos0, tk):
    B, T, _ = x.shape
    depth = p['w_in'].shape[0]
    assert B % bb == 0 and T % tt == 0 and tt >= HIST_PAD and tt % 8 == 0
    assert tk is None or (bb == 1 and tt % tk == 0)
    grid = (B // bb, T // tt)
    m = bb * tt

    def seq_spec(w):
        return pl.BlockSpec((bb, tt, w), lambda b, t: (b, t, 0))

    if hist_c.ndim == 4:
        hist_spec = pl.BlockSpec((None, bb, HIST_PAD, CONV_W), lambda b, t: (layer, b, 0, 0))
    else:
        hist_spec = pl.BlockSpec((bb, HIST_PAD, CONV_W), lambda b, t: (b, 0, 0))
    state_spec = pl.BlockSpec((bb, HIST_PAD, CONV_W), lambda b, t: (b, 0, 0))
    rope_spec = pl.BlockSpec((None, m, HEAD_W), lambda b, t: (t, 0, 0))
    kv_spec = pl.BlockSpec((None, bb, tt, HEADS, HEAD_W), lambda b, t: (layer, b, t, 0, 0))
    kv_shape = jax.ShapeDtypeStruct((depth, B, T, HEADS, HEAD_W), F32)
    if tk is None:
        q_shape, q_spec = jax.ShapeDtypeStruct((B, T, ATT_W), BF16), seq_spec(ATT_W)
        v_shape, v_spec = q_shape, q_spec
    else:
        q_shape = jax.ShapeDtypeStruct((B, ATT_W, T), BF16)
        q_spec = pl.BlockSpec((1, ATT_W, tt), lambda b, t: (b, 0, t))
        v_shape = jax.ShapeDtypeStruct((B, T // tk, ATT_W, tk), BF16)
        v_spec = pl.BlockSpec((1, tt // tk, ATT_W, tk), lambda b, t: (b, t, 0, 0))

    out_shape = (
        jax.ShapeDtypeStruct((B, T, D_MODEL), BF16),
        jax.ShapeDtypeStruct((B, T, D_MODEL), BF16),
        q_shape,
        jax.ShapeDtypeStruct((B, T, ATT_W), BF16),
        v_shape,
        kv_shape, kv_shape,
        jax.ShapeDtypeStruct((B, HIST_PAD, CONV_W), F32),
        jax.ShapeDtypeStruct((B, HIST_PAD, POOL_W), F32),
    )
    out_specs = (seq_spec(D_MODEL), seq_spec(D_MODEL), q_spec, seq_spec(ATT_W), v_spec,
                 kv_spec, kv_spec, state_spec, state_spec)
    in_specs = [seq_spec(D_MODEL), hist_spec, hist_spec, rope_spec, rope_spec,
                _layer_spec(p['norm_mix'], layer), _layer_spec(p['w_in'], layer),
                _layer_spec(p['conv_w'], layer), _layer_spec(p['w_conv_out'], layer),
                _layer_spec(p['w_pool'], layer), _layer_spec(p['pool_scale'], layer)]
    args = [x, hist_c, hist_p, cos, sin, p['norm_mix'], p['w_in'], p['conv_w'], p['w_conv_out'],
            p['w_pool'], p['pool_scale']]
    aliases = {}
    if kv_bufs is not None:
        in_specs += [pl.BlockSpec(memory_space=pl.ANY)] * 2
        aliases = {len(args): 5, len(args) + 1: 6}
        args += list(kv_bufs)
    return pl.pallas_call(
        functools.partial(_mixer_in_kernel, bb=bb, tt=tt, pos0=pos0, tk=tk,
                          aliased=kv_bufs is not None),
        grid=grid, in_specs=in_specs, out_specs=out_specs, out_shape=out_shape,
        scratch_shapes=[pltpu.VMEM((bb, HIST_PAD + tt, CONV_W), F32),
                        pltpu.VMEM((bb, HIST_PAD + tt, POOL_W), F32)],
        input_output_aliases=aliases,
        compiler_params=pltpu.CompilerParams(dimension_semantics=("parallel", "arbitrary"),
                                             vmem_limit_bytes=VMEM_LIMIT),
        name="mixer_in",
    )(*args)


def _lambda(lq1_ref, lk1_ref, lq2_ref, lk2_ref, lam_init):
    a = jnp.exp(jnp.sum(lq1_ref[...] * lk1_ref[...], axis=-1, keepdims=True))
    b = jnp.exp(jnp.sum(lq2_ref[...] * lk2_ref[...], axis=-1, keepdims=True))
    return a - b + lam_init


def _merge_out(o, mab, gc, x, wc_ref, wo_ref):
    yc = jnp.dot(o.astype(BF16), wc_ref[...], preferred_element_type=F32)
    merged = mab.astype(F32) + gc.astype(F32) * yc
    return x + jnp.dot(merged.astype(BF16), wo_ref[...], preferred_element_type=F32)


def _attn_prompt_kernel(qt_ref, k_ref, vt_ref, mab_ref, gc_ref, x_ref, lq1_ref, lk1_ref, lq2_ref,
                        lk2_ref, sg_ref, wc_ref, wo_ref, o_ref, *, tq, lam_init):
    i = pl.program_id(1)
    lam = _lambda(lq1_ref, lk1_ref, lq2_ref, lk2_ref, lam_init)
    k_chunk = lax.broadcasted_iota(jnp.int32, (tq, 2 * tq), 0) // CHUNK
    q_chunk = (lax.broadcasted_iota(jnp.int32, (tq, 2 * tq), 1) % tq) // CHUNK
    visible = k_chunk <= q_chunk
    feat = lax.broadcasted_iota(jnp.int32, (HEAD_W, tq), 0)
    qs = []
    for hd in range(HEADS):
        qt = qt_ref[0, hd * HEAD_W:(hd + 1) * HEAD_W, :]
        zero = jnp.zeros_like(qt)
        qs.append(jnp.concatenate([jnp.where(feat < HEAD_DIM, qt, zero),
                                   jnp.where(feat < HEAD_DIM, zero, qt)], axis=1))

    def scores(j):
        off = pl.multiple_of(j * tq, tq)
        return [jnp.dot(k_ref[0, pl.ds(off, tq), hd * HEAD_W:(hd + 1) * HEAD_W], qs[hd],
                        preferred_element_type=F32) for hd in range(HEADS)]

    def softmax(s, m, l):
        m_new = jnp.maximum(m, jnp.max(s, axis=0, keepdims=True))
        pr = jnp.exp2(s - m_new)
        alpha = jnp.exp2(m - m_new)
        return m_new, alpha * l + jnp.sum(pr, axis=0, keepdims=True), alpha, pr.astype(BF16)

    def weighted_values(j, acc, alpha, pr):
        return [alpha[hd] * acc[hd] + jnp.dot(vt_ref[0, j, hd * HEAD_W:(hd + 1) * HEAD_W, :],
                                              pr[hd], preferred_element_type=F32)
                for hd in range(HEADS)]

    def unzip(rows):
        return tuple(list(col) for col in zip(*rows))

    m, l, alpha, pr = unzip([
        softmax(jnp.where(visible, s, -jnp.inf), jnp.full((1, 2 * tq), -jnp.inf, F32),
                jnp.zeros((1, 2 * tq), F32)) for s in scores(i)])
    acc = [jnp.zeros((HEAD_W, 2 * tq), F32) for _ in range(HEADS)]

    def body(j, carry):
        s_cur, pend, alpha, pr, m, l, acc = carry
        acc = weighted_values(pend, acc, alpha, pr)
        s_next = scores(j + 1)
        m, l, alpha, pr = unzip([softmax(s_cur[hd], m[hd], l[hd]) for hd in range(HEADS)])
        return s_next, j, alpha, pr, m, l, acc

    _, pend, alpha, pr, m, l, acc = lax.fori_loop(
        0, i, body, (scores(0), i, alpha, pr, m, l, acc))
    acc = weighted_values(pend, acc, alpha, pr)
    heads = []
    for hd in range(HEADS):
        ot = acc[hd] / l[hd]
        ot = ot[:, :tq] - lam * ot[:, tq:]
        ot = ot * lax.rsqrt(jnp.mean(ot * ot, axis=0, keepdims=True) + SUBLN_EPS)
        heads.append(ot.T * sg_ref[...] * (1.0 - lam_init))
    o = jnp.concatenate(heads, axis=-1)
    o_ref[0] = _merge_out(o, mab_ref[0], gc_ref[0], x_ref[0], wc_ref, wo_ref)


def _attn_prompt(qt, k16, vt, mab, gc, x, layer, p, *, tq):
    B, T, _ = x.shape
    assert T % tq == 0 and tq % CHUNK == 0 and vt.shape[3] == tq
    lam_init = 0.8 - 0.6 * math.exp(-0.3 * layer)

    def tile_spec(w):
        return pl.BlockSpec((1, tq, w), lambda b, i: (b, i, 0))

    lam_specs = [_layer_spec(p[n], layer) for n in ('lambda_q1', 'lambda_k1', 'lambda_q2', 'lambda_k2')]
    return pl.pallas_call(
        functools.partial(_attn_prompt_kernel, tq=tq, lam_init=lam_init),
        grid=(B, T // tq),
        in_specs=[pl.BlockSpec((1, ATT_W, tq), lambda b, i: (b, 0, i)),
                  pl.BlockSpec((1, T, ATT_W), lambda b, i: (b, 0, 0)),
                  pl.BlockSpec((1, T // tq, ATT_W, tq), lambda b, i: (b, 0, 0, 0)),
                  tile_spec(D_MODEL), tile_spec(D_MODEL), tile_spec(D_MODEL)] + lam_specs +
                 [_layer_spec(p['subln_g'], layer), _layer_spec(p['w_attn_out'], layer),
                  _layer_spec(p['w_o'], layer)],
        out_specs=tile_spec(D_MODEL),
        out_shape=jax.ShapeDtypeStruct((B, T, D_MODEL), F32),
        compiler_params=pltpu.CompilerParams(dimension_semantics=("parallel", "arbitrary"),
                                             vmem_limit_bytes=VMEM_LIMIT),
        name="attn_prompt",
    )(qt, k16, vt, mab, gc, x, p['lambda_q1'], p['lambda_k1'], p['lambda_q2'], p['lambda_k2'],
      p['subln_g'], p['w_attn_out'], p['w_o'])


def _attn_prompt2_kernel(qt_ref, k_ref, vt_ref, mab_ref, gc_ref, x_ref, lq1_ref, lk1_ref, lq2_ref,
                         lk2_ref, sg_ref, wc_ref, wo_ref, o_ref, p_scr, acc_scr, *, tq, tk,
                         lam_init):
    i = pl.program_id(1)
    n_diag = tq // tk
    lam = _lambda(lq1_ref, lk1_ref, lq2_ref, lk2_ref, lam_init)
    feat = lax.broadcasted_iota(jnp.int32, (HEAD_W, tq), 0)
    qs = []
    for hd in range(HEADS):
        qt = qt_ref[0, hd * HEAD_W:(hd + 1) * HEAD_W, :]
        zero = jnp.zeros_like(qt)
        qs.append(jnp.concatenate([jnp.where(feat < HEAD_DIM, qt, zero),
                                   jnp.where(feat < HEAD_DIM, zero, qt)], axis=1))

    def scores(j):
        off = pl.multiple_of(j * tk, tk)
        return [jnp.dot(k_ref[0, pl.ds(off, tk), hd * HEAD_W:(hd + 1) * HEAD_W], qs[hd],
                        preferred_element_type=F32) for hd in range(HEADS)]

    def softmax(s, m, l):
        m_out, l_out, alpha_out = [], [], []
        for hd in range(HEADS):
            m_new = jnp.maximum(m[hd], jnp.max(s[hd], axis=0, keepdims=True))
            pr = jnp.exp2(s[hd] - m_new)
            alpha = jnp.exp2(m[hd] - m_new)
            p_scr[hd] = pr.astype(BF16)
            m_out.append(m_new)
            l_out.append(alpha * l[hd] + jnp.sum(pr, axis=0, keepdims=True))
            alpha_out.append(alpha)
        return m_out, l_out, alpha_out

    def flush(j, alpha):
        for hd in range(HEADS):
            pv = jnp.dot(vt_ref[0, j, hd * HEAD_W:(hd + 1) * HEAD_W, :], p_scr[hd],
                         preferred_element_type=F32)
            acc_scr[hd] = alpha[hd] * acc_scr[hd] + pv

    acc_scr[...] = jnp.zeros_like(acc_scr)
    m = [jnp.full((1, 2 * tq), -jnp.inf, F32)] * HEADS
    l = [jnp.zeros((1, 2 * tq), F32)] * HEADS
    alpha = l
    q_chunk = (lax.broadcasted_iota(jnp.int32, (tk, 2 * tq), 1) % tq) // CHUNK
    k_row = lax.broadcasted_iota(jnp.int32, (tk, 2 * tq), 0)
    for d in range(n_diag):
        j = i * n_diag + d
        s = scores(j)
        if d > 0:
            flush(j - 1, alpha)
        visible = (k_row + d * tk) // CHUNK <= q_chunk
        m, l, alpha = softmax([jnp.where(visible, sh, -jnp.inf) for sh in s], m, l)

    def body(j, carry):
        pend, alpha, m, l = carry
        s = scores(j)
        flush(pend, alpha)
        m, l, alpha = softmax(s, m, l)
        return j, alpha, m, l

    pend, alpha, m, l = lax.fori_loop(0, i * n_diag, body,
                                      (i * n_diag + n_diag - 1, alpha, m, l))
    flush(pend, alpha)
    heads = []
    for hd in range(HEADS):
        ot = acc_scr[hd] / l[hd]
        ot = ot[:, :tq] - lam * ot[:, tq:]
        ot = ot * lax.rsqrt(jnp.mean(ot * ot, axis=0, keepdims=True) + SUBLN_EPS)
        heads.append(ot.T * sg_ref[...] * (1.0 - lam_init))
    o = jnp.concatenate(heads, axis=-1)
    o_ref[0] = _merge_out(o, mab_ref[0], gc_ref[0], x_ref[0], wc_ref, wo_ref)


def _attn_prompt2(qt, k16, vt, mab, gc, x, layer, p, *, tq, tk):
    B, T, _ = x.shape
    assert T % tq == 0 and tq % tk == 0 and tk % CHUNK == 0 and vt.shape[3] == tk
    lam_init = 0.8 - 0.6 * math.exp(-0.3 * layer)

    def tile_spec(w):
        return pl.BlockSpec((1, tq, w), lambda b, i: (b, i, 0))

    lam_specs = [_layer_spec(p[n], layer) for n in ('lambda_q1', 'lambda_k1', 'lambda_q2', 'lambda_k2')]
    return pl.pallas_call(
        functools.partial(_attn_prompt2_kernel, tq=tq, tk=tk, lam_init=lam_init),
        grid=(B, T // tq),
        in_specs=[pl.BlockSpec((1, ATT_W, tq), lambda b, i: (b, 0, i)),
                  pl.BlockSpec((1, T, ATT_W), lambda b, i: (b, 0, 0)),
                  pl.BlockSpec((1, T // tk, ATT_W, tk), lambda b, i: (b, 0, 0, 0)),
                  tile_spec(D_MODEL), tile_spec(D_MODEL), tile_spec(D_MODEL)] + lam_specs +
                 [_layer_spec(p['subln_g'], layer), _layer_spec(p['w_attn_out'], layer),
                  _layer_spec(p['w_o'], layer)],
        out_specs=tile_spec(D_MODEL),
        out_shape=jax.ShapeDtypeStruct((B, T, D_MODEL), F32),
        scratch_shapes=[pltpu.VMEM((HEADS, tk, 2 * tq), BF16),
                        pltpu.VMEM((HEADS, HEAD_W, 2 * tq), F32)],
        compiler_params=pltpu.CompilerParams(dimension_semantics=("parallel", "arbitrary"),
                                             vmem_limit_bytes=VMEM_LIMIT),
        name="attn_prompt",
    )(qt, k16, vt, mab, gc, x, p['lambda_q1'], p['lambda_k1'], p['lambda_q2'], p['lambda_k2'],
      p['subln_g'], p['w_attn_out'], p['w_o'])


def _stack_components(qh):
    lane = lax.broadcasted_iota(jnp.int32, qh.shape, 1)
    zero = jnp.zeros_like(qh)
    return jnp.concatenate([jnp.where(lane < HEAD_DIM, qh, zero),
                            jnp.where(lane < HEAD_DIM, zero, qh)], axis=0)


def _softmax_step(carry, s, vj):
    m, l, acc = carry
    m_new = jnp.maximum(m, jnp.max(s, axis=-1, keepdims=True))
    pr = jnp.exp2(s - m_new)
    alpha = jnp.exp2(m - m_new)
    l = alpha * l + jnp.sum(pr, axis=-1, keepdims=True)
    acc = alpha * acc + jnp.dot(pr.astype(BF16), vj, preferred_element_type=F32)
    return m_new, l, acc


def _attn_sample_kernel(q_ref, kp_ref, vp_ref, kn_ref, vn_ref, mab_ref, gc_ref, x_ref, lq1_ref,
                        lk1_ref, lq2_ref, lk2_ref, sg_ref, wc_ref, wo_ref, o_ref, *, tq, past,
                        lam_init):
    lam = _lambda(lq1_ref, lk1_ref, lq2_ref, lk2_ref, lam_init)
    q_chunk = (past + lax.broadcasted_iota(jnp.int32, (2 * tq, 1), 0) % tq) // CHUNK
    vis_past = (lax.broadcasted_iota(jnp.int32, (2 * tq, past), 1) // CHUNK) <= q_chunk
    vis_new = ((past + lax.broadcasted_iota(jnp.int32, (2 * tq, tq), 1)) // CHUNK) <= q_chunk
    heads = []
    for hd in range(HEADS):
        lo, hi = hd * HEAD_W, (hd + 1) * HEAD_W
        qs = _stack_components(q_ref[0, :, lo:hi])
        carry = (jnp.full((2 * tq, 1), -jnp.inf, F32), jnp.zeros((2 * tq, 1), F32),
                 jnp.zeros((2 * tq, HEAD_W), F32))
        s = lax.dot_general(qs, kn_ref[0, :, lo:hi], NT_DIMS, preferred_element_type=F32)
        carry = _softmax_step(carry, jnp.where(vis_new, s, -jnp.inf), vn_ref[0, :, lo:hi])
        s = lax.dot_general(qs, kp_ref[0, :, hd, :].astype(BF16), NT_DIMS,
                            preferred_element_type=F32)
        _, l, acc = _softmax_step(carry, jnp.where(vis_past, s, -jnp.inf),
                                  vp_ref[0, :, hd, :].astype(BF16))
        o = acc / l
        o = o[:tq] - lam * o[tq:]
        heads.append(_rms(o, sg_ref[...], SUBLN_EPS) * (1.0 - lam_init))
    o = jnp.concatenate(heads, axis=-1)
    o_ref[0] = _merge_out(o, mab_ref[0], gc_ref[0], x_ref[0], wc_ref, wo_ref)


def _attn_sample(q, cache_k, cache_v, k16, v16, mab, gc, x, layer, p):
    B, T, _ = x.shape
    past = cache_k.shape[2]
    lam_init = 0.8 - 0.6 * math.exp(-0.3 * layer)

    def tile_spec(w):
        return pl.BlockSpec((1, T, w), lambda b: (b, 0, 0))

    past_spec = pl.BlockSpec((None, 1, past, HEADS, HEAD_W), lambda b: (layer, b, 0, 0, 0))
    lam_specs = [_layer_spec(p[n], layer) for n in ('lambda_q1', 'lambda_k1', 'lambda_q2', 'lambda_k2')]
    return pl.pallas_call(
        functools.partial(_attn_sample_kernel, tq=T, past=past, lam_init=lam_init),
        grid=(B,),
        in_specs=[tile_spec(ATT_W), past_spec, past_spec, tile_spec(ATT_W), tile_spec(ATT_W),
                  tile_spec(D_MODEL), tile_spec(D_MODEL), tile_spec(D_MODEL)] + lam_specs +
                 [_layer_spec(p['subln_g'], layer), _layer_spec(p['w_attn_out'], layer),
                  _layer_spec(p['w_o'], layer)],
        out_specs=tile_spec(D_MODEL),
        out_shape=jax.ShapeDtypeStruct((B, T, D_MODEL), F32),
        compiler_params=pltpu.CompilerParams(dimension_semantics=("parallel",),
                                             vmem_limit_bytes=VMEM_LIMIT),
        name="attn_sample",
    )(q, cache_k, cache_v, k16, v16, mab, gc, x, p['lambda_q1'], p['lambda_k1'], p['lambda_q2'],
      p['lambda_k2'], p['subln_g'], p['w_attn_out'], p['w_o'])


def _chunks(n, step):
    return [(lo, min(lo + step, n)) for lo in range(0, n, step)]


def _ffn_dense_kernel(x_ref, ng_ref, wg_ref, wu_ref, wd_ref, nf_ref, o_ref, *, final):
    bb, tt, _ = x_ref.shape
    x = x_ref[...].reshape(bb * tt, D_MODEL)
    h = _rms(x, ng_ref[...], EPS).astype(BF16)
    y = x
    for lo, hi in _chunks(wg_ref.shape[1], 1024):
        g = jnp.dot(h, wg_ref[:, lo:hi], preferred_element_type=F32)
        u = jnp.dot(h, wu_ref[:, lo:hi], preferred_element_type=F32)
        a = (g * jax.nn.sigmoid(g) * u).astype(BF16)
        y = y + jnp.dot(a, wd_ref[lo:hi, :], preferred_element_type=F32)
    if final:
        y = _rms(y, nf_ref[...], EPS)
    o_ref[...] = y.reshape(bb, tt, D_MODEL)


def _token_blocks(B, T, tm):
    if T >= tm:
        assert T % tm == 0
        return 1, tm
    assert tm % T == 0 and B % (tm // T) == 0
    return tm // T, T


def _ffn_dense(x, layer, p, *, tm, final):
    B, T, _ = x.shape
    bb, tt = _token_blocks(B, T, tm)
    d = layer // 2
    row_spec = pl.BlockSpec((bb, tt, D_MODEL), lambda b, t: (b, t, 0))
    return pl.pallas_call(
        functools.partial(_ffn_dense_kernel, final=final),
        grid=(B // bb, T // tt),
        in_specs=[row_spec, _layer_spec(p['norm_ffn'], layer), _layer_spec(p['w_gate_d'], d),
                  _layer_spec(p['w_up_d'], d), _layer_spec(p['w_down_d'], d),
                  _const_spec(p['norm_final'].shape)],
        out_specs=row_spec,
        out_shape=jax.ShapeDtypeStruct(x.shape, F32),
        compiler_params=pltpu.CompilerParams(dimension_semantics=("parallel", "parallel"),
                                             vmem_limit_bytes=VMEM_LIMIT),
        name="ffn_dense",
    )(x, p['norm_ffn'], p['w_gate_d'], p['w_up_d'], p['w_down_d'], p['norm_final'])


def _route_top2(logits):
    row = lax.broadcasted_iota(jnp.int32, logits.shape, 0).astype(F32)
    big = float(N_EXPERTS)
    m1 = jnp.max(logits, axis=0, keepdims=True)
    i1 = jnp.min(jnp.where(logits == m1, row, big), axis=0, keepdims=True)
    rest = jnp.where(row == i1, -jnp.inf, logits)
    m2 = jnp.max(rest, axis=0, keepdims=True)
    i2 = jnp.min(jnp.where(rest == m2, row, big), axis=0, keepdims=True)
    e2 = jnp.exp(m2 - m1)
    w1 = 1.0 / (1.0 + e2)
    w2 = e2 / (1.0 + e2)
    return jnp.where(row == i1, w1, 0.0) + jnp.where(row == i2, w2, 0.0)


def _ffn_moe_kernel(x_ref, ng_ref, wrt_ref, brt_ref, wg_ref, wu_ref, wd_ref, nf_ref, o_ref,
                    h_scr, wts_scr, rank_scr, *, final, tb, cm):
    e = pl.program_id(2)
    bb, tt, _ = x_ref.shape
    n_tok = bb * tt
    sub_blocks = _chunks(n_tok, tb)

    @pl.when(e == 0)
    def _():
        x = x_ref[...].reshape(n_tok, D_MODEL)
        h = _rms(x, ng_ref[...], EPS)
        logits = lax.dot_general(wrt_ref[...], h, NT_DIMS, preferred_element_type=F32,
                                 precision=lax.Precision.HIGHEST) + brt_ref[...]
        wts = _route_top2(logits)
        sel = (wts > 0.0).astype(BF16)
        before = (lax.broadcasted_iota(jnp.int32, (tb, tb), 0)
                  < lax.broadcasted_iota(jnp.int32, (tb, tb), 1)).astype(BF16)
        rank = jnp.concatenate(
            [jnp.dot(sel[:, lo:hi], before, preferred_element_type=F32) for lo, hi in sub_blocks],
            axis=1)
        for ee in range(N_EXPERTS):
            wts_scr[ee] = wts[ee:ee + 1]
            rank_scr[ee] = rank[ee:ee + 1]
        h_scr[...] = h.astype(BF16)
        o_ref[...] = x_ref[...]

    w_all = wts_scr[e]
    r_all = rank_scr[e]
    for lo, hi in sub_blocks:
        w_row = w_all[:, lo:hi]
        r_row = r_all[:, lo:hi]
        sel = w_row > 0.0
        n_sel = jnp.sum(sel.astype(F32)).astype(jnp.int32)
        n_chunks = sum((n_sel > c * cm).astype(jnp.int32) for c in range(-(-tb // cm)))

        def chunk(c, carry, lo=lo, hi=hi, w_row=w_row, r_row=r_row, sel=sel):
            rows = (lax.broadcasted_iota(jnp.int32, (cm, tb), 0) + c * cm).astype(F32)
            hit = (r_row == rows) & sel
            gather = jnp.where(hit, 1.0, 0.0).astype(BF16)
            xg = jnp.dot(gather, h_scr[lo:hi, :], preferred_element_type=F32).astype(BF16)
            g = jnp.dot(xg, wg_ref[...], preferred_element_type=F32)
            u = jnp.dot(xg, wu_ref[...], preferred_element_type=F32)
            a = (g * jax.nn.sigmoid(g) * u).astype(BF16)
            y = jnp.dot(a, wd_ref[...], preferred_element_type=F32).astype(BF16)
            scatter = jnp.where(hit, w_row, 0.0).astype(BF16)
            upd = lax.dot_general(scatter, y, TN_DIMS, preferred_element_type=F32)
            if bb == 1:
                o_ref[0, lo:hi, :] += upd
            else:
                o_ref[lo // tt:hi // tt] += upd.reshape((hi - lo) // tt, tt, D_MODEL)
            return carry

        lax.fori_loop(0, n_chunks, chunk, 0)

    if final:
        @pl.when(e == N_EXPERTS - 1)
        def _():
            y = o_ref[...].reshape(n_tok, D_MODEL)
            o_ref[...] = _rms(y, nf_ref[...], EPS).reshape(bb, tt, D_MODEL)


def _ffn_moe(x, layer, p, *, tm, tb, cm, final):
    B, T, _ = x.shape
    bb, tt = _token_blocks(B, T, tm)
    assert tm % tb == 0 and (bb == 1 or tb % tt == 0)
    mo = layer // 2
    row_spec = pl.BlockSpec((bb, tt, D_MODEL), lambda b, t, e: (b, t, 0))

    def expert_spec(arr):
        return pl.BlockSpec((None, None) + tuple(arr.shape[2:]), lambda b, t, e: (mo, e, 0, 0))

    return pl.pallas_call(
        functools.partial(_ffn_moe_kernel, final=final, tb=tb, cm=cm),
        grid=(B // bb, T // tt, N_EXPERTS),
        in_specs=[pl.BlockSpec((bb, tt, D_MODEL), lambda b, t, e: (b, t, 0),
                               pipeline_mode=pl.Buffered(1)),
                  _layer_spec(p['norm_ffn'], layer), _layer_spec(p['w_router_t'], mo),
                  _layer_spec(p['b_router_t'], mo), expert_spec(p['w_gate_e']),
                  expert_spec(p['w_up_e']), expert_spec(p['w_down_e']),
                  _const_spec(p['norm_final'].shape)],
        out_specs=row_spec,
        out_shape=jax.ShapeDtypeStruct(x.shape, F32),
        scratch_shapes=[pltpu.VMEM((tm, D_MODEL), BF16), pltpu.VMEM((N_EXPERTS, 1, tm), F32),
                        pltpu.VMEM((N_EXPERTS, 1, tm), F32)],
        compiler_params=pltpu.CompilerParams(
            dimension_semantics=("parallel", "parallel", "arbitrary"),
            vmem_limit_bytes=VMEM_LIMIT),
        name="ffn_moe",
    )(x, p['norm_ffn'], p['w_router_t'], p['b_router_t'], p['w_gate_e'], p['w_up_e'],
      p['w_down_e'], p['norm_final'])


def _rope_tables(pos0, T, bb, tt):
    half = HEAD_DIM // 2
    inv_freq = ROPE_THETA ** (-jnp.arange(half, dtype=F32) / half)
    ang = (pos0 + jnp.arange(T)).astype(F32)[:, None] * inv_freq[None, :]
    cos, sin = jnp.cos(ang), jnp.sin(ang)
    cos = jnp.concatenate([cos, cos, cos, cos], axis=-1)
    sin = jnp.concatenate([-sin, sin, -sin, sin], axis=-1)

    def lay(a):
        a = a.reshape(T // tt, 1, tt, HEAD_W)
        return jnp.broadcast_to(a, (T // tt, bb, tt, HEAD_W)).reshape(T // tt, bb * tt, HEAD_W)

    return lay(cos), lay(sin)


def _pad_hist(state):
    return jnp.pad(state, ((0, 0), (0, 0), (HIST_PAD - state.shape[2], 0), (0, 0)))


_MATMUL_WEIGHTS = ('w_in', 'w_conv_out', 'w_pool', 'w_attn_out', 'w_o', 'w_gate_d', 'w_up_d',
                   'w_down_d', 'w_gate_e', 'w_up_e', 'w_down_e')
_ROW_VECTORS = ('norm_mix', 'pool_scale', 'lambda_q1', 'lambda_k1', 'lambda_q2', 'lambda_k2',
                'subln_g', 'norm_ffn')


def _params(w):
    p = dict(w)
    for n in _MATMUL_WEIGHTS:
        p[n] = w[n].astype(BF16)
    for n in _ROW_VECTORS:
        p[n] = w[n][:, None, :]
    p['w_router_t'] = jnp.swapaxes(w['w_router'], 1, 2)
    p['b_router_t'] = w['b_router'][:, :, None]
    p['norm_final'] = w['norm_final'][None, :]
    return p


def _token_mixer_prompt(x, layer, p, kv_bufs, *, tt=512, tq=512, tk=256):
    B, T, _ = x.shape
    cos, sin = _rope_tables(0, T, 1, tt)
    zero_hist = jnp.zeros((B, HIST_PAD, CONV_W), F32)
    mab, gc, qt, k16, vt, k32, v32, co, po = _mixer_in(
        x, zero_hist, zero_hist, cos, sin, layer, p, kv_bufs, bb=1, tt=tt, pos0=0, tk=tk)
    x = _attn_prompt2(qt, k16, vt, mab, gc, x, layer, p, tq=tq, tk=tk)
    return x, co[:, HIST_PAD - (CONV_K - 1):], po[:, HIST_PAD - POOL_HIST:], (k32, v32)


def _token_mixer_sample(x, cache_k, cache_v, state_conv, state_pool, layer, p, kv_bufs, *, bb):
    past = cache_k.shape[2]
    T = x.shape[1]
    cos, sin = _rope_tables(past, T, bb, T)
    mab, gc, q, k16, v16, k32, v32, co, po = _mixer_in(
        x, _pad_hist(state_conv), _pad_hist(state_pool), cos, sin, layer, p, kv_bufs,
        bb=bb, tt=T, pos0=past, tk=None)
    x = _attn_sample(q, cache_k, cache_v, k16, v16, mab, gc, x, layer, p)
    return x, co[:, HIST_PAD - (CONV_K - 1):], po[:, HIST_PAD - POOL_HIST:], (k32, v32)


def _channel_mixer(x, layer, p, *, final):
    if layer % 2 == 0:
        return _ffn_dense(x, layer, p, tm=512, final=final)
    tokens = x.shape[0] * x.shape[1]
    return _ffn_moe(x, layer, p, tm=min(2048, tokens), tb=512, cm=160, final=final)


def kernel(x_prompt, x_sample, cache_k, cache_v, state_conv, state_pool, norm_mix, w_in, conv_w,
           w_conv_out, w_pool, pool_scale, lambda_q1, lambda_k1, lambda_q2, lambda_k2, subln_g,
           w_attn_out, w_o, norm_ffn, w_gate_d, w_up_d, w_down_d, w_router, b_router, w_gate_e,
           w_up_e, w_down_e, norm_final):
    depth = w_in.shape[0]
    p = _params(dict(
        norm_mix=norm_mix, w_in=w_in, conv_w=conv_w, w_conv_out=w_conv_out, w_pool=w_pool,
        pool_scale=pool_scale, lambda_q1=lambda_q1, lambda_k1=lambda_k1, lambda_q2=lambda_q2,
        lambda_k2=lambda_k2, subln_g=subln_g, w_attn_out=w_attn_out, w_o=w_o, norm_ffn=norm_ffn,
        w_gate_d=w_gate_d, w_up_d=w_up_d, w_down_d=w_down_d, w_router=w_router,
        b_router=b_router, w_gate_e=w_gate_e, w_up_e=w_up_e, w_down_e=w_down_e,
        norm_final=norm_final))

    xp, xs = x_prompt, x_sample
    kv_p = kv_s = None
    states = [[] for _ in range(4)]
    for l in range(depth):
        final = l == depth - 1
        xp, cp, pp, kv_p = _token_mixer_prompt(xp, l, p, kv_p)
        xs, cs, ps, kv_s = _token_mixer_sample(xs, cache_k, cache_v, state_conv, state_pool, l, p,
                                               kv_s, bb=16)
        for lst, a in zip(states, (cp, pp, cs, ps)):
            lst.append(a)
        xp = _channel_mixer(xp, l, p, final=final)
        xs = _channel_mixer(xs, l, p, final=final)

    return (xp, xs, kv_p[0], kv_p[1], jnp.stack(states[0]), jnp.stack(states[1]),
            kv_s[0], kv_s[1], jnp.stack(states[2]), jnp.stack(states[3]))
```

```python
import functools
import math

import jax
import jax.numpy as jnp
from jax import lax
from jax.experimental import pallas as pl
from jax.experimental.pallas import tpu as pltpu

D_MODEL = 1024
CHUNK = 64
CONV_W = 512
CONV_K = 3
POOL_W = 512
POOL_GC = 128
POOL_WINDOWS = (2, 4, 8, 16)
POOL_HIST = 15
HEADS = 4
HEAD_DIM = 64
HEAD_W = 2 * HEAD_DIM
ATT_W = HEADS * HEAD_W
ROPE_THETA = 10000.0
N_EXPERTS = 8
EPS = 1e-6
SUBLN_EPS = 1e-5

C_XA, C_BA, C_CA, C_U, C_Q, C_K, C_V, C_GA, C_GB, C_GC, C_END = (
    0, 512, 1024, 1536, 2048, 2560, 3072, 3584, 4608, 5632, 6656)

HIST_PAD = 16
VMEM_LIMIT = 56 * 1024 * 1024
Q_SCALE = HEAD_DIM ** -0.5 * math.log2(math.e)
DENOM_ROWS = 16

F32 = jnp.float32
BF16 = jnp.bfloat16
NT_DIMS = (((1,), (1,)), ((), ()))
TN_DIMS = (((0,), (0,)), ((), ()))


def _const_spec(shape):
    nd = len(shape)
    return pl.BlockSpec(tuple(shape), lambda *_: (0,) * nd, pipeline_mode=pl.Buffered(1))


def _layer_spec(arr, layer):
    nd = arr.ndim
    return pl.BlockSpec((None,) + tuple(arr.shape[1:]), lambda *_: (layer,) + (0,) * (nd - 1),
                        pipeline_mode=pl.Buffered(1))


def _rms(x, g, eps):
    return x * lax.rsqrt(jnp.mean(x * x, axis=-1, keepdims=True) + eps) * g


def _mixer_in_kernel(*refs, bb, tt, pos0, tk, aliased):
    (x_ref, hc_ref, hp_ref, cos_ref, sin_ref, ng_ref, win_ref, cw_ref, wa_ref, wp_ref,
     ps_ref) = refs[:11]
    refs = refs[11 + (2 if aliased else 0):]
    (mab_ref, gc_ref, q_ref, k16_ref, v16_ref, k32_ref, v32_ref, co_ref, po_ref, cbuf,
     ubuf) = refs
    t = pl.program_id(1)
    m = bb * tt
    x = x_ref[...].reshape(m, D_MODEL)
    h = _rms(x, ng_ref[...], EPS).astype(BF16)

    def seg(lo, hi):
        return jnp.dot(h, win_ref[:, lo:hi], preferred_element_type=F32)

    @pl.when(t == 0)
    def _():
        cbuf[:, 0:HIST_PAD, :] = hc_ref[...]
        ubuf[:, 0:HIST_PAD, :] = hp_ref[...]

    cin = (seg(C_CA, C_U) * seg(C_XA, C_BA)).reshape(bb, tt, CONV_W)
    cbuf[:, HIST_PAD:, :] = cin
    conv = cbuf[:, HIST_PAD - 2:HIST_PAD - 2 + tt, :] * cw_ref[0:1, :]
    conv = conv + cbuf[:, HIST_PAD - 1:HIST_PAD - 1 + tt, :] * cw_ref[1:2, :]
    conv = conv + cin * cw_ref[2:3, :]
    ya = jnp.dot((seg(C_BA, C_CA) * conv.reshape(m, CONV_W)).astype(BF16), wa_ref[...],
                 preferred_element_type=F32)

    ubuf[:, HIST_PAD:, :] = seg(C_U, C_Q).reshape(bb, tt, POOL_W)
    pos = pos0 + t * tt + lax.broadcasted_iota(jnp.int32, (bb, tt, POOL_GC), 1)
    yb_parts = []
    for gi, win in enumerate(POOL_WINDOWS):
        lo, hi = gi * POOL_GC, (gi + 1) * POOL_GC
        cur = ubuf[:, HIST_PAD:HIST_PAD + tt, lo:hi]
        s = cur
        for k in range(1, win):
            s = s + ubuf[:, HIST_PAD - k:HIST_PAD - k + tt, lo:hi]
        cnt = jnp.minimum(pos + 1, win).astype(F32)
        pooled = s / cnt - cur
        yb_parts.append(jnp.dot(pooled.reshape(m, POOL_GC).astype(BF16), wp_ref[gi],
                                preferred_element_type=F32))
    yb = jnp.concatenate(yb_parts, axis=-1) * ps_ref[...]

    cos4 = jnp.concatenate([cos_ref[...]] * HEADS, axis=-1)
    sin4 = jnp.concatenate([sin_ref[...]] * HEADS, axis=-1)
    lane = lax.broadcasted_iota(jnp.int32, (m, ATT_W), 1)
    first_half = (lane & (HEAD_DIM // 2)) == 0

    def rope(z):
        swapped = jnp.where(first_half, pltpu.roll(z, ATT_W - HEAD_DIM // 2, axis=1),
                            pltpu.roll(z, HEAD_DIM // 2, axis=1))
        return z * cos4 + swapped * sin4

    q = rope(seg(C_Q, C_K)) * Q_SCALE
    k = rope(seg(C_K, C_V))
    v = seg(C_V, C_GA)
    for hd in range(HEADS):
        lo, hi = hd * HEAD_W, (hd + 1) * HEAD_W
        k32_ref[:, :, hd, :] = k[:, lo:hi].reshape(bb, tt, HEAD_W)
        v32_ref[:, :, hd, :] = v[:, lo:hi].reshape(bb, tt, HEAD_W)
    k16_ref[...] = k.astype(BF16).reshape(bb, tt, ATT_W)
    if tk is None:
        q_ref[...] = q.astype(BF16).reshape(bb, tt, ATT_W)
        v16_ref[...] = v.astype(BF16).reshape(bb, tt, ATT_W)
    else:
        q_ref[0] = q.T.astype(BF16)
        for c in range(tt // tk):
            v16_ref[0, c] = v[c * tk:(c + 1) * tk, :].T.astype(BF16)

    ga = jax.nn.sigmoid(seg(C_GA, C_GB))
    gb = jax.nn.sigmoid(seg(C_GB, C_GC))
    mab_ref[...] = (ga * ya + gb * yb).astype(BF16).reshape(bb, tt, D_MODEL)
    gc_ref[...] = jax.nn.sigmoid(seg(C_GC, C_END)).astype(BF16).reshape(bb, tt, D_MODEL)

    last_c = cbuf[:, tt:tt + HIST_PAD, :]
    last_p = ubuf[:, tt:tt + HIST_PAD, :]
    co_ref[...] = last_c
    po_ref[...] = last_p
    cbuf[:, 0:HIST_PAD, :] = last_c
    ubuf[:, 0:HIST_PAD, :] = last_p


def _mixer_in(x, hist_c, hist_p, cos, sin, layer, p, kv_bufs, *, bb, tt, pos0, tk):
    B, T, _ = x.shape
    depth = p['w_in'].shape[0]
    assert B % bb == 0 and T % tt == 0 and tt >= HIST_PAD and tt % 8 == 0
    assert tk is None or (bb == 1 and tt % tk == 0)
    grid = (B // bb, T // tt)
    m = bb * tt

    def seq_spec(w):
        return pl.BlockSpec((bb, tt, w), lambda b, t: (b, t, 0))

    if hist_c.ndim == 4:
        hist_spec = pl.BlockSpec((None, bb, HIST_PAD, CONV_W), lambda b, t: (layer, b, 0, 0))
    else:
        hist_spec = pl.BlockSpec((bb, HIST_PAD, CONV_W), lambda b, t: (b, 0, 0))
    state_spec = pl.BlockSpec((bb, HIST_PAD, CONV_W), lambda b, t: (b, 0, 0))
    rope_spec = pl.BlockSpec((None, m, HEAD_W), lambda b, t: (t, 0, 0))
    kv_spec = pl.BlockSpec((None, bb, tt, HEADS, HEAD_W), lambda b, t: (layer, b, t, 0, 0))
    kv_shape = jax.ShapeDtypeStruct((depth, B, T, HEADS, HEAD_W), F32)
    if tk is None:
        q_shape, q_spec = jax.ShapeDtypeStruct((B, T, ATT_W), BF16), seq_spec(ATT_W)
        v_shape, v_spec = q_shape, q_spec
    else:
        q_shape = jax.ShapeDtypeStruct((B, ATT_W, T), BF16)
        q_spec = pl.BlockSpec((1, ATT_W, tt), lambda b, t: (b, 0, t))
        v_shape = jax.ShapeDtypeStruct((B, T // tk, ATT_W, tk), BF16)
        v_spec = pl.BlockSpec((1, tt // tk, ATT_W, tk), lambda b, t: (b, t, 0, 0))

    out_shape = (
        jax.ShapeDtypeStruct((B, T, D_MODEL), BF16),
        jax.ShapeDtypeStruct((B, T, D_MODEL), BF16),
        q_shape,
        jax.ShapeDtypeStruct((B, T, ATT_W), BF16),
        v_shape,
        kv_shape, kv_shape,
        jax.ShapeDtypeStruct((B, HIST_PAD, CONV_W), F32),
        jax.ShapeDtypeStruct((B, HIST_PAD, POOL_W), F32),
    )
    out_specs = (seq_spec(D_MODEL), seq_spec(D_MODEL), q_spec, seq_spec(ATT_W), v_spec,
                 kv_spec, kv_spec, state_spec, state_spec)
    in_specs = [seq_spec(D_MODEL), hist_spec, hist_spec, rope_spec, rope_spec,
                _layer_spec(p['norm_mix'], layer), _layer_spec(p['w_in'], layer),
                _layer_spec(p['conv_w'], layer), _layer_spec(p['w_conv_out'], layer),
                _layer_spec(p['w_pool'], layer), _layer_spec(p['pool_scale'], layer)]
    args = [x, hist_c, hist_p, cos, sin, p['norm_mix'], p['w_in'], p['conv_w'], p['w_conv_out'],
            p['w_pool'], p['pool_scale']]
    aliases = {}
    if kv_bufs is not None:
        in_specs += [pl.BlockSpec(memory_space=pl.ANY)] * 2
        aliases = {len(args): 5, len(args) + 1: 6}
        args += list(kv_bufs)
    return pl.pallas_call(
        functools.partial(_mixer_in_kernel, bb=bb, tt=tt, pos0=pos0, tk=tk,
                          aliased=kv_bufs is not None),
        grid=grid, in_specs=in_specs, out_specs=out_specs, out_shape=out_shape,
        scratch_shapes=[pltpu.VMEM((bb, HIST_PAD + tt, CONV_W), F32),
                        pltpu.VMEM((bb, HIST_PAD + tt, POOL_W), F32)],
        input_output_aliases=aliases,
        compiler_params=pltpu.CompilerParams(dimension_semantics=("parallel", "arbitrary"),
                                             vmem_limit_bytes=VMEM_LIMIT),
        name="mixer_in",
    )(*args)


def _lambda(lq1_ref, lk1_ref, lq2_ref, lk2_ref, lam_init):
    a = jnp.exp(jnp.sum(lq1_ref[...] * lk1_ref[...], axis=-1, keepdims=True))
    b = jnp.exp(jnp.sum(lq2_ref[...] * lk2_ref[...], axis=-1, keepdims=True))
    return a - b + lam_init


def _merge_out(o, mab, gc, x, wc_ref, wo_ref):
    yc = jnp.dot(o.astype(BF16), wc_ref[...], preferred_element_type=F32)
    merged = mab.astype(F32) + gc.astype(F32) * yc
    return x + jnp.dot(merged.astype(BF16), wo_ref[...], preferred_element_type=F32)


def _attn_prompt_kernel(qt_ref, k_ref, vt_ref, mab_ref, gc_ref, x_ref, lq1_ref, lk1_ref, lq2_ref,
                        lk2_ref, sg_ref, wc_ref, wo_ref, o_ref, *, tq, lam_init):
    i = pl.program_id(1)
    lam = _lambda(lq1_ref, lk1_ref, lq2_ref, lk2_ref, lam_init)
    k_chunk = lax.broadcasted_iota(jnp.int32, (tq, 2 * tq), 0) // CHUNK
    q_chunk = (lax.broadcasted_iota(jnp.int32, (tq, 2 * tq), 1) % tq) // CHUNK
    visible = k_chunk <= q_chunk
    feat = lax.broadcasted_iota(jnp.int32, (HEAD_W, tq), 0)
    qs = []
    for hd in range(HEADS):
        qt = qt_ref[0, hd * HEAD_W:(hd + 1) * HEAD_W, :]
        zero = jnp.zeros_like(qt)
        qs.append(jnp.concatenate([jnp.where(feat < HEAD_DIM, qt, zero),
                                   jnp.where(feat < HEAD_DIM, zero, qt)], axis=1))

    def scores(j):
        off = pl.multiple_of(j * tq, tq)
        return [jnp.dot(k_ref[0, pl.ds(off, tq), hd * HEAD_W:(hd + 1) * HEAD_W], qs[hd],
                        preferred_element_type=F32) for hd in range(HEADS)]

    def softmax(s, m, l):
        m_new = jnp.maximum(m, jnp.max(s, axis=0, keepdims=True))
        pr = jnp.exp2(s - m_new)
        alpha = jnp.exp2(m - m_new)
        return m_new, alpha * l + jnp.sum(pr, axis=0, keepdims=True), alpha, pr.astype(BF16)

    def weighted_values(j, acc, alpha, pr):
        return [alpha[hd] * acc[hd] + jnp.dot(vt_ref[0, j, hd * HEAD_W:(hd + 1) * HEAD_W, :],
                                              pr[hd], preferred_element_type=F32)
                for hd in range(HEADS)]

    def unzip(rows):
        return tuple(list(col) for col in zip(*rows))

    m, l, alpha, pr = unzip([
        softmax(jnp.where(visible, s, -jnp.inf), jnp.full((1, 2 * tq), -jnp.inf, F32),
                jnp.zeros((1, 2 * tq), F32)) for s in scores(i)])
    acc = [jnp.zeros((HEAD_W, 2 * tq), F32) for _ in range(HEADS)]

    def body(j, carry):
        s_cur, pend, alpha, pr, m, l, acc = carry
        acc = weighted_values(pend, acc, alpha, pr)
        s_next = scores(j + 1)
        m, l, alpha, pr = unzip([softmax(s_cur[hd], m[hd], l[hd]) for hd in range(HEADS)])
        return s_next, j, alpha, pr, m, l, acc

    _, pend, alpha, pr, m, l, acc = lax.fori_loop(
        0, i, body, (scores(0), i, alpha, pr, m, l, acc))
    acc = weighted_values(pend, acc, alpha, pr)
    heads = []
    for hd in range(HEADS):
        ot = acc[hd] / l[hd]
        ot = ot[:, :tq] - lam * ot[:, tq:]
        ot = ot * lax.rsqrt(jnp.mean(ot * ot, axis=0, keepdims=True) + SUBLN_EPS)
        heads.append(ot.T * sg_ref[...] * (1.0 - lam_init))
    o = jnp.concatenate(heads, axis=-1)
    o_ref[0] = _merge_out(o, mab_ref[0], gc_ref[0], x_ref[0], wc_ref, wo_ref)


def _attn_prompt(qt, k16, vt, mab, gc, x, layer, p, *, tq):
    B, T, _ = x.shape
    assert T % tq == 0 and tq % CHUNK == 0 and vt.shape[3] == tq
    lam_init = 0.8 - 0.6 * math.exp(-0.3 * layer)

    def tile_spec(w):
        return pl.BlockSpec((1, tq, w), lambda b, i: (b, i, 0))

    lam_specs = [_layer_spec(p[n], layer) for n in ('lambda_q1', 'lambda_k1', 'lambda_q2', 'lambda_k2')]
    return pl.pallas_call(
        functools.partial(_attn_prompt_kernel, tq=tq, lam_init=lam_init),
        grid=(B, T // tq),
        in_specs=[pl.BlockSpec((1, ATT_W, tq), lambda b, i: (b, 0, i)),
                  pl.BlockSpec((1, T, ATT_W), lambda b, i: (b, 0, 0)),
                  pl.BlockSpec((1, T // tq, ATT_W, tq), lambda b, i: (b, 0, 0, 0)),
                  tile_spec(D_MODEL), tile_spec(D_MODEL), tile_spec(D_MODEL)] + lam_specs +
                 [_layer_spec(p['subln_g'], layer), _layer_spec(p['w_attn_out'], layer),
                  _layer_spec(p['w_o'], layer)],
        out_specs=tile_spec(D_MODEL),
        out_shape=jax.ShapeDtypeStruct((B, T, D_MODEL), F32),
        compiler_params=pltpu.CompilerParams(dimension_semantics=("parallel", "arbitrary"),
                                             vmem_limit_bytes=VMEM_LIMIT),
        name="attn_prompt",
    )(qt, k16, vt, mab, gc, x, p['lambda_q1'], p['lambda_k1'], p['lambda_q2'], p['lambda_k2'],
      p['subln_g'], p['w_attn_out'], p['w_o'])


def _attn_prompt2_kernel(qt_ref, k_ref, vt_ref, mab_ref, gc_ref, x_ref, lq1_ref, lk1_ref, lq2_ref,
                         lk2_ref, sg_ref, wc_ref, wo_ref, o_ref, p_scr, acc_scr, *, tq, tk,
                         lam_init):
    i = pl.program_id(1)
    n_diag = tq // tk
    lam = _lambda(lq1_ref, lk1_ref, lq2_ref, lk2_ref, lam_init)
    feat = lax.broadcasted_iota(jnp.int32, (HEAD_W, tq), 0)
    qs = []
    for hd in range(HEADS):
        qt = qt_ref[0, hd * HEAD_W:(hd + 1) * HEAD_W, :]
        zero = jnp.zeros_like(qt)
        qs.append(jnp.concatenate([jnp.where(feat < HEAD_DIM, qt, zero),
                                   jnp.where(feat < HEAD_DIM, zero, qt)], axis=1))

    def scores(j):
        off = pl.multiple_of(j * tk, tk)
        return [jnp.dot(k_ref[0, pl.ds(off, tk), hd * HEAD_W:(hd + 1) * HEAD_W], qs[hd],
                        preferred_element_type=F32) for hd in range(HEADS)]

    def softmax(s, m):
        m_out, alpha_out = [], []
        for hd in range(HEADS):
            m_new = jnp.maximum(m[hd], jnp.max(s[hd], axis=0, keepdims=True))
            p_scr[hd] = jnp.exp2(s[hd] - m_new).astype(BF16)
            m_out.append(m_new)
            alpha_out.append(jnp.exp2(m[hd] - m_new))
        return m_out, alpha_out

    ones_rows = jnp.ones((DENOM_ROWS, tk), BF16)

    def flush(j, alpha):
        for hd in range(HEADS):
            vt1 = jnp.concatenate([vt_ref[0, j, hd * HEAD_W:(hd + 1) * HEAD_W, :], ones_rows],
                                  axis=0)
            acc_scr[hd] = alpha[hd] * acc_scr[hd] + jnp.dot(vt1, p_scr[hd],
                                                            preferred_element_type=F32)

    acc_scr[...] = jnp.zeros_like(acc_scr)
    m = [jnp.full((1, 2 * tq), -jnp.inf, F32)] * HEADS
    alpha = [jnp.zeros((1, 2 * tq), F32)] * HEADS
    q_chunk = (lax.broadcasted_iota(jnp.int32, (tk, 2 * tq), 1) % tq) // CHUNK
    k_row = lax.broadcasted_iota(jnp.int32, (tk, 2 * tq), 0)
    for d in range(n_diag):
        j = i * n_diag + d
        s = scores(j)
        if d > 0:
            flush(j - 1, alpha)
        visible = (k_row + d * tk) // CHUNK <= q_chunk
        m, alpha = softmax([jnp.where(visible, sh, -jnp.inf) for sh in s], m)

    def body(j, carry):
        pend, alpha, m = carry
        s = scores(j)
        flush(pend, alpha)
        m, alpha = softmax(s, m)
        return j, alpha, m

    pend, alpha, m = lax.fori_loop(0, i * n_diag, body, (i * n_diag + n_diag - 1, alpha, m))
    flush(pend, alpha)
    heads = []
    for hd in range(HEADS):
        ot = acc_scr[hd, 0:HEAD_W, :] / acc_scr[hd, HEAD_W:HEAD_W + 1, :]
        ot = ot[:, :tq] - lam * ot[:, tq:]
        ot = ot * lax.rsqrt(jnp.mean(ot * ot, axis=0, keepdims=True) + SUBLN_EPS)
        heads.append(ot.T * sg_ref[...] * (1.0 - lam_init))
    o = jnp.concatenate(heads, axis=-1)
    o_ref[0] = _merge_out(o, mab_ref[0], gc_ref[0], x_ref[0], wc_ref, wo_ref)


def _attn_prompt2(qt, k16, vt, mab, gc, x, layer, p, *, tq, tk):
    B, T, _ = x.shape
    assert T % tq == 0 and tq % tk == 0 and tk % CHUNK == 0 and vt.shape[3] == tk
    lam_init = 0.8 - 0.6 * math.exp(-0.3 * layer)

    def tile_spec(w):
        return pl.BlockSpec((1, tq, w), lambda b, i: (b, i, 0))

    lam_specs = [_layer_spec(p[n], layer) for n in ('lambda_q1', 'lambda_k1', 'lambda_q2', 'lambda_k2')]
    return pl.pallas_call(
        functools.partial(_attn_prompt2_kernel, tq=tq, tk=tk, lam_init=lam_init),
        grid=(B, T // tq),
        in_specs=[pl.BlockSpec((1, ATT_W, tq), lambda b, i: (b, 0, i)),
                  pl.BlockSpec((1, T, ATT_W), lambda b, i: (b, 0, 0)),
                  pl.BlockSpec((1, T // tk, ATT_W, tk), lambda b, i: (b, 0, 0, 0)),
                  tile_spec(D_MODEL), tile_spec(D_MODEL), tile_spec(D_MODEL)] + lam_specs +
                 [_layer_spec(p['subln_g'], layer), _layer_spec(p['w_attn_out'], layer),
                  _layer_spec(p['w_o'], layer)],
        out_specs=tile_spec(D_MODEL),
        out_shape=jax.ShapeDtypeStruct((B, T, D_MODEL), F32),
        scratch_shapes=[pltpu.VMEM((HEADS, tk, 2 * tq), BF16),
                        pltpu.VMEM((HEADS, HEAD_W + DENOM_ROWS, 2 * tq), F32)],
        compiler_params=pltpu.CompilerParams(dimension_semantics=("parallel", "arbitrary"),
                                             vmem_limit_bytes=VMEM_LIMIT),
        name="attn_prompt",
    )(qt, k16, vt, mab, gc, x, p['lambda_q1'], p['lambda_k1'], p['lambda_q2'], p['lambda_k2'],
      p['subln_g'], p['w_attn_out'], p['w_o'])


def _stack_components(qh):
    lane = lax.broadcasted_iota(jnp.int32, qh.shape, 1)
    zero = jnp.zeros_like(qh)
    return jnp.concatenate([jnp.where(lane < HEAD_DIM, qh, zero),
                            jnp.where(lane < HEAD_DIM, zero, qh)], axis=0)


def _softmax_step(carry, s, vj):
    m, l, acc = carry
    m_new = jnp.maximum(m, jnp.max(s, axis=-1, keepdims=True))
    pr = jnp.exp2(s - m_new)
    alpha = jnp.exp2(m - m_new)
    l = alpha * l + jnp.sum(pr, axis=-1, keepdims=True)
    acc = alpha * acc + jnp.dot(pr.astype(BF16), vj, preferred_element_type=F32)
    return m_new, l, acc


def _attn_sample_kernel(q_ref, kp_ref, vp_ref, kn_ref, vn_ref, mab_ref, gc_ref, x_ref, lq1_ref,
                        lk1_ref, lq2_ref, lk2_ref, sg_ref, wc_ref, wo_ref, o_ref, *, tq, past,
                        lam_init):
    lam = _lambda(lq1_ref, lk1_ref, lq2_ref, lk2_ref, lam_init)
    q_chunk = (past + lax.broadcasted_iota(jnp.int32, (2 * tq, 1), 0) % tq) // CHUNK
    vis_past = (lax.broadcasted_iota(jnp.int32, (2 * tq, past), 1) // CHUNK) <= q_chunk
    vis_new = ((past + lax.broadcasted_iota(jnp.int32, (2 * tq, tq), 1)) // CHUNK) <= q_chunk
    heads = []
    for hd in range(HEADS):
        lo, hi = hd * HEAD_W, (hd + 1) * HEAD_W
        qs = _stack_components(q_ref[0, :, lo:hi])
        carry = (jnp.full((2 * tq, 1), -jnp.inf, F32), jnp.zeros((2 * tq, 1), F32),
                 jnp.zeros((2 * tq, HEAD_W), F32))
        s = lax.dot_general(qs, kn_ref[0, :, lo:hi], NT_DIMS, preferred_element_type=F32)
        carry = _softmax_step(carry, jnp.where(vis_new, s, -jnp.inf), vn_ref[0, :, lo:hi])
        s = lax.dot_general(qs, kp_ref[0, :, hd, :].astype(BF16), NT_DIMS,
                            preferred_element_type=F32)
        _, l, acc = _softmax_step(carry, jnp.where(vis_past, s, -jnp.inf),
                                  vp_ref[0, :, hd, :].astype(BF16))
        o = acc / l
        o = o[:tq] - lam * o[tq:]
        heads.append(_rms(o, sg_ref[...], SUBLN_EPS) * (1.0 - lam_init))
    o = jnp.concatenate(heads, axis=-1)
    o_ref[0] = _merge_out(o, mab_ref[0], gc_ref[0], x_ref[0], wc_ref, wo_ref)


def _attn_sample(q, cache_k, cache_v, k16, v16, mab, gc, x, layer, p):
    B, T, _ = x.shape
    past = cache_k.shape[2]
    lam_init = 0.8 - 0.6 * math.exp(-0.3 * layer)

    def tile_spec(w):
        return pl.BlockSpec((1, T, w), lambda b: (b, 0, 0))

    past_spec = pl.BlockSpec((None, 1, past, HEADS, HEAD_W), lambda b: (layer, b, 0, 0, 0))
    lam_specs = [_layer_spec(p[n], layer) for n in ('lambda_q1', 'lambda_k1', 'lambda_q2', 'lambda_k2')]
    return pl.pallas_call(
        functools.partial(_attn_sample_kernel, tq=T, past=past, lam_init=lam_init),
        grid=(B,),
        in_specs=[tile_spec(ATT_W), past_spec, past_spec, tile_spec(ATT_W), tile_spec(ATT_W),
                  tile_spec(D_MODEL), tile_spec(D_MODEL), tile_spec(D_MODEL)] + lam_specs +
                 [_layer_spec(p['subln_g'], layer), _layer_spec(p['w_attn_out'], layer),
                  _layer_spec(p['w_o'], layer)],
        out_specs=tile_spec(D_MODEL),
        out_shape=jax.ShapeDtypeStruct((B, T, D_MODEL), F32),
        compiler_params=pltpu.CompilerParams(dimension_semantics=("parallel",),
                                             vmem_limit_bytes=VMEM_LIMIT),
        name="attn_sample",
    )(q, cache_k, cache_v, k16, v16, mab, gc, x, p['lambda_q1'], p['lambda_k1'], p['lambda_q2'],
      p['lambda_k2'], p['subln_g'], p['w_attn_out'], p['w_o'])


def _chunks(n, step):
    return [(lo, min(lo + step, n)) for lo in range(0, n, step)]


def _ffn_dense_kernel(x_ref, ng_ref, wg_ref, wu_ref, wd_ref, nf_ref, o_ref, *, final):
    bb, tt, _ = x_ref.shape
    x = x_ref[...].reshape(bb * tt, D_MODEL)
    h = _rms(x, ng_ref[...], EPS).astype(BF16)
    y = x
    for lo, hi in _chunks(wg_ref.shape[1], 1024):
        g = jnp.dot(h, wg_ref[:, lo:hi], preferred_element_type=F32)
        u = jnp.dot(h, wu_ref[:, lo:hi], preferred_element_type=F32)
        a = (g * jax.nn.sigmoid(g) * u).astype(BF16)
        y = y + jnp.dot(a, wd_ref[lo:hi, :], preferred_element_type=F32)
    if final:
        y = _rms(y, nf_ref[...], EPS)
    o_ref[...] = y.reshape(bb, tt, D_MODEL)


def _token_blocks(B, T, tm):
    if T >= tm:
        assert T % tm == 0
        return 1, tm
    assert tm % T == 0 and B % (tm // T) == 0
    return tm // T, T


def _ffn_dense(x, layer, p, *, tm, final):
    B, T, _ = x.shape
    bb, tt = _token_blocks(B, T, tm)
    d = layer // 2
    row_spec = pl.BlockSpec((bb, tt, D_MODEL), lambda b, t: (b, t, 0))
    return pl.pallas_call(
        functools.partial(_ffn_dense_kernel, final=final),
        grid=(B // bb, T // tt),
        in_specs=[row_spec, _layer_spec(p['norm_ffn'], layer), _layer_spec(p['w_gate_d'], d),
                  _layer_spec(p['w_up_d'], d), _layer_spec(p['w_down_d'], d),
                  _const_spec(p['norm_final'].shape)],
        out_specs=row_spec,
        out_shape=jax.ShapeDtypeStruct(x.shape, F32),
        compiler_params=pltpu.CompilerParams(dimension_semantics=("parallel", "parallel"),
                                             vmem_limit_bytes=VMEM_LIMIT),
        name="ffn_dense",
    )(x, p['norm_ffn'], p['w_gate_d'], p['w_up_d'], p['w_down_d'], p['norm_final'])


def _route_top2(logits):
    row = lax.broadcasted_iota(jnp.int32, logits.shape, 0).astype(F32)
    big = float(N_EXPERTS)
    m1 = jnp.max(logits, axis=0, keepdims=True)
    i1 = jnp.min(jnp.where(logits == m1, row, big), axis=0, keepdims=True)
    rest = jnp.where(row == i1, -jnp.inf, logits)
    m2 = jnp.max(rest, axis=0, keepdims=True)
    i2 = jnp.min(jnp.where(rest == m2, row, big), axis=0, keepdims=True)
    e2 = jnp.exp(m2 - m1)
    w1 = 1.0 / (1.0 + e2)
    w2 = e2 / (1.0 + e2)
    return jnp.where(row == i1, w1, 0.0) + jnp.where(row == i2, w2, 0.0)


def _ffn_moe_kernel(x_ref, ng_ref, wrt_ref, brt_ref, wg_ref, wu_ref, wd_ref, nf_ref, o_ref,
                    h_scr, wts_scr, rank_scr, *, final, tb, cms):
    e = pl.program_id(2)
    bb, tt, _ = x_ref.shape
    n_tok = bb * tt
    sub_blocks = _chunks(n_tok, tb)

    @pl.when(e == 0)
    def _():
        x = x_ref[...].reshape(n_tok, D_MODEL)
        h = _rms(x, ng_ref[...], EPS)
        logits = lax.dot_general(wrt_ref[...], h, NT_DIMS, preferred_element_type=F32,
                                 precision=lax.Precision.HIGHEST) + brt_ref[...]
        wts = _route_top2(logits)
        sel = (wts > 0.0).astype(BF16)
        before = (lax.broadcasted_iota(jnp.int32, (tb, tb), 0)
                  < lax.broadcasted_iota(jnp.int32, (tb, tb), 1)).astype(BF16)
        rank = jnp.concatenate(
            [jnp.dot(sel[:, lo:hi], before, preferred_element_type=F32) for lo, hi in sub_blocks],
            axis=1)
        for ee in range(N_EXPERTS):
            for sb, (lo, hi) in enumerate(sub_blocks):
                wts_scr[ee, sb] = wts[ee:ee + 1, lo:hi]
                rank_scr[ee, sb] = rank[ee:ee + 1, lo:hi]
        h_scr[...] = h.astype(BF16)
        o_ref[...] = x_ref[...]

    cm_max = cms[-1]

    def sub_block(sb, carry):
        w_row = wts_scr[e, sb]
        r_row = rank_scr[e, sb]
        sel = w_row > 0.0
        n_sel = jnp.sum(sel.astype(F32)).astype(jnp.int32)
        row0 = pl.multiple_of(sb * tb, tb)

        def run_chunk(cm, base):
            rows = (lax.broadcasted_iota(jnp.int32, (cm, tb), 0) + base).astype(F32)
            hit = (r_row == rows) & sel
            gather = jnp.where(hit, 1.0, 0.0).astype(BF16)
            xg = jnp.dot(gather, h_scr[pl.ds(row0, tb), :],
                         preferred_element_type=F32).astype(BF16)
            g = jnp.dot(xg, wg_ref[...], preferred_element_type=F32)
            u = jnp.dot(xg, wu_ref[...], preferred_element_type=F32)
            a = (g * jax.nn.sigmoid(g) * u).astype(BF16)
            y = jnp.dot(a, wd_ref[...], preferred_element_type=F32).astype(BF16)
            scatter = jnp.where(hit, w_row, 0.0).astype(BF16)
            upd = lax.dot_general(scatter, y, TN_DIMS, preferred_element_type=F32)
            if bb == 1:
                o_ref[0, pl.ds(row0, tb), :] += upd
            else:
                o_ref[pl.ds(sb * (tb // tt), tb // tt)] += upd.reshape(tb // tt, tt, D_MODEL)

        def chunk(c, carry):
            left = n_sel - c * cm_max
            size_idx = sum((left > cm).astype(jnp.int32) for cm in cms[:-1])
            lax.switch(size_idx, [functools.partial(run_chunk, cm) for cm in cms], c * cm_max)
            return carry

        n_chunks = sum((n_sel > c * cm_max).astype(jnp.int32) for c in range(-(-tb // cm_max)))
        lax.fori_loop(0, n_chunks, chunk, 0)
        return carry

    lax.fori_loop(0, n_tok // tb, sub_block, 0)

    if final:
        @pl.when(e == N_EXPERTS - 1)
        def _():
            y = o_ref[...].reshape(n_tok, D_MODEL)
            o_ref[...] = _rms(y, nf_ref[...], EPS).reshape(bb, tt, D_MODEL)


def _ffn_moe(x, layer, p, *, tm, tb, cms, final):
    B, T, _ = x.shape
    bb, tt = _token_blocks(B, T, tm)
    assert tm % tb == 0 and (bb == 1 or tb % tt == 0)
    mo = layer // 2
    row_spec = pl.BlockSpec((bb, tt, D_MODEL), lambda b, t, e: (b, t, 0))

    def expert_spec(arr):
        return pl.BlockSpec((None, None) + tuple(arr.shape[2:]), lambda b, t, e: (mo, e, 0, 0))

    return pl.pallas_call(
        functools.partial(_ffn_moe_kernel, final=final, tb=tb, cms=cms),
        grid=(B // bb, T // tt, N_EXPERTS),
        in_specs=[pl.BlockSpec((bb, tt, D_MODEL), lambda b, t, e: (b, t, 0),
                               pipeline_mode=pl.Buffered(1)),
                  _layer_spec(p['norm_ffn'], layer), _layer_spec(p['w_router_t'], mo),
                  _layer_spec(p['b_router_t'], mo), expert_spec(p['w_gate_e']),
                  expert_spec(p['w_up_e']), expert_spec(p['w_down_e']),
                  _const_spec(p['norm_final'].shape)],
        out_specs=row_spec,
        out_shape=jax.ShapeDtypeStruct(x.shape, F32),
        scratch_shapes=[pltpu.VMEM((tm, D_MODEL), BF16),
                        pltpu.VMEM((N_EXPERTS, tm // tb, 1, tb), F32),
                        pltpu.VMEM((N_EXPERTS, tm // tb, 1, tb), F32)],
        compiler_params=pltpu.CompilerParams(
            dimension_semantics=("parallel", "parallel", "arbitrary"),
            vmem_limit_bytes=VMEM_LIMIT),
        name="ffn_moe",
    )(x, p['norm_ffn'], p['w_router_t'], p['b_router_t'], p['w_gate_e'], p['w_up_e'],
      p['w_down_e'], p['norm_final'])


def _rope_tables(pos0, T, bb, tt):
    half = HEAD_DIM // 2
    inv_freq = ROPE_THETA ** (-jnp.arange(half, dtype=F32) / half)
    ang = (pos0 + jnp.arange(T)).astype(F32)[:, None] * inv_freq[None, :]
    cos, sin = jnp.cos(ang), jnp.sin(ang)
    cos = jnp.concatenate([cos, cos, cos, cos], axis=-1)
    sin = jnp.concatenate([-sin, sin, -sin, sin], axis=-1)

    def lay(a):
        a = a.reshape(T // tt, 1, tt, HEAD_W)
        return jnp.broadcast_to(a, (T // tt, bb, tt, HEAD_W)).reshape(T // tt, bb * tt, HEAD_W)

    return lay(cos), lay(sin)


def _pad_hist(state):
    return jnp.pad(state, ((0, 0), (0, 0), (HIST_PAD - state.shape[2], 0), (0, 0)))


_MATMUL_WEIGHTS = ('w_in', 'w_conv_out', 'w_pool', 'w_attn_out', 'w_o', 'w_gate_d', 'w_up_d',
                   'w_down_d', 'w_gate_e', 'w_up_e', 'w_down_e')
_ROW_VECTORS = ('norm_mix', 'pool_scale', 'lambda_q1', 'lambda_k1', 'lambda_q2', 'lambda_k2',
                'subln_g', 'norm_ffn')


def _params(w):
    p = dict(w)
    for n in _MATMUL_WEIGHTS:
        p[n] = w[n].astype(BF16)
    for n in _ROW_VECTORS:
        p[n] = w[n][:, None, :]
    p['w_router_t'] = jnp.swapaxes(w['w_router'], 1, 2)
    p['b_router_t'] = w['b_router'][:, :, None]
    p['norm_final'] = w['norm_final'][None, :]
    return p


def _token_mixer_prompt(x, layer, p, kv_bufs, *, tt=512, tq=512, tk=256):
    B, T, _ = x.shape
    cos, sin = _rope_tables(0, T, 1, tt)
    zero_hist = jnp.zeros((B, HIST_PAD, CONV_W), F32)
    mab, gc, qt, k16, vt, k32, v32, co, po = _mixer_in(
        x, zero_hist, zero_hist, cos, sin, layer, p, kv_bufs, bb=1, tt=tt, pos0=0, tk=tk)
    x = _attn_prompt2(qt, k16, vt, mab, gc, x, layer, p, tq=tq, tk=tk)
    return x, co[:, HIST_PAD - (CONV_K - 1):], po[:, HIST_PAD - POOL_HIST:], (k32, v32)


def _token_mixer_sample(x, cache_k, cache_v, state_conv, state_pool, layer, p, kv_bufs, *, bb):
    past = cache_k.shape[2]
    T = x.shape[1]
    cos, sin = _rope_tables(past, T, bb, T)
    mab, gc, q, k16, v16, k32, v32, co, po = _mixer_in(
        x, _pad_hist(state_conv), _pad_hist(state_pool), cos, sin, layer, p, kv_bufs,
        bb=bb, tt=T, pos0=past, tk=None)
    x = _attn_sample(q, cache_k, cache_v, k16, v16, mab, gc, x, layer, p)
    return x, co[:, HIST_PAD - (CONV_K - 1):], po[:, HIST_PAD - POOL_HIST:], (k32, v32)


def _channel_mixer(x, layer, p, *, final):
    if layer % 2 == 0:
        return _ffn_dense(x, layer, p, tm=512, final=final)
    tokens = x.shape[0] * x.shape[1]
    return _ffn_moe(x, layer, p, tm=min(2048, tokens), tb=512, cms=(128, 160, 192, 224, 256),
                    final=final)


def kernel(x_prompt, x_sample, cache_k, cache_v, state_conv, state_pool, norm_mix, w_in, conv_w,
           w_conv_out, w_pool, pool_scale, lambda_q1, lambda_k1, lambda_q2, lambda_k2, subln_g,
           w_attn_out, w_o, norm_ffn, w_gate_d, w_up_d, w_down_d, w_router, b_router, w_gate_e,
           w_up_e, w_down_e, norm_final):
    depth = w_in.shape[0]
    p = _params(dict(
        norm_mix=norm_mix, w_in=w_in, conv_w=conv_w, w_conv_out=w_conv_out, w_pool=w_pool,
        pool_scale=pool_scale, lambda_q1=lambda_q1, lambda_k1=lambda_k1, lambda_q2=lambda_q2,
        lambda_k2=lambda_k2, subln_g=subln_g, w_attn_out=w_attn_out, w_o=w_o, norm_ffn=norm_ffn,
        w_gate_d=w_gate_d, w_up_d=w_up_d, w_down_d=w_down_d, w_router=w_router,
        b_router=b_router, w_gate_e=w_gate_e, w_up_e=w_up_e, w_down_e=w_down_e,
        norm_final=norm_final))

    xp, xs = x_prompt, x_sample
    kv_p = kv_s = None
    states = [[] for _ in range(4)]
    for l in range(depth):
        final = l == depth - 1
        xp, cp, pp, kv_p = _token_mixer_prompt(xp, l, p, kv_p)
        xs, cs, ps, kv_s = _token_mixer_sample(xs, cache_k, cache_v, state_conv, state_pool, l, p,
                                               kv_s, bb=16)
        for lst, a in zip(states, (cp, pp, cs, ps)):
            lst.append(a)
        xp = _channel_mixer(xp, l, p, final=final)
        xs = _channel_mixer(xs, l, p, final=final)

    return (xp, xs, kv_p[0], kv_p[1], jnp.stack(states[0]), jnp.stack(states[1]),
            kv_s[0], kv_s[1], jnp.stack(states[2]), jnp.stack(states[3]))
```

```python
import functools
import math

import jax
import jax.numpy as jnp
from jax import lax
from jax.experimental import pallas as pl
from jax.experimental.pallas import tpu as pltpu

D_MODEL = 1024
CHUNK = 64
CONV_W = 512
CONV_K = 3
POOL_W = 512
POOL_GC = 128
POOL_WINDOWS = (2, 4, 8, 16)
POOL_HIST = 15
HEADS = 4
HEAD_DIM = 64
HEAD_W = 2 * HEAD_DIM
ATT_W = HEADS * HEAD_W
ROPE_THETA = 10000.0
N_EXPERTS = 8
EPS = 1e-6
SUBLN_EPS = 1e-5

C_XA, C_BA, C_CA, C_U, C_Q, C_K, C_V, C_GA, C_GB, C_GC, C_END = (
    0, 512, 1024, 1536, 2048, 2560, 3072, 3584, 4608, 5632, 6656)

HIST_PAD = 16
VMEM_LIMIT = 56 * 1024 * 1024
Q_SCALE = HEAD_DIM ** -0.5 * math.log2(math.e)
DENOM_ROWS = 16

F32 = jnp.float32
BF16 = jnp.bfloat16
NT_DIMS = (((1,), (1,)), ((), ()))
TN_DIMS = (((0,), (0,)), ((), ()))


def _const_spec(shape):
    nd = len(shape)
    return pl.BlockSpec(tuple(shape), lambda *_: (0,) * nd, pipeline_mode=pl.Buffered(1))


def _layer_spec(arr, layer):
    nd = arr.ndim
    return pl.BlockSpec((None,) + tuple(arr.shape[1:]), lambda *_: (layer,) + (0,) * (nd - 1),
                        pipeline_mode=pl.Buffered(1))


def _rms(x, g, eps):
    return x * lax.rsqrt(jnp.mean(x * x, axis=-1, keepdims=True) + eps) * g


def _mixer_in_kernel(*refs, bb, tt, pos0, tk, aliased):
    (x_ref, hc_ref, hp_ref, cos_ref, sin_ref, ng_ref, win_ref, cw_ref, wa_ref, wp_ref,
     ps_ref) = refs[:11]
    refs = refs[11 + (2 if aliased else 0):]
    (mab_ref, gc_ref, q_ref, k16_ref, v16_ref, k32_ref, v32_ref, co_ref, po_ref, cbuf,
     ubuf) = refs
    t = pl.program_id(1)
    m = bb * tt
    x = x_ref[...].reshape(m, D_MODEL)
    h = _rms(x, ng_ref[...], EPS).astype(BF16)

    def seg(lo, hi):
        return jnp.dot(h, win_ref[:, lo:hi], preferred_element_type=F32)

    @pl.when(t == 0)
    def _():
        cbuf[:, 0:HIST_PAD, :] = hc_ref[...]
        ubuf[:, 0:HIST_PAD, :] = hp_ref[...]

    z_ca, z_xa, z_ba, z_u = seg(C_CA, C_U), seg(C_XA, C_BA), seg(C_BA, C_CA), seg(C_U, C_Q)
    z_q, z_k, z_v = seg(C_Q, C_K), seg(C_K, C_V), seg(C_V, C_GA)
    cin = (z_ca * z_xa).reshape(bb, tt, CONV_W)
    cbuf[:, HIST_PAD:, :] = cin
    conv = cbuf[:, HIST_PAD - 2:HIST_PAD - 2 + tt, :] * cw_ref[0:1, :]
    conv = conv + cbuf[:, HIST_PAD - 1:HIST_PAD - 1 + tt, :] * cw_ref[1:2, :]
    conv = conv + cin * cw_ref[2:3, :]
    ya = jnp.dot((z_ba * conv.reshape(m, CONV_W)).astype(BF16), wa_ref[...],
                 preferred_element_type=F32)

    ubuf[:, HIST_PAD:, :] = z_u.reshape(bb, tt, POOL_W)
    pos = pos0 + t * tt + lax.broadcasted_iota(jnp.int32, (bb, tt, POOL_GC), 1)
    yb_parts = []
    for gi, win in enumerate(POOL_WINDOWS):
        lo, hi = gi * POOL_GC, (gi + 1) * POOL_GC
        cur = ubuf[:, HIST_PAD:HIST_PAD + tt, lo:hi]
        s = cur
        for k in range(1, win):
            s = s + ubuf[:, HIST_PAD - k:HIST_PAD - k + tt, lo:hi]
        cnt = jnp.minimum(pos + 1, win).astype(F32)
        pooled = s / cnt - cur
        yb_parts.append(jnp.dot(pooled.reshape(m, POOL_GC).astype(BF16), wp_ref[gi],
                                preferred_element_type=F32))
    yb = jnp.concatenate(yb_parts, axis=-1) * ps_ref[...]

    cos4 = jnp.concatenate([cos_ref[...]] * HEADS, axis=-1)
    sin4 = jnp.concatenate([sin_ref[...]] * HEADS, axis=-1)
    lane = lax.broadcasted_iota(jnp.int32, (m, ATT_W), 1)
    first_half = (lane & (HEAD_DIM // 2)) == 0

    def rope(z):
        swapped = jnp.where(first_half, pltpu.roll(z, ATT_W - HEAD_DIM // 2, axis=1),
                            pltpu.roll(z, HEAD_DIM // 2, axis=1))
        return z * cos4 + swapped * sin4

    q = rope(z_q) * Q_SCALE
    k = rope(z_k)
    v = z_v
    for hd in range(HEADS):
        lo, hi = hd * HEAD_W, (hd + 1) * HEAD_W
        k32_ref[:, pl.ds(hd, tt, stride=HEADS), :] = k[:, lo:hi].reshape(bb, tt, HEAD_W)
        v32_ref[:, pl.ds(hd, tt, stride=HEADS), :] = v[:, lo:hi].reshape(bb, tt, HEAD_W)
    k16_ref[...] = k.astype(BF16).reshape(bb, tt, ATT_W)
    if tk is None:
        q_ref[...] = q.astype(BF16).reshape(bb, tt, ATT_W)
        v16_ref[...] = v.astype(BF16).reshape(bb, tt, ATT_W)
    else:
        q_ref[0] = q.T.astype(BF16)
        for c in range(tt // tk):
            v16_ref[0, c] = v[c * tk:(c + 1) * tk, :].T.astype(BF16)

    ga = jax.nn.sigmoid(seg(C_GA, C_GB))
    gb = jax.nn.sigmoid(seg(C_GB, C_GC))
    mab_ref[...] = (ga * ya + gb * yb).astype(BF16).reshape(bb, tt, D_MODEL)
    gc_ref[...] = jax.nn.sigmoid(seg(C_GC, C_END)).astype(BF16).reshape(bb, tt, D_MODEL)

    last_c = cbuf[:, tt:tt + HIST_PAD, :]
    last_p = ubuf[:, tt:tt + HIST_PAD, :]
    co_ref[...] = last_c
    po_ref[...] = last_p
    cbuf[:, 0:HIST_PAD, :] = last_c
    ubuf[:, 0:HIST_PAD, :] = last_p


def _mixer_in(x, hist_c, hist_p, cos, sin, layer, p, kv_bufs, *, bb, tt, pos0, tk):
    B, T, _ = x.shape
    depth = p['w_in'].shape[0]
    assert B % bb == 0 and T % tt == 0 and tt >= HIST_PAD and tt % 8 == 0
    assert tk is None or (bb == 1 and tt % tk == 0)
    grid = (B // bb, T // tt)
    m = bb * tt

    def seq_spec(w):
        return pl.BlockSpec((bb, tt, w), lambda b, t: (b, t, 0))

    if hist_c.ndim == 4:
        hist_spec = pl.BlockSpec((None, bb, HIST_PAD, CONV_W), lambda b, t: (layer, b, 0, 0))
    else:
        hist_spec = pl.BlockSpec((bb, HIST_PAD, CONV_W), lambda b, t: (b, 0, 0))
    state_spec = pl.BlockSpec((bb, HIST_PAD, CONV_W), lambda b, t: (b, 0, 0))
    rope_spec = pl.BlockSpec((None, m, HEAD_W), lambda b, t: (t, 0, 0))
    kv_spec = pl.BlockSpec((None, bb, tt * HEADS, HEAD_W), lambda b, t: (layer, b, t, 0))
    kv_shape = jax.ShapeDtypeStruct((depth, B, T * HEADS, HEAD_W), F32)
    if tk is None:
        q_shape, q_spec = jax.ShapeDtypeStruct((B, T, ATT_W), BF16), seq_spec(ATT_W)
        v_shape, v_spec = q_shape, q_spec
    else:
        q_shape = jax.ShapeDtypeStruct((B, ATT_W, T), BF16)
        q_spec = pl.BlockSpec((1, ATT_W, tt), lambda b, t: (b, 0, t))
        v_shape = jax.ShapeDtypeStruct((B, T // tk, ATT_W, tk), BF16)
        v_spec = pl.BlockSpec((1, tt // tk, ATT_W, tk), lambda b, t: (b, t, 0, 0))

    out_shape = (
        jax.ShapeDtypeStruct((B, T, D_MODEL), BF16),
        jax.ShapeDtypeStruct((B, T, D_MODEL), BF16),
        q_shape,
        jax.ShapeDtypeStruct((B, T, ATT_W), BF16),
        v_shape,
        kv_shape, kv_shape,
        jax.ShapeDtypeStruct((B, HIST_PAD, CONV_W), F32),
        jax.ShapeDtypeStruct((B, HIST_PAD, POOL_W), F32),
    )
    out_specs = (seq_spec(D_MODEL), seq_spec(D_MODEL), q_spec, seq_spec(ATT_W), v_spec,
                 kv_spec, kv_spec, state_spec, state_spec)
    in_specs = [seq_spec(D_MODEL), hist_spec, hist_spec, rope_spec, rope_spec,
                _layer_spec(p['norm_mix'], layer), _layer_spec(p['w_in'], layer),
                _layer_spec(p['conv_w'], layer), _layer_spec(p['w_conv_out'], layer),
                _layer_spec(p['w_pool'], layer), _layer_spec(p['pool_scale'], layer)]
    args = [x, hist_c, hist_p, cos, sin, p['norm_mix'], p['w_in'], p['conv_w'], p['w_conv_out'],
            p['w_pool'], p['pool_scale']]
    aliases = {}
    if kv_bufs is not None:
        in_specs += [pl.BlockSpec(memory_space=pl.ANY)] * 2
        aliases = {len(args): 5, len(args) + 1: 6}
        args += list(kv_bufs)
    return pl.pallas_call(
        functools.partial(_mixer_in_kernel, bb=bb, tt=tt, pos0=pos0, tk=tk,
                          aliased=kv_bufs is not None),
        grid=grid, in_specs=in_specs, out_specs=out_specs, out_shape=out_shape,
        scratch_shapes=[pltpu.VMEM((bb, HIST_PAD + tt, CONV_W), F32),
                        pltpu.VMEM((bb, HIST_PAD + tt, POOL_W), F32)],
        input_output_aliases=aliases,
        compiler_params=pltpu.CompilerParams(dimension_semantics=("parallel", "arbitrary"),
                                             vmem_limit_bytes=VMEM_LIMIT),
        name="mixer_in",
    )(*args)


def _lambda(lq1_ref, lk1_ref, lq2_ref, lk2_ref, lam_init):
    a = jnp.exp(jnp.sum(lq1_ref[...] * lk1_ref[...], axis=-1, keepdims=True))
    b = jnp.exp(jnp.sum(lq2_ref[...] * lk2_ref[...], axis=-1, keepdims=True))
    return a - b + lam_init


def _merge_out(o, mab, gc, x, wc_ref, wo_ref):
    yc = jnp.dot(o.astype(BF16), wc_ref[...], preferred_element_type=F32)
    merged = mab.astype(F32) + gc.astype(F32) * yc
    return x + jnp.dot(merged.astype(BF16), wo_ref[...], preferred_element_type=F32)


def _attn_prompt_kernel(qt_ref, k_ref, vt_ref, mab_ref, gc_ref, x_ref, lq1_ref, lk1_ref, lq2_ref,
                        lk2_ref, sg_ref, wc_ref, wo_ref, o_ref, *, tq, lam_init):
    i = pl.program_id(1)
    lam = _lambda(lq1_ref, lk1_ref, lq2_ref, lk2_ref, lam_init)
    k_chunk = lax.broadcasted_iota(jnp.int32, (tq, 2 * tq), 0) // CHUNK
    q_chunk = (lax.broadcasted_iota(jnp.int32, (tq, 2 * tq), 1) % tq) // CHUNK
    visible = k_chunk <= q_chunk
    feat = lax.broadcasted_iota(jnp.int32, (HEAD_W, tq), 0)
    qs = []
    for hd in range(HEADS):
        qt = qt_ref[0, hd * HEAD_W:(hd + 1) * HEAD_W, :]
        zero = jnp.zeros_like(qt)
        qs.append(jnp.concatenate([jnp.where(feat < HEAD_DIM, qt, zero),
                                   jnp.where(feat < HEAD_DIM, zero, qt)], axis=1))

    def scores(j):
        off = pl.multiple_of(j * tq, tq)
        return [jnp.dot(k_ref[0, pl.ds(off, tq), hd * HEAD_W:(hd + 1) * HEAD_W], qs[hd],
                        preferred_element_type=F32) for hd in range(HEADS)]

    def softmax(s, m, l):
        m_new = jnp.maximum(m, jnp.max(s, axis=0, keepdims=True))
        pr = jnp.exp2(s - m_new)
        alpha = jnp.exp2(m - m_new)
        return m_new, alpha * l + jnp.sum(pr, axis=0, keepdims=True), alpha, pr.astype(BF16)

    def weighted_values(j, acc, alpha, pr):
        return [alpha[hd] * acc[hd] + jnp.dot(vt_ref[0, j, hd * HEAD_W:(hd + 1) * HEAD_W, :],
                                              pr[hd], preferred_element_type=F32)
                for hd in range(HEADS)]

    def unzip(rows):
        return tuple(list(col) for col in zip(*rows))

    m, l, alpha, pr = unzip([
        softmax(jnp.where(visible, s, -jnp.inf), jnp.full((1, 2 * tq), -jnp.inf, F32),
                jnp.zeros((1, 2 * tq), F32)) for s in scores(i)])
    acc = [jnp.zeros((HEAD_W, 2 * tq), F32) for _ in range(HEADS)]

    def body(j, carry):
        s_cur, pend, alpha, pr, m, l, acc = carry
        acc = weighted_values(pend, acc, alpha, pr)
        s_next = scores(j + 1)
        m, l, alpha, pr = unzip([softmax(s_cur[hd], m[hd], l[hd]) for hd in range(HEADS)])
        return s_next, j, alpha, pr, m, l, acc

    _, pend, alpha, pr, m, l, acc = lax.fori_loop(
        0, i, body, (scores(0), i, alpha, pr, m, l, acc))
    acc = weighted_values(pend, acc, alpha, pr)
    heads = []
    for hd in range(HEADS):
        ot = acc[hd] / l[hd]
        ot = ot[:, :tq] - lam * ot[:, tq:]
        ot = ot * lax.rsqrt(jnp.mean(ot * ot, axis=0, keepdims=True) + SUBLN_EPS)
        heads.append(ot.T * sg_ref[...] * (1.0 - lam_init))
    o = jnp.concatenate(heads, axis=-1)
    o_ref[0] = _merge_out(o, mab_ref[0], gc_ref[0], x_ref[0], wc_ref, wo_ref)


def _attn_prompt(qt, k16, vt, mab, gc, x, layer, p, *, tq):
    B, T, _ = x.shape
    assert T % tq == 0 and tq % CHUNK == 0 and vt.shape[3] == tq
    lam_init = 0.8 - 0.6 * math.exp(-0.3 * layer)

    def tile_spec(w):
        return pl.BlockSpec((1, tq, w), lambda b, i: (b, i, 0))

    lam_specs = [_layer_spec(p[n], layer) for n in ('lambda_q1', 'lambda_k1', 'lambda_q2', 'lambda_k2')]
    return pl.pallas_call(
        functools.partial(_attn_prompt_kernel, tq=tq, lam_init=lam_init),
        grid=(B, T // tq),
        in_specs=[pl.BlockSpec((1, ATT_W, tq), lambda b, i: (b, 0, i)),
                  pl.BlockSpec((1, T, ATT_W), lambda b, i: (b, 0, 0)),
                  pl.BlockSpec((1, T // tq, ATT_W, tq), lambda b, i: (b, 0, 0, 0)),
                  tile_spec(D_MODEL), tile_spec(D_MODEL), tile_spec(D_MODEL)] + lam_specs +
                 [_layer_spec(p['subln_g'], layer), _layer_spec(p['w_attn_out'], layer),
                  _layer_spec(p['w_o'], layer)],
        out_specs=tile_spec(D_MODEL),
        out_shape=jax.ShapeDtypeStruct((B, T, D_MODEL), F32),
        compiler_params=pltpu.CompilerParams(dimension_semantics=("parallel", "arbitrary"),
                                             vmem_limit_bytes=VMEM_LIMIT),
        name="attn_prompt",
    )(qt, k16, vt, mab, gc, x, p['lambda_q1'], p['lambda_k1'], p['lambda_q2'], p['lambda_k2'],
      p['subln_g'], p['w_attn_out'], p['w_o'])


def _attn_prompt2_kernel(qt_ref, k_ref, vt_ref, mab_ref, gc_ref, x_ref, lq1_ref, lk1_ref, lq2_ref,
                         lk2_ref, sg_ref, wc_ref, wo_ref, o_ref, p_scr, acc_scr, *, tq, tk,
                         lam_init):
    i = pl.program_id(1)
    n_diag = tq // tk
    lam = _lambda(lq1_ref, lk1_ref, lq2_ref, lk2_ref, lam_init)
    feat = lax.broadcasted_iota(jnp.int32, (HEAD_W, tq), 0)
    qs = []
    for hd in range(HEADS):
        qt = qt_ref[0, hd * HEAD_W:(hd + 1) * HEAD_W, :]
        zero = jnp.zeros_like(qt)
        qs.append(jnp.concatenate([jnp.where(feat < HEAD_DIM, qt, zero),
                                   jnp.where(feat < HEAD_DIM, zero, qt)], axis=1))

    def scores(j):
        off = pl.multiple_of(j * tk, tk)
        return [jnp.dot(k_ref[0, pl.ds(off, tk), hd * HEAD_W:(hd + 1) * HEAD_W], qs[hd],
                        preferred_element_type=F32) for hd in range(HEADS)]

    def softmax(s, m):
        m_out, alpha_out = [], []
        for hd in range(HEADS):
            m_new = jnp.maximum(m[hd], jnp.max(s[hd], axis=0, keepdims=True))
            p_scr[hd] = jnp.exp2(s[hd] - m_new).astype(BF16)
            m_out.append(m_new)
            alpha_out.append(jnp.exp2(m[hd] - m_new))
        return m_out, alpha_out

    ones_rows = jnp.ones((DENOM_ROWS, tk), BF16)

    def flush(j, alpha):
        for hd in range(HEADS):
            vt1 = jnp.concatenate([vt_ref[0, j, hd * HEAD_W:(hd + 1) * HEAD_W, :], ones_rows],
                                  axis=0)
            acc_scr[hd] = alpha[hd] * acc_scr[hd] + jnp.dot(vt1, p_scr[hd],
                                                            preferred_element_type=F32)

    acc_scr[...] = jnp.zeros_like(acc_scr)
    m = [jnp.full((1, 2 * tq), -jnp.inf, F32)] * HEADS
    alpha = [jnp.zeros((1, 2 * tq), F32)] * HEADS
    q_chunk = (lax.broadcasted_iota(jnp.int32, (tk, 2 * tq), 1) % tq) // CHUNK
    k_row = lax.broadcasted_iota(jnp.int32, (tk, 2 * tq), 0)
    for d in range(n_diag):
        j = i * n_diag + d
        s = scores(j)
        if d > 0:
            flush(j - 1, alpha)
        visible = (k_row + d * tk) // CHUNK <= q_chunk
        m, alpha = softmax([jnp.where(visible, sh, -jnp.inf) for sh in s], m)

    def body(j, carry):
        pend, alpha, m = carry
        s = scores(j)
        flush(pend, alpha)
        m, alpha = softmax(s, m)
        return j, alpha, m

    pend, alpha, m = lax.fori_loop(0, i * n_diag, body, (i * n_diag + n_diag - 1, alpha, m))
    flush(pend, alpha)
    heads = []
    for hd in range(HEADS):
        ot = acc_scr[hd, 0:HEAD_W, :] / acc_scr[hd, HEAD_W:HEAD_W + 1, :]
        ot = ot[:, :tq] - lam * ot[:, tq:]
        ot = ot * lax.rsqrt(jnp.mean(ot * ot, axis=0, keepdims=True) + SUBLN_EPS)
        heads.append(ot.T * sg_ref[...] * (1.0 - lam_init))
    o = jnp.concatenate(heads, axis=-1)
    o_ref[0] = _merge_out(o, mab_ref[0], gc_ref[0], x_ref[0], wc_ref, wo_ref)


def _attn_prompt2(qt, k16, vt, mab, gc, x, layer, p, *, tq, tk):
    B, T, _ = x.shape
    assert T % tq == 0 and tq % tk == 0 and tk % CHUNK == 0 and vt.shape[3] == tk
    lam_init = 0.8 - 0.6 * math.exp(-0.3 * layer)

    def tile_spec(w):
        return pl.BlockSpec((1, tq, w), lambda b, i: (b, i, 0))

    lam_specs = [_layer_spec(p[n], layer) for n in ('lambda_q1', 'lambda_k1', 'lambda_q2', 'lambda_k2')]
    return pl.pallas_call(
        functools.partial(_attn_prompt2_kernel, tq=tq, tk=tk, lam_init=lam_init),
        grid=(B, T // tq),
        in_specs=[pl.BlockSpec((1, ATT_W, tq), lambda b, i: (b, 0, i)),
                  pl.BlockSpec((1, T, ATT_W), lambda b, i: (b, 0, 0)),
                  pl.BlockSpec((1, T // tk, ATT_W, tk), lambda b, i: (b, 0, 0, 0)),
                  tile_spec(D_MODEL), tile_spec(D_MODEL), tile_spec(D_MODEL)] + lam_specs +
                 [_layer_spec(p['subln_g'], layer), _layer_spec(p['w_attn_out'], layer),
                  _layer_spec(p['w_o'], layer)],
        out_specs=tile_spec(D_MODEL),
        out_shape=jax.ShapeDtypeStruct((B, T, D_MODEL), F32),
        scratch_shapes=[pltpu.VMEM((HEADS, tk, 2 * tq), BF16),
                        pltpu.VMEM((HEADS, HEAD_W + DENOM_ROWS, 2 * tq), F32)],
        compiler_params=pltpu.CompilerParams(dimension_semantics=("parallel", "arbitrary"),
                                             vmem_limit_bytes=VMEM_LIMIT),
        name="attn_prompt",
    )(qt, k16, vt, mab, gc, x, p['lambda_q1'], p['lambda_k1'], p['lambda_q2'], p['lambda_k2'],
      p['subln_g'], p['w_attn_out'], p['w_o'])


def _stack_components(qh):
    lane = lax.broadcasted_iota(jnp.int32, qh.shape, 1)
    zero = jnp.zeros_like(qh)
    return jnp.concatenate([jnp.where(lane < HEAD_DIM, qh, zero),
                            jnp.where(lane < HEAD_DIM, zero, qh)], axis=0)


def _softmax_step(carry, s, vj):
    m, l, acc = carry
    m_new = jnp.maximum(m, jnp.max(s, axis=-1, keepdims=True))
    pr = jnp.exp2(s - m_new)
    alpha = jnp.exp2(m - m_new)
    l = alpha * l + jnp.sum(pr, axis=-1, keepdims=True)
    acc = alpha * acc + jnp.dot(pr.astype(BF16), vj, preferred_element_type=F32)
    return m_new, l, acc


def _attn_sample_kernel(q_ref, kp_ref, vp_ref, kn_ref, vn_ref, mab_ref, gc_ref, x_ref, lq1_ref,
                        lk1_ref, lq2_ref, lk2_ref, sg_ref, wc_ref, wo_ref, o_ref, *, tq, past,
                        lam_init):
    lam = _lambda(lq1_ref, lk1_ref, lq2_ref, lk2_ref, lam_init)
    q_chunk = (past + lax.broadcasted_iota(jnp.int32, (2 * tq, 1), 0) % tq) // CHUNK
    vis_past = (lax.broadcasted_iota(jnp.int32, (2 * tq, past), 1) // CHUNK) <= q_chunk
    vis_new = ((past + lax.broadcasted_iota(jnp.int32, (2 * tq, tq), 1)) // CHUNK) <= q_chunk
    heads = []
    for hd in range(HEADS):
        lo, hi = hd * HEAD_W, (hd + 1) * HEAD_W
        qs = _stack_components(q_ref[0, :, lo:hi])
        carry = (jnp.full((2 * tq, 1), -jnp.inf, F32), jnp.zeros((2 * tq, 1), F32),
                 jnp.zeros((2 * tq, HEAD_W), F32))
        s = lax.dot_general(qs, kn_ref[0, :, lo:hi], NT_DIMS, preferred_element_type=F32)
        carry = _softmax_step(carry, jnp.where(vis_new, s, -jnp.inf), vn_ref[0, :, lo:hi])
        kp = kp_ref[pl.ds(hd, past, stride=HEADS), :].astype(BF16)
        vp = vp_ref[pl.ds(hd, past, stride=HEADS), :].astype(BF16)
        s = lax.dot_general(qs, kp, NT_DIMS, preferred_element_type=F32)
        _, l, acc = _softmax_step(carry, jnp.where(vis_past, s, -jnp.inf), vp)
        o = acc / l
        o = o[:tq] - lam * o[tq:]
        heads.append(_rms(o, sg_ref[...], SUBLN_EPS) * (1.0 - lam_init))
    o = jnp.concatenate(heads, axis=-1)
    o_ref[0] = _merge_out(o, mab_ref[0], gc_ref[0], x_ref[0], wc_ref, wo_ref)


def _attn_sample(q, cache_k, cache_v, k16, v16, mab, gc, x, layer, p):
    B, T, _ = x.shape
    past = cache_k.shape[2] // HEADS
    lam_init = 0.8 - 0.6 * math.exp(-0.3 * layer)

    def tile_spec(w):
        return pl.BlockSpec((1, T, w), lambda b: (b, 0, 0))

    past_spec = pl.BlockSpec((None, None, past * HEADS, HEAD_W), lambda b: (layer, b, 0, 0))
    lam_specs = [_layer_spec(p[n], layer) for n in ('lambda_q1', 'lambda_k1', 'lambda_q2', 'lambda_k2')]
    return pl.pallas_call(
        functools.partial(_attn_sample_kernel, tq=T, past=past, lam_init=lam_init),
        grid=(B,),
        in_specs=[tile_spec(ATT_W), past_spec, past_spec, tile_spec(ATT_W), tile_spec(ATT_W),
                  tile_spec(D_MODEL), tile_spec(D_MODEL), tile_spec(D_MODEL)] + lam_specs +
                 [_layer_spec(p['subln_g'], layer), _layer_spec(p['w_attn_out'], layer),
                  _layer_spec(p['w_o'], layer)],
        out_specs=tile_spec(D_MODEL),
        out_shape=jax.ShapeDtypeStruct((B, T, D_MODEL), F32),
        compiler_params=pltpu.CompilerParams(dimension_semantics=("parallel",),
                                             vmem_limit_bytes=VMEM_LIMIT),
        name="attn_sample",
    )(q, cache_k, cache_v, k16, v16, mab, gc, x, p['lambda_q1'], p['lambda_k1'], p['lambda_q2'],
      p['lambda_k2'], p['subln_g'], p['w_attn_out'], p['w_o'])


def _chunks(n, step):
    return [(lo, min(lo + step, n)) for lo in range(0, n, step)]


def _ffn_dense_kernel(x_ref, ng_ref, wg_ref, wu_ref, wd_ref, nf_ref, o_ref, *, final):
    bb, tt, _ = x_ref.shape
    x = x_ref[...].reshape(bb * tt, D_MODEL)
    h = _rms(x, ng_ref[...], EPS).astype(BF16)
    y = x
    for lo, hi in _chunks(wg_ref.shape[1], 1024):
        g = jnp.dot(h, wg_ref[:, lo:hi], preferred_element_type=F32)
        u = jnp.dot(h, wu_ref[:, lo:hi], preferred_element_type=F32)
        a = (g * jax.nn.sigmoid(g) * u).astype(BF16)
        y = y + jnp.dot(a, wd_ref[lo:hi, :], preferred_element_type=F32)
    if final:
        y = _rms(y, nf_ref[...], EPS)
    o_ref[...] = y.reshape(bb, tt, D_MODEL)


def _token_blocks(B, T, tm):
    if T >= tm:
        assert T % tm == 0
        return 1, tm
    assert tm % T == 0 and B % (tm // T) == 0
    return tm // T, T


def _ffn_dense(x, layer, p, *, tm, final):
    B, T, _ = x.shape
    bb, tt = _token_blocks(B, T, tm)
    d = layer // 2
    row_spec = pl.BlockSpec((bb, tt, D_MODEL), lambda b, t: (b, t, 0))
    return pl.pallas_call(
        functools.partial(_ffn_dense_kernel, final=final),
        grid=(B // bb, T // tt),
        in_specs=[row_spec, _layer_spec(p['norm_ffn'], layer), _layer_spec(p['w_gate_d'], d),
                  _layer_spec(p['w_up_d'], d), _layer_spec(p['w_down_d'], d),
                  _const_spec(p['norm_final'].shape)],
        out_specs=row_spec,
        out_shape=jax.ShapeDtypeStruct(x.shape, F32),
        compiler_params=pltpu.CompilerParams(dimension_semantics=("parallel", "parallel"),
                                             vmem_limit_bytes=VMEM_LIMIT),
        name="ffn_dense",
    )(x, p['norm_ffn'], p['w_gate_d'], p['w_up_d'], p['w_down_d'], p['norm_final'])


def _route_top2(logits):
    row = lax.broadcasted_iota(jnp.int32, logits.shape, 0).astype(F32)
    big = float(N_EXPERTS)
    m1 = jnp.max(logits, axis=0, keepdims=True)
    i1 = jnp.min(jnp.where(logits == m1, row, big), axis=0, keepdims=True)
    rest = jnp.where(row == i1, -jnp.inf, logits)
    m2 = jnp.max(rest, axis=0, keepdims=True)
    i2 = jnp.min(jnp.where(rest == m2, row, big), axis=0, keepdims=True)
    e2 = jnp.exp(m2 - m1)
    w1 = 1.0 / (1.0 + e2)
    w2 = e2 / (1.0 + e2)
    return jnp.where(row == i1, w1, 0.0) + jnp.where(row == i2, w2, 0.0)


def _ffn_moe_kernel(x_ref, ng_ref, wrt_ref, brt_ref, wg_ref, wu_ref, wd_ref, nf_ref, o_ref,
                    h_scr, wts_scr, rank_scr, *, final, tb, cms):
    e = pl.program_id(2)
    bb, tt, _ = x_ref.shape
    n_tok = bb * tt
    sub_blocks = _chunks(n_tok, tb)

    @pl.when(e == 0)
    def _():
        x = x_ref[...].reshape(n_tok, D_MODEL)
        h = _rms(x, ng_ref[...], EPS)
        logits = lax.dot_general(wrt_ref[...], h, NT_DIMS, preferred_element_type=F32,
                                 precision=lax.Precision.HIGHEST) + brt_ref[...]
        wts = _route_top2(logits)
        sel = (wts > 0.0).astype(BF16)
        before = (lax.broadcasted_iota(jnp.int32, (tb, tb), 0)
                  < lax.broadcasted_iota(jnp.int32, (tb, tb), 1)).astype(BF16)
        rank = jnp.concatenate(
            [jnp.dot(sel[:, lo:hi], before, preferred_element_type=F32) for lo, hi in sub_blocks],
            axis=1)
        for ee in range(N_EXPERTS):
            for sb, (lo, hi) in enumerate(sub_blocks):
                wts_scr[ee, sb] = wts[ee:ee + 1, lo:hi]
                rank_scr[ee, sb] = rank[ee:ee + 1, lo:hi]
        h_scr[...] = h.astype(BF16)
        o_ref[...] = x_ref[...]

    cm_max = cms[-1]

    def sub_block(sb, carry):
        w_row = wts_scr[e, sb]
        r_row = rank_scr[e, sb]
        sel = w_row > 0.0
        n_sel = jnp.sum(sel.astype(F32)).astype(jnp.int32)
        row0 = pl.multiple_of(sb * tb, tb)

        def run_chunk(cm, base):
            rows = (lax.broadcasted_iota(jnp.int32, (cm, tb), 0) + base).astype(F32)
            hit = (r_row == rows) & sel
            gather = jnp.where(hit, 1.0, 0.0).astype(BF16)
            xg = jnp.dot(gather, h_scr[pl.ds(row0, tb), :],
                         preferred_element_type=F32).astype(BF16)
            g = jnp.dot(xg, wg_ref[...], preferred_element_type=F32)
            u = jnp.dot(xg, wu_ref[...], preferred_element_type=F32)
            a = (g * jax.nn.sigmoid(g) * u).astype(BF16)
            y = jnp.dot(a, wd_ref[...], preferred_element_type=F32).astype(BF16)
            scatter = jnp.where(hit, w_row, 0.0).astype(BF16)
            upd = lax.dot_general(scatter, y, TN_DIMS, preferred_element_type=F32)
            if bb == 1:
                o_ref[0, pl.ds(row0, tb), :] += upd
            else:
                o_ref[pl.ds(sb * (tb // tt), tb // tt)] += upd.reshape(tb // tt, tt, D_MODEL)

        def chunk(c, carry):
            left = n_sel - c * cm_max
            size_idx = sum((left > cm).astype(jnp.int32) for cm in cms[:-1])
            lax.switch(size_idx, [functools.partial(run_chunk, cm) for cm in cms], c * cm_max)
            return carry

        n_chunks = sum((n_sel > c * cm_max).astype(jnp.int32) for c in range(-(-tb // cm_max)))
        lax.fori_loop(0, n_chunks, chunk, 0)
        return carry

    lax.fori_loop(0, n_tok // tb, sub_block, 0)

    if final:
        @pl.when(e == N_EXPERTS - 1)
        def _():
            y = o_ref[...].reshape(n_tok, D_MODEL)
            o_ref[...] = _rms(y, nf_ref[...], EPS).reshape(bb, tt, D_MODEL)


def _ffn_moe(x, layer, p, *, tm, tb, cms, final):
    B, T, _ = x.shape
    bb, tt = _token_blocks(B, T, tm)
    assert tm % tb == 0 and (bb == 1 or tb % tt == 0)
    mo = layer // 2
    row_spec = pl.BlockSpec((bb, tt, D_MODEL), lambda b, t, e: (b, t, 0))

    def expert_spec(arr):
        return pl.BlockSpec((None, None) + tuple(arr.shape[2:]), lambda b, t, e: (mo, e, 0, 0))

    return pl.pallas_call(
        functools.partial(_ffn_moe_kernel, final=final, tb=tb, cms=cms),
        grid=(B // bb, T // tt, N_EXPERTS),
        in_specs=[pl.BlockSpec((bb, tt, D_MODEL), lambda b, t, e: (b, t, 0),
                               pipeline_mode=pl.Buffered(1)),
                  _layer_spec(p['norm_ffn'], layer), _layer_spec(p['w_router_t'], mo),
                  _layer_spec(p['b_router_t'], mo), expert_spec(p['w_gate_e']),
                  expert_spec(p['w_up_e']), expert_spec(p['w_down_e']),
                  _const_spec(p['norm_final'].shape)],
        out_specs=row_spec,
        out_shape=jax.ShapeDtypeStruct(x.shape, F32),
        scratch_shapes=[pltpu.VMEM((tm, D_MODEL), BF16),
                        pltpu.VMEM((N_EXPERTS, tm // tb, 1, tb), F32),
                        pltpu.VMEM((N_EXPERTS, tm // tb, 1, tb), F32)],
        compiler_params=pltpu.CompilerParams(
            dimension_semantics=("parallel", "parallel", "arbitrary"),
            vmem_limit_bytes=VMEM_LIMIT),
        name="ffn_moe",
    )(x, p['norm_ffn'], p['w_router_t'], p['b_router_t'], p['w_gate_e'], p['w_up_e'],
      p['w_down_e'], p['norm_final'])


def _rope_tables(pos0, T, bb, tt):
    half = HEAD_DIM // 2
    inv_freq = ROPE_THETA ** (-jnp.arange(half, dtype=F32) / half)
    ang = (pos0 + jnp.arange(T)).astype(F32)[:, None] * inv_freq[None, :]
    cos, sin = jnp.cos(ang), jnp.sin(ang)
    cos = jnp.concatenate([cos, cos, cos, cos], axis=-1)
    sin = jnp.concatenate([-sin, sin, -sin, sin], axis=-1)

    def lay(a):
        a = a.reshape(T // tt, 1, tt, HEAD_W)
        return jnp.broadcast_to(a, (T // tt, bb, tt, HEAD_W)).reshape(T // tt, bb * tt, HEAD_W)

    return lay(cos), lay(sin)


def _pad_hist(state):
    return jnp.pad(state, ((0, 0), (0, 0), (HIST_PAD - state.shape[2], 0), (0, 0)))


_MATMUL_WEIGHTS = ('w_in', 'w_conv_out', 'w_pool', 'w_attn_out', 'w_o', 'w_gate_d', 'w_up_d',
                   'w_down_d', 'w_gate_e', 'w_up_e', 'w_down_e')
_ROW_VECTORS = ('norm_mix', 'pool_scale', 'lambda_q1', 'lambda_k1', 'lambda_q2', 'lambda_k2',
                'subln_g', 'norm_ffn')


def _params(w):
    p = dict(w)
    for n in _MATMUL_WEIGHTS:
        p[n] = w[n].astype(BF16)
    for n in _ROW_VECTORS:
        p[n] = w[n][:, None, :]
    p['w_router_t'] = jnp.swapaxes(w['w_router'], 1, 2)
    p['b_router_t'] = w['b_router'][:, :, None]
    p['norm_final'] = w['norm_final'][None, :]
    return p


def _token_mixer_prompt(x, layer, p, kv_bufs, *, tt=512, tq=512, tk=256):
    B, T, _ = x.shape
    cos, sin = _rope_tables(0, T, 1, tt)
    zero_hist = jnp.zeros((B, HIST_PAD, CONV_W), F32)
    mab, gc, qt, k16, vt, k32, v32, co, po = _mixer_in(
        x, zero_hist, zero_hist, cos, sin, layer, p, kv_bufs, bb=1, tt=tt, pos0=0, tk=tk)
    x = _attn_prompt2(qt, k16, vt, mab, gc, x, layer, p, tq=tq, tk=tk)
    return x, co[:, HIST_PAD - (CONV_K - 1):], po[:, HIST_PAD - POOL_HIST:], (k32, v32)


def _token_mixer_sample(x, cache_k, cache_v, state_conv, state_pool, layer, p, kv_bufs, *, bb):
    past = cache_k.shape[2]
    T = x.shape[1]
    cos, sin = _rope_tables(past, T, bb, T)
    mab, gc, q, k16, v16, k32, v32, co, po = _mixer_in(
        x, _pad_hist(state_conv), _pad_hist(state_pool), cos, sin, layer, p, kv_bufs,
        bb=bb, tt=T, pos0=past, tk=None)
    rows = cache_k.shape[:2] + (past * HEADS, HEAD_W)
    x = _attn_sample(q, cache_k.reshape(rows), cache_v.reshape(rows), k16, v16, mab, gc, x, layer, p)
    return x, co[:, HIST_PAD - (CONV_K - 1):], po[:, HIST_PAD - POOL_HIST:], (k32, v32)


def _channel_mixer(x, layer, p, *, final):
    if layer % 2 == 0:
        return _ffn_dense(x, layer, p, tm=512, final=final)
    tokens = x.shape[0] * x.shape[1]
    return _ffn_moe(x, layer, p, tm=min(2048, tokens), tb=512, cms=(128, 160, 192, 224, 256),
                    final=final)


def kernel(x_prompt, x_sample, cache_k, cache_v, state_conv, state_pool, norm_mix, w_in, conv_w,
           w_conv_out, w_pool, pool_scale, lambda_q1, lambda_k1, lambda_q2, lambda_k2, subln_g,
           w_attn_out, w_o, norm_ffn, w_gate_d, w_up_d, w_down_d, w_router, b_router, w_gate_e,
           w_up_e, w_down_e, norm_final):
    depth = w_in.shape[0]
    p = _params(dict(
        norm_mix=norm_mix, w_in=w_in, conv_w=conv_w, w_conv_out=w_conv_out, w_pool=w_pool,
        pool_scale=pool_scale, lambda_q1=lambda_q1, lambda_k1=lambda_k1, lambda_q2=lambda_q2,
        lambda_k2=lambda_k2, subln_g=subln_g, w_attn_out=w_attn_out, w_o=w_o, norm_ffn=norm_ffn,
        w_gate_d=w_gate_d, w_up_d=w_up_d, w_down_d=w_down_d, w_router=w_router,
        b_router=b_router, w_gate_e=w_gate_e, w_up_e=w_up_e, w_down_e=w_down_e,
        norm_final=norm_final))

    xp, xs = x_prompt, x_sample
    kv_p = kv_s = None
    states = [[] for _ in range(4)]
    for l in range(depth):
        final = l == depth - 1
        xp, cp, pp, kv_p = _token_mixer_prompt(xp, l, p, kv_p)
        xs, cs, ps, kv_s = _token_mixer_sample(xs, cache_k, cache_v, state_conv, state_pool, l, p,
                                               kv_s, bb=16)
        for lst, a in zip(states, (cp, pp, cs, ps)):
            lst.append(a)
        xp = _channel_mixer(xp, l, p, final=final)
        xs = _channel_mixer(xs, l, p, final=final)

    def heads_view(a):
        return a.reshape(a.shape[0], a.shape[1], a.shape[2] // HEADS, HEADS, HEAD_W)

    return (xp, xs, heads_view(kv_p[0]), heads_view(kv_p[1]), jnp.stack(states[0]),
            jnp.stack(states[1]), heads_view(kv_s[0]), heads_view(kv_s[1]),
            jnp.stack(states[2]), jnp.stack(states[3]))
```

```python
import functools
import math

import jax
import jax.numpy as jnp
from jax import lax
from jax.experimental import pallas as pl
from jax.experimental.pallas import tpu as pltpu

D_MODEL = 1024
CHUNK = 64
CONV_W = 512
CONV_K = 3
POOL_W = 512
POOL_GC = 128
POOL_WINDOWS = (2, 4, 8, 16)
POOL_HIST = 15
HEADS = 4
HEAD_DIM = 64
HEAD_W = 2 * HEAD_DIM
ATT_W = HEADS * HEAD_W
ROPE_THETA = 10000.0
N_EXPERTS = 8
EPS = 1e-6
SUBLN_EPS = 1e-5

C_XA, C_BA, C_CA, C_U, C_Q, C_K, C_V, C_GA, C_GB, C_GC, C_END = (
    0, 512, 1024, 1536, 2048, 2560, 3072, 3584, 4608, 5632, 6656)

HIST_PAD = 16
VMEM_LIMIT = 56 * 1024 * 1024
Q_SCALE = HEAD_DIM ** -0.5 * math.log2(math.e)
DENOM_ROWS = 16

F32 = jnp.float32
BF16 = jnp.bfloat16
NT_DIMS = (((1,), (1,)), ((), ()))
TN_DIMS = (((0,), (0,)), ((), ()))


def _const_spec(shape):
    nd = len(shape)
    return pl.BlockSpec(tuple(shape), lambda *_: (0,) * nd, pipeline_mode=pl.Buffered(1))


def _layer_spec(arr, layer):
    nd = arr.ndim
    return pl.BlockSpec((None,) + tuple(arr.shape[1:]), lambda *_: (layer,) + (0,) * (nd - 1),
                        pipeline_mode=pl.Buffered(1))


def _rms(x, g, eps):
    return x * lax.rsqrt(jnp.mean(x * x, axis=-1, keepdims=True) + eps) * g


def _mixer_in_kernel(*refs, bb, tt, pos0, tk, aliased):
    (x_ref, hc_ref, hp_ref, cos_ref, sin_ref, ng_ref, win_ref, cw_ref, wa_ref, wp_ref,
     ps_ref) = refs[:11]
    refs = refs[11 + (2 if aliased else 0):]
    (mab_ref, gc_ref, q_ref, k16_ref, v16_ref, k32_ref, v32_ref, co_ref, po_ref, cbuf,
     ubuf) = refs
    t = pl.program_id(1)
    m = bb * tt
    x = x_ref[...].reshape(m, D_MODEL)
    h = _rms(x, ng_ref[...], EPS).astype(BF16)

    def seg(lo, hi):
        return jnp.dot(h, win_ref[:, lo:hi], preferred_element_type=F32)

    @pl.when(t == 0)
    def _():
        cbuf[:, 0:HIST_PAD, :] = hc_ref[...]
        ubuf[:, 0:HIST_PAD, :] = hp_ref[...]

    z_ca, z_xa, z_ba, z_u = seg(C_CA, C_U), seg(C_XA, C_BA), seg(C_BA, C_CA), seg(C_U, C_Q)
    z_q, z_k, z_v = seg(C_Q, C_K), seg(C_K, C_V), seg(C_V, C_GA)
    cin = (z_ca * z_xa).reshape(bb, tt, CONV_W)
    cbuf[:, HIST_PAD:, :] = cin
    conv = cbuf[:, HIST_PAD - 2:HIST_PAD - 2 + tt, :] * cw_ref[0:1, :]
    conv = conv + cbuf[:, HIST_PAD - 1:HIST_PAD - 1 + tt, :] * cw_ref[1:2, :]
    conv = conv + cin * cw_ref[2:3, :]
    ya = jnp.dot((z_ba * conv.reshape(m, CONV_W)).astype(BF16), wa_ref[...],
                 preferred_element_type=F32)

    ubuf[:, HIST_PAD:, :] = z_u.reshape(bb, tt, POOL_W)
    pos = pos0 + t * tt + lax.broadcasted_iota(jnp.int32, (bb, tt, POOL_GC), 1)
    yb_parts = []
    for gi, win in enumerate(POOL_WINDOWS):
        lo, hi = gi * POOL_GC, (gi + 1) * POOL_GC
        cur = ubuf[:, HIST_PAD:HIST_PAD + tt, lo:hi]
        s = cur
        for k in range(1, win):
            s = s + ubuf[:, HIST_PAD - k:HIST_PAD - k + tt, lo:hi]
        cnt = jnp.minimum(pos + 1, win).astype(F32)
        pooled = s / cnt - cur
        yb_parts.append(jnp.dot(pooled.reshape(m, POOL_GC).astype(BF16), wp_ref[gi],
                                preferred_element_type=F32))
    yb = jnp.concatenate(yb_parts, axis=-1) * ps_ref[...]

    cos4 = jnp.concatenate([cos_ref[...]] * HEADS, axis=-1)
    sin4 = jnp.concatenate([sin_ref[...]] * HEADS, axis=-1)
    lane = lax.broadcasted_iota(jnp.int32, (m, ATT_W), 1)
    first_half = (lane & (HEAD_DIM // 2)) == 0

    def rope(z):
        swapped = jnp.where(first_half, pltpu.roll(z, ATT_W - HEAD_DIM // 2, axis=1),
                            pltpu.roll(z, HEAD_DIM // 2, axis=1))
        return z * cos4 + swapped * sin4

    q = rope(z_q) * Q_SCALE
    k = rope(z_k)
    v = z_v
    for hd in range(HEADS):
        lo, hi = hd * HEAD_W, (hd + 1) * HEAD_W
        k32_ref[:, pl.ds(hd, tt, stride=HEADS), :] = k[:, lo:hi].reshape(bb, tt, HEAD_W)
        v32_ref[:, pl.ds(hd, tt, stride=HEADS), :] = v[:, lo:hi].reshape(bb, tt, HEAD_W)
    k16_ref[...] = k.astype(BF16).reshape(bb, tt, ATT_W)
    if tk is None:
        q_ref[...] = q.astype(BF16).reshape(bb, tt, ATT_W)
        v16_ref[...] = v.astype(BF16).reshape(bb, tt, ATT_W)
    else:
        q_ref[0] = q.T.astype(BF16)
        for c in range(tt // tk):
            v16_ref[0, c] = v[c * tk:(c + 1) * tk, :].T.astype(BF16)

    ga = jax.nn.sigmoid(seg(C_GA, C_GB))
    gb = jax.nn.sigmoid(seg(C_GB, C_GC))
    mab_ref[...] = (ga * ya + gb * yb).astype(BF16).reshape(bb, tt, D_MODEL)
    gc_ref[...] = jax.nn.sigmoid(seg(C_GC, C_END)).astype(BF16).reshape(bb, tt, D_MODEL)

    last_c = cbuf[:, tt:tt + HIST_PAD, :]
    last_p = ubuf[:, tt:tt + HIST_PAD, :]
    co_ref[...] = last_c
    po_ref[...] = last_p
    cbuf[:, 0:HIST_PAD, :] = last_c
    ubuf[:, 0:HIST_PAD, :] = last_p


def _mixer_in(x, hist_c, hist_p, cos, sin, layer, p, kv_bufs, *, bb, tt, pos0, tk):
    B, T, _ = x.shape
    depth = p['w_in'].shape[0]
    assert B % bb == 0 and T % tt == 0 and tt >= HIST_PAD and tt % 8 == 0
    assert tk is None or (bb == 1 and tt % tk == 0)
    grid = (B // bb, T // tt)
    m = bb * tt

    def seq_spec(w):
        return pl.BlockSpec((bb, tt, w), lambda b, t: (b, t, 0))

    if hist_c.ndim == 4:
        hist_spec = pl.BlockSpec((None, bb, HIST_PAD, CONV_W), lambda b, t: (layer, b, 0, 0))
    else:
        hist_spec = pl.BlockSpec((bb, HIST_PAD, CONV_W), lambda b, t: (b, 0, 0))
    state_spec = pl.BlockSpec((bb, HIST_PAD, CONV_W), lambda b, t: (b, 0, 0))
    rope_spec = pl.BlockSpec((None, m, HEAD_W), lambda b, t: (t, 0, 0))
    kv_spec = pl.BlockSpec((None, bb, tt * HEADS, HEAD_W), lambda b, t: (layer, b, t, 0))
    kv_shape = jax.ShapeDtypeStruct((depth, B, T * HEADS, HEAD_W), F32)
    if tk is None:
        q_shape, q_spec = jax.ShapeDtypeStruct((B, T, ATT_W), BF16), seq_spec(ATT_W)
        v_shape, v_spec = q_shape, q_spec
    else:
        q_shape = jax.ShapeDtypeStruct((B, ATT_W, T), BF16)
        q_spec = pl.BlockSpec((1, ATT_W, tt), lambda b, t: (b, 0, t))
        v_shape = jax.ShapeDtypeStruct((B, T // tk, ATT_W, tk), BF16)
        v_spec = pl.BlockSpec((1, tt // tk, ATT_W, tk), lambda b, t: (b, t, 0, 0))

    out_shape = (
        jax.ShapeDtypeStruct((B, T, D_MODEL), BF16),
        jax.ShapeDtypeStruct((B, T, D_MODEL), BF16),
        q_shape,
        jax.ShapeDtypeStruct((B, T, ATT_W), BF16),
        v_shape,
        kv_shape, kv_shape,
        jax.ShapeDtypeStruct((B, HIST_PAD, CONV_W), F32),
        jax.ShapeDtypeStruct((B, HIST_PAD, POOL_W), F32),
    )
    out_specs = (seq_spec(D_MODEL), seq_spec(D_MODEL), q_spec, seq_spec(ATT_W), v_spec,
                 kv_spec, kv_spec, state_spec, state_spec)
    in_specs = [seq_spec(D_MODEL), hist_spec, hist_spec, rope_spec, rope_spec,
                _layer_spec(p['norm_mix'], layer), _layer_spec(p['w_in'], layer),
                _layer_spec(p['conv_w'], layer), _layer_spec(p['w_conv_out'], layer),
                _layer_spec(p['w_pool'], layer), _layer_spec(p['pool_scale'], layer)]
    args = [x, hist_c, hist_p, cos, sin, p['norm_mix'], p['w_in'], p['conv_w'], p['w_conv_out'],
            p['w_pool'], p['pool_scale']]
    aliases = {}
    if kv_bufs is not None:
        in_specs += [pl.BlockSpec(memory_space=pl.ANY)] * 2
        aliases = {len(args): 5, len(args) + 1: 6}
        args += list(kv_bufs)
    return pl.pallas_call(
        functools.partial(_mixer_in_kernel, bb=bb, tt=tt, pos0=pos0, tk=tk,
                          aliased=kv_bufs is not None),
        grid=grid, in_specs=in_specs, out_specs=out_specs, out_shape=out_shape,
        scratch_shapes=[pltpu.VMEM((bb, HIST_PAD + tt, CONV_W), F32),
                        pltpu.VMEM((bb, HIST_PAD + tt, POOL_W), F32)],
        input_output_aliases=aliases,
        compiler_params=pltpu.CompilerParams(dimension_semantics=("parallel", "arbitrary"),
                                             vmem_limit_bytes=VMEM_LIMIT),
        name="mixer_in",
    )(*args)


def _lambda(lq1_ref, lk1_ref, lq2_ref, lk2_ref, lam_init):
    a = jnp.exp(jnp.sum(lq1_ref[...] * lk1_ref[...], axis=-1, keepdims=True))
    b = jnp.exp(jnp.sum(lq2_ref[...] * lk2_ref[...], axis=-1, keepdims=True))
    return a - b + lam_init


def _merge_out(o, mab, gc, x, wc_ref, wo_ref):
    yc = jnp.dot(o.astype(BF16), wc_ref[...], preferred_element_type=F32)
    merged = mab.astype(F32) + gc.astype(F32) * yc
    return x + jnp.dot(merged.astype(BF16), wo_ref[...], preferred_element_type=F32)


def _attn_prompt_kernel(qt_ref, k_ref, vt_ref, mab_ref, gc_ref, x_ref, lq1_ref, lk1_ref, lq2_ref,
                        lk2_ref, sg_ref, wc_ref, wo_ref, o_ref, *, tq, lam_init):
    i = pl.program_id(1)
    lam = _lambda(lq1_ref, lk1_ref, lq2_ref, lk2_ref, lam_init)
    k_chunk = lax.broadcasted_iota(jnp.int32, (tq, 2 * tq), 0) // CHUNK
    q_chunk = (lax.broadcasted_iota(jnp.int32, (tq, 2 * tq), 1) % tq) // CHUNK
    visible = k_chunk <= q_chunk
    feat = lax.broadcasted_iota(jnp.int32, (HEAD_W, tq), 0)
    qs = []
    for hd in range(HEADS):
        qt = qt_ref[0, hd * HEAD_W:(hd + 1) * HEAD_W, :]
        zero = jnp.zeros_like(qt)
        qs.append(jnp.concatenate([jnp.where(feat < HEAD_DIM, qt, zero),
                                   jnp.where(feat < HEAD_DIM, zero, qt)], axis=1))

    def scores(j):
        off = pl.multiple_of(j * tq, tq)
        return [jnp.dot(k_ref[0, pl.ds(off, tq), hd * HEAD_W:(hd + 1) * HEAD_W], qs[hd],
                        preferred_element_type=F32) for hd in range(HEADS)]

    def softmax(s, m, l):
        m_new = jnp.maximum(m, jnp.max(s, axis=0, keepdims=True))
        pr = jnp.exp2(s - m_new)
        alpha = jnp.exp2(m - m_new)
        return m_new, alpha * l + jnp.sum(pr, axis=0, keepdims=True), alpha, pr.astype(BF16)

    def weighted_values(j, acc, alpha, pr):
        return [alpha[hd] * acc[hd] + jnp.dot(vt_ref[0, j, hd * HEAD_W:(hd + 1) * HEAD_W, :],
                                              pr[hd], preferred_element_type=F32)
                for hd in range(HEADS)]

    def unzip(rows):
        return tuple(list(col) for col in zip(*rows))

    m, l, alpha, pr = unzip([
        softmax(jnp.where(visible, s, -jnp.inf), jnp.full((1, 2 * tq), -jnp.inf, F32),
                jnp.zeros((1, 2 * tq), F32)) for s in scores(i)])
    acc = [jnp.zeros((HEAD_W, 2 * tq), F32) for _ in range(HEADS)]

    def body(j, carry):
        s_cur, pend, alpha, pr, m, l, acc = carry
        acc = weighted_values(pend, acc, alpha, pr)
        s_next = scores(j + 1)
        m, l, alpha, pr = unzip([softmax(s_cur[hd], m[hd], l[hd]) for hd in range(HEADS)])
        return s_next, j, alpha, pr, m, l, acc

    _, pend, alpha, pr, m, l, acc = lax.fori_loop(
        0, i, body, (scores(0), i, alpha, pr, m, l, acc))
    acc = weighted_values(pend, acc, alpha, pr)
    heads = []
    for hd in range(HEADS):
        ot = acc[hd] / l[hd]
        ot = ot[:, :tq] - lam * ot[:, tq:]
        ot = ot * lax.rsqrt(jnp.mean(ot * ot, axis=0, keepdims=True) + SUBLN_EPS)
        heads.append(ot.T * sg_ref[...] * (1.0 - lam_init))
    o = jnp.concatenate(heads, axis=-1)
    o_ref[0] = _merge_out(o, mab_ref[0], gc_ref[0], x_ref[0], wc_ref, wo_ref)


def _attn_prompt(qt, k16, vt, mab, gc, x, layer, p, *, tq):
    B, T, _ = x.shape
    assert T % tq == 0 and tq % CHUNK == 0 and vt.shape[3] == tq
    lam_init = 0.8 - 0.6 * math.exp(-0.3 * layer)

    def tile_spec(w):
        return pl.BlockSpec((1, tq, w), lambda b, i: (b, i, 0))

    lam_specs = [_layer_spec(p[n], layer) for n in ('lambda_q1', 'lambda_k1', 'lambda_q2', 'lambda_k2')]
    return pl.pallas_call(
        functools.partial(_attn_prompt_kernel, tq=tq, lam_init=lam_init),
        grid=(B, T // tq),
        in_specs=[pl.BlockSpec((1, ATT_W, tq), lambda b, i: (b, 0, i)),
                  pl.BlockSpec((1, T, ATT_W), lambda b, i: (b, 0, 0)),
                  pl.BlockSpec((1, T // tq, ATT_W, tq), lambda b, i: (b, 0, 0, 0)),
                  tile_spec(D_MODEL), tile_spec(D_MODEL), tile_spec(D_MODEL)] + lam_specs +
                 [_layer_spec(p['subln_g'], layer), _layer_spec(p['w_attn_out'], layer),
                  _layer_spec(p['w_o'], layer)],
        out_specs=tile_spec(D_MODEL),
        out_shape=jax.ShapeDtypeStruct((B, T, D_MODEL), F32),
        compiler_params=pltpu.CompilerParams(dimension_semantics=("parallel", "arbitrary"),
                                             vmem_limit_bytes=VMEM_LIMIT),
        name="attn_prompt",
    )(qt, k16, vt, mab, gc, x, p['lambda_q1'], p['lambda_k1'], p['lambda_q2'], p['lambda_k2'],
      p['subln_g'], p['w_attn_out'], p['w_o'])


def _attn_prompt2_kernel(qt_ref, k_ref, vt_ref, mab_ref, gc_ref, x_ref, lq1_ref, lk1_ref, lq2_ref,
                         lk2_ref, sg_ref, wc_ref, wo_ref, o_ref, p_scr, acc_scr, *, tq, tk,
                         lam_init):
    i = pl.program_id(1)
    n_diag = tq // tk
    lam = _lambda(lq1_ref, lk1_ref, lq2_ref, lk2_ref, lam_init)
    feat = lax.broadcasted_iota(jnp.int32, (HEAD_W, tq), 0)
    qs = []
    for hd in range(HEADS):
        qt = qt_ref[0, hd * HEAD_W:(hd + 1) * HEAD_W, :]
        zero = jnp.zeros_like(qt)
        qs.append(jnp.concatenate([jnp.where(feat < HEAD_DIM, qt, zero),
                                   jnp.where(feat < HEAD_DIM, zero, qt)], axis=1))

    def scores(j):
        off = pl.multiple_of(j * tk, tk)
        return [jnp.dot(k_ref[0, pl.ds(off, tk), hd * HEAD_W:(hd + 1) * HEAD_W], qs[hd],
                        preferred_element_type=F32) for hd in range(HEADS)]

    def softmax(s, m):
        m_out, alpha_out = [], []
        for hd in range(HEADS):
            m_new = jnp.maximum(m[hd], jnp.max(s[hd], axis=0, keepdims=True))
            p_scr[hd] = jnp.exp2(s[hd] - m_new).astype(BF16)
            m_out.append(m_new)
            alpha_out.append(jnp.exp2(m[hd] - m_new))
        return m_out, alpha_out

    ones_rows = jnp.ones((DENOM_ROWS, tk), BF16)

    def flush(j, alpha):
        for hd in range(HEADS):
            vt1 = jnp.concatenate([vt_ref[0, j, hd * HEAD_W:(hd + 1) * HEAD_W, :], ones_rows],
                                  axis=0)
            acc_scr[hd] = alpha[hd] * acc_scr[hd] + jnp.dot(vt1, p_scr[hd],
                                                            preferred_element_type=F32)

    acc_scr[...] = jnp.zeros_like(acc_scr)
    m = [jnp.full((1, 2 * tq), -jnp.inf, F32)] * HEADS
    alpha = [jnp.zeros((1, 2 * tq), F32)] * HEADS
    q_chunk = (lax.broadcasted_iota(jnp.int32, (tk, 2 * tq), 1) % tq) // CHUNK
    k_row = lax.broadcasted_iota(jnp.int32, (tk, 2 * tq), 0)
    for d in range(n_diag):
        j = i * n_diag + d
        s = scores(j)
        if d > 0:
            flush(j - 1, alpha)
        visible = (k_row + d * tk) // CHUNK <= q_chunk
        m, alpha = softmax([jnp.where(visible, sh, -jnp.inf) for sh in s], m)

    def body(j, carry):
        pend, alpha, m = carry
        s = scores(j)
        flush(pend, alpha)
        m, alpha = softmax(s, m)
        return j, alpha, m

    pend, alpha, m = lax.fori_loop(0, i * n_diag, body, (i * n_diag + n_diag - 1, alpha, m))
    flush(pend, alpha)
    heads = []
    for hd in range(HEADS):
        ot = acc_scr[hd, 0:HEAD_W, :] / acc_scr[hd, HEAD_W:HEAD_W + 1, :]
        ot = ot[:, :tq] - lam * ot[:, tq:]
        ot = ot * lax.rsqrt(jnp.mean(ot * ot, axis=0, keepdims=True) + SUBLN_EPS)
        heads.append(ot.T * sg_ref[...] * (1.0 - lam_init))
    o = jnp.concatenate(heads, axis=-1)
    o_ref[0] = _merge_out(o, mab_ref[0], gc_ref[0], x_ref[0], wc_ref, wo_ref)


def _attn_prompt2(qt, k16, vt, mab, gc, x, layer, p, *, tq, tk):
    B, T, _ = x.shape
    assert T % tq == 0 and tq % tk == 0 and tk % CHUNK == 0 and vt.shape[3] == tk
    lam_init = 0.8 - 0.6 * math.exp(-0.3 * layer)

    def tile_spec(w):
        return pl.BlockSpec((1, tq, w), lambda b, i: (b, i, 0))

    lam_specs = [_layer_spec(p[n], layer) for n in ('lambda_q1', 'lambda_k1', 'lambda_q2', 'lambda_k2')]
    return pl.pallas_call(
        functools.partial(_attn_prompt2_kernel, tq=tq, tk=tk, lam_init=lam_init),
        grid=(B, T // tq),
        in_specs=[pl.BlockSpec((1, ATT_W, tq), lambda b, i: (b, 0, i)),
                  pl.BlockSpec((1, T, ATT_W), lambda b, i: (b, 0, 0)),
                  pl.BlockSpec((1, T // tk, ATT_W, tk), lambda b, i: (b, 0, 0, 0)),
                  tile_spec(D_MODEL), tile_spec(D_MODEL), tile_spec(D_MODEL)] + lam_specs +
                 [_layer_spec(p['subln_g'], layer), _layer_spec(p['w_attn_out'], layer),
                  _layer_spec(p['w_o'], layer)],
        out_specs=tile_spec(D_MODEL),
        out_shape=jax.ShapeDtypeStruct((B, T, D_MODEL), F32),
        scratch_shapes=[pltpu.VMEM((HEADS, tk, 2 * tq), BF16),
                        pltpu.VMEM((HEADS, HEAD_W + DENOM_ROWS, 2 * tq), F32)],
        compiler_params=pltpu.CompilerParams(dimension_semantics=("parallel", "arbitrary"),
                                             vmem_limit_bytes=VMEM_LIMIT),
        name="attn_prompt",
    )(qt, k16, vt, mab, gc, x, p['lambda_q1'], p['lambda_k1'], p['lambda_q2'], p['lambda_k2'],
      p['subln_g'], p['w_attn_out'], p['w_o'])


def _stack_components(qh):
    lane = lax.broadcasted_iota(jnp.int32, qh.shape, 1)
    zero = jnp.zeros_like(qh)
    return jnp.concatenate([jnp.where(lane < HEAD_DIM, qh, zero),
                            jnp.where(lane < HEAD_DIM, zero, qh)], axis=0)


def _softmax_step(carry, s, vj):
    m, l, acc = carry
    m_new = jnp.maximum(m, jnp.max(s, axis=-1, keepdims=True))
    pr = jnp.exp2(s - m_new)
    alpha = jnp.exp2(m - m_new)
    l = alpha * l + jnp.sum(pr, axis=-1, keepdims=True)
    acc = alpha * acc + jnp.dot(pr.astype(BF16), vj, preferred_element_type=F32)
    return m_new, l, acc


def _attn_sample_kernel(q_ref, kp_ref, vp_ref, kn_ref, vn_ref, mab_ref, gc_ref, x_ref, lq1_ref,
                        lk1_ref, lq2_ref, lk2_ref, sg_ref, wc_ref, wo_ref, o_ref, *, tq, past,
                        lam_init):
    lam = _lambda(lq1_ref, lk1_ref, lq2_ref, lk2_ref, lam_init)
    q_chunk = (past + lax.broadcasted_iota(jnp.int32, (2 * tq, 1), 0) % tq) // CHUNK
    vis_past = (lax.broadcasted_iota(jnp.int32, (2 * tq, past), 1) // CHUNK) <= q_chunk
    vis_new = ((past + lax.broadcasted_iota(jnp.int32, (2 * tq, tq), 1)) // CHUNK) <= q_chunk
    heads = []
    for hd in range(HEADS):
        lo, hi = hd * HEAD_W, (hd + 1) * HEAD_W
        qs = _stack_components(q_ref[0, :, lo:hi])
        carry = (jnp.full((2 * tq, 1), -jnp.inf, F32), jnp.zeros((2 * tq, 1), F32),
                 jnp.zeros((2 * tq, HEAD_W), F32))
        s = lax.dot_general(qs, kn_ref[0, :, lo:hi], NT_DIMS, preferred_element_type=F32)
        carry = _softmax_step(carry, jnp.where(vis_new, s, -jnp.inf), vn_ref[0, :, lo:hi])
        kp = kp_ref[pl.ds(hd, past, stride=HEADS), :].astype(BF16)
        vp = vp_ref[pl.ds(hd, past, stride=HEADS), :].astype(BF16)
        s = lax.dot_general(qs, kp, NT_DIMS, preferred_element_type=F32)
        _, l, acc = _softmax_step(carry, jnp.where(vis_past, s, -jnp.inf), vp)
        o = acc / l
        o = o[:tq] - lam * o[tq:]
        heads.append(_rms(o, sg_ref[...], SUBLN_EPS) * (1.0 - lam_init))
    o = jnp.concatenate(heads, axis=-1)
    o_ref[0] = _merge_out(o, mab_ref[0], gc_ref[0], x_ref[0], wc_ref, wo_ref)


def _attn_sample(q, cache_k, cache_v, k16, v16, mab, gc, x, layer, p):
    B, T, _ = x.shape
    past = cache_k.shape[2] // HEADS
    lam_init = 0.8 - 0.6 * math.exp(-0.3 * layer)

    def tile_spec(w):
        return pl.BlockSpec((1, T, w), lambda b: (b, 0, 0))

    past_spec = pl.BlockSpec((None, None, past * HEADS, HEAD_W), lambda b: (layer, b, 0, 0))
    lam_specs = [_layer_spec(p[n], layer) for n in ('lambda_q1', 'lambda_k1', 'lambda_q2', 'lambda_k2')]
    return pl.pallas_call(
        functools.partial(_attn_sample_kernel, tq=T, past=past, lam_init=lam_init),
        grid=(B,),
        in_specs=[tile_spec(ATT_W), past_spec, past_spec, tile_spec(ATT_W), tile_spec(ATT_W),
                  tile_spec(D_MODEL), tile_spec(D_MODEL), tile_spec(D_MODEL)] + lam_specs +
                 [_layer_spec(p['subln_g'], layer), _layer_spec(p['w_attn_out'], layer),
                  _layer_spec(p['w_o'], layer)],
        out_specs=tile_spec(D_MODEL),
        out_shape=jax.ShapeDtypeStruct((B, T, D_MODEL), F32),
        compiler_params=pltpu.CompilerParams(dimension_semantics=("parallel",),
                                             vmem_limit_bytes=VMEM_LIMIT),
        name="attn_sample",
    )(q, cache_k, cache_v, k16, v16, mab, gc, x, p['lambda_q1'], p['lambda_k1'], p['lambda_q2'],
      p['lambda_k2'], p['subln_g'], p['w_attn_out'], p['w_o'])


def _chunks(n, step):
    return [(lo, min(lo + step, n)) for lo in range(0, n, step)]


def _ffn_dense_kernel(x_ref, ng_ref, wg_ref, wu_ref, wd_ref, nf_ref, o_ref, *, final):
    bb, tt, _ = x_ref.shape
    x = x_ref[...].reshape(bb * tt, D_MODEL)
    h = _rms(x, ng_ref[...], EPS).astype(BF16)
    y = x
    for lo, hi in _chunks(wg_ref.shape[1], 1024):
        g = jnp.dot(h, wg_ref[:, lo:hi], preferred_element_type=F32)
        u = jnp.dot(h, wu_ref[:, lo:hi], preferred_element_type=F32)
        a = (g * jax.nn.sigmoid(g) * u).astype(BF16)
        y = y + jnp.dot(a, wd_ref[lo:hi, :], preferred_element_type=F32)
    if final:
        y = _rms(y, nf_ref[...], EPS)
    o_ref[...] = y.reshape(bb, tt, D_MODEL)


def _token_blocks(B, T, tm):
    if T >= tm:
        assert T % tm == 0
        return 1, tm
    assert tm % T == 0 and B % (tm // T) == 0
    return tm // T, T


def _ffn_dense(x, layer, p, *, tm, final):
    B, T, _ = x.shape
    bb, tt = _token_blocks(B, T, tm)
    d = layer // 2
    row_spec = pl.BlockSpec((bb, tt, D_MODEL), lambda b, t: (b, t, 0))
    return pl.pallas_call(
        functools.partial(_ffn_dense_kernel, final=final),
        grid=(B // bb, T // tt),
        in_specs=[row_spec, _layer_spec(p['norm_ffn'], layer), _layer_spec(p['w_gate_d'], d),
                  _layer_spec(p['w_up_d'], d), _layer_spec(p['w_down_d'], d),
                  _const_spec(p['norm_final'].shape)],
        out_specs=row_spec,
        out_shape=jax.ShapeDtypeStruct(x.shape, F32),
        compiler_params=pltpu.CompilerParams(dimension_semantics=("parallel", "parallel"),
                                             vmem_limit_bytes=VMEM_LIMIT),
        name="ffn_dense",
    )(x, p['norm_ffn'], p['w_gate_d'], p['w_up_d'], p['w_down_d'], p['norm_final'])


def _route_top2(logits):
    row = lax.broadcasted_iota(jnp.int32, logits.shape, 0).astype(F32)
    big = float(N_EXPERTS)
    m1 = jnp.max(logits, axis=0, keepdims=True)
    i1 = jnp.min(jnp.where(logits == m1, row, big), axis=0, keepdims=True)
    rest = jnp.where(row == i1, -jnp.inf, logits)
    m2 = jnp.max(rest, axis=0, keepdims=True)
    i2 = jnp.min(jnp.where(rest == m2, row, big), axis=0, keepdims=True)
    e2 = jnp.exp(m2 - m1)
    w1 = 1.0 / (1.0 + e2)
    w2 = e2 / (1.0 + e2)
    return jnp.where(row == i1, w1, 0.0) + jnp.where(row == i2, w2, 0.0)


def _ffn_moe_kernel(x_ref, ng_ref, wrt_ref, brt_ref, wg_ref, wu_ref, wd_ref, nf_ref, o_ref,
                    h_scr, wts_scr, rank_scr, *, final, tb, cms):
    e = pl.program_id(2)
    bb, tt, _ = x_ref.shape
    n_tok = bb * tt
    sub_blocks = _chunks(n_tok, tb)

    @pl.when(e == 0)
    def _():
        x = x_ref[...].reshape(n_tok, D_MODEL)
        h = _rms(x, ng_ref[...], EPS)
        logits = lax.dot_general(wrt_ref[...], h, NT_DIMS, preferred_element_type=F32,
                                 precision=lax.Precision.HIGHEST) + brt_ref[...]
        wts = _route_top2(logits)
        sel = (wts > 0.0).astype(BF16)
        before = (lax.broadcasted_iota(jnp.int32, (tb, tb), 0)
                  < lax.broadcasted_iota(jnp.int32, (tb, tb), 1)).astype(BF16)
        rank = jnp.concatenate(
            [jnp.dot(sel[:, lo:hi], before, preferred_element_type=F32) for lo, hi in sub_blocks],
            axis=1)
        for ee in range(N_EXPERTS):
            for sb, (lo, hi) in enumerate(sub_blocks):
                wts_scr[ee, sb] = wts[ee:ee + 1, lo:hi]
                rank_scr[ee, sb] = rank[ee:ee + 1, lo:hi]
        h_scr[...] = h.astype(BF16)
        o_ref[...] = x_ref[...]

    cm_max = cms[-1]

    def sub_block(sb, carry):
        w_row = wts_scr[e, sb]
        r_row = rank_scr[e, sb]
        sel = w_row > 0.0
        n_sel = jnp.sum(sel.astype(F32)).astype(jnp.int32)
        row0 = pl.multiple_of(sb * tb, tb)

        def run_chunk(cm, base):
            rows = (lax.broadcasted_iota(jnp.int32, (cm, tb), 0) + base).astype(F32)
            hit = (r_row == rows) & sel
            gather = jnp.where(hit, 1.0, 0.0).astype(BF16)
            xg = jnp.dot(gather, h_scr[pl.ds(row0, tb), :],
                         preferred_element_type=F32).astype(BF16)
            g = jnp.dot(xg, wg_ref[...], preferred_element_type=F32)
            u = jnp.dot(xg, wu_ref[...], preferred_element_type=F32)
            a = (g * jax.nn.sigmoid(g) * u).astype(BF16)
            y = jnp.dot(a, wd_ref[...], preferred_element_type=F32).astype(BF16)
            scatter = jnp.where(hit, w_row, 0.0).astype(BF16)
            upd = lax.dot_general(scatter, y, TN_DIMS, preferred_element_type=F32)
            if bb == 1:
                o_ref[0, pl.ds(row0, tb), :] += upd
            else:
                o_ref[pl.ds(sb * (tb // tt), tb // tt)] += upd.reshape(tb // tt, tt, D_MODEL)

        def chunk(c, carry):
            left = n_sel - c * cm_max
            size_idx = sum((left > cm).astype(jnp.int32) for cm in cms[:-1])
            lax.switch(size_idx, [functools.partial(run_chunk, cm) for cm in cms], c * cm_max)
            return carry

        n_chunks = sum((n_sel > c * cm_max).astype(jnp.int32) for c in range(-(-tb // cm_max)))
        lax.fori_loop(0, n_chunks, chunk, 0)
        return carry

    lax.fori_loop(0, n_tok // tb, sub_block, 0)

    if final:
        @pl.when(e == N_EXPERTS - 1)
        def _():
            y = o_ref[...].reshape(n_tok, D_MODEL)
            o_ref[...] = _rms(y, nf_ref[...], EPS).reshape(bb, tt, D_MODEL)


def _ffn_moe(x, layer, p, *, tm, tb, cms, final):
    B, T, _ = x.shape
    bb, tt = _token_blocks(B, T, tm)
    assert tm % tb == 0 and (bb == 1 or tb % tt == 0)
    mo = layer // 2
    row_spec = pl.BlockSpec((bb, tt, D_MODEL), lambda b, t, e: (b, t, 0))

    def expert_spec(arr):
        return pl.BlockSpec((None, None) + tuple(arr.shape[2:]), lambda b, t, e: (mo, e, 0, 0))

    return pl.pallas_call(
        functools.partial(_ffn_moe_kernel, final=final, tb=tb, cms=cms),
        grid=(B // bb, T // tt, N_EXPERTS),
        in_specs=[pl.BlockSpec((bb, tt, D_MODEL), lambda b, t, e: (b, t, 0),
                               pipeline_mode=pl.Buffered(1)),
                  _layer_spec(p['norm_ffn'], layer), _layer_spec(p['w_router_t'], mo),
                  _layer_spec(p['b_router_t'], mo), expert_spec(p['w_gate_e']),
                  expert_spec(p['w_up_e']), expert_spec(p['w_down_e']),
                  _const_spec(p['norm_final'].shape)],
        out_specs=row_spec,
        out_shape=jax.ShapeDtypeStruct(x.shape, F32),
        scratch_shapes=[pltpu.VMEM((tm, D_MODEL), BF16),
                        pltpu.VMEM((N_EXPERTS, tm // tb, 1, tb), F32),
                        pltpu.VMEM((N_EXPERTS, tm // tb, 1, tb), F32)],
        compiler_params=pltpu.CompilerParams(
            dimension_semantics=("parallel", "parallel", "arbitrary"),
            vmem_limit_bytes=VMEM_LIMIT),
        name="ffn_moe",
    )(x, p['norm_ffn'], p['w_router_t'], p['b_router_t'], p['w_gate_e'], p['w_up_e'],
      p['w_down_e'], p['norm_final'])


def _rope_tables(pos0, T, bb, tt):
    half = HEAD_DIM // 2
    inv_freq = ROPE_THETA ** (-jnp.arange(half, dtype=F32) / half)
    ang = (pos0 + jnp.arange(T)).astype(F32)[:, None] * inv_freq[None, :]
    cos, sin = jnp.cos(ang), jnp.sin(ang)
    cos = jnp.concatenate([cos, cos, cos, cos], axis=-1)
    sin = jnp.concatenate([-sin, sin, -sin, sin], axis=-1)

    def lay(a):
        a = a.reshape(T // tt, 1, tt, HEAD_W)
        return jnp.broadcast_to(a, (T // tt, bb, tt, HEAD_W)).reshape(T // tt, bb * tt, HEAD_W)

    return lay(cos), lay(sin)


def _pad_hist(state):
    return jnp.pad(state, ((0, 0), (0, 0), (HIST_PAD - state.shape[2], 0), (0, 0)))


_MATMUL_WEIGHTS = ('w_in', 'w_conv_out', 'w_pool', 'w_attn_out', 'w_o', 'w_gate_d', 'w_up_d',
                   'w_down_d', 'w_gate_e', 'w_up_e', 'w_down_e')
_ROW_VECTORS = ('norm_mix', 'pool_scale', 'lambda_q1', 'lambda_k1', 'lambda_q2', 'lambda_k2',
                'subln_g', 'norm_ffn')


def _params(w):
    p = dict(w)
    for n in _MATMUL_WEIGHTS:
        p[n] = w[n].astype(BF16)
    for n in _ROW_VECTORS:
        p[n] = w[n][:, None, :]
    p['w_router_t'] = jnp.swapaxes(w['w_router'], 1, 2)
    p['b_router_t'] = w['b_router'][:, :, None]
    p['norm_final'] = w['norm_final'][None, :]
    return p


def _token_mixer_prompt(x, layer, p, kv_bufs, *, tt=512, tq=512, tk=256):
    B, T, _ = x.shape
    cos, sin = _rope_tables(0, T, 1, tt)
    zero_hist = jnp.zeros((B, HIST_PAD, CONV_W), F32)
    mab, gc, qt, k16, vt, k32, v32, co, po = _mixer_in(
        x, zero_hist, zero_hist, cos, sin, layer, p, kv_bufs, bb=1, tt=tt, pos0=0, tk=tk)
    x = _attn_prompt2(qt, k16, vt, mab, gc, x, layer, p, tq=tq, tk=tk)
    return x, co[:, HIST_PAD - (CONV_K - 1):], po[:, HIST_PAD - POOL_HIST:], (k32, v32)


def _token_mixer_sample(x, cache_k, cache_v, state_conv, state_pool, layer, p, kv_bufs, *, bb):
    past = cache_k.shape[2]
    T = x.shape[1]
    cos, sin = _rope_tables(past, T, bb, T)
    mab, gc, q, k16, v16, k32, v32, co, po = _mixer_in(
        x, _pad_hist(state_conv), _pad_hist(state_pool), cos, sin, layer, p, kv_bufs,
        bb=bb, tt=T, pos0=past, tk=None)
    rows = cache_k.shape[:2] + (past * HEADS, HEAD_W)
    x = _attn_sample(q, cache_k.reshape(rows), cache_v.reshape(rows), k16, v16, mab, gc, x, layer, p)
    return x, co[:, HIST_PAD - (CONV_K - 1):], po[:, HIST_PAD - POOL_HIST:], (k32, v32)


def _channel_mixer(x, layer, p, *, final):
    if layer % 2 == 0:
        return _ffn_dense(x, layer, p, tm=512, final=final)
    tokens = x.shape[0] * x.shape[1]
    return _ffn_moe(x, layer, p, tm=min(2048, tokens), tb=1024, cms=(192, 256, 320, 384, 448, 512),
                    final=final)


def kernel(x_prompt, x_sample, cache_k, cache_v, state_conv, state_pool, norm_mix, w_in, conv_w,
           w_conv_out, w_pool, pool_scale, lambda_q1, lambda_k1, lambda_q2, lambda_k2, subln_g,
           w_attn_out, w_o, norm_ffn, w_gate_d, w_up_d, w_down_d, w_router, b_router, w_gate_e,
           w_up_e, w_down_e, norm_final):
    depth = w_in.shape[0]
    p = _params(dict(
        norm_mix=norm_mix, w_in=w_in, conv_w=conv_w, w_conv_out=w_conv_out, w_pool=w_pool,
        pool_scale=pool_scale, lambda_q1=lambda_q1, lambda_k1=lambda_k1, lambda_q2=lambda_q2,
        lambda_k2=lambda_k2, subln_g=subln_g, w_attn_out=w_attn_out, w_o=w_o, norm_ffn=norm_ffn,
        w_gate_d=w_gate_d, w_up_d=w_up_d, w_down_d=w_down_d, w_router=w_router,
        b_router=b_router, w_gate_e=w_gate_e, w_up_e=w_up_e, w_down_e=w_down_e,
        norm_final=norm_final))

    xp, xs = x_prompt, x_sample
    kv_p = kv_s = None
    states = [[] for _ in range(4)]
    for l in range(depth):
        final = l == depth - 1
        xp, cp, pp, kv_p = _token_mixer_prompt(xp, l, p, kv_p)
        xs, cs, ps, kv_s = _token_mixer_sample(xs, cache_k, cache_v, state_conv, state_pool, l, p,
                                               kv_s, bb=16)
        for lst, a in zip(states, (cp, pp, cs, ps)):
            lst.append(a)
        xp = _channel_mixer(xp, l, p, final=final)
        xs = _channel_mixer(xs, l, p, final=final)

    def heads_view(a):
        return a.reshape(a.shape[0], a.shape[1], a.shape[2] // HEADS, HEADS, HEAD_W)

    return (xp, xs, heads_view(kv_p[0]), heads_view(kv_p[1]), jnp.stack(states[0]),
            jnp.stack(states[1]), heads_view(kv_s[0]), heads_view(kv_s[1]),
            jnp.stack(states[2]), jnp.stack(states[3]))
```

```python
import functools
import math

import jax
import jax.numpy as jnp
from jax import lax
from jax.experimental import pallas as pl
from jax.experimental.pallas import tpu as pltpu

D_MODEL = 1024
CHUNK = 64
CONV_W = 512
CONV_K = 3
POOL_W = 512
POOL_GC = 128
POOL_WINDOWS = (2, 4, 8, 16)
POOL_HIST = 15
HEADS = 4
HEAD_DIM = 64
HEAD_W = 2 * HEAD_DIM
ATT_W = HEADS * HEAD_W
ROPE_THETA = 10000.0
N_EXPERTS = 8
EPS = 1e-6
SUBLN_EPS = 1e-5

C_XA, C_BA, C_CA, C_U, C_Q, C_K, C_V, C_GA, C_GB, C_GC, C_END = (
    0, 512, 1024, 1536, 2048, 2560, 3072, 3584, 4608, 5632, 6656)

HIST_PAD = 16
VMEM_LIMIT = 56 * 1024 * 1024
Q_SCALE = HEAD_DIM ** -0.5 * math.log2(math.e)
DENOM_ROWS = 16

F32 = jnp.float32
BF16 = jnp.bfloat16
NT_DIMS = (((1,), (1,)), ((), ()))
TN_DIMS = (((0,), (0,)), ((), ()))


def _const_spec(shape):
    nd = len(shape)
    return pl.BlockSpec(tuple(shape), lambda *_: (0,) * nd, pipeline_mode=pl.Buffered(1))


def _layer_spec(arr, layer):
    nd = arr.ndim
    return pl.BlockSpec((None,) + tuple(arr.shape[1:]), lambda *_: (layer,) + (0,) * (nd - 1),
                        pipeline_mode=pl.Buffered(1))


def _rms(x, g, eps):
    return x * lax.rsqrt(jnp.mean(x * x, axis=-1, keepdims=True) + eps) * g


def _mixer_in_kernel(*refs, bb, tt, pos0, tk, aliased):
    (x_ref, hc_ref, hp_ref, cos_ref, sin_ref, ng_ref, win_ref, cw_ref, wa_ref, wp_ref,
     ps_ref) = refs[:11]
    refs = refs[11 + (2 if aliased else 0):]
    (mab_ref, gc_ref, q_ref, k16_ref, v16_ref, k32_ref, v32_ref, co_ref, po_ref, cbuf,
     ubuf) = refs
    t = pl.program_id(1)
    m = bb * tt
    x = x_ref[...].reshape(m, D_MODEL)
    h = _rms(x, ng_ref[...], EPS).astype(BF16)

    def seg(lo, hi):
        return jnp.dot(h, win_ref[:, lo:hi], preferred_element_type=F32)

    @pl.when(t == 0)
    def _():
        cbuf[:, 0:HIST_PAD, :] = hc_ref[...]
        ubuf[:, 0:HIST_PAD, :] = hp_ref[...]

    z_ca, z_xa, z_ba, z_u = seg(C_CA, C_U), seg(C_XA, C_BA), seg(C_BA, C_CA), seg(C_U, C_Q)
    z_q, z_k, z_v = seg(C_Q, C_K), seg(C_K, C_V), seg(C_V, C_GA)
    cin = (z_ca * z_xa).reshape(bb, tt, CONV_W)
    cbuf[:, HIST_PAD:, :] = cin
    conv = cbuf[:, HIST_PAD - 2:HIST_PAD - 2 + tt, :] * cw_ref[0:1, :]
    conv = conv + cbuf[:, HIST_PAD - 1:HIST_PAD - 1 + tt, :] * cw_ref[1:2, :]
    conv = conv + cin * cw_ref[2:3, :]
    ya = jnp.dot((z_ba * conv.reshape(m, CONV_W)).astype(BF16), wa_ref[...],
                 preferred_element_type=F32)

    ubuf[:, HIST_PAD:, :] = z_u.reshape(bb, tt, POOL_W)
    pos = pos0 + t * tt + lax.broadcasted_iota(jnp.int32, (bb, tt, POOL_GC), 1)
    yb_parts = []
    for gi, win in enumerate(POOL_WINDOWS):
        lo, hi = gi * POOL_GC, (gi + 1) * POOL_GC
        cur = ubuf[:, HIST_PAD:HIST_PAD + tt, lo:hi]
        s = cur
        for k in range(1, win):
            s = s + ubuf[:, HIST_PAD - k:HIST_PAD - k + tt, lo:hi]
        cnt = jnp.minimum(pos + 1, win).astype(F32)
        pooled = s / cnt - cur
        yb_parts.append(jnp.dot(pooled.reshape(m, POOL_GC).astype(BF16), wp_ref[gi],
                                preferred_element_type=F32))
    yb = jnp.concatenate(yb_parts, axis=-1) * ps_ref[...]

    cos4 = jnp.concatenate([cos_ref[...]] * HEADS, axis=-1)
    sin4 = jnp.concatenate([sin_ref[...]] * HEADS, axis=-1)
    lane = lax.broadcasted_iota(jnp.int32, (m, ATT_W), 1)
    first_half = (lane & (HEAD_DIM // 2)) == 0

    def rope(z):
        swapped = jnp.where(first_half, pltpu.roll(z, ATT_W - HEAD_DIM // 2, axis=1),
                            pltpu.roll(z, HEAD_DIM // 2, axis=1))
        return z * cos4 + swapped * sin4

    q = rope(z_q) * Q_SCALE
    k = rope(z_k)
    v = z_v
    for hd in range(HEADS):
        lo, hi = hd * HEAD_W, (hd + 1) * HEAD_W
        k32_ref[:, pl.ds(hd, tt, stride=HEADS), :] = k[:, lo:hi].reshape(bb, tt, HEAD_W)
        v32_ref[:, pl.ds(hd, tt, stride=HEADS), :] = v[:, lo:hi].reshape(bb, tt, HEAD_W)
    k16_ref[...] = k.astype(BF16).reshape(bb, tt, ATT_W)
    if tk is None:
        q_ref[...] = q.astype(BF16).reshape(bb, tt, ATT_W)
        v16_ref[...] = v.astype(BF16).reshape(bb, tt, ATT_W)
    else:
        q_ref[0] = q.T.astype(BF16)
        for c in range(tt // tk):
            v16_ref[0, c] = v[c * tk:(c + 1) * tk, :].T.astype(BF16)

    ga = jax.nn.sigmoid(seg(C_GA, C_GB))
    gb = jax.nn.sigmoid(seg(C_GB, C_GC))
    mab_ref[...] = (ga * ya + gb * yb).astype(BF16).reshape(bb, tt, D_MODEL)
    gc_ref[...] = jax.nn.sigmoid(seg(C_GC, C_END)).astype(BF16).reshape(bb, tt, D_MODEL)

    last_c = cbuf[:, tt:tt + HIST_PAD, :]
    last_p = ubuf[:, tt:tt + HIST_PAD, :]
    co_ref[...] = last_c
    po_ref[...] = last_p
    cbuf[:, 0:HIST_PAD, :] = last_c
    ubuf[:, 0:HIST_PAD, :] = last_p


def _mixer_in(x, hist_c, hist_p, cos, sin, layer, p, kv_bufs, *, bb, tt, pos0, tk):
    B, T, _ = x.shape
    depth = p['w_in'].shape[0]
    assert B % bb == 0 and T % tt == 0 and tt >= HIST_PAD and tt % 8 == 0
    assert tk is None or (bb == 1 and tt % tk == 0)
    grid = (B // bb, T // tt)
    m = bb * tt

    def seq_spec(w):
        return pl.BlockSpec((bb, tt, w), lambda b, t: (b, t, 0))

    if hist_c.ndim == 4:
        hist_spec = pl.BlockSpec((None, bb, HIST_PAD, CONV_W), lambda b, t: (layer, b, 0, 0))
    else:
        hist_spec = pl.BlockSpec((bb, HIST_PAD, CONV_W), lambda b, t: (b, 0, 0))
    state_spec = pl.BlockSpec((bb, HIST_PAD, CONV_W), lambda b, t: (b, 0, 0))
    rope_spec = pl.BlockSpec((None, m, HEAD_W), lambda b, t: (t, 0, 0))
    kv_spec = pl.BlockSpec((None, bb, tt * HEADS, HEAD_W), lambda b, t: (layer, b, t, 0))
    kv_shape = jax.ShapeDtypeStruct((depth, B, T * HEADS, HEAD_W), F32)
    if tk is None:
        q_shape, q_spec = jax.ShapeDtypeStruct((B, T, ATT_W), BF16), seq_spec(ATT_W)
        v_shape, v_spec = q_shape, q_spec
    else:
        q_shape = jax.ShapeDtypeStruct((B, ATT_W, T), BF16)
        q_spec = pl.BlockSpec((1, ATT_W, tt), lambda b, t: (b, 0, t))
        v_shape = jax.ShapeDtypeStruct((B, T // tk, ATT_W, tk), BF16)
        v_spec = pl.BlockSpec((1, tt // tk, ATT_W, tk), lambda b, t: (b, t, 0, 0))

    out_shape = (
        jax.ShapeDtypeStruct((B, T, D_MODEL), BF16),
        jax.ShapeDtypeStruct((B, T, D_MODEL), BF16),
        q_shape,
        jax.ShapeDtypeStruct((B, T, ATT_W), BF16),
        v_shape,
        kv_shape, kv_shape,
        jax.ShapeDtypeStruct((B, HIST_PAD, CONV_W), F32),
        jax.ShapeDtypeStruct((B, HIST_PAD, POOL_W), F32),
    )
    out_specs = (seq_spec(D_MODEL), seq_spec(D_MODEL), q_spec, seq_spec(ATT_W), v_spec,
                 kv_spec, kv_spec, state_spec, state_spec)
    in_specs = [seq_spec(D_MODEL), hist_spec, hist_spec, rope_spec, rope_spec,
                _layer_spec(p['norm_mix'], layer), _layer_spec(p['w_in'], layer),
                _layer_spec(p['conv_w'], layer), _layer_spec(p['w_conv_out'], layer),
                _layer_spec(p['w_pool'], layer), _layer_spec(p['pool_scale'], layer)]
    args = [x, hist_c, hist_p, cos, sin, p['norm_mix'], p['w_in'], p['conv_w'], p['w_conv_out'],
            p['w_pool'], p['pool_scale']]
    aliases = {}
    if kv_bufs is not None:
        in_specs += [pl.BlockSpec(memory_space=pl.ANY)] * 2
        aliases = {len(args): 5, len(args) + 1: 6}
        args += list(kv_bufs)
    return pl.pallas_call(
        functools.partial(_mixer_in_kernel, bb=bb, tt=tt, pos0=pos0, tk=tk,
                          aliased=kv_bufs is not None),
        grid=grid, in_specs=in_specs, out_specs=out_specs, out_shape=out_shape,
        scratch_shapes=[pltpu.VMEM((bb, HIST_PAD + tt, CONV_W), F32),
                        pltpu.VMEM((bb, HIST_PAD + tt, POOL_W), F32)],
        input_output_aliases=aliases,
        compiler_params=pltpu.CompilerParams(dimension_semantics=("parallel", "arbitrary"),
                                             vmem_limit_bytes=VMEM_LIMIT),
        name="mixer_in",
    )(*args)


def _lambda(lq1_ref, lk1_ref, lq2_ref, lk2_ref, lam_init):
    a = jnp.exp(jnp.sum(lq1_ref[...] * lk1_ref[...], axis=-1, keepdims=True))
    b = jnp.exp(jnp.sum(lq2_ref[...] * lk2_ref[...], axis=-1, keepdims=True))
    return a - b + lam_init


def _merge_out(o, mab, gc, x, wc_ref, wo_ref):
    yc = jnp.dot(o.astype(BF16), wc_ref[...], preferred_element_type=F32)
    merged = mab.astype(F32) + gc.astype(F32) * yc
    return x + jnp.dot(merged.astype(BF16), wo_ref[...], preferred_element_type=F32)


def _attn_prompt_kernel(qt_ref, k_ref, vt_ref, mab_ref, gc_ref, x_ref, lq1_ref, lk1_ref, lq2_ref,
                        lk2_ref, sg_ref, wc_ref, wo_ref, o_ref, *, tq, lam_init):
    i = pl.program_id(1)
    lam = _lambda(lq1_ref, lk1_ref, lq2_ref, lk2_ref, lam_init)
    k_chunk = lax.broadcasted_iota(jnp.int32, (tq, 2 * tq), 0) // CHUNK
    q_chunk = (lax.broadcasted_iota(jnp.int32, (tq, 2 * tq), 1) % tq) // CHUNK
    visible = k_chunk <= q_chunk
    feat = lax.broadcasted_iota(jnp.int32, (HEAD_W, tq), 0)
    qs = []
    for hd in range(HEADS):
        qt = qt_ref[0, hd * HEAD_W:(hd + 1) * HEAD_W, :]
        zero = jnp.zeros_like(qt)
        qs.append(jnp.concatenate([jnp.where(feat < HEAD_DIM, qt, zero),
                                   jnp.where(feat < HEAD_DIM, zero, qt)], axis=1))

    def scores(j):
        off = pl.multiple_of(j * tq, tq)
        return [jnp.dot(k_ref[0, pl.ds(off, tq), hd * HEAD_W:(hd + 1) * HEAD_W], qs[hd],
                        preferred_element_type=F32) for hd in range(HEADS)]

    def softmax(s, m, l):
        m_new = jnp.maximum(m, jnp.max(s, axis=0, keepdims=True))
        pr = jnp.exp2(s - m_new)
        alpha = jnp.exp2(m - m_new)
        return m_new, alpha * l + jnp.sum(pr, axis=0, keepdims=True), alpha, pr.astype(BF16)

    def weighted_values(j, acc, alpha, pr):
        return [alpha[hd] * acc[hd] + jnp.dot(vt_ref[0, j, hd * HEAD_W:(hd + 1) * HEAD_W, :],
                                              pr[hd], preferred_element_type=F32)
                for hd in range(HEADS)]

    def unzip(rows):
        return tuple(list(col) for col in zip(*rows))

    m, l, alpha, pr = unzip([
        softmax(jnp.where(visible, s, -jnp.inf), jnp.full((1, 2 * tq), -jnp.inf, F32),
                jnp.zeros((1, 2 * tq), F32)) for s in scores(i)])
    acc = [jnp.zeros((HEAD_W, 2 * tq), F32) for _ in range(HEADS)]

    def body(j, carry):
        s_cur, pend, alpha, pr, m, l, acc = carry
        acc = weighted_values(pend, acc, alpha, pr)
        s_next = scores(j + 1)
        m, l, alpha, pr = unzip([softmax(s_cur[hd], m[hd], l[hd]) for hd in range(HEADS)])
        return s_next, j, alpha, pr, m, l, acc

    _, pend, alpha, pr, m, l, acc = lax.fori_loop(
        0, i, body, (scores(0), i, alpha, pr, m, l, acc))
    acc = weighted_values(pend, acc, alpha, pr)
    heads = []
    for hd in range(HEADS):
        ot = acc[hd] / l[hd]
        ot = ot[:, :tq] - lam * ot[:, tq:]
        ot = ot * lax.rsqrt(jnp.mean(ot * ot, axis=0, keepdims=True) + SUBLN_EPS)
        heads.append(ot.T * sg_ref[...] * (1.0 - lam_init))
    o = jnp.concatenate(heads, axis=-1)
    o_ref[0] = _merge_out(o, mab_ref[0], gc_ref[0], x_ref[0], wc_ref, wo_ref)


def _attn_prompt(qt, k16, vt, mab, gc, x, layer, p, *, tq):
    B, T, _ = x.shape
    assert T % tq == 0 and tq % CHUNK == 0 and vt.shape[3] == tq
    lam_init = 0.8 - 0.6 * math.exp(-0.3 * layer)

    def tile_spec(w):
        return pl.BlockSpec((1, tq, w), lambda b, i: (b, i, 0))

    lam_specs = [_layer_spec(p[n], layer) for n in ('lambda_q1', 'lambda_k1', 'lambda_q2', 'lambda_k2')]
    return pl.pallas_call(
        functools.partial(_attn_prompt_kernel, tq=tq, lam_init=lam_init),
        grid=(B, T // tq),
        in_specs=[pl.BlockSpec((1, ATT_W, tq), lambda b, i: (b, 0, i)),
                  pl.BlockSpec((1, T, ATT_W), lambda b, i: (b, 0, 0)),
                  pl.BlockSpec((1, T // tq, ATT_W, tq), lambda b, i: (b, 0, 0, 0)),
                  tile_spec(D_MODEL), tile_spec(D_MODEL), tile_spec(D_MODEL)] + lam_specs +
                 [_layer_spec(p['subln_g'], layer), _layer_spec(p['w_attn_out'], layer),
                  _layer_spec(p['w_o'], layer)],
        out_specs=tile_spec(D_MODEL),
        out_shape=jax.ShapeDtypeStruct((B, T, D_MODEL), F32),
        compiler_params=pltpu.CompilerParams(dimension_semantics=("parallel", "arbitrary"),
                                             vmem_limit_bytes=VMEM_LIMIT),
        name="attn_prompt",
    )(qt, k16, vt, mab, gc, x, p['lambda_q1'], p['lambda_k1'], p['lambda_q2'], p['lambda_k2'],
      p['subln_g'], p['w_attn_out'], p['w_o'])


def _attn_prompt2_kernel(qt_ref, k_ref, vt_ref, mab_ref, gc_ref, x_ref, lq1_ref, lk1_ref, lq2_ref,
                         lk2_ref, sg_ref, wc_ref, wo_ref, o_ref, p_scr, acc_scr, *, tq, tk,
                         lam_init):
    i = pl.program_id(1)
    n_diag = tq // tk
    lam = _lambda(lq1_ref, lk1_ref, lq2_ref, lk2_ref, lam_init)
    feat = lax.broadcasted_iota(jnp.int32, (HEAD_W, tq), 0)
    qs = []
    for hd in range(HEADS):
        qt = qt_ref[0, hd * HEAD_W:(hd + 1) * HEAD_W, :]
        zero = jnp.zeros_like(qt)
        qs.append(jnp.concatenate([jnp.where(feat < HEAD_DIM, qt, zero),
                                   jnp.where(feat < HEAD_DIM, zero, qt)], axis=1))

    def scores(j):
        off = pl.multiple_of(j * tk, tk)
        return [jnp.dot(k_ref[0, pl.ds(off, tk), hd * HEAD_W:(hd + 1) * HEAD_W], qs[hd],
                        preferred_element_type=F32) for hd in range(HEADS)]

    def softmax(s, m):
        m_out, alpha_out = [], []
        for hd in range(HEADS):
            m_new = jnp.maximum(m[hd], jnp.max(s[hd], axis=0, keepdims=True))
            p_scr[hd] = jnp.exp2(s[hd] - m_new).astype(BF16)
            m_out.append(m_new)
            alpha_out.append(jnp.exp2(m[hd] - m_new))
        return m_out, alpha_out

    ones_rows = jnp.ones((DENOM_ROWS, tk), BF16)

    def flush(j, alpha):
        for hd in range(HEADS):
            vt1 = jnp.concatenate([vt_ref[0, j, hd * HEAD_W:(hd + 1) * HEAD_W, :], ones_rows],
                                  axis=0)
            acc_scr[hd] = alpha[hd] * acc_scr[hd] + jnp.dot(vt1, p_scr[hd],
                                                            preferred_element_type=F32)

    acc_scr[...] = jnp.zeros_like(acc_scr)
    m = [jnp.full((1, 2 * tq), -jnp.inf, F32)] * HEADS
    alpha = [jnp.zeros((1, 2 * tq), F32)] * HEADS
    q_chunk = (lax.broadcasted_iota(jnp.int32, (tk, 2 * tq), 1) % tq) // CHUNK
    k_row = lax.broadcasted_iota(jnp.int32, (tk, 2 * tq), 0)
    for d in range(n_diag):
        j = i * n_diag + d
        s = scores(j)
        if d > 0:
            flush(j - 1, alpha)
        visible = (k_row + d * tk) // CHUNK <= q_chunk
        m, alpha = softmax([jnp.where(visible, sh, -jnp.inf) for sh in s], m)

    def body(j, carry):
        pend, alpha, m = carry
        s = scores(j)
        flush(pend, alpha)
        m, alpha = softmax(s, m)
        return j, alpha, m

    pend, alpha, m = lax.fori_loop(0, i * n_diag, body, (i * n_diag + n_diag - 1, alpha, m))
    flush(pend, alpha)
    heads = []
    for hd in range(HEADS):
        ot = acc_scr[hd, 0:HEAD_W, :] / acc_scr[hd, HEAD_W:HEAD_W + 1, :]
        ot = ot[:, :tq] - lam * ot[:, tq:]
        ot = ot * lax.rsqrt(jnp.mean(ot * ot, axis=0, keepdims=True) + SUBLN_EPS)
        heads.append(ot.T * sg_ref[...] * (1.0 - lam_init))
    o = jnp.concatenate(heads, axis=-1)
    o_ref[0] = _merge_out(o, mab_ref[0], gc_ref[0], x_ref[0], wc_ref, wo_ref)


def _attn_prompt2(qt, k16, vt, mab, gc, x, layer, p, *, tq, tk):
    B, T, _ = x.shape
    assert T % tq == 0 and tq % tk == 0 and tk % CHUNK == 0 and vt.shape[3] == tk
    lam_init = 0.8 - 0.6 * math.exp(-0.3 * layer)

    def tile_spec(w):
        return pl.BlockSpec((1, tq, w), lambda b, i: (b, i, 0))

    lam_specs = [_layer_spec(p[n], layer) for n in ('lambda_q1', 'lambda_k1', 'lambda_q2', 'lambda_k2')]
    return pl.pallas_call(
        functools.partial(_attn_prompt2_kernel, tq=tq, tk=tk, lam_init=lam_init),
        grid=(B, T // tq),
        in_specs=[pl.BlockSpec((1, ATT_W, tq), lambda b, i: (b, 0, i)),
                  pl.BlockSpec((1, T, ATT_W), lambda b, i: (b, 0, 0)),
                  pl.BlockSpec((1, T // tk, ATT_W, tk), lambda b, i: (b, 0, 0, 0)),
                  tile_spec(D_MODEL), tile_spec(D_MODEL), tile_spec(D_MODEL)] + lam_specs +
                 [_layer_spec(p['subln_g'], layer), _layer_spec(p['w_attn_out'], layer),
                  _layer_spec(p['w_o'], layer)],
        out_specs=tile_spec(D_MODEL),
        out_shape=jax.ShapeDtypeStruct((B, T, D_MODEL), F32),
        scratch_shapes=[pltpu.VMEM((HEADS, tk, 2 * tq), BF16),
                        pltpu.VMEM((HEADS, HEAD_W + DENOM_ROWS, 2 * tq), F32)],
        compiler_params=pltpu.CompilerParams(dimension_semantics=("parallel", "arbitrary"),
                                             vmem_limit_bytes=VMEM_LIMIT),
        name="attn_prompt",
    )(qt, k16, vt, mab, gc, x, p['lambda_q1'], p['lambda_k1'], p['lambda_q2'], p['lambda_k2'],
      p['subln_g'], p['w_attn_out'], p['w_o'])


def _stack_components(qh):
    lane = lax.broadcasted_iota(jnp.int32, qh.shape, 1)
    zero = jnp.zeros_like(qh)
    return jnp.concatenate([jnp.where(lane < HEAD_DIM, qh, zero),
                            jnp.where(lane < HEAD_DIM, zero, qh)], axis=0)


def _softmax_step(carry, s, vj):
    m, l, acc = carry
    m_new = jnp.maximum(m, jnp.max(s, axis=-1, keepdims=True))
    pr = jnp.exp2(s - m_new)
    alpha = jnp.exp2(m - m_new)
    l = alpha * l + jnp.sum(pr, axis=-1, keepdims=True)
    acc = alpha * acc + jnp.dot(pr.astype(BF16), vj, preferred_element_type=F32)
    return m_new, l, acc


def _attn_sample_kernel(q_ref, kp_ref, vp_ref, kn_ref, vn_ref, mab_ref, gc_ref, x_ref, lq1_ref,
                        lk1_ref, lq2_ref, lk2_ref, sg_ref, wc_ref, wo_ref, o_ref, *, tq, past,
                        lam_init):
    lam = _lambda(lq1_ref, lk1_ref, lq2_ref, lk2_ref, lam_init)
    q_chunk = (past + lax.broadcasted_iota(jnp.int32, (2 * tq, 1), 0) % tq) // CHUNK
    vis_past = (lax.broadcasted_iota(jnp.int32, (2 * tq, past), 1) // CHUNK) <= q_chunk
    vis_new = ((past + lax.broadcasted_iota(jnp.int32, (2 * tq, tq), 1)) // CHUNK) <= q_chunk
    heads = []
    for hd in range(HEADS):
        lo, hi = hd * HEAD_W, (hd + 1) * HEAD_W
        qs = _stack_components(q_ref[0, :, lo:hi])
        carry = (jnp.full((2 * tq, 1), -jnp.inf, F32), jnp.zeros((2 * tq, 1), F32),
                 jnp.zeros((2 * tq, HEAD_W), F32))
        s = lax.dot_general(qs, kn_ref[0, :, lo:hi], NT_DIMS, preferred_element_type=F32)
        carry = _softmax_step(carry, jnp.where(vis_new, s, -jnp.inf), vn_ref[0, :, lo:hi])
        kp = kp_ref[pl.ds(hd, past, stride=HEADS), :].astype(BF16)
        vp = vp_ref[pl.ds(hd, past, stride=HEADS), :].astype(BF16)
        s = lax.dot_general(qs, kp, NT_DIMS, preferred_element_type=F32)
        _, l, acc = _softmax_step(carry, jnp.where(vis_past, s, -jnp.inf), vp)
        o = acc / l
        o = o[:tq] - lam * o[tq:]
        heads.append(_rms(o, sg_ref[...], SUBLN_EPS) * (1.0 - lam_init))
    o = jnp.concatenate(heads, axis=-1)
    o_ref[0] = _merge_out(o, mab_ref[0], gc_ref[0], x_ref[0], wc_ref, wo_ref)


def _attn_sample(q, cache_k, cache_v, k16, v16, mab, gc, x, layer, p):
    B, T, _ = x.shape
    past = cache_k.shape[2] // HEADS
    lam_init = 0.8 - 0.6 * math.exp(-0.3 * layer)

    def tile_spec(w):
        return pl.BlockSpec((1, T, w), lambda b: (b, 0, 0))

    past_spec = pl.BlockSpec((None, None, past * HEADS, HEAD_W), lambda b: (layer, b, 0, 0))
    lam_specs = [_layer_spec(p[n], layer) for n in ('lambda_q1', 'lambda_k1', 'lambda_q2', 'lambda_k2')]
    return pl.pallas_call(
        functools.partial(_attn_sample_kernel, tq=T, past=past, lam_init=lam_init),
        grid=(B,),
        in_specs=[tile_spec(ATT_W), past_spec, past_spec, tile_spec(ATT_W), tile_spec(ATT_W),
                  tile_spec(D_MODEL), tile_spec(D_MODEL), tile_spec(D_MODEL)] + lam_specs +
                 [_layer_spec(p['subln_g'], layer), _layer_spec(p['w_attn_out'], layer),
                  _layer_spec(p['w_o'], layer)],
        out_specs=tile_spec(D_MODEL),
        out_shape=jax.ShapeDtypeStruct((B, T, D_MODEL), F32),
        compiler_params=pltpu.CompilerParams(dimension_semantics=("parallel",),
                                             vmem_limit_bytes=VMEM_LIMIT),
        name="attn_sample",
    )(q, cache_k, cache_v, k16, v16, mab, gc, x, p['lambda_q1'], p['lambda_k1'], p['lambda_q2'],
      p['lambda_k2'], p['subln_g'], p['w_attn_out'], p['w_o'])


def _chunks(n, step):
    return [(lo, min(lo + step, n)) for lo in range(0, n, step)]


def _ffn_dense_kernel(x_ref, ng_ref, wg_ref, wu_ref, wd_ref, nf_ref, o_ref, *, final):
    bb, tt, _ = x_ref.shape
    x = x_ref[...].reshape(bb * tt, D_MODEL)
    h = _rms(x, ng_ref[...], EPS).astype(BF16)
    y = x
    for lo, hi in _chunks(wg_ref.shape[1], 1024):
        g = jnp.dot(h, wg_ref[:, lo:hi], preferred_element_type=F32)
        u = jnp.dot(h, wu_ref[:, lo:hi], preferred_element_type=F32)
        a = (g * jax.nn.sigmoid(g) * u).astype(BF16)
        y = y + jnp.dot(a, wd_ref[lo:hi, :], preferred_element_type=F32)
    if final:
        y = _rms(y, nf_ref[...], EPS)
    o_ref[...] = y.reshape(bb, tt, D_MODEL)


def _token_blocks(B, T, tm):
    if T >= tm:
        assert T % tm == 0
        return 1, tm
    assert tm % T == 0 and B % (tm // T) == 0
    return tm // T, T


def _ffn_dense(x, layer, p, *, tm, final):
    B, T, _ = x.shape
    bb, tt = _token_blocks(B, T, tm)
    d = layer // 2
    row_spec = pl.BlockSpec((bb, tt, D_MODEL), lambda b, t: (b, t, 0))
    return pl.pallas_call(
        functools.partial(_ffn_dense_kernel, final=final),
        grid=(B // bb, T // tt),
        in_specs=[row_spec, _layer_spec(p['norm_ffn'], layer), _layer_spec(p['w_gate_d'], d),
                  _layer_spec(p['w_up_d'], d), _layer_spec(p['w_down_d'], d),
                  _const_spec(p['norm_final'].shape)],
        out_specs=row_spec,
        out_shape=jax.ShapeDtypeStruct(x.shape, F32),
        compiler_params=pltpu.CompilerParams(dimension_semantics=("parallel", "parallel"),
                                             vmem_limit_bytes=VMEM_LIMIT),
        name="ffn_dense",
    )(x, p['norm_ffn'], p['w_gate_d'], p['w_up_d'], p['w_down_d'], p['norm_final'])


def _route_top2(logits):
    row = lax.broadcasted_iota(jnp.int32, logits.shape, 0).astype(F32)
    big = float(N_EXPERTS)
    m1 = jnp.max(logits, axis=0, keepdims=True)
    i1 = jnp.min(jnp.where(logits == m1, row, big), axis=0, keepdims=True)
    rest = jnp.where(row == i1, -jnp.inf, logits)
    m2 = jnp.max(rest, axis=0, keepdims=True)
    i2 = jnp.min(jnp.where(rest == m2, row, big), axis=0, keepdims=True)
    e2 = jnp.exp(m2 - m1)
    w1 = 1.0 / (1.0 + e2)
    w2 = e2 / (1.0 + e2)
    return jnp.where(row == i1, w1, 0.0) + jnp.where(row == i2, w2, 0.0)


def _ffn_moe_kernel(x_ref, ng_ref, wrt_ref, brt_ref, wg_ref, wu_ref, wd_ref, nf_ref, o_ref,
                    h_scr, wts_scr, rank_scr, *, final, tb, cms):
    e = pl.program_id(2)
    bb, tt, _ = x_ref.shape
    n_tok = bb * tt
    sub_blocks = _chunks(n_tok, tb)

    @pl.when(e == 0)
    def _():
        x = x_ref[...].reshape(n_tok, D_MODEL)
        h = _rms(x, ng_ref[...], EPS)
        logits = lax.dot_general(wrt_ref[...], h, NT_DIMS, preferred_element_type=F32,
                                 precision=lax.Precision.HIGHEST) + brt_ref[...]
        wts = _route_top2(logits)
        sel = (wts > 0.0).astype(BF16)
        before = (lax.broadcasted_iota(jnp.int32, (tb, tb), 0)
                  < lax.broadcasted_iota(jnp.int32, (tb, tb), 1)).astype(BF16)
        rank = jnp.concatenate(
            [jnp.dot(sel[:, lo:hi], before, preferred_element_type=F32) for lo, hi in sub_blocks],
            axis=1)
        for ee in range(N_EXPERTS):
            for sb, (lo, hi) in enumerate(sub_blocks):
                wts_scr[ee, sb] = wts[ee:ee + 1, lo:hi]
                rank_scr[ee, sb] = rank[ee:ee + 1, lo:hi]
        h_scr[...] = h.astype(BF16)
        o_ref[...] = x_ref[...]

    cm_max = cms[-1]

    def sub_block(sb, carry):
        w_row = wts_scr[e, sb]
        r_row = rank_scr[e, sb]
        sel = w_row > 0.0
        n_sel = jnp.sum(sel.astype(F32)).astype(jnp.int32)
        row0 = pl.multiple_of(sb * tb, tb)

        def run_chunk(cm, base):
            rows = (lax.broadcasted_iota(jnp.int32, (cm, tb), 0) + base).astype(F32)
            hit = (r_row == rows) & sel
            gather = jnp.where(hit, 1.0, 0.0).astype(BF16)
            xg = jnp.dot(gather, h_scr[pl.ds(row0, tb), :],
                         preferred_element_type=F32).astype(BF16)
            g = jnp.dot(xg, wg_ref[...], preferred_element_type=F32)
            u = jnp.dot(xg, wu_ref[...], preferred_element_type=F32)
            a = (g * jax.nn.sigmoid(g) * u).astype(BF16)
            y = jnp.dot(a, wd_ref[...], preferred_element_type=F32).astype(BF16)
            scatter = jnp.where(hit, w_row, 0.0).astype(BF16)
            upd = lax.dot_general(scatter, y, TN_DIMS, preferred_element_type=F32)
            if bb == 1:
                o_ref[0, pl.ds(row0, tb), :] += upd
            else:
                o_ref[pl.ds(sb * (tb // tt), tb // tt)] += upd.reshape(tb // tt, tt, D_MODEL)

        def chunk(c, carry):
            left = n_sel - c * cm_max
            size_idx = sum((left > cm).astype(jnp.int32) for cm in cms[:-1])
            lax.switch(size_idx, [functools.partial(run_chunk, cm) for cm in cms], c * cm_max)
            return carry

        n_chunks = sum((n_sel > c * cm_max).astype(jnp.int32) for c in range(-(-tb // cm_max)))
        lax.fori_loop(0, n_chunks, chunk, 0)
        return carry

    lax.fori_loop(0, n_tok // tb, sub_block, 0)

    if final:
        @pl.when(e == N_EXPERTS - 1)
        def _():
            y = o_ref[...].reshape(n_tok, D_MODEL)
            o_ref[...] = _rms(y, nf_ref[...], EPS).reshape(bb, tt, D_MODEL)


def _ffn_moe(x, layer, p, *, tm, tb, cms, final):
    B, T, _ = x.shape
    bb, tt = _token_blocks(B, T, tm)
    assert tm % tb == 0 and (bb == 1 or tb % tt == 0)
    mo = layer // 2
    row_spec = pl.BlockSpec((bb, tt, D_MODEL), lambda b, t, e: (b, t, 0))

    def expert_spec(arr):
        return pl.BlockSpec((None, None) + tuple(arr.shape[2:]), lambda b, t, e: (mo, e, 0, 0))

    return pl.pallas_call(
        functools.partial(_ffn_moe_kernel, final=final, tb=tb, cms=cms),
        grid=(B // bb, T // tt, N_EXPERTS),
        in_specs=[pl.BlockSpec((bb, tt, D_MODEL), lambda b, t, e: (b, t, 0),
                               pipeline_mode=pl.Buffered(1)),
                  _layer_spec(p['norm_ffn'], layer), _layer_spec(p['w_router_t'], mo),
                  _layer_spec(p['b_router_t'], mo), expert_spec(p['w_gate_e']),
                  expert_spec(p['w_up_e']), expert_spec(p['w_down_e']),
                  _const_spec(p['norm_final'].shape)],
        out_specs=row_spec,
        out_shape=jax.ShapeDtypeStruct(x.shape, F32),
        scratch_shapes=[pltpu.VMEM((tm, D_MODEL), BF16),
                        pltpu.VMEM((N_EXPERTS, tm // tb, 1, tb), F32),
                        pltpu.VMEM((N_EXPERTS, tm // tb, 1, tb), F32)],
        compiler_params=pltpu.CompilerParams(
            dimension_semantics=("parallel", "parallel", "arbitrary"),
            vmem_limit_bytes=VMEM_LIMIT),
        name="ffn_moe",
    )(x, p['norm_ffn'], p['w_router_t'], p['b_router_t'], p['w_gate_e'], p['w_up_e'],
      p['w_down_e'], p['norm_final'])


def _rope_tables(pos0, T, bb, tt):
    half = HEAD_DIM // 2
    inv_freq = ROPE_THETA ** (-jnp.arange(half, dtype=F32) / half)
    ang = (pos0 + jnp.arange(T)).astype(F32)[:, None] * inv_freq[None, :]
    cos, sin = jnp.cos(ang), jnp.sin(ang)
    cos = jnp.concatenate([cos, cos, cos, cos], axis=-1)
    sin = jnp.concatenate([-sin, sin, -sin, sin], axis=-1)

    def lay(a):
        a = a.reshape(T // tt, 1, tt, HEAD_W)
        return jnp.broadcast_to(a, (T // tt, bb, tt, HEAD_W)).reshape(T // tt, bb * tt, HEAD_W)

    return lay(cos), lay(sin)


def _pad_hist(state):
    return jnp.pad(state, ((0, 0), (0, 0), (HIST_PAD - state.shape[2], 0), (0, 0)))


_MATMUL_WEIGHTS = ('w_in', 'w_conv_out', 'w_pool', 'w_attn_out', 'w_o', 'w_gate_d', 'w_up_d',
                   'w_down_d', 'w_gate_e', 'w_up_e', 'w_down_e')
_ROW_VECTORS = ('norm_mix', 'pool_scale', 'lambda_q1', 'lambda_k1', 'lambda_q2', 'lambda_k2',
                'subln_g', 'norm_ffn')


def _params(w):
    p = dict(w)
    for n in _MATMUL_WEIGHTS:
        p[n] = w[n].astype(BF16)
    for n in _ROW_VECTORS:
        p[n] = w[n][:, None, :]
    p['w_router_t'] = jnp.swapaxes(w['w_router'], 1, 2)
    p['b_router_t'] = w['b_router'][:, :, None]
    p['norm_final'] = w['norm_final'][None, :]
    return p


def _token_mixer_prompt(x, layer, p, kv_bufs, *, tt=512, tq=512, tk=512):
    B, T, _ = x.shape
    cos, sin = _rope_tables(0, T, 1, tt)
    zero_hist = jnp.zeros((B, HIST_PAD, CONV_W), F32)
    mab, gc, qt, k16, vt, k32, v32, co, po = _mixer_in(
        x, zero_hist, zero_hist, cos, sin, layer, p, kv_bufs, bb=1, tt=tt, pos0=0, tk=tk)
    x = _attn_prompt2(qt, k16, vt, mab, gc, x, layer, p, tq=tq, tk=tk)
    return x, co[:, HIST_PAD - (CONV_K - 1):], po[:, HIST_PAD - POOL_HIST:], (k32, v32)


def _token_mixer_sample(x, cache_k, cache_v, state_conv, state_pool, layer, p, kv_bufs, *, bb):
    past = cache_k.shape[2]
    T = x.shape[1]
    cos, sin = _rope_tables(past, T, bb, T)
    mab, gc, q, k16, v16, k32, v32, co, po = _mixer_in(
        x, _pad_hist(state_conv), _pad_hist(state_pool), cos, sin, layer, p, kv_bufs,
        bb=bb, tt=T, pos0=past, tk=None)
    rows = cache_k.shape[:2] + (past * HEADS, HEAD_W)
    x = _attn_sample(q, cache_k.reshape(rows), cache_v.reshape(rows), k16, v16, mab, gc, x, layer, p)
    return x, co[:, HIST_PAD - (CONV_K - 1):], po[:, HIST_PAD - POOL_HIST:], (k32, v32)


def _channel_mixer(x, layer, p, *, final):
    if layer % 2 == 0:
        return _ffn_dense(x, layer, p, tm=512, final=final)
    tokens = x.shape[0] * x.shape[1]
    return _ffn_moe(x, layer, p, tm=min(2048, tokens), tb=512, cms=(128, 160, 192, 224, 256),
                    final=final)


def kernel(x_prompt, x_sample, cache_k, cache_v, state_conv, state_pool, norm_mix, w_in, conv_w,
           w_conv_out, w_pool, pool_scale, lambda_q1, lambda_k1, lambda_q2, lambda_k2, subln_g,
           w_attn_out, w_o, norm_ffn, w_gate_d, w_up_d, w_down_d, w_router, b_router, w_gate_e,
           w_up_e, w_down_e, norm_final):
    depth = w_in.shape[0]
    p = _params(dict(
        norm_mix=norm_mix, w_in=w_in, conv_w=conv_w, w_conv_out=w_conv_out, w_pool=w_pool,
        pool_scale=pool_scale, lambda_q1=lambda_q1, lambda_k1=lambda_k1, lambda_q2=lambda_q2,
        lambda_k2=lambda_k2, subln_g=subln_g, w_attn_out=w_attn_out, w_o=w_o, norm_ffn=norm_ffn,
        w_gate_d=w_gate_d, w_up_d=w_up_d, w_down_d=w_down_d, w_router=w_router,
        b_router=b_router, w_gate_e=w_gate_e, w_up_e=w_up_e, w_down_e=w_down_e,
        norm_final=norm_final))

    xp, xs = x_prompt, x_sample
    kv_p = kv_s = None
    states = [[] for _ in range(4)]
    for l in range(depth):
        final = l == depth - 1
        xp, cp, pp, kv_p = _token_mixer_prompt(xp, l, p, kv_p)
        xs, cs, ps, kv_s = _token_mixer_sample(xs, cache_k, cache_v, state_conv, state_pool, l, p,
                                               kv_s, bb=16)
        for lst, a in zip(states, (cp, pp, cs, ps)):
            lst.append(a)
        xp = _channel_mixer(xp, l, p, final=final)
        xs = _channel_mixer(xs, l, p, final=final)

    def heads_view(a):
        return a.reshape(a.shape[0], a.shape[1], a.shape[2] // HEADS, HEADS, HEAD_W)

    return (xp, xs, heads_view(kv_p[0]), heads_view(kv_p[1]), jnp.stack(states[0]),
            jnp.stack(states[1]), heads_view(kv_s[0]), heads_view(kv_s[1]),
            jnp.stack(states[2]), jnp.stack(states[3]))
```

```python
import functools
import math

import jax
import jax.numpy as jnp
from jax import lax
from jax.experimental import pallas as pl
from jax.experimental.pallas import tpu as pltpu

D_MODEL = 1024
CHUNK = 64
CONV_W = 512
CONV_K = 3
POOL_W = 512
POOL_GC = 128
POOL_WINDOWS = (2, 4, 8, 16)
POOL_HIST = 15
HEADS = 4
HEAD_DIM = 64
HEAD_W = 2 * HEAD_DIM
ATT_W = HEADS * HEAD_W
ROPE_THETA = 10000.0
N_EXPERTS = 8
EPS = 1e-6
SUBLN_EPS = 1e-5

C_XA, C_BA, C_CA, C_U, C_Q, C_K, C_V, C_GA, C_GB, C_GC, C_END = (
    0, 512, 1024, 1536, 2048, 2560, 3072, 3584, 4608, 5632, 6656)

HIST_PAD = 16
Q_SCALE = HEAD_DIM ** -0.5 * math.log2(math.e)
DENOM_ROWS = 16

VMEM_LIMIT = 56 * 1024 * 1024
PROMPT_TILE = 512
KEY_TILE = 512
SAMPLE_MIXER_SEQS = 16
SAMPLE_ATTN_SEQS = 4
DENSE_TOKENS = 512
MOE_TOKENS = 2048
MOE_SUB_TOKENS = 512
MOE_CHUNK_ROWS = (128, 160, 192, 224, 256)

F32 = jnp.float32
BF16 = jnp.bfloat16
NT_DIMS = (((1,), (1,)), ((), ()))
TN_DIMS = (((0,), (0,)), ((), ()))


def _const_spec(shape):
    nd = len(shape)
    return pl.BlockSpec(tuple(shape), lambda *_: (0,) * nd, pipeline_mode=pl.Buffered(1))


def _layer_spec(arr, layer):
    nd = arr.ndim
    return pl.BlockSpec((None,) + tuple(arr.shape[1:]), lambda *_: (layer,) + (0,) * (nd - 1),
                        pipeline_mode=pl.Buffered(1))


def _rms(x, g, eps):
    return x * lax.rsqrt(jnp.mean(x * x, axis=-1, keepdims=True) + eps) * g


def _chunks(n, step):
    return [(lo, min(lo + step, n)) for lo in range(0, n, step)]


def _mixer_in_kernel(*refs, bb, tt, pos0, tk, aliased):
    (x_ref, hc_ref, hp_ref, cos_ref, sin_ref, ng_ref, win_ref, cw_ref, wa_ref, wp_ref,
     ps_ref) = refs[:11]
    refs = refs[11 + (2 if aliased else 0):]
    (mab_ref, gc_ref, q_ref, k16_ref, v16_ref, k32_ref, v32_ref, co_ref, po_ref, cbuf,
     ubuf) = refs
    t = pl.program_id(1)
    m = bb * tt
    x = x_ref[...].reshape(m, D_MODEL)
    h = _rms(x, ng_ref[...], EPS).astype(BF16)

    def seg(lo, hi):
        return jnp.dot(h, win_ref[:, lo:hi], preferred_element_type=F32)

    @pl.when(t == 0)
    def _():
        cbuf[:, 0:HIST_PAD, :] = hc_ref[...]
        ubuf[:, 0:HIST_PAD, :] = hp_ref[...]

    z_ca, z_xa, z_ba, z_u = seg(C_CA, C_U), seg(C_XA, C_BA), seg(C_BA, C_CA), seg(C_U, C_Q)
    z_q, z_k, z_v = seg(C_Q, C_K), seg(C_K, C_V), seg(C_V, C_GA)

    cin = (z_ca * z_xa).reshape(bb, tt, CONV_W)
    cbuf[:, HIST_PAD:, :] = cin
    conv = cbuf[:, HIST_PAD - 2:HIST_PAD - 2 + tt, :] * cw_ref[0:1, :]
    conv = conv + cbuf[:, HIST_PAD - 1:HIST_PAD - 1 + tt, :] * cw_ref[1:2, :]
    conv = conv + cin * cw_ref[2:3, :]
    ya = jnp.dot((z_ba * conv.reshape(m, CONV_W)).astype(BF16), wa_ref[...],
                 preferred_element_type=F32)

    ubuf[:, HIST_PAD:, :] = z_u.reshape(bb, tt, POOL_W)
    pos = pos0 + t * tt + lax.broadcasted_iota(jnp.int32, (bb, tt, POOL_GC), 1)
    yb_parts = []
    for gi, win in enumerate(POOL_WINDOWS):
        lo, hi = gi * POOL_GC, (gi + 1) * POOL_GC
        cur = ubuf[:, HIST_PAD:HIST_PAD + tt, lo:hi]
        s = cur
        for k in range(1, win):
            s = s + ubuf[:, HIST_PAD - k:HIST_PAD - k + tt, lo:hi]
        cnt = jnp.minimum(pos + 1, win).astype(F32)
        pooled = s / cnt - cur
        yb_parts.append(jnp.dot(pooled.reshape(m, POOL_GC).astype(BF16), wp_ref[gi],
                                preferred_element_type=F32))
    yb = jnp.concatenate(yb_parts, axis=-1) * ps_ref[...]

    cos4 = jnp.concatenate([cos_ref[...]] * HEADS, axis=-1)
    sin4 = jnp.concatenate([sin_ref[...]] * HEADS, axis=-1)
    lane = lax.broadcasted_iota(jnp.int32, (m, ATT_W), 1)
    first_half = (lane & (HEAD_DIM // 2)) == 0

    def rope(z):
        swapped = jnp.where(first_half, pltpu.roll(z, ATT_W - HEAD_DIM // 2, axis=1),
                            pltpu.roll(z, HEAD_DIM // 2, axis=1))
        return z * cos4 + swapped * sin4

    q = rope(z_q) * Q_SCALE
    k = rope(z_k)
    v = z_v
    for hd in range(HEADS):
        lo, hi = hd * HEAD_W, (hd + 1) * HEAD_W
        k32_ref[:, pl.ds(hd, tt, stride=HEADS), :] = k[:, lo:hi].reshape(bb, tt, HEAD_W)
        v32_ref[:, pl.ds(hd, tt, stride=HEADS), :] = v[:, lo:hi].reshape(bb, tt, HEAD_W)
    k16_ref[...] = k.astype(BF16).reshape(bb, tt, ATT_W)
    if tk is None:
        q_ref[...] = q.astype(BF16).reshape(bb, tt, ATT_W)
        v16_ref[...] = v.astype(BF16).reshape(bb, tt, ATT_W)
    else:
        q_ref[0] = q.T.astype(BF16)
        for c in range(tt // tk):
            v16_ref[0, c] = v[c * tk:(c + 1) * tk, :].T.astype(BF16)

    ga = jax.nn.sigmoid(seg(C_GA, C_GB))
    gb = jax.nn.sigmoid(seg(C_GB, C_GC))
    mab_ref[...] = (ga * ya + gb * yb).astype(BF16).reshape(bb, tt, D_MODEL)
    gc_ref[...] = jax.nn.sigmoid(seg(C_GC, C_END)).astype(BF16).reshape(bb, tt, D_MODEL)

    last_c = cbuf[:, tt:tt + HIST_PAD, :]
    last_p = ubuf[:, tt:tt + HIST_PAD, :]
    co_ref[...] = last_c
    po_ref[...] = last_p
    cbuf[:, 0:HIST_PAD, :] = last_c
    ubuf[:, 0:HIST_PAD, :] = last_p


def _mixer_in(x, hist_c, hist_p, cos, sin, layer, p, kv_bufs, *, bb, tt, pos0, tk):
    B, T, _ = x.shape
    depth = p['w_in'].shape[0]
    assert B % bb == 0 and T % tt == 0 and tt >= HIST_PAD and tt % 8 == 0
    assert tk is None or (bb == 1 and tt % tk == 0)
    grid = (B // bb, T // tt)
    m = bb * tt

    def seq_spec(w):
        return pl.BlockSpec((bb, tt, w), lambda b, t: (b, t, 0))

    if hist_c.ndim == 4:
        hist_spec = pl.BlockSpec((None, bb, HIST_PAD, CONV_W), lambda b, t: (layer, b, 0, 0))
    else:
        hist_spec = pl.BlockSpec((bb, HIST_PAD, CONV_W), lambda b, t: (b, 0, 0))
    state_spec = pl.BlockSpec((bb, HIST_PAD, CONV_W), lambda b, t: (b, 0, 0))
    rope_spec = pl.BlockSpec((None, m, HEAD_W), lambda b, t: (t, 0, 0))
    kv_spec = pl.BlockSpec((None, bb, tt * HEADS, HEAD_W), lambda b, t: (layer, b, t, 0))
    kv_shape = jax.ShapeDtypeStruct((depth, B, T * HEADS, HEAD_W), F32)
    if tk is None:
        q_shape, q_spec = jax.ShapeDtypeStruct((B, T, ATT_W), BF16), seq_spec(ATT_W)
        v_shape, v_spec = q_shape, q_spec
    else:
        q_shape = jax.ShapeDtypeStruct((B, ATT_W, T), BF16)
        q_spec = pl.BlockSpec((1, ATT_W, tt), lambda b, t: (b, 0, t))
        v_shape = jax.ShapeDtypeStruct((B, T // tk, ATT_W, tk), BF16)
        v_spec = pl.BlockSpec((1, tt // tk, ATT_W, tk), lambda b, t: (b, t, 0, 0))

    out_shape = (
        jax.ShapeDtypeStruct((B, T, D_MODEL), BF16),
        jax.ShapeDtypeStruct((B, T, D_MODEL), BF16),
        q_shape,
        jax.ShapeDtypeStruct((B, T, ATT_W), BF16),
        v_shape,
        kv_shape, kv_shape,
        jax.ShapeDtypeStruct((B, HIST_PAD, CONV_W), F32),
        jax.ShapeDtypeStruct((B, HIST_PAD, POOL_W), F32),
    )
    out_specs = (seq_spec(D_MODEL), seq_spec(D_MODEL), q_spec, seq_spec(ATT_W), v_spec,
                 kv_spec, kv_spec, state_spec, state_spec)
    in_specs = [seq_spec(D_MODEL), hist_spec, hist_spec, rope_spec, rope_spec,
                _layer_spec(p['norm_mix'], layer), _layer_spec(p['w_in'], layer),
                _layer_spec(p['conv_w'], layer), _layer_spec(p['w_conv_out'], layer),
                _layer_spec(p['w_pool'], layer), _layer_spec(p['pool_scale'], layer)]
    args = [x, hist_c, hist_p, cos, sin, p['norm_mix'], p['w_in'], p['conv_w'], p['w_conv_out'],
            p['w_pool'], p['pool_scale']]
    aliases = {}
    if kv_bufs is not None:
        in_specs += [pl.BlockSpec(memory_space=pl.ANY)] * 2
        aliases = {len(args): 5, len(args) + 1: 6}
        args += list(kv_bufs)
    return pl.pallas_call(
        functools.partial(_mixer_in_kernel, bb=bb, tt=tt, pos0=pos0, tk=tk,
                          aliased=kv_bufs is not None),
        grid=grid, in_specs=in_specs, out_specs=out_specs, out_shape=out_shape,
        scratch_shapes=[pltpu.VMEM((bb, HIST_PAD + tt, CONV_W), F32),
                        pltpu.VMEM((bb, HIST_PAD + tt, POOL_W), F32)],
        input_output_aliases=aliases,
        compiler_params=pltpu.CompilerParams(dimension_semantics=("parallel", "arbitrary"),
                                             vmem_limit_bytes=VMEM_LIMIT),
        name="mixer_in",
    )(*args)


def _lambda(lq1_ref, lk1_ref, lq2_ref, lk2_ref, lam_init):
    a = jnp.exp(jnp.sum(lq1_ref[...] * lk1_ref[...], axis=-1, keepdims=True))
    b = jnp.exp(jnp.sum(lq2_ref[...] * lk2_ref[...], axis=-1, keepdims=True))
    return a - b + lam_init


def _lam_init(layer):
    return 0.8 - 0.6 * math.exp(-0.3 * layer)


def _merge_out(o, mab, gc, x, wc_ref, wo_ref):
    yc = jnp.dot(o.astype(BF16), wc_ref[...], preferred_element_type=F32)
    merged = mab.astype(F32) + gc.astype(F32) * yc
    return x + jnp.dot(merged.astype(BF16), wo_ref[...], preferred_element_type=F32)


def _attn_weight_specs(p, layer):
    names = ('lambda_q1', 'lambda_k1', 'lambda_q2', 'lambda_k2', 'subln_g', 'w_attn_out', 'w_o')
    return [_layer_spec(p[n], layer) for n in names], [p[n] for n in names]


def _attn_prompt_kernel(qt_ref, k_ref, vt_ref, mab_ref, gc_ref, x_ref, lq1_ref, lk1_ref, lq2_ref,
                        lk2_ref, sg_ref, wc_ref, wo_ref, o_ref, p_scr, acc_scr, *, tq, tk,
                        lam_init):
    i = pl.program_id(1)
    n_diag = tq // tk
    lam = _lambda(lq1_ref, lk1_ref, lq2_ref, lk2_ref, lam_init)
    feat = lax.broadcasted_iota(jnp.int32, (HEAD_W, tq), 0)
    qs = []
    for hd in range(HEADS):
        qt = qt_ref[0, hd * HEAD_W:(hd + 1) * HEAD_W, :]
        zero = jnp.zeros_like(qt)
        qs.append(jnp.concatenate([jnp.where(feat < HEAD_DIM, qt, zero),
                                   jnp.where(feat < HEAD_DIM, zero, qt)], axis=1))

    def scores(j):
        off = pl.multiple_of(j * tk, tk)
        return [jnp.dot(k_ref[0, pl.ds(off, tk), hd * HEAD_W:(hd + 1) * HEAD_W], qs[hd],
                        preferred_element_type=F32) for hd in range(HEADS)]

    def softmax(s, m):
        m_out, alpha_out = [], []
        for hd in range(HEADS):
            m_new = jnp.maximum(m[hd], jnp.max(s[hd], axis=0, keepdims=True))
            p_scr[hd] = jnp.exp2(s[hd] - m_new).astype(BF16)
            m_out.append(m_new)
            alpha_out.append(jnp.exp2(m[hd] - m_new))
        return m_out, alpha_out

    ones_rows = jnp.ones((DENOM_ROWS, tk), BF16)

    def flush(j, alpha):
        for hd in range(HEADS):
            vt1 = jnp.concatenate([vt_ref[0, j, hd * HEAD_W:(hd + 1) * HEAD_W, :], ones_rows],
                                  axis=0)
            acc_scr[hd] = alpha[hd] * acc_scr[hd] + jnp.dot(vt1, p_scr[hd],
                                                            preferred_element_type=F32)

    acc_scr[...] = jnp.zeros_like(acc_scr)
    m = [jnp.full((1, 2 * tq), -jnp.inf, F32)] * HEADS
    alpha = [jnp.zeros((1, 2 * tq), F32)] * HEADS
    q_chunk = (lax.broadcasted_iota(jnp.int32, (tk, 2 * tq), 1) % tq) // CHUNK
    k_row = lax.broadcasted_iota(jnp.int32, (tk, 2 * tq), 0)
    for d in range(n_diag):
        j = i * n_diag + d
        s = scores(j)
        if d > 0:
            flush(j - 1, alpha)
        visible = (k_row + d * tk) // CHUNK <= q_chunk
        m, alpha = softmax([jnp.where(visible, sh, -jnp.inf) for sh in s], m)

    def body(j, carry):
        pend, alpha, m = carry
        s = scores(j)
        flush(pend, alpha)
        m, alpha = softmax(s, m)
        return j, alpha, m

    pend, alpha, m = lax.fori_loop(0, i * n_diag, body, (i * n_diag + n_diag - 1, alpha, m))
    flush(pend, alpha)
    heads = []
    for hd in range(HEADS):
        ot = acc_scr[hd, 0:HEAD_W, :] / acc_scr[hd, HEAD_W:HEAD_W + 1, :]
        ot = ot[:, :tq] - lam * ot[:, tq:]
        ot = ot * lax.rsqrt(jnp.mean(ot * ot, axis=0, keepdims=True) + SUBLN_EPS)
        heads.append(ot.T * sg_ref[...] * (1.0 - lam_init))
    o = jnp.concatenate(heads, axis=-1)
    o_ref[0] = _merge_out(o, mab_ref[0], gc_ref[0], x_ref[0], wc_ref, wo_ref)


def _attn_prompt(qt, k16, vt, mab, gc, x, layer, p, *, tq, tk):
    B, T, _ = x.shape
    assert T % tq == 0 and tq % tk == 0 and tk % CHUNK == 0 and vt.shape[3] == tk

    def tile_spec(w):
        return pl.BlockSpec((1, tq, w), lambda b, i: (b, i, 0))

    weight_specs, weights = _attn_weight_specs(p, layer)
    return pl.pallas_call(
        functools.partial(_attn_prompt_kernel, tq=tq, tk=tk, lam_init=_lam_init(layer)),
        grid=(B, T // tq),
        in_specs=[pl.BlockSpec((1, ATT_W, tq), lambda b, i: (b, 0, i)),
                  pl.BlockSpec((1, T, ATT_W), lambda b, i: (b, 0, 0)),
                  pl.BlockSpec((1, T // tk, ATT_W, tk), lambda b, i: (b, 0, 0, 0)),
                  tile_spec(D_MODEL), tile_spec(D_MODEL), tile_spec(D_MODEL)] + weight_specs,
        out_specs=tile_spec(D_MODEL),
        out_shape=jax.ShapeDtypeStruct((B, T, D_MODEL), F32),
        scratch_shapes=[pltpu.VMEM((HEADS, tk, 2 * tq), BF16),
                        pltpu.VMEM((HEADS, HEAD_W + DENOM_ROWS, 2 * tq), F32)],
        compiler_params=pltpu.CompilerParams(dimension_semantics=("parallel", "arbitrary"),
                                             vmem_limit_bytes=VMEM_LIMIT),
        name="attn_prompt",
    )(qt, k16, vt, mab, gc, x, *weights)


def _stack_components(qh):
    lane = lax.broadcasted_iota(jnp.int32, qh.shape, 1)
    zero = jnp.zeros_like(qh)
    return jnp.concatenate([jnp.where(lane < HEAD_DIM, qh, zero),
                            jnp.where(lane < HEAD_DIM, zero, qh)], axis=0)


def _softmax_step(carry, s, vj):
    m, l, acc = carry
    m_new = jnp.maximum(m, jnp.max(s, axis=-1, keepdims=True))
    pr = jnp.exp2(s - m_new)
    alpha = jnp.exp2(m - m_new)
    l = alpha * l + jnp.sum(pr, axis=-1, keepdims=True)
    acc = alpha * acc + jnp.dot(pr.astype(BF16), vj, preferred_element_type=F32)
    return m_new, l, acc


def _attn_sample_kernel(q_ref, kp_ref, vp_ref, kn_ref, vn_ref, mab_ref, gc_ref, x_ref, lq1_ref,
                        lk1_ref, lq2_ref, lk2_ref, sg_ref, wc_ref, wo_ref, o_ref, *, past,
                        lam_init):
    n_seq, tq, _ = q_ref.shape
    lam = _lambda(lq1_ref, lk1_ref, lq2_ref, lk2_ref, lam_init)
    q_chunk = (past + lax.broadcasted_iota(jnp.int32, (2 * tq, 1), 0) % tq) // CHUNK
    vis_past = (lax.broadcasted_iota(jnp.int32, (2 * tq, past), 1) // CHUNK) <= q_chunk
    vis_new = ((past + lax.broadcasted_iota(jnp.int32, (2 * tq, tq), 1)) // CHUNK) <= q_chunk
    rows = []
    for sq in range(n_seq):
        heads = []
        for hd in range(HEADS):
            lo, hi = hd * HEAD_W, (hd + 1) * HEAD_W
            qs = _stack_components(q_ref[sq, :, lo:hi])
            carry = (jnp.full((2 * tq, 1), -jnp.inf, F32), jnp.zeros((2 * tq, 1), F32),
                     jnp.zeros((2 * tq, HEAD_W), F32))
            s = lax.dot_general(qs, kn_ref[sq, :, lo:hi], NT_DIMS, preferred_element_type=F32)
            carry = _softmax_step(carry, jnp.where(vis_new, s, -jnp.inf), vn_ref[sq, :, lo:hi])
            kp = kp_ref[sq, pl.ds(hd, past, stride=HEADS), :].astype(BF16)
            vp = vp_ref[sq, pl.ds(hd, past, stride=HEADS), :].astype(BF16)
            s = lax.dot_general(qs, kp, NT_DIMS, preferred_element_type=F32)
            _, l, acc = _softmax_step(carry, jnp.where(vis_past, s, -jnp.inf), vp)
            o = acc / l
            o = o[:tq] - lam * o[tq:]
            heads.append(_rms(o, sg_ref[...], SUBLN_EPS) * (1.0 - lam_init))
        rows.append(jnp.concatenate(heads, axis=-1))
    o = jnp.concatenate(rows, axis=0)
    flat = (n_seq * tq, D_MODEL)
    out = _merge_out(o, mab_ref[...].reshape(flat), gc_ref[...].reshape(flat),
                     x_ref[...].reshape(flat), wc_ref, wo_ref)
    o_ref[...] = out.reshape(n_seq, tq, D_MODEL)


def _attn_sample(q, cache_k, cache_v, k16, v16, mab, gc, x, layer, p, *, seqs):
    B, T, _ = x.shape
    past = cache_k.shape[2] // HEADS
    assert B % seqs == 0

    def tile_spec(w):
        return pl.BlockSpec((seqs, T, w), lambda b: (b, 0, 0))

    past_spec = pl.BlockSpec((None, seqs, past * HEADS, HEAD_W), lambda b: (layer, b, 0, 0))
    weight_specs, weights = _attn_weight_specs(p, layer)
    return pl.pallas_call(
        functools.partial(_attn_sample_kernel, past=past, lam_init=_lam_init(layer)),
        grid=(B // seqs,),
        in_specs=[tile_spec(ATT_W), past_spec, past_spec, tile_spec(ATT_W), tile_spec(ATT_W),
                  tile_spec(D_MODEL), tile_spec(D_MODEL), tile_spec(D_MODEL)] + weight_specs,
        out_specs=tile_spec(D_MODEL),
        out_shape=jax.ShapeDtypeStruct((B, T, D_MODEL), F32),
        compiler_params=pltpu.CompilerParams(dimension_semantics=("parallel",),
                                             vmem_limit_bytes=VMEM_LIMIT),
        name="attn_sample",
    )(q, cache_k, cache_v, k16, v16, mab, gc, x, *weights)


def _ffn_dense_kernel(x_ref, ng_ref, wg_ref, wu_ref, wd_ref, nf_ref, o_ref, *, final):
    bb, tt, _ = x_ref.shape
    x = x_ref[...].reshape(bb * tt, D_MODEL)
    h = _rms(x, ng_ref[...], EPS).astype(BF16)
    y = x
    for lo, hi in _chunks(wg_ref.shape[1], 1024):
        g = jnp.dot(h, wg_ref[:, lo:hi], preferred_element_type=F32)
        u = jnp.dot(h, wu_ref[:, lo:hi], preferred_element_type=F32)
        a = (g * jax.nn.sigmoid(g) * u).astype(BF16)
        y = y + jnp.dot(a, wd_ref[lo:hi, :], preferred_element_type=F32)
    if final:
        y = _rms(y, nf_ref[...], EPS)
    o_ref[...] = y.reshape(bb, tt, D_MODEL)


def _token_blocks(B, T, tm):
    if T >= tm:
        assert T % tm == 0
        return 1, tm
    assert tm % T == 0 and B % (tm // T) == 0
    return tm // T, T


def _ffn_dense(x, layer, p, *, tm, final):
    B, T, _ = x.shape
    bb, tt = _token_blocks(B, T, tm)
    d = layer // 2
    row_spec = pl.BlockSpec((bb, tt, D_MODEL), lambda b, t: (b, t, 0))
    return pl.pallas_call(
        functools.partial(_ffn_dense_kernel, final=final),
        grid=(B // bb, T // tt),
        in_specs=[row_spec, _layer_spec(p['norm_ffn'], layer), _layer_spec(p['w_gate_d'], d),
                  _layer_spec(p['w_up_d'], d), _layer_spec(p['w_down_d'], d),
                  _const_spec(p['norm_final'].shape)],
        out_specs=row_spec,
        out_shape=jax.ShapeDtypeStruct(x.shape, F32),
        compiler_params=pltpu.CompilerParams(dimension_semantics=("parallel", "parallel"),
                                             vmem_limit_bytes=VMEM_LIMIT),
        name="ffn_dense",
    )(x, p['norm_ffn'], p['w_gate_d'], p['w_up_d'], p['w_down_d'], p['norm_final'])


def _route_top2(logits):
    row = lax.broadcasted_iota(jnp.int32, logits.shape, 0).astype(F32)
    big = float(N_EXPERTS)
    m1 = jnp.max(logits, axis=0, keepdims=True)
    i1 = jnp.min(jnp.where(logits == m1, row, big), axis=0, keepdims=True)
    rest = jnp.where(row == i1, -jnp.inf, logits)
    m2 = jnp.max(rest, axis=0, keepdims=True)
    i2 = jnp.min(jnp.where(rest == m2, row, big), axis=0, keepdims=True)
    e2 = jnp.exp(m2 - m1)
    w1 = 1.0 / (1.0 + e2)
    w2 = e2 / (1.0 + e2)
    return jnp.where(row == i1, w1, 0.0) + jnp.where(row == i2, w2, 0.0)


def _ffn_moe_kernel(x_ref, ng_ref, wrt_ref, brt_ref, wg_ref, wu_ref, wd_ref, nf_ref, o_ref,
                    h_scr, wts_scr, rank_scr, *, final, tb, cms):
    e = pl.program_id(2)
    bb, tt, _ = x_ref.shape
    n_tok = bb * tt
    sub_blocks = _chunks(n_tok, tb)

    @pl.when(e == 0)
    def _():
        x = x_ref[...].reshape(n_tok, D_MODEL)
        h = _rms(x, ng_ref[...], EPS)
        logits = lax.dot_general(wrt_ref[...], h, NT_DIMS, preferred_element_type=F32,
                                 precision=lax.Precision.HIGHEST) + brt_ref[...]
        wts = _route_top2(logits)
        sel = (wts > 0.0).astype(BF16)
        before = (lax.broadcasted_iota(jnp.int32, (tb, tb), 0)
                  < lax.broadcasted_iota(jnp.int32, (tb, tb), 1)).astype(BF16)
        rank = jnp.concatenate(
            [jnp.dot(sel[:, lo:hi], before, preferred_element_type=F32) for lo, hi in sub_blocks],
            axis=1)
        for ee in range(N_EXPERTS):
            for sb, (lo, hi) in enumerate(sub_blocks):
                wts_scr[ee, sb] = wts[ee:ee + 1, lo:hi]
                rank_scr[ee, sb] = rank[ee:ee + 1, lo:hi]
        h_scr[...] = h.astype(BF16)
        o_ref[...] = x_ref[...]

    cm_max = cms[-1]

    def sub_block(sb, carry):
        w_row = wts_scr[e, sb]
        r_row = rank_scr[e, sb]
        sel = w_row > 0.0
        n_sel = jnp.sum(sel.astype(F32)).astype(jnp.int32)
        row0 = pl.multiple_of(sb * tb, tb)

        def run_chunk(cm, base):
            rows = (lax.broadcasted_iota(jnp.int32, (cm, tb), 0) + base).astype(F32)
            hit = (r_row == rows) & sel
            gather = jnp.where(hit, 1.0, 0.0).astype(BF16)
            xg = jnp.dot(gather, h_scr[pl.ds(row0, tb), :],
                         preferred_element_type=F32).astype(BF16)
            g = jnp.dot(xg, wg_ref[...], preferred_element_type=F32)
            u = jnp.dot(xg, wu_ref[...], preferred_element_type=F32)
            a = (g * jax.nn.sigmoid(g) * u).astype(BF16)
            y = jnp.dot(a, wd_ref[...], preferred_element_type=F32).astype(BF16)
            scatter = jnp.where(hit, w_row, 0.0).astype(BF16)
            upd = lax.dot_general(scatter, y, TN_DIMS, preferred_element_type=F32)
            if bb == 1:
                o_ref[0, pl.ds(row0, tb), :] += upd
            else:
                o_ref[pl.ds(sb * (tb // tt), tb // tt)] += upd.reshape(tb // tt, tt, D_MODEL)

        def chunk(c, carry):
            left = n_sel - c * cm_max
            size_idx = sum((left > cm).astype(jnp.int32) for cm in cms[:-1])
            lax.switch(size_idx, [functools.partial(run_chunk, cm) for cm in cms], c * cm_max)
            return carry

        n_chunks = sum((n_sel > c * cm_max).astype(jnp.int32) for c in range(-(-tb // cm_max)))
        lax.fori_loop(0, n_chunks, chunk, 0)
        return carry

    lax.fori_loop(0, n_tok // tb, sub_block, 0)

    if final:
        @pl.when(e == N_EXPERTS - 1)
        def _():
            y = o_ref[...].reshape(n_tok, D_MODEL)
            o_ref[...] = _rms(y, nf_ref[...], EPS).reshape(bb, tt, D_MODEL)


def _ffn_moe(x, layer, p, *, tm, tb, cms, final):
    B, T, _ = x.shape
    bb, tt = _token_blocks(B, T, tm)
    assert tm % tb == 0 and (bb == 1 or tb % tt == 0)
    mo = layer // 2
    row_spec = pl.BlockSpec((bb, tt, D_MODEL), lambda b, t, e: (b, t, 0))

    def expert_spec(arr):
        return pl.BlockSpec((None, None) + tuple(arr.shape[2:]), lambda b, t, e: (mo, e, 0, 0))

    return pl.pallas_call(
        functools.partial(_ffn_moe_kernel, final=final, tb=tb, cms=cms),
        grid=(B // bb, T // tt, N_EXPERTS),
        in_specs=[pl.BlockSpec((bb, tt, D_MODEL), lambda b, t, e: (b, t, 0),
                               pipeline_mode=pl.Buffered(1)),
                  _layer_spec(p['norm_ffn'], layer), _layer_spec(p['w_router_t'], mo),
                  _layer_spec(p['b_router_t'], mo), expert_spec(p['w_gate_e']),
                  expert_spec(p['w_up_e']), expert_spec(p['w_down_e']),
                  _const_spec(p['norm_final'].shape)],
        out_specs=row_spec,
        out_shape=jax.ShapeDtypeStruct(x.shape, F32),
        scratch_shapes=[pltpu.VMEM((tm, D_MODEL), BF16),
                        pltpu.VMEM((N_EXPERTS, tm // tb, 1, tb), F32),
                        pltpu.VMEM((N_EXPERTS, tm // tb, 1, tb), F32)],
        compiler_params=pltpu.CompilerParams(
            dimension_semantics=("parallel", "parallel", "arbitrary"),
            vmem_limit_bytes=VMEM_LIMIT),
        name="ffn_moe",
    )(x, p['norm_ffn'], p['w_router_t'], p['b_router_t'], p['w_gate_e'], p['w_up_e'],
      p['w_down_e'], p['norm_final'])


def _rope_tables(pos0, T, bb, tt):
    half = HEAD_DIM // 2
    inv_freq = ROPE_THETA ** (-jnp.arange(half, dtype=F32) / half)
    ang = (pos0 + jnp.arange(T)).astype(F32)[:, None] * inv_freq[None, :]
    cos, sin = jnp.cos(ang), jnp.sin(ang)
    cos = jnp.concatenate([cos, cos, cos, cos], axis=-1)
    sin = jnp.concatenate([-sin, sin, -sin, sin], axis=-1)

    def lay(a):
        a = a.reshape(T // tt, 1, tt, HEAD_W)
        return jnp.broadcast_to(a, (T // tt, bb, tt, HEAD_W)).reshape(T // tt, bb * tt, HEAD_W)

    return lay(cos), lay(sin)


def _pad_hist(state):
    return jnp.pad(state, ((0, 0), (0, 0), (HIST_PAD - state.shape[2], 0), (0, 0)))


_MATMUL_WEIGHTS = ('w_in', 'w_conv_out', 'w_pool', 'w_attn_out', 'w_o', 'w_gate_d', 'w_up_d',
                   'w_down_d', 'w_gate_e', 'w_up_e', 'w_down_e')
_ROW_VECTORS = ('norm_mix', 'pool_scale', 'lambda_q1', 'lambda_k1', 'lambda_q2', 'lambda_k2',
                'subln_g', 'norm_ffn')


def _params(w):
    p = dict(w)
    for n in _MATMUL_WEIGHTS:
        p[n] = w[n].astype(BF16)
    for n in _ROW_VECTORS:
        p[n] = w[n][:, None, :]
    p['w_router_t'] = jnp.swapaxes(w['w_router'], 1, 2)
    p['b_router_t'] = w['b_router'][:, :, None]
    p['norm_final'] = w['norm_final'][None, :]
    return p


def _new_state(co, po):
    return co[:, HIST_PAD - (CONV_K - 1):], po[:, HIST_PAD - POOL_HIST:]


def _token_mixer_prompt(x, layer, p, kv_bufs):
    B, T, _ = x.shape
    tt = min(PROMPT_TILE, T)
    tk = min(KEY_TILE, tt)
    cos, sin = _rope_tables(0, T, 1, tt)
    zero_hist = jnp.zeros((B, HIST_PAD, CONV_W), F32)
    mab, gc, qt, k16, vt, k32, v32, co, po = _mixer_in(
        x, zero_hist, zero_hist, cos, sin, layer, p, kv_bufs, bb=1, tt=tt, pos0=0, tk=tk)
    x = _attn_prompt(qt, k16, vt, mab, gc, x, layer, p, tq=tt, tk=tk)
    return (x,) + _new_state(co, po) + ((k32, v32),)


def _token_mixer_sample(x, cache_k, cache_v, state_conv, state_pool, layer, p, kv_bufs):
    past = cache_k.shape[2]
    B, T, _ = x.shape
    bb = min(SAMPLE_MIXER_SEQS, B)
    cos, sin = _rope_tables(past, T, bb, T)
    mab, gc, q, k16, v16, k32, v32, co, po = _mixer_in(
        x, _pad_hist(state_conv), _pad_hist(state_pool), cos, sin, layer, p, kv_bufs,
        bb=bb, tt=T, pos0=past, tk=None)
    rows = cache_k.shape[:2] + (past * HEADS, HEAD_W)
    x = _attn_sample(q, cache_k.reshape(rows), cache_v.reshape(rows), k16, v16, mab, gc, x, layer,
                     p, seqs=min(SAMPLE_ATTN_SEQS, B))
    return (x,) + _new_state(co, po) + ((k32, v32),)


def _channel_mixer(x, layer, p, *, final):
    tokens = x.shape[0] * x.shape[1]
    if layer % 2 == 0:
        return _ffn_dense(x, layer, p, tm=min(DENSE_TOKENS, tokens), final=final)
    tm = min(MOE_TOKENS, tokens)
    return _ffn_moe(x, layer, p, tm=tm, tb=min(MOE_SUB_TOKENS, tm), cms=MOE_CHUNK_ROWS,
                    final=final)


def kernel(x_prompt, x_sample, cache_k, cache_v, state_conv, state_pool, norm_mix, w_in, conv_w,
           w_conv_out, w_pool, pool_scale, lambda_q1, lambda_k1, lambda_q2, lambda_k2, subln_g,
           w_attn_out, w_o, norm_ffn, w_gate_d, w_up_d, w_down_d, w_router, b_router, w_gate_e,
           w_up_e, w_down_e, norm_final):
    depth = w_in.shape[0]
    p = _params(dict(
        norm_mix=norm_mix, w_in=w_in, conv_w=conv_w, w_conv_out=w_conv_out, w_pool=w_pool,
        pool_scale=pool_scale, lambda_q1=lambda_q1, lambda_k1=lambda_k1, lambda_q2=lambda_q2,
        lambda_k2=lambda_k2, subln_g=subln_g, w_attn_out=w_attn_out, w_o=w_o, norm_ffn=norm_ffn,
        w_gate_d=w_gate_d, w_up_d=w_up_d, w_down_d=w_down_d, w_router=w_router,
        b_router=b_router, w_gate_e=w_gate_e, w_up_e=w_up_e, w_down_e=w_down_e,
        norm_final=norm_final))

    xp, xs = x_prompt, x_sample
    kv_p = kv_s = None
    states = [[] for _ in range(4)]
    for l in range(depth):
        final = l == depth - 1
        xp, cp, pp, kv_p = _token_mixer_prompt(xp, l, p, kv_p)
        xs, cs, ps, kv_s = _token_mixer_sample(xs, cache_k, cache_v, state_conv, state_pool, l, p,
                                               kv_s)
        for lst, a in zip(states, (cp, pp, cs, ps)):
            lst.append(a)
        xp = _channel_mixer(xp, l, p, final=final)
        xs = _channel_mixer(xs, l, p, final=final)

    def heads_view(a):
        return a.reshape(a.shape[0], a.shape[1], a.shape[2] // HEADS, HEADS, HEAD_W)

    return (xp, xs, heads_view(kv_p[0]), heads_view(kv_p[1]), jnp.stack(states[0]),
            jnp.stack(states[1]), heads_view(kv_s[0]), heads_view(kv_s[1]),
            jnp.stack(states[2]), jnp.stack(states[3]))
```

```python
import functools
import math

import jax
import jax.numpy as jnp
from jax import lax
from jax.experimental import pallas as pl
from jax.experimental.pallas import tpu as pltpu

D_MODEL = 1024
CHUNK = 64
CONV_W = 512
CONV_K = 3
POOL_W = 512
POOL_GC = 128
POOL_WINDOWS = (2, 4, 8, 16)
POOL_HIST = 15
HEADS = 4
HEAD_DIM = 64
HEAD_W = 2 * HEAD_DIM
ATT_W = HEADS * HEAD_W
ROPE_THETA = 10000.0
N_EXPERTS = 8
EPS = 1e-6
SUBLN_EPS = 1e-5

C_XA, C_BA, C_CA, C_U, C_Q, C_K, C_V, C_GA, C_GB, C_GC, C_END = (
    0, 512, 1024, 1536, 2048, 2560, 3072, 3584, 4608, 5632, 6656)

HIST_PAD = 16
Q_SCALE = HEAD_DIM ** -0.5 * math.log2(math.e)
DENOM_ROWS = 16

VMEM_LIMIT = 56 * 1024 * 1024
PROMPT_TILE = 512
KEY_TILE = 512
SAMPLE_MIXER_SEQS = 16
SAMPLE_ATTN_SEQS = 4
DENSE_TOKENS = 512
MOE_TOKENS = 1024
MOE_SUB_TOKENS = 512
MOE_CHUNK_ROWS = (128, 160, 192, 224, 256)

F32 = jnp.float32
BF16 = jnp.bfloat16
NT_DIMS = (((1,), (1,)), ((), ()))
TN_DIMS = (((0,), (0,)), ((), ()))


def _const_spec(shape):
    nd = len(shape)
    return pl.BlockSpec(tuple(shape), lambda *_: (0,) * nd, pipeline_mode=pl.Buffered(1))


def _layer_spec(arr, layer):
    nd = arr.ndim
    return pl.BlockSpec((None,) + tuple(arr.shape[1:]), lambda *_: (layer,) + (0,) * (nd - 1),
                        pipeline_mode=pl.Buffered(1))


def _rms(x, g, eps):
    return x * lax.rsqrt(jnp.mean(x * x, axis=-1, keepdims=True) + eps) * g


def _chunks(n, step):
    return [(lo, min(lo + step, n)) for lo in range(0, n, step)]


def _mixer_in_kernel(*refs, bb, tt, pos0, tk, aliased):
    (x_ref, hc_ref, hp_ref, cos_ref, sin_ref, ng_ref, win_ref, cw_ref, wa_ref, wp_ref,
     ps_ref) = refs[:11]
    refs = refs[11 + (2 if aliased else 0):]
    (mab_ref, gc_ref, q_ref, k16_ref, v16_ref, k32_ref, v32_ref, co_ref, po_ref, cbuf,
     ubuf) = refs
    t = pl.program_id(1)
    m = bb * tt
    x = x_ref[...].reshape(m, D_MODEL)
    h = _rms(x, ng_ref[...], EPS).astype(BF16)

    def seg(lo, hi):
        return jnp.dot(h, win_ref[:, lo:hi], preferred_element_type=F32)

    @pl.when(t == 0)
    def _():
        cbuf[:, 0:HIST_PAD, :] = hc_ref[...]
        ubuf[:, 0:HIST_PAD, :] = hp_ref[...]

    z_ca, z_xa, z_ba, z_u = seg(C_CA, C_U), seg(C_XA, C_BA), seg(C_BA, C_CA), seg(C_U, C_Q)
    z_q, z_k, z_v = seg(C_Q, C_K), seg(C_K, C_V), seg(C_V, C_GA)

    cin = (z_ca * z_xa).reshape(bb, tt, CONV_W)
    cbuf[:, HIST_PAD:, :] = cin
    conv = cbuf[:, HIST_PAD - 2:HIST_PAD - 2 + tt, :] * cw_ref[0:1, :]
    conv = conv + cbuf[:, HIST_PAD - 1:HIST_PAD - 1 + tt, :] * cw_ref[1:2, :]
    conv = conv + cin * cw_ref[2:3, :]
    ya = jnp.dot((z_ba * conv.reshape(m, CONV_W)).astype(BF16), wa_ref[...],
                 preferred_element_type=F32)

    ubuf[:, HIST_PAD:, :] = z_u.reshape(bb, tt, POOL_W)
    pos = pos0 + t * tt + lax.broadcasted_iota(jnp.int32, (bb, tt, POOL_GC), 1)
    yb_parts = []
    for gi, win in enumerate(POOL_WINDOWS):
        lo, hi = gi * POOL_GC, (gi + 1) * POOL_GC
        cur = ubuf[:, HIST_PAD:HIST_PAD + tt, lo:hi]
        s = cur
        for k in range(1, win):
            s = s + ubuf[:, HIST_PAD - k:HIST_PAD - k + tt, lo:hi]
        cnt = jnp.minimum(pos + 1, win).astype(F32)
        pooled = s / cnt - cur
        yb_parts.append(jnp.dot(pooled.reshape(m, POOL_GC).astype(BF16), wp_ref[gi],
                                preferred_element_type=F32))
    yb = jnp.concatenate(yb_parts, axis=-1) * ps_ref[...]

    cos4 = jnp.concatenate([cos_ref[...]] * HEADS, axis=-1)
    sin4 = jnp.concatenate([sin_ref[...]] * HEADS, axis=-1)
    lane = lax.broadcasted_iota(jnp.int32, (m, ATT_W), 1)
    first_half = (lane & (HEAD_DIM // 2)) == 0

    def rope(z):
        swapped = jnp.where(first_half, pltpu.roll(z, ATT_W - HEAD_DIM // 2, axis=1),
                            pltpu.roll(z, HEAD_DIM // 2, axis=1))
        return z * cos4 + swapped * sin4

    q = rope(z_q) * Q_SCALE
    k = rope(z_k)
    v = z_v
    for hd in range(HEADS):
        lo, hi = hd * HEAD_W, (hd + 1) * HEAD_W
        k32_ref[:, pl.ds(hd, tt, stride=HEADS), :] = k[:, lo:hi].reshape(bb, tt, HEAD_W)
        v32_ref[:, pl.ds(hd, tt, stride=HEADS), :] = v[:, lo:hi].reshape(bb, tt, HEAD_W)
    k16_ref[...] = k.astype(BF16).reshape(bb, tt, ATT_W)
    if tk is None:
        q_ref[...] = q.astype(BF16).reshape(bb, tt, ATT_W)
        v16_ref[...] = v.astype(BF16).reshape(bb, tt, ATT_W)
    else:
        q_ref[0] = q.T.astype(BF16)
        for c in range(tt // tk):
            v16_ref[0, c] = v[c * tk:(c + 1) * tk, :].T.astype(BF16)

    ga = jax.nn.sigmoid(seg(C_GA, C_GB))
    gb = jax.nn.sigmoid(seg(C_GB, C_GC))
    mab_ref[...] = (ga * ya + gb * yb).astype(BF16).reshape(bb, tt, D_MODEL)
    gc_ref[...] = jax.nn.sigmoid(seg(C_GC, C_END)).astype(BF16).reshape(bb, tt, D_MODEL)

    last_c = cbuf[:, tt:tt + HIST_PAD, :]
    last_p = ubuf[:, tt:tt + HIST_PAD, :]
    co_ref[...] = last_c
    po_ref[...] = last_p
    cbuf[:, 0:HIST_PAD, :] = last_c
    ubuf[:, 0:HIST_PAD, :] = last_p


def _mixer_in(x, hist_c, hist_p, cos, sin, layer, p, kv_bufs, *, bb, tt, pos0, tk):
    B, T, _ = x.shape
    depth = p['w_in'].shape[0]
    assert B % bb == 0 and T % tt == 0 and tt >= HIST_PAD and tt % 8 == 0
    assert tk is None or (bb == 1 and tt % tk == 0)
    grid = (B // bb, T // tt)
    m = bb * tt

    def seq_spec(w):
        return pl.BlockSpec((bb, tt, w), lambda b, t: (b, t, 0))

    if hist_c.ndim == 4:
        hist_spec = pl.BlockSpec((None, bb, HIST_PAD, CONV_W), lambda b, t: (layer, b, 0, 0))
    else:
        hist_spec = pl.BlockSpec((bb, HIST_PAD, CONV_W), lambda b, t: (b, 0, 0))
    state_spec = pl.BlockSpec((bb, HIST_PAD, CONV_W), lambda b, t: (b, 0, 0))
    rope_spec = pl.BlockSpec((None, m, HEAD_W), lambda b, t: (t, 0, 0))
    kv_spec = pl.BlockSpec((None, bb, tt * HEADS, HEAD_W), lambda b, t: (layer, b, t, 0))
    kv_shape = jax.ShapeDtypeStruct((depth, B, T * HEADS, HEAD_W), F32)
    if tk is None:
        q_shape, q_spec = jax.ShapeDtypeStruct((B, T, ATT_W), BF16), seq_spec(ATT_W)
        v_shape, v_spec = q_shape, q_spec
    else:
        q_shape = jax.ShapeDtypeStruct((B, ATT_W, T), BF16)
        q_spec = pl.BlockSpec((1, ATT_W, tt), lambda b, t: (b, 0, t))
        v_shape = jax.ShapeDtypeStruct((B, T // tk, ATT_W, tk), BF16)
        v_spec = pl.BlockSpec((1, tt // tk, ATT_W, tk), lambda b, t: (b, t, 0, 0))

    out_shape = (
        jax.ShapeDtypeStruct((B, T, D_MODEL), BF16),
        jax.ShapeDtypeStruct((B, T, D_MODEL), BF16),
        q_shape,
        jax.ShapeDtypeStruct((B, T, ATT_W), BF16),
        v_shape,
        kv_shape, kv_shape,
        jax.ShapeDtypeStruct((B, HIST_PAD, CONV_W), F32),
        jax.ShapeDtypeStruct((B, HIST_PAD, POOL_W), F32),
    )
    out_specs = (seq_spec(D_MODEL), seq_spec(D_MODEL), q_spec, seq_spec(ATT_W), v_spec,
                 kv_spec, kv_spec, state_spec, state_spec)
    in_specs = [seq_spec(D_MODEL), hist_spec, hist_spec, rope_spec, rope_spec,
                _layer_spec(p['norm_mix'], layer), _layer_spec(p['w_in'], layer),
                _layer_spec(p['conv_w'], layer), _layer_spec(p['w_conv_out'], layer),
                _layer_spec(p['w_pool'], layer), _layer_spec(p['pool_scale'], layer)]
    args = [x, hist_c, hist_p, cos, sin, p['norm_mix'], p['w_in'], p['conv_w'], p['w_conv_out'],
            p['w_pool'], p['pool_scale']]
    aliases = {}
    if kv_bufs is not None:
        in_specs += [pl.BlockSpec(memory_space=pl.ANY)] * 2
        aliases = {len(args): 5, len(args) + 1: 6}
        args += list(kv_bufs)
    return pl.pallas_call(
        functools.partial(_mixer_in_kernel, bb=bb, tt=tt, pos0=pos0, tk=tk,
                          aliased=kv_bufs is not None),
        grid=grid, in_specs=in_specs, out_specs=out_specs, out_shape=out_shape,
        scratch_shapes=[pltpu.VMEM((bb, HIST_PAD + tt, CONV_W), F32),
                        pltpu.VMEM((bb, HIST_PAD + tt, POOL_W), F32)],
        input_output_aliases=aliases,
        compiler_params=pltpu.CompilerParams(dimension_semantics=("parallel", "arbitrary"),
                                             vmem_limit_bytes=VMEM_LIMIT),
        name="mixer_in",
    )(*args)


def _lambda(lq1_ref, lk1_ref, lq2_ref, lk2_ref, lam_init):
    a = jnp.exp(jnp.sum(lq1_ref[...] * lk1_ref[...], axis=-1, keepdims=True))
    b = jnp.exp(jnp.sum(lq2_ref[...] * lk2_ref[...], axis=-1, keepdims=True))
    return a - b + lam_init


def _lam_init(layer):
    return 0.8 - 0.6 * math.exp(-0.3 * layer)


def _merge_out(o, mab, gc, x, wc_ref, wo_ref):
    yc = jnp.dot(o.astype(BF16), wc_ref[...], preferred_element_type=F32)
    merged = mab.astype(F32) + gc.astype(F32) * yc
    return x + jnp.dot(merged.astype(BF16), wo_ref[...], preferred_element_type=F32)


def _attn_weight_specs(p, layer):
    names = ('lambda_q1', 'lambda_k1', 'lambda_q2', 'lambda_k2', 'subln_g', 'w_attn_out', 'w_o')
    return [_layer_spec(p[n], layer) for n in names], [p[n] for n in names]


def _attn_prompt_kernel(qt_ref, k_ref, vt_ref, mab_ref, gc_ref, x_ref, lq1_ref, lk1_ref, lq2_ref,
                        lk2_ref, sg_ref, wc_ref, wo_ref, o_ref, p_scr, acc_scr, *, tq, tk,
                        lam_init):
    i = pl.program_id(1)
    n_diag = tq // tk
    lam = _lambda(lq1_ref, lk1_ref, lq2_ref, lk2_ref, lam_init)
    feat = lax.broadcasted_iota(jnp.int32, (HEAD_W, tq), 0)
    qs = []
    for hd in range(HEADS):
        qt = qt_ref[0, hd * HEAD_W:(hd + 1) * HEAD_W, :]
        zero = jnp.zeros_like(qt)
        qs.append(jnp.concatenate([jnp.where(feat < HEAD_DIM, qt, zero),
                                   jnp.where(feat < HEAD_DIM, zero, qt)], axis=1))

    def scores(j):
        off = pl.multiple_of(j * tk, tk)
        return [jnp.dot(k_ref[0, pl.ds(off, tk), hd * HEAD_W:(hd + 1) * HEAD_W], qs[hd],
                        preferred_element_type=F32) for hd in range(HEADS)]

    def softmax(s, m):
        m_out, alpha_out = [], []
        for hd in range(HEADS):
            m_new = jnp.maximum(m[hd], jnp.max(s[hd], axis=0, keepdims=True))
            p_scr[hd] = jnp.exp2(s[hd] - m_new).astype(BF16)
            m_out.append(m_new)
            alpha_out.append(jnp.exp2(m[hd] - m_new))
        return m_out, alpha_out

    ones_rows = jnp.ones((DENOM_ROWS, tk), BF16)

    def flush(j, alpha):
        for hd in range(HEADS):
            vt1 = jnp.concatenate([vt_ref[0, j, hd * HEAD_W:(hd + 1) * HEAD_W, :], ones_rows],
                                  axis=0)
            acc_scr[hd] = alpha[hd] * acc_scr[hd] + jnp.dot(vt1, p_scr[hd],
                                                            preferred_element_type=F32)

    acc_scr[...] = jnp.zeros_like(acc_scr)
    m = [jnp.full((1, 2 * tq), -jnp.inf, F32)] * HEADS
    alpha = [jnp.zeros((1, 2 * tq), F32)] * HEADS
    q_chunk = (lax.broadcasted_iota(jnp.int32, (tk, 2 * tq), 1) % tq) // CHUNK
    k_row = lax.broadcasted_iota(jnp.int32, (tk, 2 * tq), 0)
    for d in range(n_diag):
        j = i * n_diag + d
        s = scores(j)
        if d > 0:
            flush(j - 1, alpha)
        visible = (k_row + d * tk) // CHUNK <= q_chunk
        m, alpha = softmax([jnp.where(visible, sh, -jnp.inf) for sh in s], m)

    def body(j, carry):
        pend, alpha, m = carry
        s = scores(j)
        flush(pend, alpha)
        m, alpha = softmax(s, m)
        return j, alpha, m

    pend, alpha, m = lax.fori_loop(0, i * n_diag, body, (i * n_diag + n_diag - 1, alpha, m))
    flush(pend, alpha)
    heads = []
    for hd in range(HEADS):
        ot = acc_scr[hd, 0:HEAD_W, :] / acc_scr[hd, HEAD_W:HEAD_W + 1, :]
        ot = ot[:, :tq] - lam * ot[:, tq:]
        ot = ot * lax.rsqrt(jnp.mean(ot * ot, axis=0, keepdims=True) + SUBLN_EPS)
        heads.append(ot.T * sg_ref[...] * (1.0 - lam_init))
    o = jnp.concatenate(heads, axis=-1)
    o_ref[0] = _merge_out(o, mab_ref[0], gc_ref[0], x_ref[0], wc_ref, wo_ref)


def _attn_prompt(qt, k16, vt, mab, gc, x, layer, p, *, tq, tk):
    B, T, _ = x.shape
    assert T % tq == 0 and tq % tk == 0 and tk % CHUNK == 0 and vt.shape[3] == tk

    def tile_spec(w):
        return pl.BlockSpec((1, tq, w), lambda b, i: (b, i, 0))

    weight_specs, weights = _attn_weight_specs(p, layer)
    return pl.pallas_call(
        functools.partial(_attn_prompt_kernel, tq=tq, tk=tk, lam_init=_lam_init(layer)),
        grid=(B, T // tq),
        in_specs=[pl.BlockSpec((1, ATT_W, tq), lambda b, i: (b, 0, i)),
                  pl.BlockSpec((1, T, ATT_W), lambda b, i: (b, 0, 0)),
                  pl.BlockSpec((1, T // tk, ATT_W, tk), lambda b, i: (b, 0, 0, 0)),
                  tile_spec(D_MODEL), tile_spec(D_MODEL), tile_spec(D_MODEL)] + weight_specs,
        out_specs=tile_spec(D_MODEL),
        out_shape=jax.ShapeDtypeStruct((B, T, D_MODEL), F32),
        scratch_shapes=[pltpu.VMEM((HEADS, tk, 2 * tq), BF16),
                        pltpu.VMEM((HEADS, HEAD_W + DENOM_ROWS, 2 * tq), F32)],
        compiler_params=pltpu.CompilerParams(dimension_semantics=("parallel", "arbitrary"),
                                             vmem_limit_bytes=VMEM_LIMIT),
        name="attn_prompt",
    )(qt, k16, vt, mab, gc, x, *weights)


def _stack_components(qh):
    lane = lax.broadcasted_iota(jnp.int32, qh.shape, 1)
    zero = jnp.zeros_like(qh)
    return jnp.concatenate([jnp.where(lane < HEAD_DIM, qh, zero),
                            jnp.where(lane < HEAD_DIM, zero, qh)], axis=0)


def _softmax_step(carry, s, vj):
    m, l, acc = carry
    m_new = jnp.maximum(m, jnp.max(s, axis=-1, keepdims=True))
    pr = jnp.exp2(s - m_new)
    alpha = jnp.exp2(m - m_new)
    l = alpha * l + jnp.sum(pr, axis=-1, keepdims=True)
    acc = alpha * acc + jnp.dot(pr.astype(BF16), vj, preferred_element_type=F32)
    return m_new, l, acc


def _attn_sample_kernel(q_ref, kp_ref, vp_ref, kn_ref, vn_ref, mab_ref, gc_ref, x_ref, lq1_ref,
                        lk1_ref, lq2_ref, lk2_ref, sg_ref, wc_ref, wo_ref, o_ref, *, past,
                        lam_init):
    n_seq, tq, _ = q_ref.shape
    lam = _lambda(lq1_ref, lk1_ref, lq2_ref, lk2_ref, lam_init)
    q_chunk = (past + lax.broadcasted_iota(jnp.int32, (2 * tq, 1), 0) % tq) // CHUNK
    vis_past = (lax.broadcasted_iota(jnp.int32, (2 * tq, past), 1) // CHUNK) <= q_chunk
    vis_new = ((past + lax.broadcasted_iota(jnp.int32, (2 * tq, tq), 1)) // CHUNK) <= q_chunk
    rows = []
    for sq in range(n_seq):
        heads = []
        for hd in range(HEADS):
            lo, hi = hd * HEAD_W, (hd + 1) * HEAD_W
            qs = _stack_components(q_ref[sq, :, lo:hi])
            carry = (jnp.full((2 * tq, 1), -jnp.inf, F32), jnp.zeros((2 * tq, 1), F32),
                     jnp.zeros((2 * tq, HEAD_W), F32))
            s = lax.dot_general(qs, kn_ref[sq, :, lo:hi], NT_DIMS, preferred_element_type=F32)
            carry = _softmax_step(carry, jnp.where(vis_new, s, -jnp.inf), vn_ref[sq, :, lo:hi])
            kp = kp_ref[sq, pl.ds(hd, past, stride=HEADS), :].astype(BF16)
            vp = vp_ref[sq, pl.ds(hd, past, stride=HEADS), :].astype(BF16)
            s = lax.dot_general(qs, kp, NT_DIMS, preferred_element_type=F32)
            _, l, acc = _softmax_step(carry, jnp.where(vis_past, s, -jnp.inf), vp)
            o = acc / l
            o = o[:tq] - lam * o[tq:]
            heads.append(_rms(o, sg_ref[...], SUBLN_EPS) * (1.0 - lam_init))
        rows.append(jnp.concatenate(heads, axis=-1))
    o = jnp.concatenate(rows, axis=0)
    flat = (n_seq * tq, D_MODEL)
    out = _merge_out(o, mab_ref[...].reshape(flat), gc_ref[...].reshape(flat),
                     x_ref[...].reshape(flat), wc_ref, wo_ref)
    o_ref[...] = out.reshape(n_seq, tq, D_MODEL)


def _attn_sample(q, cache_k, cache_v, k16, v16, mab, gc, x, layer, p, *, seqs):
    B, T, _ = x.shape
    past = cache_k.shape[2] // HEADS
    assert B % seqs == 0

    def tile_spec(w):
        return pl.BlockSpec((seqs, T, w), lambda b: (b, 0, 0))

    past_spec = pl.BlockSpec((None, seqs, past * HEADS, HEAD_W), lambda b: (layer, b, 0, 0))
    weight_specs, weights = _attn_weight_specs(p, layer)
    return pl.pallas_call(
        functools.partial(_attn_sample_kernel, past=past, lam_init=_lam_init(layer)),
        grid=(B // seqs,),
        in_specs=[tile_spec(ATT_W), past_spec, past_spec, tile_spec(ATT_W), tile_spec(ATT_W),
                  tile_spec(D_MODEL), tile_spec(D_MODEL), tile_spec(D_MODEL)] + weight_specs,
        out_specs=tile_spec(D_MODEL),
        out_shape=jax.ShapeDtypeStruct((B, T, D_MODEL), F32),
        compiler_params=pltpu.CompilerParams(dimension_semantics=("parallel",),
                                             vmem_limit_bytes=VMEM_LIMIT),
        name="attn_sample",
    )(q, cache_k, cache_v, k16, v16, mab, gc, x, *weights)


def _ffn_dense_kernel(x_ref, ng_ref, wg_ref, wu_ref, wd_ref, nf_ref, o_ref, *, final):
    bb, tt, _ = x_ref.shape
    x = x_ref[...].reshape(bb * tt, D_MODEL)
    h = _rms(x, ng_ref[...], EPS).astype(BF16)
    y = x
    for lo, hi in _chunks(wg_ref.shape[1], 1024):
        g = jnp.dot(h, wg_ref[:, lo:hi], preferred_element_type=F32)
        u = jnp.dot(h, wu_ref[:, lo:hi], preferred_element_type=F32)
        a = (g * jax.nn.sigmoid(g) * u).astype(BF16)
        y = y + jnp.dot(a, wd_ref[lo:hi, :], preferred_element_type=F32)
    if final:
        y = _rms(y, nf_ref[...], EPS)
    o_ref[...] = y.reshape(bb, tt, D_MODEL)


def _token_blocks(B, T, tm):
    if T >= tm:
        assert T % tm == 0
        return 1, tm
    assert tm % T == 0 and B % (tm // T) == 0
    return tm // T, T


def _ffn_dense(x, layer, p, *, tm, final):
    B, T, _ = x.shape
    bb, tt = _token_blocks(B, T, tm)
    d = layer // 2
    row_spec = pl.BlockSpec((bb, tt, D_MODEL), lambda b, t: (b, t, 0))
    return pl.pallas_call(
        functools.partial(_ffn_dense_kernel, final=final),
        grid=(B // bb, T // tt),
        in_specs=[row_spec, _layer_spec(p['norm_ffn'], layer), _layer_spec(p['w_gate_d'], d),
                  _layer_spec(p['w_up_d'], d), _layer_spec(p['w_down_d'], d),
                  _const_spec(p['norm_final'].shape)],
        out_specs=row_spec,
        out_shape=jax.ShapeDtypeStruct(x.shape, F32),
        compiler_params=pltpu.CompilerParams(dimension_semantics=("parallel", "parallel"),
                                             vmem_limit_bytes=VMEM_LIMIT),
        name="ffn_dense",
    )(x, p['norm_ffn'], p['w_gate_d'], p['w_up_d'], p['w_down_d'], p['norm_final'])


def _route_top2(logits):
    row = lax.broadcasted_iota(jnp.int32, logits.shape, 0).astype(F32)
    big = float(N_EXPERTS)
    m1 = jnp.max(logits, axis=0, keepdims=True)
    i1 = jnp.min(jnp.where(logits == m1, row, big), axis=0, keepdims=True)
    rest = jnp.where(row == i1, -jnp.inf, logits)
    m2 = jnp.max(rest, axis=0, keepdims=True)
    i2 = jnp.min(jnp.where(rest == m2, row, big), axis=0, keepdims=True)
    e2 = jnp.exp(m2 - m1)
    w1 = 1.0 / (1.0 + e2)
    w2 = e2 / (1.0 + e2)
    return jnp.where(row == i1, w1, 0.0) + jnp.where(row == i2, w2, 0.0)


def _ffn_moe_kernel(x_ref, ng_ref, wrt_ref, brt_ref, wg_ref, wu_ref, wd_ref, nf_ref, o_ref,
                    h_scr, wts_scr, rank_scr, *, final, tb, cms):
    e = pl.program_id(2)
    bb, tt, _ = x_ref.shape
    n_tok = bb * tt
    sub_blocks = _chunks(n_tok, tb)

    @pl.when(e == 0)
    def _():
        x = x_ref[...].reshape(n_tok, D_MODEL)
        h = _rms(x, ng_ref[...], EPS)
        logits = lax.dot_general(wrt_ref[...], h, NT_DIMS, preferred_element_type=F32,
                                 precision=lax.Precision.HIGHEST) + brt_ref[...]
        wts = _route_top2(logits)
        sel = (wts > 0.0).astype(BF16)
        before = (lax.broadcasted_iota(jnp.int32, (tb, tb), 0)
                  < lax.broadcasted_iota(jnp.int32, (tb, tb), 1)).astype(BF16)
        rank = jnp.concatenate(
            [jnp.dot(sel[:, lo:hi], before, preferred_element_type=F32) for lo, hi in sub_blocks],
            axis=1)
        for ee in range(N_EXPERTS):
            for sb, (lo, hi) in enumerate(sub_blocks):
                wts_scr[ee, sb] = wts[ee:ee + 1, lo:hi]
                rank_scr[ee, sb] = rank[ee:ee + 1, lo:hi]
        h_scr[...] = h.astype(BF16)
        o_ref[...] = x_ref[...]

    cm_max = cms[-1]

    def sub_block(sb, carry):
        w_row = wts_scr[e, sb]
        r_row = rank_scr[e, sb]
        sel = w_row > 0.0
        n_sel = jnp.sum(sel.astype(F32)).astype(jnp.int32)
        row0 = pl.multiple_of(sb * tb, tb)

        def run_chunk(cm, base):
            rows = (lax.broadcasted_iota(jnp.int32, (cm, tb), 0) + base).astype(F32)
            hit = (r_row == rows) & sel
            gather = jnp.where(hit, 1.0, 0.0).astype(BF16)
            xg = jnp.dot(gather, h_scr[pl.ds(row0, tb), :],
                         preferred_element_type=F32).astype(BF16)
            g = jnp.dot(xg, wg_ref[...], preferred_element_type=F32)
            u = jnp.dot(xg, wu_ref[...], preferred_element_type=F32)
            a = (g * jax.nn.sigmoid(g) * u).astype(BF16)
            y = jnp.dot(a, wd_ref[...], preferred_element_type=F32).astype(BF16)
            scatter = jnp.where(hit, w_row, 0.0).astype(BF16)
            upd = lax.dot_general(scatter, y, TN_DIMS, preferred_element_type=F32)
            if bb == 1:
                o_ref[0, pl.ds(row0, tb), :] += upd
            else:
                o_ref[pl.ds(sb * (tb // tt), tb // tt)] += upd.reshape(tb // tt, tt, D_MODEL)

        def chunk(c, carry):
            left = n_sel - c * cm_max
            size_idx = sum((left > cm).astype(jnp.int32) for cm in cms[:-1])
            lax.switch(size_idx, [functools.partial(run_chunk, cm) for cm in cms], c * cm_max)
            return carry

        n_chunks = sum((n_sel > c * cm_max).astype(jnp.int32) for c in range(-(-tb // cm_max)))
        lax.fori_loop(0, n_chunks, chunk, 0)
        return carry

    lax.fori_loop(0, n_tok // tb, sub_block, 0)

    if final:
        @pl.when(e == N_EXPERTS - 1)
        def _():
            y = o_ref[...].reshape(n_tok, D_MODEL)
            o_ref[...] = _rms(y, nf_ref[...], EPS).reshape(bb, tt, D_MODEL)


def _ffn_moe(x, layer, p, *, tm, tb, cms, final):
    B, T, _ = x.shape
    bb, tt = _token_blocks(B, T, tm)
    assert tm % tb == 0 and (bb == 1 or tb % tt == 0)
    mo = layer // 2
    row_spec = pl.BlockSpec((bb, tt, D_MODEL), lambda b, t, e: (b, t, 0))

    def expert_spec(arr):
        return pl.BlockSpec((None, None) + tuple(arr.shape[2:]), lambda b, t, e: (mo, e, 0, 0))

    return pl.pallas_call(
        functools.partial(_ffn_moe_kernel, final=final, tb=tb, cms=cms),
        grid=(B // bb, T // tt, N_EXPERTS),
        in_specs=[row_spec, _layer_spec(p['norm_ffn'], layer), _layer_spec(p['w_router_t'], mo),
                  _layer_spec(p['b_router_t'], mo), expert_spec(p['w_gate_e']),
                  expert_spec(p['w_up_e']), expert_spec(p['w_down_e']),
                  _const_spec(p['norm_final'].shape)],
        out_specs=row_spec,
        out_shape=jax.ShapeDtypeStruct(x.shape, F32),
        scratch_shapes=[pltpu.VMEM((tm, D_MODEL), BF16),
                        pltpu.VMEM((N_EXPERTS, tm // tb, 1, tb), F32),
                        pltpu.VMEM((N_EXPERTS, tm // tb, 1, tb), F32)],
        compiler_params=pltpu.CompilerParams(
            dimension_semantics=("parallel", "parallel", "arbitrary"),
            vmem_limit_bytes=VMEM_LIMIT),
        name="ffn_moe",
    )(x, p['norm_ffn'], p['w_router_t'], p['b_router_t'], p['w_gate_e'], p['w_up_e'],
      p['w_down_e'], p['norm_final'])


def _rope_tables(pos0, T, bb, tt):
    half = HEAD_DIM // 2
    inv_freq = ROPE_THETA ** (-jnp.arange(half, dtype=F32) / half)
    ang = (pos0 + jnp.arange(T)).astype(F32)[:, None] * inv_freq[None, :]
    cos, sin = jnp.cos(ang), jnp.sin(ang)
    cos = jnp.concatenate([cos, cos, cos, cos], axis=-1)
    sin = jnp.concatenate([-sin, sin, -sin, sin], axis=-1)

    def lay(a):
        a = a.reshape(T // tt, 1, tt, HEAD_W)
        return jnp.broadcast_to(a, (T // tt, bb, tt, HEAD_W)).reshape(T // tt, bb * tt, HEAD_W)

    return lay(cos), lay(sin)


def _pad_hist(state):
    return jnp.pad(state, ((0, 0), (0, 0), (HIST_PAD - state.shape[2], 0), (0, 0)))


_MATMUL_WEIGHTS = ('w_in', 'w_conv_out', 'w_pool', 'w_attn_out', 'w_o', 'w_gate_d', 'w_up_d',
                   'w_down_d', 'w_gate_e', 'w_up_e', 'w_down_e')
_ROW_VECTORS = ('norm_mix', 'pool_scale', 'lambda_q1', 'lambda_k1', 'lambda_q2', 'lambda_k2',
                'subln_g', 'norm_ffn')


def _params(w):
    p = dict(w)
    for n in _MATMUL_WEIGHTS:
        p[n] = w[n].astype(BF16)
    for n in _ROW_VECTORS:
        p[n] = w[n][:, None, :]
    p['w_router_t'] = jnp.swapaxes(w['w_router'], 1, 2)
    p['b_router_t'] = w['b_router'][:, :, None]
    p['norm_final'] = w['norm_final'][None, :]
    return p


def _new_state(co, po):
    return co[:, HIST_PAD - (CONV_K - 1):], po[:, HIST_PAD - POOL_HIST:]


def _token_mixer_prompt(x, layer, p, kv_bufs):
    B, T, _ = x.shape
    tt = min(PROMPT_TILE, T)
    tk = min(KEY_TILE, tt)
    cos, sin = _rope_tables(0, T, 1, tt)
    zero_hist = jnp.zeros((B, HIST_PAD, CONV_W), F32)
    mab, gc, qt, k16, vt, k32, v32, co, po = _mixer_in(
        x, zero_hist, zero_hist, cos, sin, layer, p, kv_bufs, bb=1, tt=tt, pos0=0, tk=tk)
    x = _attn_prompt(qt, k16, vt, mab, gc, x, layer, p, tq=tt, tk=tk)
    return (x,) + _new_state(co, po) + ((k32, v32),)


def _token_mixer_sample(x, cache_k, cache_v, state_conv, state_pool, layer, p, kv_bufs):
    past = cache_k.shape[2]
    B, T, _ = x.shape
    bb = min(SAMPLE_MIXER_SEQS, B)
    cos, sin = _rope_tables(past, T, bb, T)
    mab, gc, q, k16, v16, k32, v32, co, po = _mixer_in(
        x, _pad_hist(state_conv), _pad_hist(state_pool), cos, sin, layer, p, kv_bufs,
        bb=bb, tt=T, pos0=past, tk=None)
    rows = cache_k.shape[:2] + (past * HEADS, HEAD_W)
    x = _attn_sample(q, cache_k.reshape(rows), cache_v.reshape(rows), k16, v16, mab, gc, x, layer,
                     p, seqs=min(SAMPLE_ATTN_SEQS, B))
    return (x,) + _new_state(co, po) + ((k32, v32),)


def _channel_mixer(x, layer, p, *, final):
    tokens = x.shape[0] * x.shape[1]
    if layer % 2 == 0:
        return _ffn_dense(x, layer, p, tm=min(DENSE_TOKENS, tokens), final=final)
    tm = min(MOE_TOKENS, tokens)
    return _ffn_moe(x, layer, p, tm=tm, tb=min(MOE_SUB_TOKENS, tm), cms=MOE_CHUNK_ROWS,
                    final=final)


def kernel(x_prompt, x_sample, cache_k, cache_v, state_conv, state_pool, norm_mix, w_in, conv_w,
           w_conv_out, w_pool, pool_scale, lambda_q1, lambda_k1, lambda_q2, lambda_k2, subln_g,
           w_attn_out, w_o, norm_ffn, w_gate_d, w_up_d, w_down_d, w_router, b_router, w_gate_e,
           w_up_e, w_down_e, norm_final):
    depth = w_in.shape[0]
    p = _params(dict(
        norm_mix=norm_mix, w_in=w_in, conv_w=conv_w, w_conv_out=w_conv_out, w_pool=w_pool,
        pool_scale=pool_scale, lambda_q1=lambda_q1, lambda_k1=lambda_k1, lambda_q2=lambda_q2,
        lambda_k2=lambda_k2, subln_g=subln_g, w_attn_out=w_attn_out, w_o=w_o, norm_ffn=norm_ffn,
        w_gate_d=w_gate_d, w_up_d=w_up_d, w_down_d=w_down_d, w_router=w_router,
        b_router=b_router, w_gate_e=w_gate_e, w_up_e=w_up_e, w_down_e=w_down_e,
        norm_final=norm_final))

    xp, xs = x_prompt, x_sample
    kv_p = kv_s = None
    states = [[] for _ in range(4)]
    for l in range(depth):
        final = l == depth - 1
        xp, cp, pp, kv_p = _token_mixer_prompt(xp, l, p, kv_p)
        xs, cs, ps, kv_s = _token_mixer_sample(xs, cache_k, cache_v, state_conv, state_pool, l, p,
                                               kv_s)
        for lst, a in zip(states, (cp, pp, cs, ps)):
            lst.append(a)
        xp = _channel_mixer(xp, l, p, final=final)
        xs = _channel_mixer(xs, l, p, final=final)

    def heads_view(a):
        return a.reshape(a.shape[0], a.shape[1], a.shape[2] // HEADS, HEADS, HEAD_W)

    return (xp, xs, heads_view(kv_p[0]), heads_view(kv_p[1]), jnp.stack(states[0]),
            jnp.stack(states[1]), heads_view(kv_s[0]), heads_view(kv_s[1]),
            jnp.stack(states[2]), jnp.stack(states[3]))
```

```python
import functools
import math

import jax
import jax.numpy as jnp
from jax import lax
from jax.experimental import pallas as pl
from jax.experimental.pallas import tpu as pltpu

D_MODEL = 1024
CHUNK = 64
CONV_W = 512
CONV_K = 3
POOL_W = 512
POOL_GC = 128
POOL_WINDOWS = (2, 4, 8, 16)
POOL_HIST = 15
HEADS = 4
HEAD_DIM = 64
HEAD_W = 2 * HEAD_DIM
ATT_W = HEADS * HEAD_W
ROPE_THETA = 10000.0
N_EXPERTS = 8
EPS = 1e-6
SUBLN_EPS = 1e-5

C_XA, C_BA, C_CA, C_U, C_Q, C_K, C_V, C_GA, C_GB, C_GC, C_END = (
    0, 512, 1024, 1536, 2048, 2560, 3072, 3584, 4608, 5632, 6656)

HIST_PAD = 16
Q_SCALE = HEAD_DIM ** -0.5 * math.log2(math.e)
DENOM_ROWS = 16

VMEM_LIMIT = 56 * 1024 * 1024
PROMPT_TILE = 512
KEY_TILE = 512
SAMPLE_MIXER_SEQS = 16
SAMPLE_ATTN_SEQS = 4
DENSE_TOKENS = 1024
MOE_TOKENS = 1024
MOE_SUB_TOKENS = 512
MOE_CHUNK_ROWS = (128, 160, 192, 224, 256)

F32 = jnp.float32
BF16 = jnp.bfloat16
NT_DIMS = (((1,), (1,)), ((), ()))
TN_DIMS = (((0,), (0,)), ((), ()))


def _const_spec(shape):
    nd = len(shape)
    return pl.BlockSpec(tuple(shape), lambda *_: (0,) * nd, pipeline_mode=pl.Buffered(1))


def _layer_spec(arr, layer):
    nd = arr.ndim
    return pl.BlockSpec((None,) + tuple(arr.shape[1:]), lambda *_: (layer,) + (0,) * (nd - 1),
                        pipeline_mode=pl.Buffered(1))


def _rms(x, g, eps):
    return x * lax.rsqrt(jnp.mean(x * x, axis=-1, keepdims=True) + eps) * g


def _chunks(n, step):
    return [(lo, min(lo + step, n)) for lo in range(0, n, step)]


def _mixer_in_kernel(*refs, bb, tt, pos0, tk, aliased):
    (x_ref, hc_ref, hp_ref, cos_ref, sin_ref, ng_ref, win_ref, cw_ref, wa_ref, wp_ref,
     ps_ref) = refs[:11]
    refs = refs[11 + (2 if aliased else 0):]
    (mab_ref, gc_ref, q_ref, k16_ref, v16_ref, k32_ref, v32_ref, co_ref, po_ref, cbuf,
     ubuf) = refs
    t = pl.program_id(1)
    m = bb * tt
    x = x_ref[...].reshape(m, D_MODEL)
    h = _rms(x, ng_ref[...], EPS).astype(BF16)

    def seg(lo, hi):
        return jnp.dot(h, win_ref[:, lo:hi], preferred_element_type=F32)

    @pl.when(t == 0)
    def _():
        cbuf[:, 0:HIST_PAD, :] = hc_ref[...]
        ubuf[:, 0:HIST_PAD, :] = hp_ref[...]

    z_ca, z_xa, z_ba, z_u = seg(C_CA, C_U), seg(C_XA, C_BA), seg(C_BA, C_CA), seg(C_U, C_Q)
    z_q, z_k, z_v = seg(C_Q, C_K), seg(C_K, C_V), seg(C_V, C_GA)

    cin = (z_ca * z_xa).reshape(bb, tt, CONV_W)
    cbuf[:, HIST_PAD:, :] = cin
    conv = cbuf[:, HIST_PAD - 2:HIST_PAD - 2 + tt, :] * cw_ref[0:1, :]
    conv = conv + cbuf[:, HIST_PAD - 1:HIST_PAD - 1 + tt, :] * cw_ref[1:2, :]
    conv = conv + cin * cw_ref[2:3, :]
    ya = jnp.dot((z_ba * conv.reshape(m, CONV_W)).astype(BF16), wa_ref[...],
                 preferred_element_type=F32)

    ubuf[:, HIST_PAD:, :] = z_u.reshape(bb, tt, POOL_W)
    pos = pos0 + t * tt + lax.broadcasted_iota(jnp.int32, (bb, tt, POOL_GC), 1)
    yb_parts = []
    for gi, win in enumerate(POOL_WINDOWS):
        lo, hi = gi * POOL_GC, (gi + 1) * POOL_GC
        cur = ubuf[:, HIST_PAD:HIST_PAD + tt, lo:hi]
        s = cur
        for k in range(1, win):
            s = s + ubuf[:, HIST_PAD - k:HIST_PAD - k + tt, lo:hi]
        cnt = jnp.minimum(pos + 1, win).astype(F32)
        pooled = s / cnt - cur
        yb_parts.append(jnp.dot(pooled.reshape(m, POOL_GC).astype(BF16), wp_ref[gi],
                                preferred_element_type=F32))
    yb = jnp.concatenate(yb_parts, axis=-1) * ps_ref[...]

    cos4 = jnp.concatenate([cos_ref[...]] * HEADS, axis=-1)
    sin4 = jnp.concatenate([sin_ref[...]] * HEADS, axis=-1)
    lane = lax.broadcasted_iota(jnp.int32, (m, ATT_W), 1)
    first_half = (lane & (HEAD_DIM // 2)) == 0

    def rope(z):
        swapped = jnp.where(first_half, pltpu.roll(z, ATT_W - HEAD_DIM // 2, axis=1),
                            pltpu.roll(z, HEAD_DIM // 2, axis=1))
        return z * cos4 + swapped * sin4

    q = rope(z_q) * Q_SCALE
    k = rope(z_k)
    v = z_v
    for hd in range(HEADS):
        lo, hi = hd * HEAD_W, (hd + 1) * HEAD_W
        k32_ref[:, pl.ds(hd, tt, stride=HEADS), :] = k[:, lo:hi].reshape(bb, tt, HEAD_W)
        v32_ref[:, pl.ds(hd, tt, stride=HEADS), :] = v[:, lo:hi].reshape(bb, tt, HEAD_W)
    k16_ref[...] = k.astype(BF16).reshape(bb, tt, ATT_W)
    if tk is None:
        q_ref[...] = q.astype(BF16).reshape(bb, tt, ATT_W)
        v16_ref[...] = v.astype(BF16).reshape(bb, tt, ATT_W)
    else:
        q_ref[0] = q.T.astype(BF16)
        for c in range(tt // tk):
            v16_ref[0, c] = v[c * tk:(c + 1) * tk, :].T.astype(BF16)

    ga = jax.nn.sigmoid(seg(C_GA, C_GB))
    gb = jax.nn.sigmoid(seg(C_GB, C_GC))
    mab_ref[...] = (ga * ya + gb * yb).astype(BF16).reshape(bb, tt, D_MODEL)
    gc_ref[...] = jax.nn.sigmoid(seg(C_GC, C_END)).astype(BF16).reshape(bb, tt, D_MODEL)

    last_c = cbuf[:, tt:tt + HIST_PAD, :]
    last_p = ubuf[:, tt:tt + HIST_PAD, :]
    co_ref[...] = last_c
    po_ref[...] = last_p
    cbuf[:, 0:HIST_PAD, :] = last_c
    ubuf[:, 0:HIST_PAD, :] = last_p


def _mixer_in(x, hist_c, hist_p, cos, sin, layer, p, kv_bufs, *, bb, tt, pos0, tk):
    B, T, _ = x.shape
    depth = p['w_in'].shape[0]
    assert B % bb == 0 and T % tt == 0 and tt >= HIST_PAD and tt % 8 == 0
    assert tk is None or (bb == 1 and tt % tk == 0)
    grid = (B // bb, T // tt)
    m = bb * tt

    def seq_spec(w):
        return pl.BlockSpec((bb, tt, w), lambda b, t: (b, t, 0))

    if hist_c.ndim == 4:
        hist_spec = pl.BlockSpec((None, bb, HIST_PAD, CONV_W), lambda b, t: (layer, b, 0, 0))
    else:
        hist_spec = pl.BlockSpec((bb, HIST_PAD, CONV_W), lambda b, t: (b, 0, 0))
    state_spec = pl.BlockSpec((bb, HIST_PAD, CONV_W), lambda b, t: (b, 0, 0))
    rope_spec = pl.BlockSpec((None, m, HEAD_W), lambda b, t: (t, 0, 0))
    kv_spec = pl.BlockSpec((None, bb, tt * HEADS, HEAD_W), lambda b, t: (layer, b, t, 0))
    kv_shape = jax.ShapeDtypeStruct((depth, B, T * HEADS, HEAD_W), F32)
    if tk is None:
        q_shape, q_spec = jax.ShapeDtypeStruct((B, T, ATT_W), BF16), seq_spec(ATT_W)
        v_shape, v_spec = q_shape, q_spec
    else:
        q_shape = jax.ShapeDtypeStruct((B, ATT_W, T), BF16)
        q_spec = pl.BlockSpec((1, ATT_W, tt), lambda b, t: (b, 0, t))
        v_shape = jax.ShapeDtypeStruct((B, T // tk, ATT_W, tk), BF16)
        v_spec = pl.BlockSpec((1, tt // tk, ATT_W, tk), lambda b, t: (b, t, 0, 0))

    out_shape = (
        jax.ShapeDtypeStruct((B, T, D_MODEL), BF16),
        jax.ShapeDtypeStruct((B, T, D_MODEL), BF16),
        q_shape,
        jax.ShapeDtypeStruct((B, T, ATT_W), BF16),
        v_shape,
        kv_shape, kv_shape,
        jax.ShapeDtypeStruct((B, HIST_PAD, CONV_W), F32),
        jax.ShapeDtypeStruct((B, HIST_PAD, POOL_W), F32),
    )
    out_specs = (seq_spec(D_MODEL), seq_spec(D_MODEL), q_spec, seq_spec(ATT_W), v_spec,
                 kv_spec, kv_spec, state_spec, state_spec)
    in_specs = [seq_spec(D_MODEL), hist_spec, hist_spec, rope_spec, rope_spec,
                _layer_spec(p['norm_mix'], layer), _layer_spec(p['w_in'], layer),
                _layer_spec(p['conv_w'], layer), _layer_spec(p['w_conv_out'], layer),
                _layer_spec(p['w_pool'], layer), _layer_spec(p['pool_scale'], layer)]
    args = [x, hist_c, hist_p, cos, sin, p['norm_mix'], p['w_in'], p['conv_w'], p['w_conv_out'],
            p['w_pool'], p['pool_scale']]
    aliases = {}
    if kv_bufs is not None:
        in_specs += [pl.BlockSpec(memory_space=pl.ANY)] * 2
        aliases = {len(args): 5, len(args) + 1: 6}
        args += list(kv_bufs)
    return pl.pallas_call(
        functools.partial(_mixer_in_kernel, bb=bb, tt=tt, pos0=pos0, tk=tk,
                          aliased=kv_bufs is not None),
        grid=grid, in_specs=in_specs, out_specs=out_specs, out_shape=out_shape,
        scratch_shapes=[pltpu.VMEM((bb, HIST_PAD + tt, CONV_W), F32),
                        pltpu.VMEM((bb, HIST_PAD + tt, POOL_W), F32)],
        input_output_aliases=aliases,
        compiler_params=pltpu.CompilerParams(dimension_semantics=("parallel", "arbitrary"),
                                             vmem_limit_bytes=VMEM_LIMIT),
        name="mixer_in",
    )(*args)


def _lambda(lq1_ref, lk1_ref, lq2_ref, lk2_ref, lam_init):
    a = jnp.exp(jnp.sum(lq1_ref[...] * lk1_ref[...], axis=-1, keepdims=True))
    b = jnp.exp(jnp.sum(lq2_ref[...] * lk2_ref[...], axis=-1, keepdims=True))
    return a - b + lam_init


def _lam_init(layer):
    return 0.8 - 0.6 * math.exp(-0.3 * layer)


def _merge_out(o, mab, gc, x, wc_ref, wo_ref):
    yc = jnp.dot(o.astype(BF16), wc_ref[...], preferred_element_type=F32)
    merged = mab.astype(F32) + gc.astype(F32) * yc
    return x + jnp.dot(merged.astype(BF16), wo_ref[...], preferred_element_type=F32)


def _attn_weight_specs(p, layer):
    names = ('lambda_q1', 'lambda_k1', 'lambda_q2', 'lambda_k2', 'subln_g', 'w_attn_out', 'w_o')
    return [_layer_spec(p[n], layer) for n in names], [p[n] for n in names]


def _attn_prompt_kernel(qt_ref, k_ref, vt_ref, mab_ref, gc_ref, x_ref, lq1_ref, lk1_ref, lq2_ref,
                        lk2_ref, sg_ref, wc_ref, wo_ref, o_ref, p_scr, acc_scr, *, tq, tk,
                        lam_init):
    i = pl.program_id(1)
    n_diag = tq // tk
    lam = _lambda(lq1_ref, lk1_ref, lq2_ref, lk2_ref, lam_init)
    feat = lax.broadcasted_iota(jnp.int32, (HEAD_W, tq), 0)
    qs = []
    for hd in range(HEADS):
        qt = qt_ref[0, hd * HEAD_W:(hd + 1) * HEAD_W, :]
        zero = jnp.zeros_like(qt)
        qs.append(jnp.concatenate([jnp.where(feat < HEAD_DIM, qt, zero),
                                   jnp.where(feat < HEAD_DIM, zero, qt)], axis=1))

    def scores(j):
        off = pl.multiple_of(j * tk, tk)
        return [jnp.dot(k_ref[0, pl.ds(off, tk), hd * HEAD_W:(hd + 1) * HEAD_W], qs[hd],
                        preferred_element_type=F32) for hd in range(HEADS)]

    def softmax(s, m):
        m_out, alpha_out = [], []
        for hd in range(HEADS):
            m_new = jnp.maximum(m[hd], jnp.max(s[hd], axis=0, keepdims=True))
            p_scr[hd] = jnp.exp2(s[hd] - m_new).astype(BF16)
            m_out.append(m_new)
            alpha_out.append(jnp.exp2(m[hd] - m_new))
        return m_out, alpha_out

    ones_rows = jnp.ones((DENOM_ROWS, tk), BF16)

    def flush(j, alpha):
        for hd in range(HEADS):
            vt1 = jnp.concatenate([vt_ref[0, j, hd * HEAD_W:(hd + 1) * HEAD_W, :], ones_rows],
                                  axis=0)
            acc_scr[hd] = alpha[hd] * acc_scr[hd] + jnp.dot(vt1, p_scr[hd],
                                                            preferred_element_type=F32)

    acc_scr[...] = jnp.zeros_like(acc_scr)
    m = [jnp.full((1, 2 * tq), -jnp.inf, F32)] * HEADS
    alpha = [jnp.zeros((1, 2 * tq), F32)] * HEADS
    q_chunk = (lax.broadcasted_iota(jnp.int32, (tk, 2 * tq), 1) % tq) // CHUNK
    k_row = lax.broadcasted_iota(jnp.int32, (tk, 2 * tq), 0)
    for d in range(n_diag):
        j = i * n_diag + d
        s = scores(j)
        if d > 0:
            flush(j - 1, alpha)
        visible = (k_row + d * tk) // CHUNK <= q_chunk
        m, alpha = softmax([jnp.where(visible, sh, -jnp.inf) for sh in s], m)

    def body(j, carry):
        pend, alpha, m = carry
        s = scores(j)
        flush(pend, alpha)
        m, alpha = softmax(s, m)
        return j, alpha, m

    pend, alpha, m = lax.fori_loop(0, i * n_diag, body, (i * n_diag + n_diag - 1, alpha, m))
    flush(pend, alpha)
    heads = []
    for hd in range(HEADS):
        ot = acc_scr[hd, 0:HEAD_W, :] / acc_scr[hd, HEAD_W:HEAD_W + 1, :]
        ot = ot[:, :tq] - lam * ot[:, tq:]
        ot = ot * lax.rsqrt(jnp.mean(ot * ot, axis=0, keepdims=True) + SUBLN_EPS)
        heads.append(ot.T * sg_ref[...] * (1.0 - lam_init))
    o = jnp.concatenate(heads, axis=-1)
    o_ref[0] = _merge_out(o, mab_ref[0], gc_ref[0], x_ref[0], wc_ref, wo_ref)


def _attn_prompt(qt, k16, vt, mab, gc, x, layer, p, *, tq, tk):
    B, T, _ = x.shape
    assert T % tq == 0 and tq % tk == 0 and tk % CHUNK == 0 and vt.shape[3] == tk

    def tile_spec(w):
        return pl.BlockSpec((1, tq, w), lambda b, i: (b, i, 0))

    weight_specs, weights = _attn_weight_specs(p, layer)
    return pl.pallas_call(
        functools.partial(_attn_prompt_kernel, tq=tq, tk=tk, lam_init=_lam_init(layer)),
        grid=(B, T // tq),
        in_specs=[pl.BlockSpec((1, ATT_W, tq), lambda b, i: (b, 0, i)),
                  pl.BlockSpec((1, T, ATT_W), lambda b, i: (b, 0, 0), pipeline_mode=pl.Buffered(1)),
                  pl.BlockSpec((1, T // tk, ATT_W, tk), lambda b, i: (b, 0, 0, 0),
                               pipeline_mode=pl.Buffered(1)),
                  tile_spec(D_MODEL), tile_spec(D_MODEL), tile_spec(D_MODEL)] + weight_specs,
        out_specs=tile_spec(D_MODEL),
        out_shape=jax.ShapeDtypeStruct((B, T, D_MODEL), F32),
        scratch_shapes=[pltpu.VMEM((HEADS, tk, 2 * tq), BF16),
                        pltpu.VMEM((HEADS, HEAD_W + DENOM_ROWS, 2 * tq), F32)],
        compiler_params=pltpu.CompilerParams(dimension_semantics=("parallel", "arbitrary"),
                                             vmem_limit_bytes=VMEM_LIMIT),
        name="attn_prompt",
    )(qt, k16, vt, mab, gc, x, *weights)


def _stack_components(qh):
    lane = lax.broadcasted_iota(jnp.int32, qh.shape, 1)
    zero = jnp.zeros_like(qh)
    return jnp.concatenate([jnp.where(lane < HEAD_DIM, qh, zero),
                            jnp.where(lane < HEAD_DIM, zero, qh)], axis=0)


def _softmax_step(carry, s, vj):
    m, l, acc = carry
    m_new = jnp.maximum(m, jnp.max(s, axis=-1, keepdims=True))
    pr = jnp.exp2(s - m_new)
    alpha = jnp.exp2(m - m_new)
    l = alpha * l + jnp.sum(pr, axis=-1, keepdims=True)
    acc = alpha * acc + jnp.dot(pr.astype(BF16), vj, preferred_element_type=F32)
    return m_new, l, acc


def _attn_sample_kernel(q_ref, kp_ref, vp_ref, kn_ref, vn_ref, mab_ref, gc_ref, x_ref, lq1_ref,
                        lk1_ref, lq2_ref, lk2_ref, sg_ref, wc_ref, wo_ref, o_ref, *, past,
                        lam_init):
    n_seq, tq, _ = q_ref.shape
    lam = _lambda(lq1_ref, lk1_ref, lq2_ref, lk2_ref, lam_init)
    q_chunk = (past + lax.broadcasted_iota(jnp.int32, (2 * tq, 1), 0) % tq) // CHUNK
    vis_past = (lax.broadcasted_iota(jnp.int32, (2 * tq, past), 1) // CHUNK) <= q_chunk
    vis_new = ((past + lax.broadcasted_iota(jnp.int32, (2 * tq, tq), 1)) // CHUNK) <= q_chunk
    rows = []
    for sq in range(n_seq):
        heads = []
        for hd in range(HEADS):
            lo, hi = hd * HEAD_W, (hd + 1) * HEAD_W
            qs = _stack_components(q_ref[sq, :, lo:hi])
            carry = (jnp.full((2 * tq, 1), -jnp.inf, F32), jnp.zeros((2 * tq, 1), F32),
                     jnp.zeros((2 * tq, HEAD_W), F32))
            s = lax.dot_general(qs, kn_ref[sq, :, lo:hi], NT_DIMS, preferred_element_type=F32)
            carry = _softmax_step(carry, jnp.where(vis_new, s, -jnp.inf), vn_ref[sq, :, lo:hi])
            kp = kp_ref[sq, pl.ds(hd, past, stride=HEADS), :].astype(BF16)
            vp = vp_ref[sq, pl.ds(hd, past, stride=HEADS), :].astype(BF16)
            s = lax.dot_general(qs, kp, NT_DIMS, preferred_element_type=F32)
            _, l, acc = _softmax_step(carry, jnp.where(vis_past, s, -jnp.inf), vp)
            o = acc / l
            o = o[:tq] - lam * o[tq:]
            heads.append(_rms(o, sg_ref[...], SUBLN_EPS) * (1.0 - lam_init))
        rows.append(jnp.concatenate(heads, axis=-1))
    o = jnp.concatenate(rows, axis=0)
    flat = (n_seq * tq, D_MODEL)
    out = _merge_out(o, mab_ref[...].reshape(flat), gc_ref[...].reshape(flat),
                     x_ref[...].reshape(flat), wc_ref, wo_ref)
    o_ref[...] = out.reshape(n_seq, tq, D_MODEL)


def _attn_sample(q, cache_k, cache_v, k16, v16, mab, gc, x, layer, p, *, seqs):
    B, T, _ = x.shape
    past = cache_k.shape[2] // HEADS
    assert B % seqs == 0

    def tile_spec(w):
        return pl.BlockSpec((seqs, T, w), lambda b: (b, 0, 0))

    past_spec = pl.BlockSpec((None, seqs, past * HEADS, HEAD_W), lambda b: (layer, b, 0, 0))
    weight_specs, weights = _attn_weight_specs(p, layer)
    return pl.pallas_call(
        functools.partial(_attn_sample_kernel, past=past, lam_init=_lam_init(layer)),
        grid=(B // seqs,),
        in_specs=[tile_spec(ATT_W), past_spec, past_spec, tile_spec(ATT_W), tile_spec(ATT_W),
                  tile_spec(D_MODEL), tile_spec(D_MODEL), tile_spec(D_MODEL)] + weight_specs,
        out_specs=tile_spec(D_MODEL),
        out_shape=jax.ShapeDtypeStruct((B, T, D_MODEL), F32),
        compiler_params=pltpu.CompilerParams(dimension_semantics=("parallel",),
                                             vmem_limit_bytes=VMEM_LIMIT),
        name="attn_sample",
    )(q, cache_k, cache_v, k16, v16, mab, gc, x, *weights)


def _ffn_dense_kernel(x_ref, ng_ref, wg_ref, wu_ref, wd_ref, nf_ref, o_ref, *, final):
    bb, tt, _ = x_ref.shape
    x = x_ref[...].reshape(bb * tt, D_MODEL)
    h = _rms(x, ng_ref[...], EPS).astype(BF16)
    y = x
    for lo, hi in _chunks(wg_ref.shape[1], 1024):
        g = jnp.dot(h, wg_ref[:, lo:hi], preferred_element_type=F32)
        u = jnp.dot(h, wu_ref[:, lo:hi], preferred_element_type=F32)
        a = (g * jax.nn.sigmoid(g) * u).astype(BF16)
        y = y + jnp.dot(a, wd_ref[lo:hi, :], preferred_element_type=F32)
    if final:
        y = _rms(y, nf_ref[...], EPS)
    o_ref[...] = y.reshape(bb, tt, D_MODEL)


def _token_blocks(B, T, tm):
    if T >= tm:
        assert T % tm == 0
        return 1, tm
    assert tm % T == 0 and B % (tm // T) == 0
    return tm // T, T


def _ffn_dense(x, layer, p, *, tm, final):
    B, T, _ = x.shape
    bb, tt = _token_blocks(B, T, tm)
    d = layer // 2
    row_spec = pl.BlockSpec((bb, tt, D_MODEL), lambda b, t: (b, t, 0))
    return pl.pallas_call(
        functools.partial(_ffn_dense_kernel, final=final),
        grid=(B // bb, T // tt),
        in_specs=[row_spec, _layer_spec(p['norm_ffn'], layer), _layer_spec(p['w_gate_d'], d),
                  _layer_spec(p['w_up_d'], d), _layer_spec(p['w_down_d'], d),
                  _const_spec(p['norm_final'].shape)],
        out_specs=row_spec,
        out_shape=jax.ShapeDtypeStruct(x.shape, F32),
        compiler_params=pltpu.CompilerParams(dimension_semantics=("parallel", "parallel"),
                                             vmem_limit_bytes=VMEM_LIMIT),
        name="ffn_dense",
    )(x, p['norm_ffn'], p['w_gate_d'], p['w_up_d'], p['w_down_d'], p['norm_final'])


def _route_top2(logits):
    row = lax.broadcasted_iota(jnp.int32, logits.shape, 0).astype(F32)
    big = float(N_EXPERTS)
    m1 = jnp.max(logits, axis=0, keepdims=True)
    i1 = jnp.min(jnp.where(logits == m1, row, big), axis=0, keepdims=True)
    rest = jnp.where(row == i1, -jnp.inf, logits)
    m2 = jnp.max(rest, axis=0, keepdims=True)
    i2 = jnp.min(jnp.where(rest == m2, row, big), axis=0, keepdims=True)
    e2 = jnp.exp(m2 - m1)
    w1 = 1.0 / (1.0 + e2)
    w2 = e2 / (1.0 + e2)
    return jnp.where(row == i1, w1, 0.0) + jnp.where(row == i2, w2, 0.0)


def _ffn_moe_kernel(x_ref, ng_ref, wrt_ref, brt_ref, wg_ref, wu_ref, wd_ref, nf_ref, o_ref,
                    h_scr, wts_scr, rank_scr, *, final, tb, cms):
    e = pl.program_id(2)
    bb, tt, _ = x_ref.shape
    n_tok = bb * tt
    sub_blocks = _chunks(n_tok, tb)

    @pl.when(e == 0)
    def _():
        x = x_ref[...].reshape(n_tok, D_MODEL)
        h = _rms(x, ng_ref[...], EPS)
        logits = lax.dot_general(wrt_ref[...], h, NT_DIMS, preferred_element_type=F32,
                                 precision=lax.Precision.HIGHEST) + brt_ref[...]
        wts = _route_top2(logits)
        sel = (wts > 0.0).astype(BF16)
        before = (lax.broadcasted_iota(jnp.int32, (tb, tb), 0)
                  < lax.broadcasted_iota(jnp.int32, (tb, tb), 1)).astype(BF16)
        rank = jnp.concatenate(
            [jnp.dot(sel[:, lo:hi], before, preferred_element_type=F32) for lo, hi in sub_blocks],
            axis=1)
        for ee in range(N_EXPERTS):
            for sb, (lo, hi) in enumerate(sub_blocks):
                wts_scr[ee, sb] = wts[ee:ee + 1, lo:hi]
                rank_scr[ee, sb] = rank[ee:ee + 1, lo:hi]
        h_scr[...] = h.astype(BF16)
        o_ref[...] = x_ref[...]

    cm_max = cms[-1]

    def sub_block(sb, carry):
        w_row = wts_scr[e, sb]
        r_row = rank_scr[e, sb]
        sel = w_row > 0.0
        n_sel = jnp.sum(sel.astype(F32)).astype(jnp.int32)
        row0 = pl.multiple_of(sb * tb, tb)

        def run_chunk(cm, base):
            rows = (lax.broadcasted_iota(jnp.int32, (cm, tb), 0) + base).astype(F32)
            hit = (r_row == rows) & sel
            gather = jnp.where(hit, 1.0, 0.0).astype(BF16)
            xg = jnp.dot(gather, h_scr[pl.ds(row0, tb), :],
                         preferred_element_type=F32).astype(BF16)
            g = jnp.dot(xg, wg_ref[...], preferred_element_type=F32)
            u = jnp.dot(xg, wu_ref[...], preferred_element_type=F32)
            a = (g * jax.nn.sigmoid(g) * u).astype(BF16)
            y = jnp.dot(a, wd_ref[...], preferred_element_type=F32).astype(BF16)
            scatter = jnp.where(hit, w_row, 0.0).astype(BF16)
            upd = lax.dot_general(scatter, y, TN_DIMS, preferred_element_type=F32)
            if bb == 1:
                o_ref[0, pl.ds(row0, tb), :] += upd
            else:
                o_ref[pl.ds(sb * (tb // tt), tb // tt)] += upd.reshape(tb // tt, tt, D_MODEL)

        def chunk(c, carry):
            left = n_sel - c * cm_max
            size_idx = sum((left > cm).astype(jnp.int32) for cm in cms[:-1])
            lax.switch(size_idx, [functools.partial(run_chunk, cm) for cm in cms], c * cm_max)
            return carry

        n_chunks = sum((n_sel > c * cm_max).astype(jnp.int32) for c in range(-(-tb // cm_max)))
        lax.fori_loop(0, n_chunks, chunk, 0)
        return carry

    lax.fori_loop(0, n_tok // tb, sub_block, 0)

    if final:
        @pl.when(e == N_EXPERTS - 1)
        def _():
            y = o_ref[...].reshape(n_tok, D_MODEL)
            o_ref[...] = _rms(y, nf_ref[...], EPS).reshape(bb, tt, D_MODEL)


def _ffn_moe(x, layer, p, *, tm, tb, cms, final):
    B, T, _ = x.shape
    bb, tt = _token_blocks(B, T, tm)
    assert tm % tb == 0 and (bb == 1 or tb % tt == 0)
    mo = layer // 2
    row_spec = pl.BlockSpec((bb, tt, D_MODEL), lambda b, t, e: (b, t, 0))

    def expert_spec(arr):
        return pl.BlockSpec((None, None) + tuple(arr.shape[2:]), lambda b, t, e: (mo, e, 0, 0))

    return pl.pallas_call(
        functools.partial(_ffn_moe_kernel, final=final, tb=tb, cms=cms),
        grid=(B // bb, T // tt, N_EXPERTS),
        in_specs=[row_spec, _layer_spec(p['norm_ffn'], layer), _layer_spec(p['w_router_t'], mo),
                  _layer_spec(p['b_router_t'], mo), expert_spec(p['w_gate_e']),
                  expert_spec(p['w_up_e']), expert_spec(p['w_down_e']),
                  _const_spec(p['norm_final'].shape)],
        out_specs=row_spec,
        out_shape=jax.ShapeDtypeStruct(x.shape, F32),
        scratch_shapes=[pltpu.VMEM((tm, D_MODEL), BF16),
                        pltpu.VMEM((N_EXPERTS, tm // tb, 1, tb), F32),
                        pltpu.VMEM((N_EXPERTS, tm // tb, 1, tb), F32)],
        compiler_params=pltpu.CompilerParams(
            dimension_semantics=("parallel", "parallel", "arbitrary"),
            vmem_limit_bytes=VMEM_LIMIT),
        name="ffn_moe",
    )(x, p['norm_ffn'], p['w_router_t'], p['b_router_t'], p['w_gate_e'], p['w_up_e'],
      p['w_down_e'], p['norm_final'])


def _rope_tables(pos0, T, bb, tt):
    half = HEAD_DIM // 2
    inv_freq = ROPE_THETA ** (-jnp.arange(half, dtype=F32) / half)
    ang = (pos0 + jnp.arange(T)).astype(F32)[:, None] * inv_freq[None, :]
    cos, sin = jnp.cos(ang), jnp.sin(ang)
    cos = jnp.concatenate([cos, cos, cos, cos], axis=-1)
    sin = jnp.concatenate([-sin, sin, -sin, sin], axis=-1)

    def lay(a):
        a = a.reshape(T // tt, 1, tt, HEAD_W)
        return jnp.broadcast_to(a, (T // tt, bb, tt, HEAD_W)).reshape(T // tt, bb * tt, HEAD_W)

    return lay(cos), lay(sin)


def _pad_hist(state):
    return jnp.pad(state, ((0, 0), (0, 0), (HIST_PAD - state.shape[2], 0), (0, 0)))


_MATMUL_WEIGHTS = ('w_in', 'w_conv_out', 'w_pool', 'w_attn_out', 'w_o', 'w_gate_d', 'w_up_d',
                   'w_down_d', 'w_gate_e', 'w_up_e', 'w_down_e')
_ROW_VECTORS = ('norm_mix', 'pool_scale', 'lambda_q1', 'lambda_k1', 'lambda_q2', 'lambda_k2',
                'subln_g', 'norm_ffn')


def _params(w):
    p = dict(w)
    for n in _MATMUL_WEIGHTS:
        p[n] = w[n].astype(BF16)
    for n in _ROW_VECTORS:
        p[n] = w[n][:, None, :]
    p['w_router_t'] = jnp.swapaxes(w['w_router'], 1, 2)
    p['b_router_t'] = w['b_router'][:, :, None]
    p['norm_final'] = w['norm_final'][None, :]
    return p


def _new_state(co, po):
    return co[:, HIST_PAD - (CONV_K - 1):], po[:, HIST_PAD - POOL_HIST:]


def _token_mixer_prompt(x, layer, p, kv_bufs):
    B, T, _ = x.shape
    tt = min(PROMPT_TILE, T)
    tk = min(KEY_TILE, tt)
    cos, sin = _rope_tables(0, T, 1, tt)
    zero_hist = jnp.zeros((B, HIST_PAD, CONV_W), F32)
    mab, gc, qt, k16, vt, k32, v32, co, po = _mixer_in(
        x, zero_hist, zero_hist, cos, sin, layer, p, kv_bufs, bb=1, tt=tt, pos0=0, tk=tk)
    x = _attn_prompt(qt, k16, vt, mab, gc, x, layer, p, tq=tt, tk=tk)
    return (x,) + _new_state(co, po) + ((k32, v32),)


def _token_mixer_sample(x, cache_k, cache_v, state_conv, state_pool, layer, p, kv_bufs):
    past = cache_k.shape[2]
    B, T, _ = x.shape
    bb = min(SAMPLE_MIXER_SEQS, B)
    cos, sin = _rope_tables(past, T, bb, T)
    mab, gc, q, k16, v16, k32, v32, co, po = _mixer_in(
        x, _pad_hist(state_conv), _pad_hist(state_pool), cos, sin, layer, p, kv_bufs,
        bb=bb, tt=T, pos0=past, tk=None)
    rows = cache_k.shape[:2] + (past * HEADS, HEAD_W)
    x = _attn_sample(q, cache_k.reshape(rows), cache_v.reshape(rows), k16, v16, mab, gc, x, layer,
                     p, seqs=min(SAMPLE_ATTN_SEQS, B))
    return (x,) + _new_state(co, po) + ((k32, v32),)


def _channel_mixer(x, layer, p, *, final):
    tokens = x.shape[0] * x.shape[1]
    if layer % 2 == 0:
        return _ffn_dense(x, layer, p, tm=min(DENSE_TOKENS, tokens), final=final)
    tm = min(MOE_TOKENS, tokens)
    return _ffn_moe(x, layer, p, tm=tm, tb=min(MOE_SUB_TOKENS, tm), cms=MOE_CHUNK_ROWS,
                    final=final)


def kernel(x_prompt, x_sample, cache_k, cache_v, state_conv, state_pool, norm_mix, w_in, conv_w,
           w_conv_out, w_pool, pool_scale, lambda_q1, lambda_k1, lambda_q2, lambda_k2, subln_g,
           w_attn_out, w_o, norm_ffn, w_gate_d, w_up_d, w_down_d, w_router, b_router, w_gate_e,
           w_up_e, w_down_e, norm_final):
    depth = w_in.shape[0]
    p = _params(dict(
        norm_mix=norm_mix, w_in=w_in, conv_w=conv_w, w_conv_out=w_conv_out, w_pool=w_pool,
        pool_scale=pool_scale, lambda_q1=lambda_q1, lambda_k1=lambda_k1, lambda_q2=lambda_q2,
        lambda_k2=lambda_k2, subln_g=subln_g, w_attn_out=w_attn_out, w_o=w_o, norm_ffn=norm_ffn,
        w_gate_d=w_gate_d, w_up_d=w_up_d, w_down_d=w_down_d, w_router=w_router,
        b_router=b_router, w_gate_e=w_gate_e, w_up_e=w_up_e, w_down_e=w_down_e,
        norm_final=norm_final))

    xp, xs = x_prompt, x_sample
    kv_p = kv_s = None
    states = [[] for _ in range(4)]
    for l in range(depth):
        final = l == depth - 1
        xp, cp, pp, kv_p = _token_mixer_prompt(xp, l, p, kv_p)
        xs, cs, ps, kv_s = _token_mixer_sample(xs, cache_k, cache_v, state_conv, state_pool, l, p,
                                               kv_s)
        for lst, a in zip(states, (cp, pp, cs, ps)):
            lst.append(a)
        xp = _channel_mixer(xp, l, p, final=final)
        xs = _channel_mixer(xs, l, p, final=final)

    def heads_view(a):
        return a.reshape(a.shape[0], a.shape[1], a.shape[2] // HEADS, HEADS, HEAD_W)

    return (xp, xs, heads_view(kv_p[0]), heads_view(kv_p[1]), jnp.stack(states[0]),
            jnp.stack(states[1]), heads_view(kv_s[0]), heads_view(kv_s[1]),
            jnp.stack(states[2]), jnp.stack(states[3]))
```

```python
import functools
import math

import jax
import jax.numpy as jnp
from jax import lax
from jax.experimental import pallas as pl
from jax.experimental.pallas import tpu as pltpu

D_MODEL = 1024
CHUNK = 64
CONV_W = 512
CONV_K = 3
POOL_W = 512
POOL_GC = 128
POOL_WINDOWS = (2, 4, 8, 16)
POOL_HIST = 15
HEADS = 4
HEAD_DIM = 64
HEAD_W = 2 * HEAD_DIM
ATT_W = HEADS * HEAD_W
ROPE_THETA = 10000.0
N_EXPERTS = 8
EPS = 1e-6
SUBLN_EPS = 1e-5

C_XA, C_BA, C_CA, C_U, C_Q, C_K, C_V, C_GA, C_GB, C_GC, C_END = (
    0, 512, 1024, 1536, 2048, 2560, 3072, 3584, 4608, 5632, 6656)

HIST_PAD = 16
Q_SCALE = HEAD_DIM ** -0.5 * math.log2(math.e)
DENOM_ROWS = 16

VMEM_LIMIT = 56 * 1024 * 1024
PROMPT_TILE = 512
KEY_TILE = 512
SAMPLE_MIXER_SEQS = 16
SAMPLE_ATTN_SEQS = 4
DENSE_TOKENS = 1024
MOE_TOKENS = 1024
MOE_SUB_TOKENS = 512
MOE_CHUNK_ROWS = (128, 160, 192, 224, 256)

F32 = jnp.float32
BF16 = jnp.bfloat16
NT_DIMS = (((1,), (1,)), ((), ()))
TN_DIMS = (((0,), (0,)), ((), ()))


def _const_spec(shape):
    nd = len(shape)
    return pl.BlockSpec(tuple(shape), lambda *_: (0,) * nd, pipeline_mode=pl.Buffered(1))


def _layer_spec(arr, layer):
    nd = arr.ndim
    return pl.BlockSpec((None,) + tuple(arr.shape[1:]), lambda *_: (layer,) + (0,) * (nd - 1),
                        pipeline_mode=pl.Buffered(1))


def _rms(x, g, eps):
    return x * lax.rsqrt(jnp.mean(x * x, axis=-1, keepdims=True) + eps) * g


def _chunks(n, step):
    return [(lo, min(lo + step, n)) for lo in range(0, n, step)]


def _mixer_in_kernel(*refs, bb, tt, pos0, tk, aliased):
    (x_ref, hc_ref, hp_ref, cos_ref, sin_ref, ng_ref, win_ref, cw_ref, wa_ref, wp_ref,
     ps_ref) = refs[:11]
    refs = refs[11 + (2 if aliased else 0):]
    (mab_ref, gc_ref, q_ref, k16_ref, v16_ref, k32_ref, v32_ref, co_ref, po_ref, cbuf,
     ubuf) = refs
    t = pl.program_id(1)
    m = bb * tt
    x = x_ref[...].reshape(m, D_MODEL)
    h = _rms(x, ng_ref[...], EPS).astype(BF16)

    def seg(lo, hi):
        return jnp.dot(h, win_ref[:, lo:hi], preferred_element_type=F32)

    @pl.when(t == 0)
    def _():
        cbuf[:, 0:HIST_PAD, :] = hc_ref[...]
        ubuf[:, 0:HIST_PAD, :] = hp_ref[...]

    z_ca, z_xa, z_ba, z_u = seg(C_CA, C_U), seg(C_XA, C_BA), seg(C_BA, C_CA), seg(C_U, C_Q)
    z_q, z_k, z_v = seg(C_Q, C_K), seg(C_K, C_V), seg(C_V, C_GA)

    cin = (z_ca * z_xa).reshape(bb, tt, CONV_W)
    cbuf[:, HIST_PAD:, :] = cin
    conv = cbuf[:, HIST_PAD - 2:HIST_PAD - 2 + tt, :] * cw_ref[0:1, :]
    conv = conv + cbuf[:, HIST_PAD - 1:HIST_PAD - 1 + tt, :] * cw_ref[1:2, :]
    conv = conv + cin * cw_ref[2:3, :]
    ya = jnp.dot((z_ba * conv.reshape(m, CONV_W)).astype(BF16), wa_ref[...],
                 preferred_element_type=F32)

    ubuf[:, HIST_PAD:, :] = z_u.reshape(bb, tt, POOL_W)
    pos = pos0 + t * tt + lax.broadcasted_iota(jnp.int32, (bb, tt, POOL_GC), 1)
    yb_parts = []
    for gi, win in enumerate(POOL_WINDOWS):
        lo, hi = gi * POOL_GC, (gi + 1) * POOL_GC
        cur = ubuf[:, HIST_PAD:HIST_PAD + tt, lo:hi]
        s = cur
        for k in range(1, win):
            s = s + ubuf[:, HIST_PAD - k:HIST_PAD - k + tt, lo:hi]
        cnt = jnp.minimum(pos + 1, win).astype(F32)
        pooled = s / cnt - cur
        yb_parts.append(jnp.dot(pooled.reshape(m, POOL_GC).astype(BF16), wp_ref[gi],
                                preferred_element_type=F32))
    yb = jnp.concatenate(yb_parts, axis=-1) * ps_ref[...]

    cos4 = jnp.concatenate([cos_ref[...]] * HEADS, axis=-1)
    sin4 = jnp.concatenate([sin_ref[...]] * HEADS, axis=-1)
    lane = lax.broadcasted_iota(jnp.int32, (m, ATT_W), 1)
    first_half = (lane & (HEAD_DIM // 2)) == 0

    def rope(z):
        swapped = jnp.where(first_half, pltpu.roll(z, ATT_W - HEAD_DIM // 2, axis=1),
                            pltpu.roll(z, HEAD_DIM // 2, axis=1))
        return z * cos4 + swapped * sin4

    q = rope(z_q) * Q_SCALE
    k = rope(z_k)
    v = z_v
    for hd in range(HEADS):
        lo, hi = hd * HEAD_W, (hd + 1) * HEAD_W
        k32_ref[:, pl.ds(hd, tt, stride=HEADS), :] = k[:, lo:hi].reshape(bb, tt, HEAD_W)
        v32_ref[:, pl.ds(hd, tt, stride=HEADS), :] = v[:, lo:hi].reshape(bb, tt, HEAD_W)
    k16_ref[...] = k.astype(BF16).reshape(bb, tt, ATT_W)
    if tk is None:
        q_ref[...] = q.astype(BF16).reshape(bb, tt, ATT_W)
        v16_ref[...] = v.astype(BF16).reshape(bb, tt, ATT_W)
    else:
        q_ref[0] = q.T.astype(BF16)
        for c in range(tt // tk):
            v16_ref[0, c] = v[c * tk:(c + 1) * tk, :].T.astype(BF16)

    ga = jax.nn.sigmoid(seg(C_GA, C_GB))
    gb = jax.nn.sigmoid(seg(C_GB, C_GC))
    mab_ref[...] = (ga * ya + gb * yb).astype(BF16).reshape(bb, tt, D_MODEL)
    gc_ref[...] = jax.nn.sigmoid(seg(C_GC, C_END)).astype(BF16).reshape(bb, tt, D_MODEL)

    last_c = cbuf[:, tt:tt + HIST_PAD, :]
    last_p = ubuf[:, tt:tt + HIST_PAD, :]
    co_ref[...] = last_c
    po_ref[...] = last_p
    cbuf[:, 0:HIST_PAD, :] = last_c
    ubuf[:, 0:HIST_PAD, :] = last_p


def _mixer_in(x, hist_c, hist_p, cos, sin, layer, p, kv_bufs, *, bb, tt, pos0, tk):
    B, T, _ = x.shape
    depth = p['w_in'].shape[0]
    assert B % bb == 0 and T % tt == 0 and tt >= HIST_PAD and tt % 8 == 0
    assert tk is None or (bb == 1 and tt % tk == 0)
    grid = (B // bb, T // tt)
    m = bb * tt

    def seq_spec(w):
        return pl.BlockSpec((bb, tt, w), lambda b, t: (b, t, 0))

    if hist_c.ndim == 4:
        hist_spec = pl.BlockSpec((None, bb, HIST_PAD, CONV_W), lambda b, t: (layer, b, 0, 0))
    else:
        hist_spec = pl.BlockSpec((bb, HIST_PAD, CONV_W), lambda b, t: (b, 0, 0))
    state_spec = pl.BlockSpec((bb, HIST_PAD, CONV_W), lambda b, t: (b, 0, 0))
    rope_spec = pl.BlockSpec((None, m, HEAD_W), lambda b, t: (t, 0, 0))
    kv_spec = pl.BlockSpec((None, bb, tt * HEADS, HEAD_W), lambda b, t: (layer, b, t, 0))
    kv_shape = jax.ShapeDtypeStruct((depth, B, T * HEADS, HEAD_W), F32)
    if tk is None:
        q_shape, q_spec = jax.ShapeDtypeStruct((B, T, ATT_W), BF16), seq_spec(ATT_W)
        v_shape, v_spec = q_shape, q_spec
    else:
        q_shape = jax.ShapeDtypeStruct((B, ATT_W, T), BF16)
        q_spec = pl.BlockSpec((1, ATT_W, tt), lambda b, t: (b, 0, t))
        v_shape = jax.ShapeDtypeStruct((B, T // tk, ATT_W, tk), BF16)
        v_spec = pl.BlockSpec((1, tt // tk, ATT_W, tk), lambda b, t: (b, t, 0, 0))

    out_shape = (
        jax.ShapeDtypeStruct((B, T, D_MODEL), BF16),
        jax.ShapeDtypeStruct((B, T, D_MODEL), BF16),
        q_shape,
        jax.ShapeDtypeStruct((B, T, ATT_W), BF16),
        v_shape,
        kv_shape, kv_shape,
        jax.ShapeDtypeStruct((B, HIST_PAD, CONV_W), F32),
        jax.ShapeDtypeStruct((B, HIST_PAD, POOL_W), F32),
    )
    out_specs = (seq_spec(D_MODEL), seq_spec(D_MODEL), q_spec, seq_spec(ATT_W), v_spec,
                 kv_spec, kv_spec, state_spec, state_spec)
    in_specs = [seq_spec(D_MODEL), hist_spec, hist_spec, rope_spec, rope_spec,
                _layer_spec(p['norm_mix'], layer), _layer_spec(p['w_in'], layer),
                _layer_spec(p['conv_w'], layer), _layer_spec(p['w_conv_out'], layer),
                _layer_spec(p['w_pool'], layer), _layer_spec(p['pool_scale'], layer)]
    args = [x, hist_c, hist_p, cos, sin, p['norm_mix'], p['w_in'], p['conv_w'], p['w_conv_out'],
            p['w_pool'], p['pool_scale']]
    aliases = {}
    if kv_bufs is not None:
        in_specs += [pl.BlockSpec(memory_space=pl.ANY)] * 2
        aliases = {len(args): 5, len(args) + 1: 6}
        args += list(kv_bufs)
    return pl.pallas_call(
        functools.partial(_mixer_in_kernel, bb=bb, tt=tt, pos0=pos0, tk=tk,
                          aliased=kv_bufs is not None),
        grid=grid, in_specs=in_specs, out_specs=out_specs, out_shape=out_shape,
        scratch_shapes=[pltpu.VMEM((bb, HIST_PAD + tt, CONV_W), F32),
                        pltpu.VMEM((bb, HIST_PAD + tt, POOL_W), F32)],
        input_output_aliases=aliases,
        compiler_params=pltpu.CompilerParams(dimension_semantics=("parallel", "arbitrary"),
                                             vmem_limit_bytes=VMEM_LIMIT),
        name="mixer_in",
    )(*args)


def _lambda(lq1_ref, lk1_ref, lq2_ref, lk2_ref, lam_init):
    a = jnp.exp(jnp.sum(lq1_ref[...] * lk1_ref[...], axis=-1, keepdims=True))
    b = jnp.exp(jnp.sum(lq2_ref[...] * lk2_ref[...], axis=-1, keepdims=True))
    return a - b + lam_init


def _lam_init(layer):
    return 0.8 - 0.6 * math.exp(-0.3 * layer)


def _merge_out(o, mab, gc, x, wc_ref, wo_ref):
    yc = jnp.dot(o.astype(BF16), wc_ref[...], preferred_element_type=F32)
    merged = mab.astype(F32) + gc.astype(F32) * yc
    return x + jnp.dot(merged.astype(BF16), wo_ref[...], preferred_element_type=F32)


def _attn_weight_specs(p, layer):
    names = ('lambda_q1', 'lambda_k1', 'lambda_q2', 'lambda_k2', 'subln_g', 'w_attn_out', 'w_o')
    return [_layer_spec(p[n], layer) for n in names], [p[n] for n in names]


def _attn_prompt_kernel(qt_ref, k_ref, vt_ref, mab_ref, gc_ref, x_ref, lq1_ref, lk1_ref, lq2_ref,
                        lk2_ref, sg_ref, wc_ref, wo_ref, o_ref, p_scr, acc_scr, *, tq, tk,
                        lam_init):
    i = pl.program_id(1)
    n_diag = tq // tk
    lam = _lambda(lq1_ref, lk1_ref, lq2_ref, lk2_ref, lam_init)
    feat = lax.broadcasted_iota(jnp.int32, (HEAD_W, tq), 0)
    qs = []
    for hd in range(HEADS):
        qt = qt_ref[0, hd * HEAD_W:(hd + 1) * HEAD_W, :]
        zero = jnp.zeros_like(qt)
        qs.append(jnp.concatenate([jnp.where(feat < HEAD_DIM, qt, zero),
                                   jnp.where(feat < HEAD_DIM, zero, qt)], axis=1))

    def scores(j):
        off = pl.multiple_of(j * tk, tk)
        return [jnp.dot(k_ref[0, pl.ds(off, tk), hd * HEAD_W:(hd + 1) * HEAD_W], qs[hd],
                        preferred_element_type=F32) for hd in range(HEADS)]

    def softmax(s, m):
        m_out, alpha_out = [], []
        for hd in range(HEADS):
            m_new = jnp.maximum(m[hd], jnp.max(s[hd], axis=0, keepdims=True))
            p_scr[hd] = jnp.exp2(s[hd] - m_new).astype(BF16)
            m_out.append(m_new)
            alpha_out.append(jnp.exp2(m[hd] - m_new))
        return m_out, alpha_out

    ones_rows = jnp.ones((DENOM_ROWS, tk), BF16)

    def flush(j, alpha):
        for hd in range(HEADS):
            vt1 = jnp.concatenate([vt_ref[0, j, hd * HEAD_W:(hd + 1) * HEAD_W, :], ones_rows],
                                  axis=0)
            acc_scr[hd] = alpha[hd] * acc_scr[hd] + jnp.dot(vt1, p_scr[hd],
                                                            preferred_element_type=F32)

    acc_scr[...] = jnp.zeros_like(acc_scr)
    m = [jnp.full((1, 2 * tq), -jnp.inf, F32)] * HEADS
    alpha = [jnp.zeros((1, 2 * tq), F32)] * HEADS
    q_chunk = (lax.broadcasted_iota(jnp.int32, (tk, 2 * tq), 1) % tq) // CHUNK
    k_row = lax.broadcasted_iota(jnp.int32, (tk, 2 * tq), 0)
    for d in range(n_diag):
        j = i * n_diag + d
        s = scores(j)
        if d > 0:
            flush(j - 1, alpha)
        visible = (k_row + d * tk) // CHUNK <= q_chunk
        m, alpha = softmax([jnp.where(visible, sh, -jnp.inf) for sh in s], m)

    def body(j, carry):
        pend, alpha, m = carry
        s = scores(j)
        flush(pend, alpha)
        m, alpha = softmax(s, m)
        return j, alpha, m

    pend, alpha, m = lax.fori_loop(0, i * n_diag, body, (i * n_diag + n_diag - 1, alpha, m))
    flush(pend, alpha)
    heads = []
    for hd in range(HEADS):
        ot = acc_scr[hd, 0:HEAD_W, :] / acc_scr[hd, HEAD_W:HEAD_W + 1, :]
        ot = ot[:, :tq] - lam * ot[:, tq:]
        ot = ot * lax.rsqrt(jnp.mean(ot * ot, axis=0, keepdims=True) + SUBLN_EPS)
        heads.append(ot.T * sg_ref[...] * (1.0 - lam_init))
    o = jnp.concatenate(heads, axis=-1)
    o_ref[0] = _merge_out(o, mab_ref[0], gc_ref[0], x_ref[0], wc_ref, wo_ref)


def _attn_prompt(qt, k16, vt, mab, gc, x, layer, p, *, tq, tk):
    B, T, _ = x.shape
    assert T % tq == 0 and tq % tk == 0 and tk % CHUNK == 0 and vt.shape[3] == tk

    def tile_spec(w):
        return pl.BlockSpec((1, tq, w), lambda b, i: (b, i, 0))

    weight_specs, weights = _attn_weight_specs(p, layer)
    return pl.pallas_call(
        functools.partial(_attn_prompt_kernel, tq=tq, tk=tk, lam_init=_lam_init(layer)),
        grid=(B, T // tq),
        in_specs=[pl.BlockSpec((1, ATT_W, tq), lambda b, i: (b, 0, i)),
                  pl.BlockSpec((1, T, ATT_W), lambda b, i: (b, 0, 0)),
                  pl.BlockSpec((1, T // tk, ATT_W, tk), lambda b, i: (b, 0, 0, 0)),
                  tile_spec(D_MODEL), tile_spec(D_MODEL), tile_spec(D_MODEL)] + weight_specs,
        out_specs=tile_spec(D_MODEL),
        out_shape=jax.ShapeDtypeStruct((B, T, D_MODEL), F32),
        scratch_shapes=[pltpu.VMEM((HEADS, tk, 2 * tq), BF16),
                        pltpu.VMEM((HEADS, HEAD_W + DENOM_ROWS, 2 * tq), F32)],
        compiler_params=pltpu.CompilerParams(dimension_semantics=("parallel", "arbitrary"),
                                             vmem_limit_bytes=VMEM_LIMIT),
        name="attn_prompt",
    )(qt, k16, vt, mab, gc, x, *weights)


def _stack_components(qh):
    lane = lax.broadcasted_iota(jnp.int32, qh.shape, 1)
    zero = jnp.zeros_like(qh)
    return jnp.concatenate([jnp.where(lane < HEAD_DIM, qh, zero),
                            jnp.where(lane < HEAD_DIM, zero, qh)], axis=0)


def _softmax_step(carry, s, vj):
    m, l, acc = carry
    m_new = jnp.maximum(m, jnp.max(s, axis=-1, keepdims=True))
    pr = jnp.exp2(s - m_new)
    alpha = jnp.exp2(m - m_new)
    l = alpha * l + jnp.sum(pr, axis=-1, keepdims=True)
    acc = alpha * acc + jnp.dot(pr.astype(BF16), vj, preferred_element_type=F32)
    return m_new, l, acc


def _attn_sample_kernel(q_ref, kp_ref, vp_ref, kn_ref, vn_ref, mab_ref, gc_ref, x_ref, lq1_ref,
                        lk1_ref, lq2_ref, lk2_ref, sg_ref, wc_ref, wo_ref, o_ref, *, past,
                        lam_init):
    n_seq, tq, _ = q_ref.shape
    lam = _lambda(lq1_ref, lk1_ref, lq2_ref, lk2_ref, lam_init)
    q_chunk = (past + lax.broadcasted_iota(jnp.int32, (2 * tq, 1), 0) % tq) // CHUNK
    vis_past = (lax.broadcasted_iota(jnp.int32, (2 * tq, past), 1) // CHUNK) <= q_chunk
    vis_new = ((past + lax.broadcasted_iota(jnp.int32, (2 * tq, tq), 1)) // CHUNK) <= q_chunk
    rows = []
    for sq in range(n_seq):
        heads = []
        for hd in range(HEADS):
            lo, hi = hd * HEAD_W, (hd + 1) * HEAD_W
            qs = _stack_components(q_ref[sq, :, lo:hi])
            carry = (jnp.full((2 * tq, 1), -jnp.inf, F32), jnp.zeros((2 * tq, 1), F32),
                     jnp.zeros((2 * tq, HEAD_W), F32))
            s = lax.dot_general(qs, kn_ref[sq, :, lo:hi], NT_DIMS, preferred_element_type=F32)
            carry = _softmax_step(carry, jnp.where(vis_new, s, -jnp.inf), vn_ref[sq, :, lo:hi])
            kp = kp_ref[sq, pl.ds(hd, past, stride=HEADS), :].astype(BF16)
            vp = vp_ref[sq, pl.ds(hd, past, stride=HEADS), :].astype(BF16)
            s = lax.dot_general(qs, kp, NT_DIMS, preferred_element_type=F32)
            _, l, acc = _softmax_step(carry, jnp.where(vis_past, s, -jnp.inf), vp)
            o = acc / l
            o = o[:tq] - lam * o[tq:]
            heads.append(_rms(o, sg_ref[...], SUBLN_EPS) * (1.0 - lam_init))
        rows.append(jnp.concatenate(heads, axis=-1))
    o = jnp.concatenate(rows, axis=0)
    flat = (n_seq * tq, D_MODEL)
    out = _merge_out(o, mab_ref[...].reshape(flat), gc_ref[...].reshape(flat),
                     x_ref[...].reshape(flat), wc_ref, wo_ref)
    o_ref[...] = out.reshape(n_seq, tq, D_MODEL)


def _attn_sample(q, cache_k, cache_v, k16, v16, mab, gc, x, layer, p, *, seqs):
    B, T, _ = x.shape
    past = cache_k.shape[2] // HEADS
    assert B % seqs == 0

    def tile_spec(w):
        return pl.BlockSpec((seqs, T, w), lambda b: (b, 0, 0))

    past_spec = pl.BlockSpec((None, seqs, past * HEADS, HEAD_W), lambda b: (layer, b, 0, 0))
    weight_specs, weights = _attn_weight_specs(p, layer)
    return pl.pallas_call(
        functools.partial(_attn_sample_kernel, past=past, lam_init=_lam_init(layer)),
        grid=(B // seqs,),
        in_specs=[tile_spec(ATT_W), past_spec, past_spec, tile_spec(ATT_W), tile_spec(ATT_W),
                  tile_spec(D_MODEL), tile_spec(D_MODEL), tile_spec(D_MODEL)] + weight_specs,
        out_specs=tile_spec(D_MODEL),
        out_shape=jax.ShapeDtypeStruct((B, T, D_MODEL), F32),
        compiler_params=pltpu.CompilerParams(dimension_semantics=("parallel",),
                                             vmem_limit_bytes=VMEM_LIMIT),
        name="attn_sample",
    )(q, cache_k, cache_v, k16, v16, mab, gc, x, *weights)


def _ffn_dense_kernel(x_ref, ng_ref, wg_ref, wu_ref, wd_ref, nf_ref, o_ref, *, final):
    bb, tt, _ = x_ref.shape
    x = x_ref[...].reshape(bb * tt, D_MODEL)
    h = _rms(x, ng_ref[...], EPS).astype(BF16)
    y = x
    for lo, hi in _chunks(wg_ref.shape[1], 1024):
        g = jnp.dot(h, wg_ref[:, lo:hi], preferred_element_type=F32)
        u = jnp.dot(h, wu_ref[:, lo:hi], preferred_element_type=F32)
        a = (g * jax.nn.sigmoid(g) * u).astype(BF16)
        y = y + jnp.dot(a, wd_ref[lo:hi, :], preferred_element_type=F32)
    if final:
        y = _rms(y, nf_ref[...], EPS)
    o_ref[...] = y.reshape(bb, tt, D_MODEL)


def _token_blocks(B, T, tm):
    if T >= tm:
        assert T % tm == 0
        return 1, tm
    assert tm % T == 0 and B % (tm // T) == 0
    return tm // T, T


def _ffn_dense(x, layer, p, *, tm, final):
    B, T, _ = x.shape
    bb, tt = _token_blocks(B, T, tm)
    d = layer // 2
    row_spec = pl.BlockSpec((bb, tt, D_MODEL), lambda b, t: (b, t, 0))
    return pl.pallas_call(
        functools.partial(_ffn_dense_kernel, final=final),
        grid=(B // bb, T // tt),
        in_specs=[row_spec, _layer_spec(p['norm_ffn'], layer), _layer_spec(p['w_gate_d'], d),
                  _layer_spec(p['w_up_d'], d), _layer_spec(p['w_down_d'], d),
                  _const_spec(p['norm_final'].shape)],
        out_specs=row_spec,
        out_shape=jax.ShapeDtypeStruct(x.shape, F32),
        compiler_params=pltpu.CompilerParams(dimension_semantics=("parallel", "parallel"),
                                             vmem_limit_bytes=VMEM_LIMIT),
        name="ffn_dense",
    )(x, p['norm_ffn'], p['w_gate_d'], p['w_up_d'], p['w_down_d'], p['norm_final'])


def _route_top2(logits):
    row = lax.broadcasted_iota(jnp.int32, logits.shape, 0).astype(F32)
    big = float(N_EXPERTS)
    m1 = jnp.max(logits, axis=0, keepdims=True)
    i1 = jnp.min(jnp.where(logits == m1, row, big), axis=0, keepdims=True)
    rest = jnp.where(row == i1, -jnp.inf, logits)
    m2 = jnp.max(rest, axis=0, keepdims=True)
    i2 = jnp.min(jnp.where(rest == m2, row, big), axis=0, keepdims=True)
    e2 = jnp.exp(m2 - m1)
    w1 = 1.0 / (1.0 + e2)
    w2 = e2 / (1.0 + e2)
    return jnp.where(row == i1, w1, 0.0) + jnp.where(row == i2, w2, 0.0)


def _ffn_moe_kernel(x_ref, ng_ref, wrt_ref, brt_ref, wg_ref, wu_ref, wd_ref, nf_ref, o_ref,
                    h_scr, wts_scr, rank_scr, *, final, tb, cms):
    e = pl.program_id(2)
    bb, tt, _ = x_ref.shape
    n_tok = bb * tt
    sub_blocks = _chunks(n_tok, tb)

    @pl.when(e == 0)
    def _():
        x = x_ref[...].reshape(n_tok, D_MODEL)
        h = _rms(x, ng_ref[...], EPS)
        logits = lax.dot_general(wrt_ref[...], h, NT_DIMS, preferred_element_type=F32,
                                 precision=lax.Precision.HIGHEST) + brt_ref[...]
        wts = _route_top2(logits)
        sel = (wts > 0.0).astype(BF16)
        before = (lax.broadcasted_iota(jnp.int32, (tb, tb), 0)
                  < lax.broadcasted_iota(jnp.int32, (tb, tb), 1)).astype(BF16)
        rank = jnp.concatenate(
            [jnp.dot(sel[:, lo:hi], before, preferred_element_type=F32) for lo, hi in sub_blocks],
            axis=1)
        for ee in range(N_EXPERTS):
            for sb, (lo, hi) in enumerate(sub_blocks):
                wts_scr[ee, sb] = wts[ee:ee + 1, lo:hi]
                rank_scr[ee, sb] = rank[ee:ee + 1, lo:hi]
        h_scr[...] = h.astype(BF16)
        o_ref[...] = x_ref[...]

    cm_max = cms[-1]

    def sub_block(sb, carry):
        w_row = wts_scr[e, sb]
        r_row = rank_scr[e, sb]
        sel = w_row > 0.0
        n_sel = jnp.sum(sel.astype(F32)).astype(jnp.int32)
        row0 = pl.multiple_of(sb * tb, tb)

        def run_chunk(cm, base):
            rows = (lax.broadcasted_iota(jnp.int32, (cm, tb), 0) + base).astype(F32)
            hit = (r_row == rows) & sel
            gather = jnp.where(hit, 1.0, 0.0).astype(BF16)
            xg = jnp.dot(gather, h_scr[pl.ds(row0, tb), :],
                         preferred_element_type=F32).astype(BF16)
            g = jnp.dot(xg, wg_ref[...], preferred_element_type=F32)
            u = jnp.dot(xg, wu_ref[...], preferred_element_type=F32)
            a = (g * jax.nn.sigmoid(g) * u).astype(BF16)
            y = jnp.dot(a, wd_ref[...], preferred_element_type=F32).astype(BF16)
            scatter = jnp.where(hit, w_row, 0.0).astype(BF16)
            upd = lax.dot_general(scatter, y, TN_DIMS, preferred_element_type=F32)
            if bb == 1:
                o_ref[0, pl.ds(row0, tb), :] += upd
            else:
                o_ref[pl.ds(sb * (tb // tt), tb // tt)] += upd.reshape(tb // tt, tt, D_MODEL)

        def chunk(c, carry):
            left = n_sel - c * cm_max
            size_idx = sum((left > cm).astype(jnp.int32) for cm in cms[:-1])
            lax.switch(size_idx, [functools.partial(run_chunk, cm) for cm in cms], c * cm_max)
            return carry

        n_chunks = sum((n_sel > c * cm_max).astype(jnp.int32) for c in range(-(-tb // cm_max)))
        lax.fori_loop(0, n_chunks, chunk, 0)
        return carry

    lax.fori_loop(0, n_tok // tb, sub_block, 0)

    if final:
        @pl.when(e == N_EXPERTS - 1)
        def _():
            y = o_ref[...].reshape(n_tok, D_MODEL)
            o_ref[...] = _rms(y, nf_ref[...], EPS).reshape(bb, tt, D_MODEL)


def _ffn_moe(x, layer, p, *, tm, tb, cms, final):
    B, T, _ = x.shape
    bb, tt = _token_blocks(B, T, tm)
    assert tm % tb == 0 and (bb == 1 or tb % tt == 0)
    mo = layer // 2
    row_spec = pl.BlockSpec((bb, tt, D_MODEL), lambda b, t, e: (b, t, 0))

    def expert_spec(arr):
        return pl.BlockSpec((None, None) + tuple(arr.shape[2:]), lambda b, t, e: (mo, e, 0, 0))

    return pl.pallas_call(
        functools.partial(_ffn_moe_kernel, final=final, tb=tb, cms=cms),
        grid=(B // bb, T // tt, N_EXPERTS),
        in_specs=[row_spec, _layer_spec(p['norm_ffn'], layer), _layer_spec(p['w_router_t'], mo),
                  _layer_spec(p['b_router_t'], mo), expert_spec(p['w_gate_e']),
                  expert_spec(p['w_up_e']), expert_spec(p['w_down_e']),
                  _const_spec(p['norm_final'].shape)],
        out_specs=row_spec,
        out_shape=jax.ShapeDtypeStruct(x.shape, F32),
        scratch_shapes=[pltpu.VMEM((tm, D_MODEL), BF16),
                        pltpu.VMEM((N_EXPERTS, tm // tb, 1, tb), F32),
                        pltpu.VMEM((N_EXPERTS, tm // tb, 1, tb), F32)],
        compiler_params=pltpu.CompilerParams(
            dimension_semantics=("parallel", "parallel", "arbitrary"),
            vmem_limit_bytes=VMEM_LIMIT),
        name="ffn_moe",
    )(x, p['norm_ffn'], p['w_router_t'], p['b_router_t'], p['w_gate_e'], p['w_up_e'],
      p['w_down_e'], p['norm_final'])


def _rope_tables(pos0, T, bb, tt):
    half = HEAD_DIM // 2
    inv_freq = ROPE_THETA ** (-jnp.arange(half, dtype=F32) / half)
    ang = (pos0 + jnp.arange(T)).astype(F32)[:, None] * inv_freq[None, :]
    cos, sin = jnp.cos(ang), jnp.sin(ang)
    cos = jnp.concatenate([cos, cos, cos, cos], axis=-1)
    sin = jnp.concatenate([-sin, sin, -sin, sin], axis=-1)

    def lay(a):
        a = a.reshape(T // tt, 1, tt, HEAD_W)
        return jnp.broadcast_to(a, (T // tt, bb, tt, HEAD_W)).reshape(T // tt, bb * tt, HEAD_W)

    return lay(cos), lay(sin)


def _pad_hist(state):
    return jnp.pad(state, ((0, 0), (0, 0), (HIST_PAD - state.shape[2], 0), (0, 0)))


_MATMUL_WEIGHTS = ('w_in', 'w_conv_out', 'w_pool', 'w_attn_out', 'w_o', 'w_gate_d', 'w_up_d',
                   'w_down_d', 'w_gate_e', 'w_up_e', 'w_down_e')
_ROW_VECTORS = ('norm_mix', 'pool_scale', 'lambda_q1', 'lambda_k1', 'lambda_q2', 'lambda_k2',
                'subln_g', 'norm_ffn')


def _params(w):
    p = dict(w)
    for n in _MATMUL_WEIGHTS:
        p[n] = w[n].astype(BF16)
    for n in _ROW_VECTORS:
        p[n] = w[n][:, None, :]
    p['w_router_t'] = jnp.swapaxes(w['w_router'], 1, 2)
    p['b_router_t'] = w['b_router'][:, :, None]
    p['norm_final'] = w['norm_final'][None, :]
    return p


def _new_state(co, po):
    return co[:, HIST_PAD - (CONV_K - 1):], po[:, HIST_PAD - POOL_HIST:]


def _token_mixer_prompt(x, layer, p, kv_bufs):
    B, T, _ = x.shape
    tt = min(PROMPT_TILE, T)
    tk = min(KEY_TILE, tt)
    cos, sin = _rope_tables(0, T, 1, tt)
    zero_hist = jnp.zeros((B, HIST_PAD, CONV_W), F32)
    mab, gc, qt, k16, vt, k32, v32, co, po = _mixer_in(
        x, zero_hist, zero_hist, cos, sin, layer, p, kv_bufs, bb=1, tt=tt, pos0=0, tk=tk)
    x = _attn_prompt(qt, k16, vt, mab, gc, x, layer, p, tq=tt, tk=tk)
    return (x,) + _new_state(co, po) + ((k32, v32),)


def _token_mixer_sample(x, cache_k, cache_v, state_conv, state_pool, layer, p, kv_bufs):
    past = cache_k.shape[2]
    B, T, _ = x.shape
    bb = min(SAMPLE_MIXER_SEQS, B)
    cos, sin = _rope_tables(past, T, bb, T)
    mab, gc, q, k16, v16, k32, v32, co, po = _mixer_in(
        x, _pad_hist(state_conv), _pad_hist(state_pool), cos, sin, layer, p, kv_bufs,
        bb=bb, tt=T, pos0=past, tk=None)
    rows = cache_k.shape[:2] + (past * HEADS, HEAD_W)
    x = _attn_sample(q, cache_k.reshape(rows), cache_v.reshape(rows), k16, v16, mab, gc, x, layer,
                     p, seqs=min(SAMPLE_ATTN_SEQS, B))
    return (x,) + _new_state(co, po) + ((k32, v32),)


def _channel_mixer(x, layer, p, *, final):
    tokens = x.shape[0] * x.shape[1]
    if layer % 2 == 0:
        return _ffn_dense(x, layer, p, tm=min(DENSE_TOKENS, tokens), final=final)
    tm = min(MOE_TOKENS, tokens)
    return _ffn_moe(x, layer, p, tm=tm, tb=min(MOE_SUB_TOKENS, tm), cms=MOE_CHUNK_ROWS,
                    final=final)


def kernel(x_prompt, x_sample, cache_k, cache_v, state_conv, state_pool, norm_mix, w_in, conv_w,
           w_conv_out, w_pool, pool_scale, lambda_q1, lambda_k1, lambda_q2, lambda_k2, subln_g,
           w_attn_out, w_o, norm_ffn, w_gate_d, w_up_d, w_down_d, w_router, b_router, w_gate_e,
           w_up_e, w_down_e, norm_final):
    depth = w_in.shape[0]
    p = _params(dict(
        norm_mix=norm_mix, w_in=w_in, conv_w=conv_w, w_conv_out=w_conv_out, w_pool=w_pool,
        pool_scale=pool_scale, lambda_q1=lambda_q1, lambda_k1=lambda_k1, lambda_q2=lambda_q2,
        lambda_k2=lambda_k2, subln_g=subln_g, w_attn_out=w_attn_out, w_o=w_o, norm_ffn=norm_ffn,
        w_gate_d=w_gate_d, w_up_d=w_up_d, w_down_d=w_down_d, w_router=w_router,
        b_router=b_router, w_gate_e=w_gate_e, w_up_e=w_up_e, w_down_e=w_down_e,
        norm_final=norm_final))

    xp, xs = x_prompt, x_sample
    kv_p = kv_s = None
    states = [[] for _ in range(4)]
    for l in range(depth):
        final = l == depth - 1
        xp, cp, pp, kv_p = _token_mixer_prompt(xp, l, p, kv_p)
        xs, cs, ps, kv_s = _token_mixer_sample(xs, cache_k, cache_v, state_conv, state_pool, l, p,
                                               kv_s)
        for lst, a in zip(states, (cp, pp, cs, ps)):
            lst.append(a)
        xp = _channel_mixer(xp, l, p, final=final)
        xs = _channel_mixer(xs, l, p, final=final)

    def heads_view(a):
        return a.reshape(a.shape[0], a.shape[1], a.shape[2] // HEADS, HEADS, HEAD_W)

    return (xp, xs, heads_view(kv_p[0]), heads_view(kv_p[1]), jnp.stack(states[0]),
            jnp.stack(states[1]), heads_view(kv_s[0]), heads_view(kv_s[1]),
            jnp.stack(states[2]), jnp.stack(states[3]))
```

```python
import functools
import math

import jax
import jax.numpy as jnp
from jax import lax
from jax.experimental import pallas as pl
from jax.experimental.pallas import tpu as pltpu

D_MODEL = 1024
CHUNK = 64
CONV_W = 512
CONV_K = 3
POOL_W = 512
POOL_GC = 128
POOL_WINDOWS = (2, 4, 8, 16)
POOL_HIST = 15
HEADS = 4
HEAD_DIM = 64
HEAD_W = 2 * HEAD_DIM
ATT_W = HEADS * HEAD_W
ROPE_THETA = 10000.0
N_EXPERTS = 8
EPS = 1e-6
SUBLN_EPS = 1e-5

C_XA, C_BA, C_CA, C_U, C_Q, C_K, C_V, C_GA, C_GB, C_GC, C_END = (
    0, 512, 1024, 1536, 2048, 2560, 3072, 3584, 4608, 5632, 6656)

HIST_PAD = 16
Q_SCALE = HEAD_DIM ** -0.5 * math.log2(math.e)
DENOM_ROWS = 16

VMEM_LIMIT = 56 * 1024 * 1024
PROMPT_TILE = 512
KEY_TILE = 512
SAMPLE_MIXER_SEQS = 16
SAMPLE_ATTN_SEQS = 4
DENSE_TOKENS = 1024
MOE_TOKENS = 1024
MOE_SUB_TOKENS = 512
MOE_CHUNK_ROWS = (128, 160, 192, 224, 256)

F32 = jnp.float32
BF16 = jnp.bfloat16
NT_DIMS = (((1,), (1,)), ((), ()))
TN_DIMS = (((0,), (0,)), ((), ()))


def _const_spec(shape):
    nd = len(shape)
    return pl.BlockSpec(tuple(shape), lambda *_: (0,) * nd, pipeline_mode=pl.Buffered(1))


def _layer_spec(arr, layer):
    nd = arr.ndim
    return pl.BlockSpec((None,) + tuple(arr.shape[1:]), lambda *_: (layer,) + (0,) * (nd - 1),
                        pipeline_mode=pl.Buffered(1))


def _rms(x, g, eps):
    return x * lax.rsqrt(jnp.mean(x * x, axis=-1, keepdims=True) + eps) * g


def _chunks(n, step):
    return [(lo, min(lo + step, n)) for lo in range(0, n, step)]


def _mixer_in_kernel(*refs, bb, tt, pos0, tk, aliased):
    (x_ref, hc_ref, hp_ref, cos_ref, sin_ref, ng_ref, win_ref, cw_ref, wa_ref, wp_ref,
     ps_ref) = refs[:11]
    refs = refs[11 + (2 if aliased else 0):]
    (mab_ref, gc_ref, q_ref, k16_ref, v16_ref, k32_ref, v32_ref, co_ref, po_ref, cbuf,
     ubuf) = refs
    t = pl.program_id(1)
    m = bb * tt
    x = x_ref[...].reshape(m, D_MODEL)
    h = _rms(x, ng_ref[...], EPS).astype(BF16)

    def seg(lo, hi):
        return jnp.dot(h, win_ref[:, lo:hi], preferred_element_type=F32)

    @pl.when(t == 0)
    def _():
        cbuf[:, 0:HIST_PAD, :] = hc_ref[...]
        ubuf[:, 0:HIST_PAD, :] = hp_ref[...]

    z_ca, z_xa, z_ba, z_u = seg(C_CA, C_U), seg(C_XA, C_BA), seg(C_BA, C_CA), seg(C_U, C_Q)
    z_q, z_k, z_v = seg(C_Q, C_K), seg(C_K, C_V), seg(C_V, C_GA)

    cin = (z_ca * z_xa).reshape(bb, tt, CONV_W)
    cbuf[:, HIST_PAD:, :] = cin
    conv = cbuf[:, HIST_PAD - 2:HIST_PAD - 2 + tt, :] * cw_ref[0:1, :]
    conv = conv + cbuf[:, HIST_PAD - 1:HIST_PAD - 1 + tt, :] * cw_ref[1:2, :]
    conv = conv + cin * cw_ref[2:3, :]
    ya = jnp.dot((z_ba * conv.reshape(m, CONV_W)).astype(BF16), wa_ref[...],
                 preferred_element_type=F32)

    ubuf[:, HIST_PAD:, :] = z_u.reshape(bb, tt, POOL_W)
    pos = pos0 + t * tt + lax.broadcasted_iota(jnp.int32, (bb, tt, POOL_GC), 1)
    yb_parts = []
    for gi, win in enumerate(POOL_WINDOWS):
        lo, hi = gi * POOL_GC, (gi + 1) * POOL_GC
        cur = ubuf[:, HIST_PAD:HIST_PAD + tt, lo:hi]
        s = cur
        for k in range(1, win):
            s = s + ubuf[:, HIST_PAD - k:HIST_PAD - k + tt, lo:hi]
        cnt = jnp.minimum(pos + 1, win).astype(F32)
        pooled = s / cnt - cur
        yb_parts.append(jnp.dot(pooled.reshape(m, POOL_GC).astype(BF16), wp_ref[gi],
                                preferred_element_type=F32))
    yb = jnp.concatenate(yb_parts, axis=-1) * ps_ref[...]

    cos4 = jnp.concatenate([cos_ref[...]] * HEADS, axis=-1)
    sin4 = jnp.concatenate([sin_ref[...]] * HEADS, axis=-1)
    lane = lax.broadcasted_iota(jnp.int32, (m, ATT_W), 1)
    first_half = (lane & (HEAD_DIM // 2)) == 0

    def rope(z):
        swapped = jnp.where(first_half, pltpu.roll(z, ATT_W - HEAD_DIM // 2, axis=1),
                            pltpu.roll(z, HEAD_DIM // 2, axis=1))
        return z * cos4 + swapped * sin4

    q = rope(z_q) * Q_SCALE
    k = rope(z_k)
    v = z_v
    for hd in range(HEADS):
        lo, hi = hd * HEAD_W, (hd + 1) * HEAD_W
        k32_ref[:, pl.ds(hd, tt, stride=HEADS), :] = k[:, lo:hi].reshape(bb, tt, HEAD_W)
        v32_ref[:, pl.ds(hd, tt, stride=HEADS), :] = v[:, lo:hi].reshape(bb, tt, HEAD_W)
    k16_ref[...] = k.astype(BF16).reshape(bb, tt, ATT_W)
    if tk is None:
        q_ref[...] = q.astype(BF16).reshape(bb, tt, ATT_W)
        v16_ref[...] = v.astype(BF16).reshape(bb, tt, ATT_W)
    else:
        q_ref[0] = q.T.astype(BF16)
        for c in range(tt // tk):
            v16_ref[0, c] = v[c * tk:(c + 1) * tk, :].T.astype(BF16)

    ga = jax.nn.sigmoid(seg(C_GA, C_GB))
    gb = jax.nn.sigmoid(seg(C_GB, C_GC))
    mab_ref[...] = (ga * ya + gb * yb).astype(BF16).reshape(bb, tt, D_MODEL)
    gc_ref[...] = jax.nn.sigmoid(seg(C_GC, C_END)).astype(BF16).reshape(bb, tt, D_MODEL)

    last_c = cbuf[:, tt:tt + HIST_PAD, :]
    last_p = ubuf[:, tt:tt + HIST_PAD, :]
    co_ref[...] = last_c
    po_ref[...] = last_p
    cbuf[:, 0:HIST_PAD, :] = last_c
    ubuf[:, 0:HIST_PAD, :] = last_p


def _mixer_in(x, hist_c, hist_p, cos, sin, layer, p, kv_bufs, *, bb, tt, pos0, tk):
    B, T, _ = x.shape
    depth = p['w_in'].shape[0]
    assert B % bb == 0 and T % tt == 0 and tt >= HIST_PAD and tt % 8 == 0
    assert tk is None or (bb == 1 and tt % tk == 0)
    grid = (B // bb, T // tt)
    m = bb * tt

    def seq_spec(w):
        return pl.BlockSpec((bb, tt, w), lambda b, t: (b, t, 0))

    if hist_c.ndim == 4:
        hist_spec = pl.BlockSpec((None, bb, HIST_PAD, CONV_W), lambda b, t: (layer, b, 0, 0))
    else:
        hist_spec = pl.BlockSpec((bb, HIST_PAD, CONV_W), lambda b, t: (b, 0, 0))
    state_spec = pl.BlockSpec((bb, HIST_PAD, CONV_W), lambda b, t: (b, 0, 0))
    rope_spec = pl.BlockSpec((None, m, HEAD_W), lambda b, t: (t, 0, 0))
    kv_spec = pl.BlockSpec((None, bb, tt * HEADS, HEAD_W), lambda b, t: (layer, b, t, 0))
    kv_shape = jax.ShapeDtypeStruct((depth, B, T * HEADS, HEAD_W), F32)
    if tk is None:
        q_shape, q_spec = jax.ShapeDtypeStruct((B, T, ATT_W), BF16), seq_spec(ATT_W)
        v_shape, v_spec = q_shape, q_spec
    else:
        q_shape = jax.ShapeDtypeStruct((B, ATT_W, T), BF16)
        q_spec = pl.BlockSpec((1, ATT_W, tt), lambda b, t: (b, 0, t))
        v_shape = jax.ShapeDtypeStruct((B, T // tk, ATT_W, tk), BF16)
        v_spec = pl.BlockSpec((1, tt // tk, ATT_W, tk), lambda b, t: (b, t, 0, 0))

    out_shape = (
        jax.ShapeDtypeStruct((B, T, D_MODEL), BF16),
        jax.ShapeDtypeStruct((B, T, D_MODEL), BF16),
        q_shape,
        jax.ShapeDtypeStruct((B, T, ATT_W), BF16),
        v_shape,
        kv_shape, kv_shape,
        jax.ShapeDtypeStruct((B, HIST_PAD, CONV_W), F32),
        jax.ShapeDtypeStruct((B, HIST_PAD, POOL_W), F32),
    )
    out_specs = (seq_spec(D_MODEL), seq_spec(D_MODEL), q_spec, seq_spec(ATT_W), v_spec,
                 kv_spec, kv_spec, state_spec, state_spec)
    in_specs = [seq_spec(D_MODEL), hist_spec, hist_spec, rope_spec, rope_spec,
                _layer_spec(p['norm_mix'], layer), _layer_spec(p['w_in'], layer),
                _layer_spec(p['conv_w'], layer), _layer_spec(p['w_conv_out'], layer),
                _layer_spec(p['w_pool'], layer), _layer_spec(p['pool_scale'], layer)]
    args = [x, hist_c, hist_p, cos, sin, p['norm_mix'], p['w_in'], p['conv_w'], p['w_conv_out'],
            p['w_pool'], p['pool_scale']]
    aliases = {}
    if kv_bufs is not None:
        in_specs += [pl.BlockSpec(memory_space=pl.ANY)] * 2
        aliases = {len(args): 5, len(args) + 1: 6}
        args += list(kv_bufs)
    return pl.pallas_call(
        functools.partial(_mixer_in_kernel, bb=bb, tt=tt, pos0=pos0, tk=tk,
                          aliased=kv_bufs is not None),
        grid=grid, in_specs=in_specs, out_specs=out_specs, out_shape=out_shape,
        scratch_shapes=[pltpu.VMEM((bb, HIST_PAD + tt, CONV_W), F32),
                        pltpu.VMEM((bb, HIST_PAD + tt, POOL_W), F32)],
        input_output_aliases=aliases,
        compiler_params=pltpu.CompilerParams(dimension_semantics=("parallel", "arbitrary"),
                                             vmem_limit_bytes=VMEM_LIMIT),
        name="mixer_in",
    )(*args)


def _lambda(lq1_ref, lk1_ref, lq2_ref, lk2_ref, lam_init):
    a = jnp.exp(jnp.sum(lq1_ref[...] * lk1_ref[...], axis=-1, keepdims=True))
    b = jnp.exp(jnp.sum(lq2_ref[...] * lk2_ref[...], axis=-1, keepdims=True))
    return a - b + lam_init


def _lam_init(layer):
    return 0.8 - 0.6 * math.exp(-0.3 * layer)


def _merge_out(o, mab, gc, x, wc_ref, wo_ref):
    yc = jnp.dot(o.astype(BF16), wc_ref[...], preferred_element_type=F32)
    merged = mab.astype(F32) + gc.astype(F32) * yc
    return x + jnp.dot(merged.astype(BF16), wo_ref[...], preferred_element_type=F32)


def _attn_weight_specs(p, layer):
    names = ('lambda_q1', 'lambda_k1', 'lambda_q2', 'lambda_k2', 'subln_g', 'w_attn_out', 'w_o')
    return [_layer_spec(p[n], layer) for n in names], [p[n] for n in names]


def _attn_prompt_kernel(qt_ref, k_ref, vt_ref, mab_ref, gc_ref, x_ref, lq1_ref, lk1_ref, lq2_ref,
                        lk2_ref, sg_ref, wc_ref, wo_ref, o_ref, p_scr, acc_scr, *, tq, tk,
                        lam_init):
    i = pl.program_id(1)
    n_diag = tq // tk
    lam = _lambda(lq1_ref, lk1_ref, lq2_ref, lk2_ref, lam_init)
    feat = lax.broadcasted_iota(jnp.int32, (HEAD_W, tq), 0)
    qs = []
    for hd in range(HEADS):
        qt = qt_ref[0, hd * HEAD_W:(hd + 1) * HEAD_W, :]
        zero = jnp.zeros_like(qt)
        qs.append(jnp.concatenate([jnp.where(feat < HEAD_DIM, qt, zero),
                                   jnp.where(feat < HEAD_DIM, zero, qt)], axis=1))

    def scores(j):
        off = pl.multiple_of(j * tk, tk)
        return [jnp.dot(k_ref[0, pl.ds(off, tk), hd * HEAD_W:(hd + 1) * HEAD_W], qs[hd],
                        preferred_element_type=F32) for hd in range(HEADS)]

    def softmax(s, m):
        m_out, alpha_out = [], []
        for hd in range(HEADS):
            m_new = jnp.maximum(m[hd], jnp.max(s[hd], axis=0, keepdims=True))
            p_scr[hd] = jnp.exp2(s[hd] - m_new).astype(BF16)
            m_out.append(m_new)
            alpha_out.append(jnp.exp2(m[hd] - m_new))
        return m_out, alpha_out

    ones_rows = jnp.ones((DENOM_ROWS, tk), BF16)

    def flush(j, alpha):
        for hd in range(HEADS):
            vt1 = jnp.concatenate([vt_ref[0, j, hd * HEAD_W:(hd + 1) * HEAD_W, :], ones_rows],
                                  axis=0)
            acc_scr[hd] = alpha[hd] * acc_scr[hd] + jnp.dot(vt1, p_scr[hd],
                                                            preferred_element_type=F32)

    acc_scr[...] = jnp.zeros_like(acc_scr)
    m = [jnp.full((1, 2 * tq), -jnp.inf, F32)] * HEADS
    alpha = [jnp.zeros((1, 2 * tq), F32)] * HEADS
    q_chunk = (lax.broadcasted_iota(jnp.int32, (tk, 2 * tq), 1) % tq) // CHUNK
    k_row = lax.broadcasted_iota(jnp.int32, (tk, 2 * tq), 0)
    for d in range(n_diag):
        j = i * n_diag + d
        s = scores(j)
        if d > 0:
            flush(j - 1, alpha)
        visible = (k_row + d * tk) // CHUNK <= q_chunk
        m, alpha = softmax([jnp.where(visible, sh, -jnp.inf) for sh in s], m)

    def body(j, carry):
        pend, alpha, m = carry
        s = scores(j)
        flush(pend, alpha)
        m, alpha = softmax(s, m)
        return j, alpha, m

    pend, alpha, m = lax.fori_loop(0, i * n_diag, body, (i * n_diag + n_diag - 1, alpha, m))
    flush(pend, alpha)
    heads = []
    for hd in range(HEADS):
        ot = acc_scr[hd, 0:HEAD_W, :] / acc_scr[hd, HEAD_W:HEAD_W + 1, :]
        ot = ot[:, :tq] - lam * ot[:, tq:]
        ot = ot * lax.rsqrt(jnp.mean(ot * ot, axis=0, keepdims=True) + SUBLN_EPS)
        heads.append(ot.T * sg_ref[...] * (1.0 - lam_init))
    o = jnp.concatenate(heads, axis=-1)
    o_ref[0] = _merge_out(o, mab_ref[0], gc_ref[0], x_ref[0], wc_ref, wo_ref)


def _attn_prompt(qt, k16, vt, mab, gc, x, layer, p, *, tq, tk):
    B, T, _ = x.shape
    assert T % tq == 0 and tq % tk == 0 and tk % CHUNK == 0 and vt.shape[3] == tk

    def tile_spec(w):
        return pl.BlockSpec((1, tq, w), lambda b, i: (b, i, 0))

    weight_specs, weights = _attn_weight_specs(p, layer)
    return pl.pallas_call(
        functools.partial(_attn_prompt_kernel, tq=tq, tk=tk, lam_init=_lam_init(layer)),
        grid=(B, T // tq),
        in_specs=[pl.BlockSpec((1, ATT_W, tq), lambda b, i: (b, 0, i)),
                  pl.BlockSpec((1, T, ATT_W), lambda b, i: (b, 0, 0)),
                  pl.BlockSpec((1, T // tk, ATT_W, tk), lambda b, i: (b, 0, 0, 0)),
                  tile_spec(D_MODEL), tile_spec(D_MODEL), tile_spec(D_MODEL)] + weight_specs,
        out_specs=tile_spec(D_MODEL),
        out_shape=jax.ShapeDtypeStruct((B, T, D_MODEL), F32),
        scratch_shapes=[pltpu.VMEM((HEADS, tk, 2 * tq), BF16),
                        pltpu.VMEM((HEADS, HEAD_W + DENOM_ROWS, 2 * tq), F32)],
        compiler_params=pltpu.CompilerParams(dimension_semantics=("parallel", "arbitrary"),
                                             vmem_limit_bytes=VMEM_LIMIT),
        name="attn_prompt",
    )(qt, k16, vt, mab, gc, x, *weights)


def _stack_components(qh):
    lane = lax.broadcasted_iota(jnp.int32, qh.shape, 1)
    zero = jnp.zeros_like(qh)
    return jnp.concatenate([jnp.where(lane < HEAD_DIM, qh, zero),
                            jnp.where(lane < HEAD_DIM, zero, qh)], axis=0)


def _softmax_step(carry, s, vj):
    m, l, acc = carry
    m_new = jnp.maximum(m, jnp.max(s, axis=-1, keepdims=True))
    pr = jnp.exp2(s - m_new)
    alpha = jnp.exp2(m - m_new)
    l = alpha * l + jnp.sum(pr, axis=-1, keepdims=True)
    acc = alpha * acc + jnp.dot(pr.astype(BF16), vj, preferred_element_type=F32)
    return m_new, l, acc


def _attn_sample_kernel(q_ref, kp_ref, vp_ref, kn_ref, vn_ref, mab_ref, gc_ref, x_ref, lq1_ref,
                        lk1_ref, lq2_ref, lk2_ref, sg_ref, wc_ref, wo_ref, o_ref, *, past,
                        lam_init):
    n_seq, tq, _ = q_ref.shape
    lam = _lambda(lq1_ref, lk1_ref, lq2_ref, lk2_ref, lam_init)
    q_chunk = (past + lax.broadcasted_iota(jnp.int32, (2 * tq, 1), 0) % tq) // CHUNK
    vis_past = (lax.broadcasted_iota(jnp.int32, (2 * tq, past), 1) // CHUNK) <= q_chunk
    vis_new = ((past + lax.broadcasted_iota(jnp.int32, (2 * tq, tq), 1)) // CHUNK) <= q_chunk
    rows = []
    for sq in range(n_seq):
        heads = []
        for hd in range(HEADS):
            lo, hi = hd * HEAD_W, (hd + 1) * HEAD_W
            qs = _stack_components(q_ref[sq, :, lo:hi])
            carry = (jnp.full((2 * tq, 1), -jnp.inf, F32), jnp.zeros((2 * tq, 1), F32),
                     jnp.zeros((2 * tq, HEAD_W), F32))
            s = lax.dot_general(qs, kn_ref[sq, :, lo:hi], NT_DIMS, preferred_element_type=F32)
            carry = _softmax_step(carry, jnp.where(vis_new, s, -jnp.inf), vn_ref[sq, :, lo:hi])
            kp = kp_ref[sq, pl.ds(hd, past, stride=HEADS), :].astype(BF16)
            vp = vp_ref[sq, pl.ds(hd, past, stride=HEADS), :].astype(BF16)
            s = lax.dot_general(qs, kp, NT_DIMS, preferred_element_type=F32)
            _, l, acc = _softmax_step(carry, jnp.where(vis_past, s, -jnp.inf), vp)
            o = acc / l
            o = o[:tq] - lam * o[tq:]
            heads.append(_rms(o, sg_ref[...], SUBLN_EPS) * (1.0 - lam_init))
        rows.append(jnp.concatenate(heads, axis=-1))
    o = jnp.concatenate(rows, axis=0)
    flat = (n_seq * tq, D_MODEL)
    out = _merge_out(o, mab_ref[...].reshape(flat), gc_ref[...].reshape(flat),
                     x_ref[...].reshape(flat), wc_ref, wo_ref)
    o_ref[...] = out.reshape(n_seq, tq, D_MODEL)


def _attn_sample(q, cache_k, cache_v, k16, v16, mab, gc, x, layer, p, *, seqs):
    B, T, _ = x.shape
    past = cache_k.shape[2] // HEADS
    assert B % seqs == 0

    def tile_spec(w):
        return pl.BlockSpec((seqs, T, w), lambda b: (b, 0, 0))

    past_spec = pl.BlockSpec((None, seqs, past * HEADS, HEAD_W), lambda b: (layer, b, 0, 0))
    weight_specs, weights = _attn_weight_specs(p, layer)
    return pl.pallas_call(
        functools.partial(_attn_sample_kernel, past=past, lam_init=_lam_init(layer)),
        grid=(B // seqs,),
        in_specs=[tile_spec(ATT_W), past_spec, past_spec, tile_spec(ATT_W), tile_spec(ATT_W),
                  tile_spec(D_MODEL), tile_spec(D_MODEL), tile_spec(D_MODEL)] + weight_specs,
        out_specs=tile_spec(D_MODEL),
        out_shape=jax.ShapeDtypeStruct((B, T, D_MODEL), F32),
        compiler_params=pltpu.CompilerParams(dimension_semantics=("parallel",),
                                             vmem_limit_bytes=VMEM_LIMIT),
        name="attn_sample",
    )(q, cache_k, cache_v, k16, v16, mab, gc, x, *weights)


def _ffn_dense_kernel(x_ref, ng_ref, wg_ref, wu_ref, wd_ref, nf_ref, o_ref, *, final):
    bb, tt, _ = x_ref.shape
    x = x_ref[...].reshape(bb * tt, D_MODEL)
    h = _rms(x, ng_ref[...], EPS).astype(BF16)
    y = x
    for lo, hi in _chunks(wg_ref.shape[1], 1024):
        g = jnp.dot(h, wg_ref[:, lo:hi], preferred_element_type=F32)
        u = jnp.dot(h, wu_ref[:, lo:hi], preferred_element_type=F32)
        a = (g * jax.nn.sigmoid(g) * u).astype(BF16)
        y = y + jnp.dot(a, wd_ref[lo:hi, :], preferred_element_type=F32)
    if final:
        y = _rms(y, nf_ref[...], EPS)
    o_ref[...] = y.reshape(bb, tt, D_MODEL)


def _token_blocks(B, T, tm):
    if T >= tm:
        assert T % tm == 0
        return 1, tm
    assert tm % T == 0 and B % (tm // T) == 0
    return tm // T, T


def _ffn_dense(x, layer, p, *, tm, final):
    B, T, _ = x.shape
    bb, tt = _token_blocks(B, T, tm)
    d = layer // 2
    row_spec = pl.BlockSpec((bb, tt, D_MODEL), lambda b, t: (b, t, 0))
    return pl.pallas_call(
        functools.partial(_ffn_dense_kernel, final=final),
        grid=(B // bb, T // tt),
        in_specs=[row_spec, _layer_spec(p['norm_ffn'], layer), _layer_spec(p['w_gate_d'], d),
                  _layer_spec(p['w_up_d'], d), _layer_spec(p['w_down_d'], d),
                  _const_spec(p['norm_final'].shape)],
        out_specs=row_spec,
        out_shape=jax.ShapeDtypeStruct(x.shape, F32),
        compiler_params=pltpu.CompilerParams(dimension_semantics=("parallel", "parallel"),
                                             vmem_limit_bytes=VMEM_LIMIT),
        name="ffn_dense",
    )(x, p['norm_ffn'], p['w_gate_d'], p['w_up_d'], p['w_down_d'], p['norm_final'])


def _route_top2(logits):
    row = lax.broadcasted_iota(jnp.int32, logits.shape, 0).astype(F32)
    big = float(N_EXPERTS)
    m1 = jnp.max(logits, axis=0, keepdims=True)
    i1 = jnp.min(jnp.where(logits == m1, row, big), axis=0, keepdims=True)
    rest = jnp.where(row == i1, -jnp.inf, logits)
    m2 = jnp.max(rest, axis=0, keepdims=True)
    i2 = jnp.min(jnp.where(rest == m2, row, big), axis=0, keepdims=True)
    e2 = jnp.exp(m2 - m1)
    w1 = 1.0 / (1.0 + e2)
    w2 = e2 / (1.0 + e2)
    return jnp.where(row == i1, w1, 0.0) + jnp.where(row == i2, w2, 0.0)


def _ffn_moe_kernel(x_ref, ng_ref, wrt_ref, brt_ref, wg_ref, wu_ref, wd_ref, nf_ref, o_ref,
                    h_scr, wts_scr, rank_scr, *, final, tb, cms):
    e = pl.program_id(2)
    bb, tt, _ = x_ref.shape
    n_tok = bb * tt
    sub_blocks = _chunks(n_tok, tb)

    @pl.when(e == 0)
    def _():
        x = x_ref[...].reshape(n_tok, D_MODEL)
        h = _rms(x, ng_ref[...], EPS)
        logits = lax.dot_general(wrt_ref[...], h, NT_DIMS, preferred_element_type=F32,
                                 precision=lax.Precision.HIGHEST) + brt_ref[...]
        wts = _route_top2(logits)
        sel = (wts > 0.0).astype(BF16)
        before = (lax.broadcasted_iota(jnp.int32, (tb, tb), 0)
                  < lax.broadcasted_iota(jnp.int32, (tb, tb), 1)).astype(BF16)
        rank = jnp.concatenate(
            [jnp.dot(sel[:, lo:hi], before, preferred_element_type=F32) for lo, hi in sub_blocks],
            axis=1)
        for ee in range(N_EXPERTS):
            for sb, (lo, hi) in enumerate(sub_blocks):
                wts_scr[ee, sb] = wts[ee:ee + 1, lo:hi]
                rank_scr[ee, sb] = rank[ee:ee + 1, lo:hi]
        h_scr[...] = h.astype(BF16)
        o_ref[...] = x_ref[...]

    cm_max = cms[-1]

    def count(w_row):
        return jnp.sum((w_row > 0.0).astype(F32)).astype(jnp.int32)

    def run_chunk(cm, nsb, r_row, w_row, base, sb):
        width = nsb * tb
        row0 = pl.multiple_of(sb * tb, tb)
        rows = (lax.broadcasted_iota(jnp.int32, (cm, width), 0) + base).astype(F32)
        hit = (r_row == rows) & (w_row > 0.0)
        gather = jnp.where(hit, 1.0, 0.0).astype(BF16)
        xg = jnp.dot(gather, h_scr[pl.ds(row0, width), :],
                     preferred_element_type=F32).astype(BF16)
        g = jnp.dot(xg, wg_ref[...], preferred_element_type=F32)
        u = jnp.dot(xg, wu_ref[...], preferred_element_type=F32)
        a = (g * jax.nn.sigmoid(g) * u).astype(BF16)
        y = jnp.dot(a, wd_ref[...], preferred_element_type=F32).astype(BF16)
        scatter = jnp.where(hit, w_row, 0.0).astype(BF16)
        upd = lax.dot_general(scatter, y, TN_DIMS, preferred_element_type=F32)
        if bb == 1:
            o_ref[0, pl.ds(row0, width), :] += upd
        else:
            o_ref[pl.ds(sb * (tb // tt), width // tt)] += upd.reshape(width // tt, tt, D_MODEL)

    def run_span(nsb, r_row, w_row, n_sel, sb):
        def chunk(c, carry):
            left = n_sel - c * cm_max
            size_idx = sum((left > cm).astype(jnp.int32) for cm in cms[:-1])
            lax.switch(size_idx,
                       [functools.partial(run_chunk, cm, nsb, r_row, w_row) for cm in cms],
                       c * cm_max, sb)
            return carry

        n_chunks = sum((n_sel > c * cm_max).astype(jnp.int32)
                       for c in range(-(-(nsb * tb) // cm_max)))
        lax.fori_loop(0, n_chunks, chunk, 0)

    def sub_block_pair(pair, carry):
        sb = 2 * pair
        w0, w1 = wts_scr[e, sb], wts_scr[e, sb + 1]
        r0, r1 = rank_scr[e, sb], rank_scr[e, sb + 1]
        n0, n1 = count(w0), count(w1)

        def together():
            r_pair = jnp.concatenate([r0, r1 + n0.astype(F32)], axis=1)
            run_span(2, r_pair, jnp.concatenate([w0, w1], axis=1), n0 + n1, sb)

        def apart():
            run_span(1, r0, w0, n0, sb)
            run_span(1, r1, w1, n1, sb + 1)

        lax.cond(n0 + n1 <= cm_max, together, apart)
        return carry

    lax.fori_loop(0, n_tok // (2 * tb), sub_block_pair, 0)

    if final:
        @pl.when(e == N_EXPERTS - 1)
        def _():
            y = o_ref[...].reshape(n_tok, D_MODEL)
            o_ref[...] = _rms(y, nf_ref[...], EPS).reshape(bb, tt, D_MODEL)


def _ffn_moe(x, layer, p, *, tm, tb, cms, final):
    B, T, _ = x.shape
    bb, tt = _token_blocks(B, T, tm)
    assert tm % (2 * tb) == 0 and (bb == 1 or tb % tt == 0)
    mo = layer // 2
    row_spec = pl.BlockSpec((bb, tt, D_MODEL), lambda b, t, e: (b, t, 0))

    def expert_spec(arr):
        return pl.BlockSpec((None, None) + tuple(arr.shape[2:]), lambda b, t, e: (mo, e, 0, 0))

    return pl.pallas_call(
        functools.partial(_ffn_moe_kernel, final=final, tb=tb, cms=cms),
        grid=(B // bb, T // tt, N_EXPERTS),
        in_specs=[row_spec, _layer_spec(p['norm_ffn'], layer), _layer_spec(p['w_router_t'], mo),
                  _layer_spec(p['b_router_t'], mo), expert_spec(p['w_gate_e']),
                  expert_spec(p['w_up_e']), expert_spec(p['w_down_e']),
                  _const_spec(p['norm_final'].shape)],
        out_specs=row_spec,
        out_shape=jax.ShapeDtypeStruct(x.shape, F32),
        scratch_shapes=[pltpu.VMEM((tm, D_MODEL), BF16),
                        pltpu.VMEM((N_EXPERTS, tm // tb, 1, tb), F32),
                        pltpu.VMEM((N_EXPERTS, tm // tb, 1, tb), F32)],
        compiler_params=pltpu.CompilerParams(
            dimension_semantics=("parallel", "parallel", "arbitrary"),
            vmem_limit_bytes=VMEM_LIMIT),
        name="ffn_moe",
    )(x, p['norm_ffn'], p['w_router_t'], p['b_router_t'], p['w_gate_e'], p['w_up_e'],
      p['w_down_e'], p['norm_final'])


def _rope_tables(pos0, T, bb, tt):
    half = HEAD_DIM // 2
    inv_freq = ROPE_THETA ** (-jnp.arange(half, dtype=F32) / half)
    ang = (pos0 + jnp.arange(T)).astype(F32)[:, None] * inv_freq[None, :]
    cos, sin = jnp.cos(ang), jnp.sin(ang)
    cos = jnp.concatenate([cos, cos, cos, cos], axis=-1)
    sin = jnp.concatenate([-sin, sin, -sin, sin], axis=-1)

    def lay(a):
        a = a.reshape(T // tt, 1, tt, HEAD_W)
        return jnp.broadcast_to(a, (T // tt, bb, tt, HEAD_W)).reshape(T // tt, bb * tt, HEAD_W)

    return lay(cos), lay(sin)


def _pad_hist(state):
    return jnp.pad(state, ((0, 0), (0, 0), (HIST_PAD - state.shape[2], 0), (0, 0)))


_MATMUL_WEIGHTS = ('w_in', 'w_conv_out', 'w_pool', 'w_attn_out', 'w_o', 'w_gate_d', 'w_up_d',
                   'w_down_d', 'w_gate_e', 'w_up_e', 'w_down_e')
_ROW_VECTORS = ('norm_mix', 'pool_scale', 'lambda_q1', 'lambda_k1', 'lambda_q2', 'lambda_k2',
                'subln_g', 'norm_ffn')


def _params(w):
    p = dict(w)
    for n in _MATMUL_WEIGHTS:
        p[n] = w[n].astype(BF16)
    for n in _ROW_VECTORS:
        p[n] = w[n][:, None, :]
    p['w_router_t'] = jnp.swapaxes(w['w_router'], 1, 2)
    p['b_router_t'] = w['b_router'][:, :, None]
    p['norm_final'] = w['norm_final'][None, :]
    return p


def _new_state(co, po):
    return co[:, HIST_PAD - (CONV_K - 1):], po[:, HIST_PAD - POOL_HIST:]


def _token_mixer_prompt(x, layer, p, kv_bufs):
    B, T, _ = x.shape
    tt = min(PROMPT_TILE, T)
    tk = min(KEY_TILE, tt)
    cos, sin = _rope_tables(0, T, 1, tt)
    zero_hist = jnp.zeros((B, HIST_PAD, CONV_W), F32)
    mab, gc, qt, k16, vt, k32, v32, co, po = _mixer_in(
        x, zero_hist, zero_hist, cos, sin, layer, p, kv_bufs, bb=1, tt=tt, pos0=0, tk=tk)
    x = _attn_prompt(qt, k16, vt, mab, gc, x, layer, p, tq=tt, tk=tk)
    return (x,) + _new_state(co, po) + ((k32, v32),)


def _token_mixer_sample(x, cache_k, cache_v, state_conv, state_pool, layer, p, kv_bufs):
    past = cache_k.shape[2]
    B, T, _ = x.shape
    bb = min(SAMPLE_MIXER_SEQS, B)
    cos, sin = _rope_tables(past, T, bb, T)
    mab, gc, q, k16, v16, k32, v32, co, po = _mixer_in(
        x, _pad_hist(state_conv), _pad_hist(state_pool), cos, sin, layer, p, kv_bufs,
        bb=bb, tt=T, pos0=past, tk=None)
    rows = cache_k.shape[:2] + (past * HEADS, HEAD_W)
    x = _attn_sample(q, cache_k.reshape(rows), cache_v.reshape(rows), k16, v16, mab, gc, x, layer,
                     p, seqs=min(SAMPLE_ATTN_SEQS, B))
    return (x,) + _new_state(co, po) + ((k32, v32),)


def _channel_mixer(x, layer, p, *, final):
    tokens = x.shape[0] * x.shape[1]
    if layer % 2 == 0:
        return _ffn_dense(x, layer, p, tm=min(DENSE_TOKENS, tokens), final=final)
    tm = min(MOE_TOKENS, tokens)
    return _ffn_moe(x, layer, p, tm=tm, tb=min(MOE_SUB_TOKENS, tm), cms=MOE_CHUNK_ROWS,
                    final=final)


def kernel(x_prompt, x_sample, cache_k, cache_v, state_conv, state_pool, norm_mix, w_in, conv_w,
           w_conv_out, w_pool, pool_scale, lambda_q1, lambda_k1, lambda_q2, lambda_k2, subln_g,
           w_attn_out, w_o, norm_ffn, w_gate_d, w_up_d, w_down_d, w_router, b_router, w_gate_e,
           w_up_e, w_down_e, norm_final):
    depth = w_in.shape[0]
    p = _params(dict(
        norm_mix=norm_mix, w_in=w_in, conv_w=conv_w, w_conv_out=w_conv_out, w_pool=w_pool,
        pool_scale=pool_scale, lambda_q1=lambda_q1, lambda_k1=lambda_k1, lambda_q2=lambda_q2,
        lambda_k2=lambda_k2, subln_g=subln_g, w_attn_out=w_attn_out, w_o=w_o, norm_ffn=norm_ffn,
        w_gate_d=w_gate_d, w_up_d=w_up_d, w_down_d=w_down_d, w_router=w_router,
        b_router=b_router, w_gate_e=w_gate_e, w_up_e=w_up_e, w_down_e=w_down_e,
        norm_final=norm_final))

    xp, xs = x_prompt, x_sample
    kv_p = kv_s = None
    states = [[] for _ in range(4)]
    for l in range(depth):
        final = l == depth - 1
        xp, cp, pp, kv_p = _token_mixer_prompt(xp, l, p, kv_p)
        xs, cs, ps, kv_s = _token_mixer_sample(xs, cache_k, cache_v, state_conv, state_pool, l, p,
                                               kv_s)
        for lst, a in zip(states, (cp, pp, cs, ps)):
            lst.append(a)
        xp = _channel_mixer(xp, l, p, final=final)
        xs = _channel_mixer(xs, l, p, final=final)

    def heads_view(a):
        return a.reshape(a.shape[0], a.shape[1], a.shape[2] // HEADS, HEADS, HEAD_W)

    return (xp, xs, heads_view(kv_p[0]), heads_view(kv_p[1]), jnp.stack(states[0]),
            jnp.stack(states[1]), heads_view(kv_s[0]), heads_view(kv_s[1]),
            jnp.stack(states[2]), jnp.stack(states[3]))
```

```python
import functools
import math

import jax
import jax.numpy as jnp
from jax import lax
from jax.experimental import pallas as pl
from jax.experimental.pallas import tpu as pltpu

D_MODEL = 1024
CHUNK = 64
CONV_W = 512
CONV_K = 3
POOL_W = 512
POOL_GC = 128
POOL_WINDOWS = (2, 4, 8, 16)
POOL_HIST = 15
HEADS = 4
HEAD_DIM = 64
HEAD_W = 2 * HEAD_DIM
ATT_W = HEADS * HEAD_W
ROPE_THETA = 10000.0
N_EXPERTS = 8
EPS = 1e-6
SUBLN_EPS = 1e-5

C_XA, C_BA, C_CA, C_U, C_Q, C_K, C_V, C_GA, C_GB, C_GC, C_END = (
    0, 512, 1024, 1536, 2048, 2560, 3072, 3584, 4608, 5632, 6656)

HIST_PAD = 16
Q_SCALE = HEAD_DIM ** -0.5 * math.log2(math.e)
DENOM_ROWS = 16

VMEM_LIMIT = 56 * 1024 * 1024
PROMPT_TILE = 512
KEY_TILE = 512
SAMPLE_MIXER_SEQS = 16
SAMPLE_ATTN_SEQS = 4
DENSE_TOKENS = 1024
MOE_TOKENS = 1024
MOE_SUB_TOKENS = 512
MOE_CHUNK_ROWS = (128, 192, 256)

F32 = jnp.float32
BF16 = jnp.bfloat16
NT_DIMS = (((1,), (1,)), ((), ()))
TN_DIMS = (((0,), (0,)), ((), ()))


def _const_spec(shape):
    nd = len(shape)
    return pl.BlockSpec(tuple(shape), lambda *_: (0,) * nd, pipeline_mode=pl.Buffered(1))


def _layer_spec(arr, layer):
    nd = arr.ndim
    return pl.BlockSpec((None,) + tuple(arr.shape[1:]), lambda *_: (layer,) + (0,) * (nd - 1),
                        pipeline_mode=pl.Buffered(1))


def _rms(x, g, eps):
    return x * lax.rsqrt(jnp.mean(x * x, axis=-1, keepdims=True) + eps) * g


def _chunks(n, step):
    return [(lo, min(lo + step, n)) for lo in range(0, n, step)]


def _mixer_in_kernel(*refs, bb, tt, pos0, tk, aliased):
    (x_ref, hc_ref, hp_ref, cos_ref, sin_ref, ng_ref, win_ref, cw_ref, wa_ref, wp_ref,
     ps_ref) = refs[:11]
    refs = refs[11 + (2 if aliased else 0):]
    (mab_ref, gc_ref, q_ref, k16_ref, v16_ref, k32_ref, v32_ref, co_ref, po_ref, cbuf,
     ubuf) = refs
    t = pl.program_id(1)
    m = bb * tt
    x = x_ref[...].reshape(m, D_MODEL)
    h = _rms(x, ng_ref[...], EPS).astype(BF16)

    def seg(lo, hi):
        return jnp.dot(h, win_ref[:, lo:hi], preferred_element_type=F32)

    @pl.when(t == 0)
    def _():
        cbuf[:, 0:HIST_PAD, :] = hc_ref[...]
        ubuf[:, 0:HIST_PAD, :] = hp_ref[...]

    z_ca, z_xa, z_ba, z_u = seg(C_CA, C_U), seg(C_XA, C_BA), seg(C_BA, C_CA), seg(C_U, C_Q)
    z_q, z_k, z_v = seg(C_Q, C_K), seg(C_K, C_V), seg(C_V, C_GA)

    cin = (z_ca * z_xa).reshape(bb, tt, CONV_W)
    cbuf[:, HIST_PAD:, :] = cin
    conv = cbuf[:, HIST_PAD - 2:HIST_PAD - 2 + tt, :] * cw_ref[0:1, :]
    conv = conv + cbuf[:, HIST_PAD - 1:HIST_PAD - 1 + tt, :] * cw_ref[1:2, :]
    conv = conv + cin * cw_ref[2:3, :]
    ya = jnp.dot((z_ba * conv.reshape(m, CONV_W)).astype(BF16), wa_ref[...],
                 preferred_element_type=F32)

    ubuf[:, HIST_PAD:, :] = z_u.reshape(bb, tt, POOL_W)
    pos = pos0 + t * tt + lax.broadcasted_iota(jnp.int32, (bb, tt, POOL_GC), 1)
    yb_parts = []
    for gi, win in enumerate(POOL_WINDOWS):
        lo, hi = gi * POOL_GC, (gi + 1) * POOL_GC
        cur = ubuf[:, HIST_PAD:HIST_PAD + tt, lo:hi]
        s = cur
        for k in range(1, win):
            s = s + ubuf[:, HIST_PAD - k:HIST_PAD - k + tt, lo:hi]
        cnt = jnp.minimum(pos + 1, win).astype(F32)
        pooled = s / cnt - cur
        yb_parts.append(jnp.dot(pooled.reshape(m, POOL_GC).astype(BF16), wp_ref[gi],
                                preferred_element_type=F32))
    yb = jnp.concatenate(yb_parts, axis=-1) * ps_ref[...]

    cos4 = jnp.concatenate([cos_ref[...]] * HEADS, axis=-1)
    sin4 = jnp.concatenate([sin_ref[...]] * HEADS, axis=-1)
    lane = lax.broadcasted_iota(jnp.int32, (m, ATT_W), 1)
    first_half = (lane & (HEAD_DIM // 2)) == 0

    def rope(z):
        swapped = jnp.where(first_half, pltpu.roll(z, ATT_W - HEAD_DIM // 2, axis=1),
                            pltpu.roll(z, HEAD_DIM // 2, axis=1))
        return z * cos4 + swapped * sin4

    q = rope(z_q) * Q_SCALE
    k = rope(z_k)
    v = z_v
    for hd in range(HEADS):
        lo, hi = hd * HEAD_W, (hd + 1) * HEAD_W
        k32_ref[:, pl.ds(hd, tt, stride=HEADS), :] = k[:, lo:hi].reshape(bb, tt, HEAD_W)
        v32_ref[:, pl.ds(hd, tt, stride=HEADS), :] = v[:, lo:hi].reshape(bb, tt, HEAD_W)
    k16_ref[...] = k.astype(BF16).reshape(bb, tt, ATT_W)
    if tk is None:
        q_ref[...] = q.astype(BF16).reshape(bb, tt, ATT_W)
        v16_ref[...] = v.astype(BF16).reshape(bb, tt, ATT_W)
    else:
        q_ref[0] = q.T.astype(BF16)
        for c in range(tt // tk):
            v16_ref[0, c] = v[c * tk:(c + 1) * tk, :].T.astype(BF16)

    ga = jax.nn.sigmoid(seg(C_GA, C_GB))
    gb = jax.nn.sigmoid(seg(C_GB, C_GC))
    mab_ref[...] = (ga * ya + gb * yb).astype(BF16).reshape(bb, tt, D_MODEL)
    gc_ref[...] = jax.nn.sigmoid(seg(C_GC, C_END)).astype(BF16).reshape(bb, tt, D_MODEL)

    last_c = cbuf[:, tt:tt + HIST_PAD, :]
    last_p = ubuf[:, tt:tt + HIST_PAD, :]
    co_ref[...] = last_c
    po_ref[...] = last_p
    cbuf[:, 0:HIST_PAD, :] = last_c
    ubuf[:, 0:HIST_PAD, :] = last_p


def _mixer_in(x, hist_c, hist_p, cos, sin, layer, p, kv_bufs, *, bb, tt, pos0, tk):
    B, T, _ = x.shape
    depth = p['w_in'].shape[0]
    assert B % bb == 0 and T % tt == 0 and tt >= HIST_PAD and tt % 8 == 0
    assert tk is None or (bb == 1 and tt % tk == 0)
    grid = (B // bb, T // tt)
    m = bb * tt

    def seq_spec(w):
        return pl.BlockSpec((bb, tt, w), lambda b, t: (b, t, 0))

    if hist_c.ndim == 4:
        hist_spec = pl.BlockSpec((None, bb, HIST_PAD, CONV_W), lambda b, t: (layer, b, 0, 0))
    else:
        hist_spec = pl.BlockSpec((bb, HIST_PAD, CONV_W), lambda b, t: (b, 0, 0))
    state_spec = pl.BlockSpec((bb, HIST_PAD, CONV_W), lambda b, t: (b, 0, 0))
    rope_spec = pl.BlockSpec((None, m, HEAD_W), lambda b, t: (t, 0, 0))
    kv_spec = pl.BlockSpec((None, bb, tt * HEADS, HEAD_W), lambda b, t: (layer, b, t, 0))
    kv_shape = jax.ShapeDtypeStruct((depth, B, T * HEADS, HEAD_W), F32)
    if tk is None:
        q_shape, q_spec = jax.ShapeDtypeStruct((B, T, ATT_W), BF16), seq_spec(ATT_W)
        v_shape, v_spec = q_shape, q_spec
    else:
        q_shape = jax.ShapeDtypeStruct((B, ATT_W, T), BF16)
        q_spec = pl.BlockSpec((1, ATT_W, tt), lambda b, t: (b, 0, t))
        v_shape = jax.ShapeDtypeStruct((B, T // tk, ATT_W, tk), BF16)
        v_spec = pl.BlockSpec((1, tt // tk, ATT_W, tk), lambda b, t: (b, t, 0, 0))

    out_shape = (
        jax.ShapeDtypeStruct((B, T, D_MODEL), BF16),
        jax.ShapeDtypeStruct((B, T, D_MODEL), BF16),
        q_shape,
        jax.ShapeDtypeStruct((B, T, ATT_W), BF16),
        v_shape,
        kv_shape, kv_shape,
        jax.ShapeDtypeStruct((B, HIST_PAD, CONV_W), F32),
        jax.ShapeDtypeStruct((B, HIST_PAD, POOL_W), F32),
    )
    out_specs = (seq_spec(D_MODEL), seq_spec(D_MODEL), q_spec, seq_spec(ATT_W), v_spec,
                 kv_spec, kv_spec, state_spec, state_spec)
    in_specs = [seq_spec(D_MODEL), hist_spec, hist_spec, rope_spec, rope_spec,
                _layer_spec(p['norm_mix'], layer), _layer_spec(p['w_in'], layer),
                _layer_spec(p['conv_w'], layer), _layer_spec(p['w_conv_out'], layer),
                _layer_spec(p['w_pool'], layer), _layer_spec(p['pool_scale'], layer)]
    args = [x, hist_c, hist_p, cos, sin, p['norm_mix'], p['w_in'], p['conv_w'], p['w_conv_out'],
            p['w_pool'], p['pool_scale']]
    aliases = {}
    if kv_bufs is not None:
        in_specs += [pl.BlockSpec(memory_space=pl.ANY)] * 2
        aliases = {len(args): 5, len(args) + 1: 6}
        args += list(kv_bufs)
    return pl.pallas_call(
        functools.partial(_mixer_in_kernel, bb=bb, tt=tt, pos0=pos0, tk=tk,
                          aliased=kv_bufs is not None),
        grid=grid, in_specs=in_specs, out_specs=out_specs, out_shape=out_shape,
        scratch_shapes=[pltpu.VMEM((bb, HIST_PAD + tt, CONV_W), F32),
                        pltpu.VMEM((bb, HIST_PAD + tt, POOL_W), F32)],
        input_output_aliases=aliases,
        compiler_params=pltpu.CompilerParams(dimension_semantics=("parallel", "arbitrary"),
                                             vmem_limit_bytes=VMEM_LIMIT),
        name="mixer_in",
    )(*args)


def _lambda(lq1_ref, lk1_ref, lq2_ref, lk2_ref, lam_init):
    a = jnp.exp(jnp.sum(lq1_ref[...] * lk1_ref[...], axis=-1, keepdims=True))
    b = jnp.exp(jnp.sum(lq2_ref[...] * lk2_ref[...], axis=-1, keepdims=True))
    return a - b + lam_init


def _lam_init(layer):
    return 0.8 - 0.6 * math.exp(-0.3 * layer)


def _merge_out(o, mab, gc, x, wc_ref, wo_ref):
    yc = jnp.dot(o.astype(BF16), wc_ref[...], preferred_element_type=F32)
    merged = mab.astype(F32) + gc.astype(F32) * yc
    return x + jnp.dot(merged.astype(BF16), wo_ref[...], preferred_element_type=F32)


def _attn_weight_specs(p, layer):
    names = ('lambda_q1', 'lambda_k1', 'lambda_q2', 'lambda_k2', 'subln_g', 'w_attn_out', 'w_o')
    return [_layer_spec(p[n], layer) for n in names], [p[n] for n in names]


def _attn_prompt_kernel(qt_ref, k_ref, vt_ref, mab_ref, gc_ref, x_ref, lq1_ref, lk1_ref, lq2_ref,
                        lk2_ref, sg_ref, wc_ref, wo_ref, o_ref, p_scr, acc_scr, *, tq, tk,
                        lam_init):
    i = pl.program_id(1)
    n_diag = tq // tk
    lam = _lambda(lq1_ref, lk1_ref, lq2_ref, lk2_ref, lam_init)
    feat = lax.broadcasted_iota(jnp.int32, (HEAD_W, tq), 0)
    qs = []
    for hd in range(HEADS):
        qt = qt_ref[0, hd * HEAD_W:(hd + 1) * HEAD_W, :]
        zero = jnp.zeros_like(qt)
        qs.append(jnp.concatenate([jnp.where(feat < HEAD_DIM, qt, zero),
                                   jnp.where(feat < HEAD_DIM, zero, qt)], axis=1))

    def scores(j):
        off = pl.multiple_of(j * tk, tk)
        return [jnp.dot(k_ref[0, pl.ds(off, tk), hd * HEAD_W:(hd + 1) * HEAD_W], qs[hd],
                        preferred_element_type=F32) for hd in range(HEADS)]

    def softmax(s, m):
        m_out, alpha_out = [], []
        for hd in range(HEADS):
            m_new = jnp.maximum(m[hd], jnp.max(s[hd], axis=0, keepdims=True))
            p_scr[hd] = jnp.exp2(s[hd] - m_new).astype(BF16)
            m_out.append(m_new)
            alpha_out.append(jnp.exp2(m[hd] - m_new))
        return m_out, alpha_out

    ones_rows = jnp.ones((DENOM_ROWS, tk), BF16)

    def flush(j, alpha):
        for hd in range(HEADS):
            vt1 = jnp.concatenate([vt_ref[0, j, hd * HEAD_W:(hd + 1) * HEAD_W, :], ones_rows],
                                  axis=0)
            acc_scr[hd] = alpha[hd] * acc_scr[hd] + jnp.dot(vt1, p_scr[hd],
                                                            preferred_element_type=F32)

    acc_scr[...] = jnp.zeros_like(acc_scr)
    m = [jnp.full((1, 2 * tq), -jnp.inf, F32)] * HEADS
    alpha = [jnp.zeros((1, 2 * tq), F32)] * HEADS
    q_chunk = (lax.broadcasted_iota(jnp.int32, (tk, 2 * tq), 1) % tq) // CHUNK
    k_row = lax.broadcasted_iota(jnp.int32, (tk, 2 * tq), 0)
    for d in range(n_diag):
        j = i * n_diag + d
        s = scores(j)
        if d > 0:
            flush(j - 1, alpha)
        visible = (k_row + d * tk) // CHUNK <= q_chunk
        m, alpha = softmax([jnp.where(visible, sh, -jnp.inf) for sh in s], m)

    def body(j, carry):
        pend, alpha, m = carry
        s = scores(j)
        flush(pend, alpha)
        m, alpha = softmax(s, m)
        return j, alpha, m

    pend, alpha, m = lax.fori_loop(0, i * n_diag, body, (i * n_diag + n_diag - 1, alpha, m))
    flush(pend, alpha)
    heads = []
    for hd in range(HEADS):
        ot = acc_scr[hd, 0:HEAD_W, :] / acc_scr[hd, HEAD_W:HEAD_W + 1, :]
        ot = ot[:, :tq] - lam * ot[:, tq:]
        ot = ot * lax.rsqrt(jnp.mean(ot * ot, axis=0, keepdims=True) + SUBLN_EPS)
        heads.append(ot.T * sg_ref[...] * (1.0 - lam_init))
    o = jnp.concatenate(heads, axis=-1)
    o_ref[0] = _merge_out(o, mab_ref[0], gc_ref[0], x_ref[0], wc_ref, wo_ref)


def _attn_prompt(qt, k16, vt, mab, gc, x, layer, p, *, tq, tk):
    B, T, _ = x.shape
    assert T % tq == 0 and tq % tk == 0 and tk % CHUNK == 0 and vt.shape[3] == tk

    def tile_spec(w):
        return pl.BlockSpec((1, tq, w), lambda b, i: (b, i, 0))

    weight_specs, weights = _attn_weight_specs(p, layer)
    return pl.pallas_call(
        functools.partial(_attn_prompt_kernel, tq=tq, tk=tk, lam_init=_lam_init(layer)),
        grid=(B, T // tq),
        in_specs=[pl.BlockSpec((1, ATT_W, tq), lambda b, i: (b, 0, i)),
                  pl.BlockSpec((1, T, ATT_W), lambda b, i: (b, 0, 0)),
                  pl.BlockSpec((1, T // tk, ATT_W, tk), lambda b, i: (b, 0, 0, 0)),
                  tile_spec(D_MODEL), tile_spec(D_MODEL), tile_spec(D_MODEL)] + weight_specs,
        out_specs=tile_spec(D_MODEL),
        out_shape=jax.ShapeDtypeStruct((B, T, D_MODEL), F32),
        scratch_shapes=[pltpu.VMEM((HEADS, tk, 2 * tq), BF16),
                        pltpu.VMEM((HEADS, HEAD_W + DENOM_ROWS, 2 * tq), F32)],
        compiler_params=pltpu.CompilerParams(dimension_semantics=("parallel", "arbitrary"),
                                             vmem_limit_bytes=VMEM_LIMIT),
        name="attn_prompt",
    )(qt, k16, vt, mab, gc, x, *weights)


def _stack_components(qh):
    lane = lax.broadcasted_iota(jnp.int32, qh.shape, 1)
    zero = jnp.zeros_like(qh)
    return jnp.concatenate([jnp.where(lane < HEAD_DIM, qh, zero),
                            jnp.where(lane < HEAD_DIM, zero, qh)], axis=0)


def _softmax_step(carry, s, vj):
    m, l, acc = carry
    m_new = jnp.maximum(m, jnp.max(s, axis=-1, keepdims=True))
    pr = jnp.exp2(s - m_new)
    alpha = jnp.exp2(m - m_new)
    l = alpha * l + jnp.sum(pr, axis=-1, keepdims=True)
    acc = alpha * acc + jnp.dot(pr.astype(BF16), vj, preferred_element_type=F32)
    return m_new, l, acc


def _attn_sample_kernel(q_ref, kp_ref, vp_ref, kn_ref, vn_ref, mab_ref, gc_ref, x_ref, lq1_ref,
                        lk1_ref, lq2_ref, lk2_ref, sg_ref, wc_ref, wo_ref, o_ref, *, past,
                        lam_init):
    n_seq, tq, _ = q_ref.shape
    lam = _lambda(lq1_ref, lk1_ref, lq2_ref, lk2_ref, lam_init)
    q_chunk = (past + lax.broadcasted_iota(jnp.int32, (2 * tq, 1), 0) % tq) // CHUNK
    vis_past = (lax.broadcasted_iota(jnp.int32, (2 * tq, past), 1) // CHUNK) <= q_chunk
    vis_new = ((past + lax.broadcasted_iota(jnp.int32, (2 * tq, tq), 1)) // CHUNK) <= q_chunk
    rows = []
    for sq in range(n_seq):
        heads = []
        for hd in range(HEADS):
            lo, hi = hd * HEAD_W, (hd + 1) * HEAD_W
            qs = _stack_components(q_ref[sq, :, lo:hi])
            carry = (jnp.full((2 * tq, 1), -jnp.inf, F32), jnp.zeros((2 * tq, 1), F32),
                     jnp.zeros((2 * tq, HEAD_W), F32))
            s = lax.dot_general(qs, kn_ref[sq, :, lo:hi], NT_DIMS, preferred_element_type=F32)
            carry = _softmax_step(carry, jnp.where(vis_new, s, -jnp.inf), vn_ref[sq, :, lo:hi])
            kp = kp_ref[sq, pl.ds(hd, past, stride=HEADS), :].astype(BF16)
            vp = vp_ref[sq, pl.ds(hd, past, stride=HEADS), :].astype(BF16)
            s = lax.dot_general(qs, kp, NT_DIMS, preferred_element_type=F32)
            _, l, acc = _softmax_step(carry, jnp.where(vis_past, s, -jnp.inf), vp)
            o = acc / l
            o = o[:tq] - lam * o[tq:]
            heads.append(_rms(o, sg_ref[...], SUBLN_EPS) * (1.0 - lam_init))
        rows.append(jnp.concatenate(heads, axis=-1))
    o = jnp.concatenate(rows, axis=0)
    flat = (n_seq * tq, D_MODEL)
    out = _merge_out(o, mab_ref[...].reshape(flat), gc_ref[...].reshape(flat),
                     x_ref[...].reshape(flat), wc_ref, wo_ref)
    o_ref[...] = out.reshape(n_seq, tq, D_MODEL)


def _attn_sample(q, cache_k, cache_v, k16, v16, mab, gc, x, layer, p, *, seqs):
    B, T, _ = x.shape
    past = cache_k.shape[2] // HEADS
    assert B % seqs == 0

    def tile_spec(w):
        return pl.BlockSpec((seqs, T, w), lambda b: (b, 0, 0))

    past_spec = pl.BlockSpec((None, seqs, past * HEADS, HEAD_W), lambda b: (layer, b, 0, 0))
    weight_specs, weights = _attn_weight_specs(p, layer)
    return pl.pallas_call(
        functools.partial(_attn_sample_kernel, past=past, lam_init=_lam_init(layer)),
        grid=(B // seqs,),
        in_specs=[tile_spec(ATT_W), past_spec, past_spec, tile_spec(ATT_W), tile_spec(ATT_W),
                  tile_spec(D_MODEL), tile_spec(D_MODEL), tile_spec(D_MODEL)] + weight_specs,
        out_specs=tile_spec(D_MODEL),
        out_shape=jax.ShapeDtypeStruct((B, T, D_MODEL), F32),
        compiler_params=pltpu.CompilerParams(dimension_semantics=("parallel",),
                                             vmem_limit_bytes=VMEM_LIMIT),
        name="attn_sample",
    )(q, cache_k, cache_v, k16, v16, mab, gc, x, *weights)


def _ffn_dense_kernel(x_ref, ng_ref, wg_ref, wu_ref, wd_ref, nf_ref, o_ref, *, final):
    bb, tt, _ = x_ref.shape
    x = x_ref[...].reshape(bb * tt, D_MODEL)
    h = _rms(x, ng_ref[...], EPS).astype(BF16)
    y = x
    for lo, hi in _chunks(wg_ref.shape[1], 1024):
        g = jnp.dot(h, wg_ref[:, lo:hi], preferred_element_type=F32)
        u = jnp.dot(h, wu_ref[:, lo:hi], preferred_element_type=F32)
        a = (g * jax.nn.sigmoid(g) * u).astype(BF16)
        y = y + jnp.dot(a, wd_ref[lo:hi, :], preferred_element_type=F32)
    if final:
        y = _rms(y, nf_ref[...], EPS)
    o_ref[...] = y.reshape(bb, tt, D_MODEL)


def _token_blocks(B, T, tm):
    if T >= tm:
        assert T % tm == 0
        return 1, tm
    assert tm % T == 0 and B % (tm // T) == 0
    return tm // T, T


def _ffn_dense(x, layer, p, *, tm, final):
    B, T, _ = x.shape
    bb, tt = _token_blocks(B, T, tm)
    d = layer // 2
    row_spec = pl.BlockSpec((bb, tt, D_MODEL), lambda b, t: (b, t, 0))
    return pl.pallas_call(
        functools.partial(_ffn_dense_kernel, final=final),
        grid=(B // bb, T // tt),
        in_specs=[row_spec, _layer_spec(p['norm_ffn'], layer), _layer_spec(p['w_gate_d'], d),
                  _layer_spec(p['w_up_d'], d), _layer_spec(p['w_down_d'], d),
                  _const_spec(p['norm_final'].shape)],
        out_specs=row_spec,
        out_shape=jax.ShapeDtypeStruct(x.shape, F32),
        compiler_params=pltpu.CompilerParams(dimension_semantics=("parallel", "parallel"),
                                             vmem_limit_bytes=VMEM_LIMIT),
        name="ffn_dense",
    )(x, p['norm_ffn'], p['w_gate_d'], p['w_up_d'], p['w_down_d'], p['norm_final'])


def _route_top2(logits):
    row = lax.broadcasted_iota(jnp.int32, logits.shape, 0).astype(F32)
    big = float(N_EXPERTS)
    m1 = jnp.max(logits, axis=0, keepdims=True)
    i1 = jnp.min(jnp.where(logits == m1, row, big), axis=0, keepdims=True)
    rest = jnp.where(row == i1, -jnp.inf, logits)
    m2 = jnp.max(rest, axis=0, keepdims=True)
    i2 = jnp.min(jnp.where(rest == m2, row, big), axis=0, keepdims=True)
    e2 = jnp.exp(m2 - m1)
    w1 = 1.0 / (1.0 + e2)
    w2 = e2 / (1.0 + e2)
    return jnp.where(row == i1, w1, 0.0) + jnp.where(row == i2, w2, 0.0)


def _ffn_moe_kernel(x_ref, ng_ref, wrt_ref, brt_ref, wg_ref, wu_ref, wd_ref, nf_ref, o_ref,
                    h_scr, wts_scr, rank_scr, *, final, tb, cms):
    e = pl.program_id(2)
    bb, tt, _ = x_ref.shape
    n_tok = bb * tt
    sub_blocks = _chunks(n_tok, tb)

    @pl.when(e == 0)
    def _():
        x = x_ref[...].reshape(n_tok, D_MODEL)
        h = _rms(x, ng_ref[...], EPS)
        logits = lax.dot_general(wrt_ref[...], h, NT_DIMS, preferred_element_type=F32,
                                 precision=lax.Precision.HIGHEST) + brt_ref[...]
        wts = _route_top2(logits)
        sel = (wts > 0.0).astype(BF16)
        before = (lax.broadcasted_iota(jnp.int32, (tb, tb), 0)
                  < lax.broadcasted_iota(jnp.int32, (tb, tb), 1)).astype(BF16)
        rank = jnp.concatenate(
            [jnp.dot(sel[:, lo:hi], before, preferred_element_type=F32) for lo, hi in sub_blocks],
            axis=1)
        for ee in range(N_EXPERTS):
            for sb, (lo, hi) in enumerate(sub_blocks):
                wts_scr[ee, sb] = wts[ee:ee + 1, lo:hi]
                rank_scr[ee, sb] = rank[ee:ee + 1, lo:hi]
        h_scr[...] = h.astype(BF16)
        o_ref[...] = x_ref[...]

    cm_max = cms[-1]

    def sub_block(sb, carry):
        w_row = wts_scr[e, sb]
        r_row = rank_scr[e, sb]
        sel = w_row > 0.0
        n_sel = jnp.sum(sel.astype(F32)).astype(jnp.int32)
        row0 = pl.multiple_of(sb * tb, tb)

        def run_chunk(cm, base):
            rows = (lax.broadcasted_iota(jnp.int32, (cm, tb), 0) + base).astype(F32)
            hit = (r_row == rows) & sel
            gather = jnp.where(hit, 1.0, 0.0).astype(BF16)
            xg = jnp.dot(gather, h_scr[pl.ds(row0, tb), :],
                         preferred_element_type=F32).astype(BF16)
            g = jnp.dot(xg, wg_ref[...], preferred_element_type=F32)
            u = jnp.dot(xg, wu_ref[...], preferred_element_type=F32)
            a = (g * jax.nn.sigmoid(g) * u).astype(BF16)
            y = jnp.dot(a, wd_ref[...], preferred_element_type=F32).astype(BF16)
            scatter = jnp.where(hit, w_row, 0.0).astype(BF16)
            upd = lax.dot_general(scatter, y, TN_DIMS, preferred_element_type=F32)
            if bb == 1:
                o_ref[0, pl.ds(row0, tb), :] += upd
            else:
                o_ref[pl.ds(sb * (tb // tt), tb // tt)] += upd.reshape(tb // tt, tt, D_MODEL)

        def chunk(c, carry):
            left = n_sel - c * cm_max
            size_idx = sum((left > cm).astype(jnp.int32) for cm in cms[:-1])
            lax.switch(size_idx, [functools.partial(run_chunk, cm) for cm in cms], c * cm_max)
            return carry

        n_chunks = sum((n_sel > c * cm_max).astype(jnp.int32) for c in range(-(-tb // cm_max)))
        lax.fori_loop(0, n_chunks, chunk, 0)
        return carry

    lax.fori_loop(0, n_tok // tb, sub_block, 0)

    if final:
        @pl.when(e == N_EXPERTS - 1)
        def _():
            y = o_ref[...].reshape(n_tok, D_MODEL)
            o_ref[...] = _rms(y, nf_ref[...], EPS).reshape(bb, tt, D_MODEL)


def _ffn_moe(x, layer, p, *, tm, tb, cms, final):
    B, T, _ = x.shape
    bb, tt = _token_blocks(B, T, tm)
    assert tm % tb == 0 and (bb == 1 or tb % tt == 0)
    mo = layer // 2
    row_spec = pl.BlockSpec((bb, tt, D_MODEL), lambda b, t, e: (b, t, 0))

    def expert_spec(arr):
        return pl.BlockSpec((None, None) + tuple(arr.shape[2:]), lambda b, t, e: (mo, e, 0, 0))

    return pl.pallas_call(
        functools.partial(_ffn_moe_kernel, final=final, tb=tb, cms=cms),
        grid=(B // bb, T // tt, N_EXPERTS),
        in_specs=[row_spec, _layer_spec(p['norm_ffn'], layer), _layer_spec(p['w_router_t'], mo),
                  _layer_spec(p['b_router_t'], mo), expert_spec(p['w_gate_e']),
                  expert_spec(p['w_up_e']), expert_spec(p['w_down_e']),
                  _const_spec(p['norm_final'].shape)],
        out_specs=row_spec,
        out_shape=jax.ShapeDtypeStruct(x.shape, F32),
        scratch_shapes=[pltpu.VMEM((tm, D_MODEL), BF16),
                        pltpu.VMEM((N_EXPERTS, tm // tb, 1, tb), F32),
                        pltpu.VMEM((N_EXPERTS, tm // tb, 1, tb), F32)],
        compiler_params=pltpu.CompilerParams(
            dimension_semantics=("parallel", "parallel", "arbitrary"),
            vmem_limit_bytes=VMEM_LIMIT),
        name="ffn_moe",
    )(x, p['norm_ffn'], p['w_router_t'], p['b_router_t'], p['w_gate_e'], p['w_up_e'],
      p['w_down_e'], p['norm_final'])


def _rope_tables(pos0, T, bb, tt):
    half = HEAD_DIM // 2
    inv_freq = ROPE_THETA ** (-jnp.arange(half, dtype=F32) / half)
    ang = (pos0 + jnp.arange(T)).astype(F32)[:, None] * inv_freq[None, :]
    cos, sin = jnp.cos(ang), jnp.sin(ang)
    cos = jnp.concatenate([cos, cos, cos, cos], axis=-1)
    sin = jnp.concatenate([-sin, sin, -sin, sin], axis=-1)

    def lay(a):
        a = a.reshape(T // tt, 1, tt, HEAD_W)
        return jnp.broadcast_to(a, (T // tt, bb, tt, HEAD_W)).reshape(T // tt, bb * tt, HEAD_W)

    return lay(cos), lay(sin)


def _pad_hist(state):
    return jnp.pad(state, ((0, 0), (0, 0), (HIST_PAD - state.shape[2], 0), (0, 0)))


_MATMUL_WEIGHTS = ('w_in', 'w_conv_out', 'w_pool', 'w_attn_out', 'w_o', 'w_gate_d', 'w_up_d',
                   'w_down_d', 'w_gate_e', 'w_up_e', 'w_down_e')
_ROW_VECTORS = ('norm_mix', 'pool_scale', 'lambda_q1', 'lambda_k1', 'lambda_q2', 'lambda_k2',
                'subln_g', 'norm_ffn')


def _params(w):
    p = dict(w)
    for n in _MATMUL_WEIGHTS:
        p[n] = w[n].astype(BF16)
    for n in _ROW_VECTORS:
        p[n] = w[n][:, None, :]
    p['w_router_t'] = jnp.swapaxes(w['w_router'], 1, 2)
    p['b_router_t'] = w['b_router'][:, :, None]
    p['norm_final'] = w['norm_final'][None, :]
    return p


def _new_state(co, po):
    return co[:, HIST_PAD - (CONV_K - 1):], po[:, HIST_PAD - POOL_HIST:]


def _token_mixer_prompt(x, layer, p, kv_bufs):
    B, T, _ = x.shape
    tt = min(PROMPT_TILE, T)
    tk = min(KEY_TILE, tt)
    cos, sin = _rope_tables(0, T, 1, tt)
    zero_hist = jnp.zeros((B, HIST_PAD, CONV_W), F32)
    mab, gc, qt, k16, vt, k32, v32, co, po = _mixer_in(
        x, zero_hist, zero_hist, cos, sin, layer, p, kv_bufs, bb=1, tt=tt, pos0=0, tk=tk)
    x = _attn_prompt(qt, k16, vt, mab, gc, x, layer, p, tq=tt, tk=tk)
    return (x,) + _new_state(co, po) + ((k32, v32),)


def _token_mixer_sample(x, cache_k, cache_v, state_conv, state_pool, layer, p, kv_bufs):
    past = cache_k.shape[2]
    B, T, _ = x.shape
    bb = min(SAMPLE_MIXER_SEQS, B)
    cos, sin = _rope_tables(past, T, bb, T)
    mab, gc, q, k16, v16, k32, v32, co, po = _mixer_in(
        x, _pad_hist(state_conv), _pad_hist(state_pool), cos, sin, layer, p, kv_bufs,
        bb=bb, tt=T, pos0=past, tk=None)
    rows = cache_k.shape[:2] + (past * HEADS, HEAD_W)
    x = _attn_sample(q, cache_k.reshape(rows), cache_v.reshape(rows), k16, v16, mab, gc, x, layer,
                     p, seqs=min(SAMPLE_ATTN_SEQS, B))
    return (x,) + _new_state(co, po) + ((k32, v32),)


def _channel_mixer(x, layer, p, *, final):
    tokens = x.shape[0] * x.shape[1]
    if layer % 2 == 0:
        return _ffn_dense(x, layer, p, tm=min(DENSE_TOKENS, tokens), final=final)
    tm = min(MOE_TOKENS, tokens)
    return _ffn_moe(x, layer, p, tm=tm, tb=min(MOE_SUB_TOKENS, tm), cms=MOE_CHUNK_ROWS,
                    final=final)


def kernel(x_prompt, x_sample, cache_k, cache_v, state_conv, state_pool, norm_mix, w_in, conv_w,
           w_conv_out, w_pool, pool_scale, lambda_q1, lambda_k1, lambda_q2, lambda_k2, subln_g,
           w_attn_out, w_o, norm_ffn, w_gate_d, w_up_d, w_down_d, w_router, b_router, w_gate_e,
           w_up_e, w_down_e, norm_final):
    depth = w_in.shape[0]
    p = _params(dict(
        norm_mix=norm_mix, w_in=w_in, conv_w=conv_w, w_conv_out=w_conv_out, w_pool=w_pool,
        pool_scale=pool_scale, lambda_q1=lambda_q1, lambda_k1=lambda_k1, lambda_q2=lambda_q2,
        lambda_k2=lambda_k2, subln_g=subln_g, w_attn_out=w_attn_out, w_o=w_o, norm_ffn=norm_ffn,
        w_gate_d=w_gate_d, w_up_d=w_up_d, w_down_d=w_down_d, w_router=w_router,
        b_router=b_router, w_gate_e=w_gate_e, w_up_e=w_up_e, w_down_e=w_down_e,
        norm_final=norm_final))

    xp, xs = x_prompt, x_sample
    kv_p = kv_s = None
    states = [[] for _ in range(4)]
    for l in range(depth):
        final = l == depth - 1
        xp, cp, pp, kv_p = _token_mixer_prompt(xp, l, p, kv_p)
        xs, cs, ps, kv_s = _token_mixer_sample(xs, cache_k, cache_v, state_conv, state_pool, l, p,
                                               kv_s)
        for lst, a in zip(states, (cp, pp, cs, ps)):
            lst.append(a)
        xp = _channel_mixer(xp, l, p, final=final)
        xs = _channel_mixer(xs, l, p, final=final)

    def heads_view(a):
        return a.reshape(a.shape[0], a.shape[1], a.shape[2] // HEADS, HEADS, HEAD_W)

    return (xp, xs, heads_view(kv_p[0]), heads_view(kv_p[1]), jnp.stack(states[0]),
            jnp.stack(states[1]), heads_view(kv_s[0]), heads_view(kv_s[1]),
            jnp.stack(states[2]), jnp.stack(states[3]))
```

```python
import functools
import math

import jax
import jax.numpy as jnp
from jax import lax
from jax.experimental import pallas as pl
from jax.experimental.pallas import tpu as pltpu

D_MODEL = 1024
CHUNK = 64
CONV_W = 512
CONV_K = 3
POOL_W = 512
POOL_GC = 128
POOL_WINDOWS = (2, 4, 8, 16)
POOL_HIST = 15
HEADS = 4
HEAD_DIM = 64
HEAD_W = 2 * HEAD_DIM
ATT_W = HEADS * HEAD_W
ROPE_THETA = 10000.0
N_EXPERTS = 8
EPS = 1e-6
SUBLN_EPS = 1e-5

C_XA, C_BA, C_CA, C_U, C_Q, C_K, C_V, C_GA, C_GB, C_GC, C_END = (
    0, 512, 1024, 1536, 2048, 2560, 3072, 3584, 4608, 5632, 6656)

HIST_PAD = 16
Q_SCALE = HEAD_DIM ** -0.5 * math.log2(math.e)
DENOM_ROWS = 16

VMEM_LIMIT = 56 * 1024 * 1024
PROMPT_TILE = 512
KEY_TILE = 512
SAMPLE_MIXER_SEQS = 16
SAMPLE_ATTN_SEQS = 4
DENSE_TOKENS = 1024
MOE_TOKENS = 1024
MOE_SUB_TOKENS = 512
MOE_CHUNK_ROWS = tuple(range(128, 257, 16))

F32 = jnp.float32
BF16 = jnp.bfloat16
NT_DIMS = (((1,), (1,)), ((), ()))
TN_DIMS = (((0,), (0,)), ((), ()))


def _const_spec(shape):
    nd = len(shape)
    return pl.BlockSpec(tuple(shape), lambda *_: (0,) * nd, pipeline_mode=pl.Buffered(1))


def _layer_spec(arr, layer):
    nd = arr.ndim
    return pl.BlockSpec((None,) + tuple(arr.shape[1:]), lambda *_: (layer,) + (0,) * (nd - 1),
                        pipeline_mode=pl.Buffered(1))


def _rms(x, g, eps):
    return x * lax.rsqrt(jnp.mean(x * x, axis=-1, keepdims=True) + eps) * g


def _chunks(n, step):
    return [(lo, min(lo + step, n)) for lo in range(0, n, step)]


def _mixer_in_kernel(*refs, bb, tt, pos0, tk, aliased):
    (x_ref, hc_ref, hp_ref, cos_ref, sin_ref, ng_ref, win_ref, cw_ref, wa_ref, wp_ref,
     ps_ref) = refs[:11]
    refs = refs[11 + (2 if aliased else 0):]
    (mab_ref, gc_ref, q_ref, k16_ref, v16_ref, k32_ref, v32_ref, co_ref, po_ref, cbuf,
     ubuf) = refs
    t = pl.program_id(1)
    m = bb * tt
    x = x_ref[...].reshape(m, D_MODEL)
    h = _rms(x, ng_ref[...], EPS).astype(BF16)

    def seg(lo, hi):
        return jnp.dot(h, win_ref[:, lo:hi], preferred_element_type=F32)

    @pl.when(t == 0)
    def _():
        cbuf[:, 0:HIST_PAD, :] = hc_ref[...]
        ubuf[:, 0:HIST_PAD, :] = hp_ref[...]

    z_ca, z_xa, z_ba, z_u = seg(C_CA, C_U), seg(C_XA, C_BA), seg(C_BA, C_CA), seg(C_U, C_Q)
    z_q, z_k, z_v = seg(C_Q, C_K), seg(C_K, C_V), seg(C_V, C_GA)

    cin = (z_ca * z_xa).reshape(bb, tt, CONV_W)
    cbuf[:, HIST_PAD:, :] = cin
    conv = cbuf[:, HIST_PAD - 2:HIST_PAD - 2 + tt, :] * cw_ref[0:1, :]
    conv = conv + cbuf[:, HIST_PAD - 1:HIST_PAD - 1 + tt, :] * cw_ref[1:2, :]
    conv = conv + cin * cw_ref[2:3, :]
    ya = jnp.dot((z_ba * conv.reshape(m, CONV_W)).astype(BF16), wa_ref[...],
                 preferred_element_type=F32)

    ubuf[:, HIST_PAD:, :] = z_u.reshape(bb, tt, POOL_W)
    pos = pos0 + t * tt + lax.broadcasted_iota(jnp.int32, (bb, tt, POOL_GC), 1)
    yb_parts = []
    for gi, win in enumerate(POOL_WINDOWS):
        lo, hi = gi * POOL_GC, (gi + 1) * POOL_GC
        cur = ubuf[:, HIST_PAD:HIST_PAD + tt, lo:hi]
        s = cur
        for k in range(1, win):
            s = s + ubuf[:, HIST_PAD - k:HIST_PAD - k + tt, lo:hi]
        cnt = jnp.minimum(pos + 1, win).astype(F32)
        pooled = s / cnt - cur
        yb_parts.append(jnp.dot(pooled.reshape(m, POOL_GC).astype(BF16), wp_ref[gi],
                                preferred_element_type=F32))
    yb = jnp.concatenate(yb_parts, axis=-1) * ps_ref[...]

    cos4 = jnp.concatenate([cos_ref[...]] * HEADS, axis=-1)
    sin4 = jnp.concatenate([sin_ref[...]] * HEADS, axis=-1)
    lane = lax.broadcasted_iota(jnp.int32, (m, ATT_W), 1)
    first_half = (lane & (HEAD_DIM // 2)) == 0

    def rope(z):
        swapped = jnp.where(first_half, pltpu.roll(z, ATT_W - HEAD_DIM // 2, axis=1),
                            pltpu.roll(z, HEAD_DIM // 2, axis=1))
        return z * cos4 + swapped * sin4

    q = rope(z_q) * Q_SCALE
    k = rope(z_k)
    v = z_v
    for hd in range(HEADS):
        lo, hi = hd * HEAD_W, (hd + 1) * HEAD_W
        k32_ref[:, pl.ds(hd, tt, stride=HEADS), :] = k[:, lo:hi].reshape(bb, tt, HEAD_W)
        v32_ref[:, pl.ds(hd, tt, stride=HEADS), :] = v[:, lo:hi].reshape(bb, tt, HEAD_W)
    k16_ref[...] = k.astype(BF16).reshape(bb, tt, ATT_W)
    if tk is None:
        q_ref[...] = q.astype(BF16).reshape(bb, tt, ATT_W)
        v16_ref[...] = v.astype(BF16).reshape(bb, tt, ATT_W)
    else:
        q_ref[0] = q.T.astype(BF16)
        for c in range(tt // tk):
            v16_ref[0, c] = v[c * tk:(c + 1) * tk, :].T.astype(BF16)

    ga = jax.nn.sigmoid(seg(C_GA, C_GB))
    gb = jax.nn.sigmoid(seg(C_GB, C_GC))
    mab_ref[...] = (ga * ya + gb * yb).astype(BF16).reshape(bb, tt, D_MODEL)
    gc_ref[...] = jax.nn.sigmoid(seg(C_GC, C_END)).astype(BF16).reshape(bb, tt, D_MODEL)

    last_c = cbuf[:, tt:tt + HIST_PAD, :]
    last_p = ubuf[:, tt:tt + HIST_PAD, :]
    co_ref[...] = last_c
    po_ref[...] = last_p
    cbuf[:, 0:HIST_PAD, :] = last_c
    ubuf[:, 0:HIST_PAD, :] = last_p


def _mixer_in(x, hist_c, hist_p, cos, sin, layer, p, kv_bufs, *, bb, tt, pos0, tk):
    B, T, _ = x.shape
    depth = p['w_in'].shape[0]
    assert B % bb == 0 and T % tt == 0 and tt >= HIST_PAD and tt % 8 == 0
    assert tk is None or (bb == 1 and tt % tk == 0)
    grid = (B // bb, T // tt)
    m = bb * tt

    def seq_spec(w):
        return pl.BlockSpec((bb, tt, w), lambda b, t: (b, t, 0))

    if hist_c.ndim == 4:
        hist_spec = pl.BlockSpec((None, bb, HIST_PAD, CONV_W), lambda b, t: (layer, b, 0, 0))
    else:
        hist_spec = pl.BlockSpec((bb, HIST_PAD, CONV_W), lambda b, t: (b, 0, 0))
    state_spec = pl.BlockSpec((bb, HIST_PAD, CONV_W), lambda b, t: (b, 0, 0))
    rope_spec = pl.BlockSpec((None, m, HEAD_W), lambda b, t: (t, 0, 0))
    kv_spec = pl.BlockSpec((None, bb, tt * HEADS, HEAD_W), lambda b, t: (layer, b, t, 0))
    kv_shape = jax.ShapeDtypeStruct((depth, B, T * HEADS, HEAD_W), F32)
    if tk is None:
        q_shape, q_spec = jax.ShapeDtypeStruct((B, T, ATT_W), BF16), seq_spec(ATT_W)
        v_shape, v_spec = q_shape, q_spec
    else:
        q_shape = jax.ShapeDtypeStruct((B, ATT_W, T), BF16)
        q_spec = pl.BlockSpec((1, ATT_W, tt), lambda b, t: (b, 0, t))
        v_shape = jax.ShapeDtypeStruct((B, T // tk, ATT_W, tk), BF16)
        v_spec = pl.BlockSpec((1, tt // tk, ATT_W, tk), lambda b, t: (b, t, 0, 0))

    out_shape = (
        jax.ShapeDtypeStruct((B, T, D_MODEL), BF16),
        jax.ShapeDtypeStruct((B, T, D_MODEL), BF16),
        q_shape,
        jax.ShapeDtypeStruct((B, T, ATT_W), BF16),
        v_shape,
        kv_shape, kv_shape,
        jax.ShapeDtypeStruct((B, HIST_PAD, CONV_W), F32),
        jax.ShapeDtypeStruct((B, HIST_PAD, POOL_W), F32),
    )
    out_specs = (seq_spec(D_MODEL), seq_spec(D_MODEL), q_spec, seq_spec(ATT_W), v_spec,
                 kv_spec, kv_spec, state_spec, state_spec)
    in_specs = [seq_spec(D_MODEL), hist_spec, hist_spec, rope_spec, rope_spec,
                _layer_spec(p['norm_mix'], layer), _layer_spec(p['w_in'], layer),
                _layer_spec(p['conv_w'], layer), _layer_spec(p['w_conv_out'], layer),
                _layer_spec(p['w_pool'], layer), _layer_spec(p['pool_scale'], layer)]
    args = [x, hist_c, hist_p, cos, sin, p['norm_mix'], p['w_in'], p['conv_w'], p['w_conv_out'],
            p['w_pool'], p['pool_scale']]
    aliases = {}
    if kv_bufs is not None:
        in_specs += [pl.BlockSpec(memory_space=pl.ANY)] * 2
        aliases = {len(args): 5, len(args) + 1: 6}
        args += list(kv_bufs)
    return pl.pallas_call(
        functools.partial(_mixer_in_kernel, bb=bb, tt=tt, pos0=pos0, tk=tk,
                          aliased=kv_bufs is not None),
        grid=grid, in_specs=in_specs, out_specs=out_specs, out_shape=out_shape,
        scratch_shapes=[pltpu.VMEM((bb, HIST_PAD + tt, CONV_W), F32),
                        pltpu.VMEM((bb, HIST_PAD + tt, POOL_W), F32)],
        input_output_aliases=aliases,
        compiler_params=pltpu.CompilerParams(dimension_semantics=("parallel", "arbitrary"),
                                             vmem_limit_bytes=VMEM_LIMIT),
        name="mixer_in",
    )(*args)


def _lambda(lq1_ref, lk1_ref, lq2_ref, lk2_ref, lam_init):
    a = jnp.exp(jnp.sum(lq1_ref[...] * lk1_ref[...], axis=-1, keepdims=True))
    b = jnp.exp(jnp.sum(lq2_ref[...] * lk2_ref[...], axis=-1, keepdims=True))
    return a - b + lam_init


def _lam_init(layer):
    return 0.8 - 0.6 * math.exp(-0.3 * layer)


def _merge_out(o, mab, gc, x, wc_ref, wo_ref):
    yc = jnp.dot(o.astype(BF16), wc_ref[...], preferred_element_type=F32)
    merged = mab.astype(F32) + gc.astype(F32) * yc
    return x + jnp.dot(merged.astype(BF16), wo_ref[...], preferred_element_type=F32)


def _attn_weight_specs(p, layer):
    names = ('lambda_q1', 'lambda_k1', 'lambda_q2', 'lambda_k2', 'subln_g', 'w_attn_out', 'w_o')
    return [_layer_spec(p[n], layer) for n in names], [p[n] for n in names]


def _attn_prompt_kernel(qt_ref, k_ref, vt_ref, mab_ref, gc_ref, x_ref, lq1_ref, lk1_ref, lq2_ref,
                        lk2_ref, sg_ref, wc_ref, wo_ref, o_ref, p_scr, acc_scr, *, tq, tk,
                        lam_init):
    i = pl.program_id(1)
    n_diag = tq // tk
    lam = _lambda(lq1_ref, lk1_ref, lq2_ref, lk2_ref, lam_init)
    feat = lax.broadcasted_iota(jnp.int32, (HEAD_W, tq), 0)
    qs = []
    for hd in range(HEADS):
        qt = qt_ref[0, hd * HEAD_W:(hd + 1) * HEAD_W, :]
        zero = jnp.zeros_like(qt)
        qs.append(jnp.concatenate([jnp.where(feat < HEAD_DIM, qt, zero),
                                   jnp.where(feat < HEAD_DIM, zero, qt)], axis=1))

    def scores(j):
        off = pl.multiple_of(j * tk, tk)
        return [jnp.dot(k_ref[0, pl.ds(off, tk), hd * HEAD_W:(hd + 1) * HEAD_W], qs[hd],
                        preferred_element_type=F32) for hd in range(HEADS)]

    def softmax(s, m):
        m_out, alpha_out = [], []
        for hd in range(HEADS):
            m_new = jnp.maximum(m[hd], jnp.max(s[hd], axis=0, keepdims=True))
            p_scr[hd] = jnp.exp2(s[hd] - m_new).astype(BF16)
            m_out.append(m_new)
            alpha_out.append(jnp.exp2(m[hd] - m_new))
        return m_out, alpha_out

    ones_rows = jnp.ones((DENOM_ROWS, tk), BF16)

    def flush(j, alpha):
        for hd in range(HEADS):
            vt1 = jnp.concatenate([vt_ref[0, j, hd * HEAD_W:(hd + 1) * HEAD_W, :], ones_rows],
                                  axis=0)
            acc_scr[hd] = alpha[hd] * acc_scr[hd] + jnp.dot(vt1, p_scr[hd],
                                                            preferred_element_type=F32)

    acc_scr[...] = jnp.zeros_like(acc_scr)
    m = [jnp.full((1, 2 * tq), -jnp.inf, F32)] * HEADS
    alpha = [jnp.zeros((1, 2 * tq), F32)] * HEADS
    q_chunk = (lax.broadcasted_iota(jnp.int32, (tk, 2 * tq), 1) % tq) // CHUNK
    k_row = lax.broadcasted_iota(jnp.int32, (tk, 2 * tq), 0)
    for d in range(n_diag):
        j = i * n_diag + d
        s = scores(j)
        if d > 0:
            flush(j - 1, alpha)
        visible = (k_row + d * tk) // CHUNK <= q_chunk
        m, alpha = softmax([jnp.where(visible, sh, -jnp.inf) for sh in s], m)

    def body(j, carry):
        pend, alpha, m = carry
        s = scores(j)
        flush(pend, alpha)
        m, alpha = softmax(s, m)
        return j, alpha, m

    pend, alpha, m = lax.fori_loop(0, i * n_diag, body, (i * n_diag + n_diag - 1, alpha, m))
    flush(pend, alpha)
    heads = []
    for hd in range(HEADS):
        ot = acc_scr[hd, 0:HEAD_W, :] / acc_scr[hd, HEAD_W:HEAD_W + 1, :]
        ot = ot[:, :tq] - lam * ot[:, tq:]
        ot = ot * lax.rsqrt(jnp.mean(ot * ot, axis=0, keepdims=True) + SUBLN_EPS)
        heads.append(ot.T * sg_ref[...] * (1.0 - lam_init))
    o = jnp.concatenate(heads, axis=-1)
    o_ref[0] = _merge_out(o, mab_ref[0], gc_ref[0], x_ref[0], wc_ref, wo_ref)


def _attn_prompt(qt, k16, vt, mab, gc, x, layer, p, *, tq, tk):
    B, T, _ = x.shape
    assert T % tq == 0 and tq % tk == 0 and tk % CHUNK == 0 and vt.shape[3] == tk

    def tile_spec(w):
        return pl.BlockSpec((1, tq, w), lambda b, i: (b, i, 0))

    weight_specs, weights = _attn_weight_specs(p, layer)
    return pl.pallas_call(
        functools.partial(_attn_prompt_kernel, tq=tq, tk=tk, lam_init=_lam_init(layer)),
        grid=(B, T // tq),
        in_specs=[pl.BlockSpec((1, ATT_W, tq), lambda b, i: (b, 0, i)),
                  pl.BlockSpec((1, T, ATT_W), lambda b, i: (b, 0, 0)),
                  pl.BlockSpec((1, T // tk, ATT_W, tk), lambda b, i: (b, 0, 0, 0)),
                  tile_spec(D_MODEL), tile_spec(D_MODEL), tile_spec(D_MODEL)] + weight_specs,
        out_specs=tile_spec(D_MODEL),
        out_shape=jax.ShapeDtypeStruct((B, T, D_MODEL), F32),
        scratch_shapes=[pltpu.VMEM((HEADS, tk, 2 * tq), BF16),
                        pltpu.VMEM((HEADS, HEAD_W + DENOM_ROWS, 2 * tq), F32)],
        compiler_params=pltpu.CompilerParams(dimension_semantics=("parallel", "arbitrary"),
                                             vmem_limit_bytes=VMEM_LIMIT),
        name="attn_prompt",
    )(qt, k16, vt, mab, gc, x, *weights)


def _stack_components(qh):
    lane = lax.broadcasted_iota(jnp.int32, qh.shape, 1)
    zero = jnp.zeros_like(qh)
    return jnp.concatenate([jnp.where(lane < HEAD_DIM, qh, zero),
                            jnp.where(lane < HEAD_DIM, zero, qh)], axis=0)


def _softmax_step(carry, s, vj):
    m, l, acc = carry
    m_new = jnp.maximum(m, jnp.max(s, axis=-1, keepdims=True))
    pr = jnp.exp2(s - m_new)
    alpha = jnp.exp2(m - m_new)
    l = alpha * l + jnp.sum(pr, axis=-1, keepdims=True)
    acc = alpha * acc + jnp.dot(pr.astype(BF16), vj, preferred_element_type=F32)
    return m_new, l, acc


def _attn_sample_kernel(q_ref, kp_ref, vp_ref, kn_ref, vn_ref, mab_ref, gc_ref, x_ref, lq1_ref,
                        lk1_ref, lq2_ref, lk2_ref, sg_ref, wc_ref, wo_ref, o_ref, *, past,
                        lam_init):
    n_seq, tq, _ = q_ref.shape
    lam = _lambda(lq1_ref, lk1_ref, lq2_ref, lk2_ref, lam_init)
    q_chunk = (past + lax.broadcasted_iota(jnp.int32, (2 * tq, 1), 0) % tq) // CHUNK
    vis_past = (lax.broadcasted_iota(jnp.int32, (2 * tq, past), 1) // CHUNK) <= q_chunk
    vis_new = ((past + lax.broadcasted_iota(jnp.int32, (2 * tq, tq), 1)) // CHUNK) <= q_chunk
    rows = []
    for sq in range(n_seq):
        heads = []
        for hd in range(HEADS):
            lo, hi = hd * HEAD_W, (hd + 1) * HEAD_W
            qs = _stack_components(q_ref[sq, :, lo:hi])
            carry = (jnp.full((2 * tq, 1), -jnp.inf, F32), jnp.zeros((2 * tq, 1), F32),
                     jnp.zeros((2 * tq, HEAD_W), F32))
            s = lax.dot_general(qs, kn_ref[sq, :, lo:hi], NT_DIMS, preferred_element_type=F32)
            carry = _softmax_step(carry, jnp.where(vis_new, s, -jnp.inf), vn_ref[sq, :, lo:hi])
            kp = kp_ref[sq, pl.ds(hd, past, stride=HEADS), :].astype(BF16)
            vp = vp_ref[sq, pl.ds(hd, past, stride=HEADS), :].astype(BF16)
            s = lax.dot_general(qs, kp, NT_DIMS, preferred_element_type=F32)
            _, l, acc = _softmax_step(carry, jnp.where(vis_past, s, -jnp.inf), vp)
            o = acc / l
            o = o[:tq] - lam * o[tq:]
            heads.append(_rms(o, sg_ref[...], SUBLN_EPS) * (1.0 - lam_init))
        rows.append(jnp.concatenate(heads, axis=-1))
    o = jnp.concatenate(rows, axis=0)
    flat = (n_seq * tq, D_MODEL)
    out = _merge_out(o, mab_ref[...].reshape(flat), gc_ref[...].reshape(flat),
                     x_ref[...].reshape(flat), wc_ref, wo_ref)
    o_ref[...] = out.reshape(n_seq, tq, D_MODEL)


def _attn_sample(q, cache_k, cache_v, k16, v16, mab, gc, x, layer, p, *, seqs):
    B, T, _ = x.shape
    past = cache_k.shape[2] // HEADS
    assert B % seqs == 0

    def tile_spec(w):
        return pl.BlockSpec((seqs, T, w), lambda b: (b, 0, 0))

    past_spec = pl.BlockSpec((None, seqs, past * HEADS, HEAD_W), lambda b: (layer, b, 0, 0))
    weight_specs, weights = _attn_weight_specs(p, layer)
    return pl.pallas_call(
        functools.partial(_attn_sample_kernel, past=past, lam_init=_lam_init(layer)),
        grid=(B // seqs,),
        in_specs=[tile_spec(ATT_W), past_spec, past_spec, tile_spec(ATT_W), tile_spec(ATT_W),
                  tile_spec(D_MODEL), tile_spec(D_MODEL), tile_spec(D_MODEL)] + weight_specs,
        out_specs=tile_spec(D_MODEL),
        out_shape=jax.ShapeDtypeStruct((B, T, D_MODEL), F32),
        compiler_params=pltpu.CompilerParams(dimension_semantics=("parallel",),
                                             vmem_limit_bytes=VMEM_LIMIT),
        name="attn_sample",
    )(q, cache_k, cache_v, k16, v16, mab, gc, x, *weights)


def _ffn_dense_kernel(x_ref, ng_ref, wg_ref, wu_ref, wd_ref, nf_ref, o_ref, *, final):
    bb, tt, _ = x_ref.shape
    x = x_ref[...].reshape(bb * tt, D_MODEL)
    h = _rms(x, ng_ref[...], EPS).astype(BF16)
    y = x
    for lo, hi in _chunks(wg_ref.shape[1], 1024):
        g = jnp.dot(h, wg_ref[:, lo:hi], preferred_element_type=F32)
        u = jnp.dot(h, wu_ref[:, lo:hi], preferred_element_type=F32)
        a = (g * jax.nn.sigmoid(g) * u).astype(BF16)
        y = y + jnp.dot(a, wd_ref[lo:hi, :], preferred_element_type=F32)
    if final:
        y = _rms(y, nf_ref[...], EPS)
    o_ref[...] = y.reshape(bb, tt, D_MODEL)


def _token_blocks(B, T, tm):
    if T >= tm:
        assert T % tm == 0
        return 1, tm
    assert tm % T == 0 and B % (tm // T) == 0
    return tm // T, T


def _ffn_dense(x, layer, p, *, tm, final):
    B, T, _ = x.shape
    bb, tt = _token_blocks(B, T, tm)
    d = layer // 2
    row_spec = pl.BlockSpec((bb, tt, D_MODEL), lambda b, t: (b, t, 0))
    return pl.pallas_call(
        functools.partial(_ffn_dense_kernel, final=final),
        grid=(B // bb, T // tt),
        in_specs=[row_spec, _layer_spec(p['norm_ffn'], layer), _layer_spec(p['w_gate_d'], d),
                  _layer_spec(p['w_up_d'], d), _layer_spec(p['w_down_d'], d),
                  _const_spec(p['norm_final'].shape)],
        out_specs=row_spec,
        out_shape=jax.ShapeDtypeStruct(x.shape, F32),
        compiler_params=pltpu.CompilerParams(dimension_semantics=("parallel", "parallel"),
                                             vmem_limit_bytes=VMEM_LIMIT),
        name="ffn_dense",
    )(x, p['norm_ffn'], p['w_gate_d'], p['w_up_d'], p['w_down_d'], p['norm_final'])


def _route_top2(logits):
    row = lax.broadcasted_iota(jnp.int32, logits.shape, 0).astype(F32)
    big = float(N_EXPERTS)
    m1 = jnp.max(logits, axis=0, keepdims=True)
    i1 = jnp.min(jnp.where(logits == m1, row, big), axis=0, keepdims=True)
    rest = jnp.where(row == i1, -jnp.inf, logits)
    m2 = jnp.max(rest, axis=0, keepdims=True)
    i2 = jnp.min(jnp.where(rest == m2, row, big), axis=0, keepdims=True)
    e2 = jnp.exp(m2 - m1)
    w1 = 1.0 / (1.0 + e2)
    w2 = e2 / (1.0 + e2)
    return jnp.where(row == i1, w1, 0.0) + jnp.where(row == i2, w2, 0.0)


def _ffn_moe_kernel(x_ref, ng_ref, wrt_ref, brt_ref, wg_ref, wu_ref, wd_ref, nf_ref, o_ref,
                    h_scr, wts_scr, rank_scr, *, final, tb, cms):
    e = pl.program_id(2)
    bb, tt, _ = x_ref.shape
    n_tok = bb * tt
    sub_blocks = _chunks(n_tok, tb)

    @pl.when(e == 0)
    def _():
        x = x_ref[...].reshape(n_tok, D_MODEL)
        h = _rms(x, ng_ref[...], EPS)
        logits = lax.dot_general(wrt_ref[...], h, NT_DIMS, preferred_element_type=F32,
                                 precision=lax.Precision.HIGHEST) + brt_ref[...]
        wts = _route_top2(logits)
        sel = (wts > 0.0).astype(BF16)
        before = (lax.broadcasted_iota(jnp.int32, (tb, tb), 0)
                  < lax.broadcasted_iota(jnp.int32, (tb, tb), 1)).astype(BF16)
        rank = jnp.concatenate(
            [jnp.dot(sel[:, lo:hi], before, preferred_element_type=F32) for lo, hi in sub_blocks],
            axis=1)
        for ee in range(N_EXPERTS):
            for sb, (lo, hi) in enumerate(sub_blocks):
                wts_scr[ee, sb] = wts[ee:ee + 1, lo:hi]
                rank_scr[ee, sb] = rank[ee:ee + 1, lo:hi]
        h_scr[...] = h.astype(BF16)
        o_ref[...] = x_ref[...]

    cm_max = cms[-1]

    def sub_block(sb, carry):
        w_row = wts_scr[e, sb]
        r_row = rank_scr[e, sb]
        sel = w_row > 0.0
        n_sel = jnp.sum(sel.astype(F32)).astype(jnp.int32)
        row0 = pl.multiple_of(sb * tb, tb)

        def run_chunk(cm, base):
            rows = (lax.broadcasted_iota(jnp.int32, (cm, tb), 0) + base).astype(F32)
            hit = (r_row == rows) & sel
            gather = jnp.where(hit, 1.0, 0.0).astype(BF16)
            xg = jnp.dot(gather, h_scr[pl.ds(row0, tb), :],
                         preferred_element_type=F32).astype(BF16)
            g = jnp.dot(xg, wg_ref[...], preferred_element_type=F32)
            u = jnp.dot(xg, wu_ref[...], preferred_element_type=F32)
            a = (g * jax.nn.sigmoid(g) * u).astype(BF16)
            y = jnp.dot(a, wd_ref[...], preferred_element_type=F32).astype(BF16)
            scatter = jnp.where(hit, w_row, 0.0).astype(BF16)
            upd = lax.dot_general(scatter, y, TN_DIMS, preferred_element_type=F32)
            if bb == 1:
                o_ref[0, pl.ds(row0, tb), :] += upd
            else:
                o_ref[pl.ds(sb * (tb // tt), tb // tt)] += upd.reshape(tb // tt, tt, D_MODEL)

        def chunk(c, carry):
            left = n_sel - c * cm_max
            size_idx = sum((left > cm).astype(jnp.int32) for cm in cms[:-1])
            lax.switch(size_idx, [functools.partial(run_chunk, cm) for cm in cms], c * cm_max)
            return carry

        n_chunks = sum((n_sel > c * cm_max).astype(jnp.int32) for c in range(-(-tb // cm_max)))
        lax.fori_loop(0, n_chunks, chunk, 0)
        return carry

    lax.fori_loop(0, n_tok // tb, sub_block, 0)

    if final:
        @pl.when(e == N_EXPERTS - 1)
        def _():
            y = o_ref[...].reshape(n_tok, D_MODEL)
            o_ref[...] = _rms(y, nf_ref[...], EPS).reshape(bb, tt, D_MODEL)


def _ffn_moe(x, layer, p, *, tm, tb, cms, final):
    B, T, _ = x.shape
    bb, tt = _token_blocks(B, T, tm)
    assert tm % tb == 0 and (bb == 1 or tb % tt == 0)
    mo = layer // 2
    row_spec = pl.BlockSpec((bb, tt, D_MODEL), lambda b, t, e: (b, t, 0))

    def expert_spec(arr):
        return pl.BlockSpec((None, None) + tuple(arr.shape[2:]), lambda b, t, e: (mo, e, 0, 0))

    return pl.pallas_call(
        functools.partial(_ffn_moe_kernel, final=final, tb=tb, cms=cms),
        grid=(B // bb, T // tt, N_EXPERTS),
        in_specs=[row_spec, _layer_spec(p['norm_ffn'], layer), _layer_spec(p['w_router_t'], mo),
                  _layer_spec(p['b_router_t'], mo), expert_spec(p['w_gate_e']),
                  expert_spec(p['w_up_e']), expert_spec(p['w_down_e']),
                  _const_spec(p['norm_final'].shape)],
        out_specs=row_spec,
        out_shape=jax.ShapeDtypeStruct(x.shape, F32),
        scratch_shapes=[pltpu.VMEM((tm, D_MODEL), BF16),
                        pltpu.VMEM((N_EXPERTS, tm // tb, 1, tb), F32),
                        pltpu.VMEM((N_EXPERTS, tm // tb, 1, tb), F32)],
        compiler_params=pltpu.CompilerParams(
            dimension_semantics=("parallel", "parallel", "arbitrary"),
            vmem_limit_bytes=VMEM_LIMIT),
        name="ffn_moe",
    )(x, p['norm_ffn'], p['w_router_t'], p['b_router_t'], p['w_gate_e'], p['w_up_e'],
      p['w_down_e'], p['norm_final'])


def _rope_tables(pos0, T, bb, tt):
    half = HEAD_DIM // 2
    inv_freq = ROPE_THETA ** (-jnp.arange(half, dtype=F32) / half)
    ang = (pos0 + jnp.arange(T)).astype(F32)[:, None] * inv_freq[None, :]
    cos, sin = jnp.cos(ang), jnp.sin(ang)
    cos = jnp.concatenate([cos, cos, cos, cos], axis=-1)
    sin = jnp.concatenate([-sin, sin, -sin, sin], axis=-1)

    def lay(a):
        a = a.reshape(T // tt, 1, tt, HEAD_W)
        return jnp.broadcast_to(a, (T // tt, bb, tt, HEAD_W)).reshape(T // tt, bb * tt, HEAD_W)

    return lay(cos), lay(sin)


def _pad_hist(state):
    return jnp.pad(state, ((0, 0), (0, 0), (HIST_PAD - state.shape[2], 0), (0, 0)))


_MATMUL_WEIGHTS = ('w_in', 'w_conv_out', 'w_pool', 'w_attn_out', 'w_o', 'w_gate_d', 'w_up_d',
                   'w_down_d', 'w_gate_e', 'w_up_e', 'w_down_e')
_ROW_VECTORS = ('norm_mix', 'pool_scale', 'lambda_q1', 'lambda_k1', 'lambda_q2', 'lambda_k2',
                'subln_g', 'norm_ffn')


def _params(w):
    p = dict(w)
    for n in _MATMUL_WEIGHTS:
        p[n] = w[n].astype(BF16)
    for n in _ROW_VECTORS:
        p[n] = w[n][:, None, :]
    p['w_router_t'] = jnp.swapaxes(w['w_router'], 1, 2)
    p['b_router_t'] = w['b_router'][:, :, None]
    p['norm_final'] = w['norm_final'][None, :]
    return p


def _new_state(co, po):
    return co[:, HIST_PAD - (CONV_K - 1):], po[:, HIST_PAD - POOL_HIST:]


def _token_mixer_prompt(x, layer, p, kv_bufs):
    B, T, _ = x.shape
    tt = min(PROMPT_TILE, T)
    tk = min(KEY_TILE, tt)
    cos, sin = _rope_tables(0, T, 1, tt)
    zero_hist = jnp.zeros((B, HIST_PAD, CONV_W), F32)
    mab, gc, qt, k16, vt, k32, v32, co, po = _mixer_in(
        x, zero_hist, zero_hist, cos, sin, layer, p, kv_bufs, bb=1, tt=tt, pos0=0, tk=tk)
    x = _attn_prompt(qt, k16, vt, mab, gc, x, layer, p, tq=tt, tk=tk)
    return (x,) + _new_state(co, po) + ((k32, v32),)


def _token_mixer_sample(x, cache_k, cache_v, state_conv, state_pool, layer, p, kv_bufs):
    past = cache_k.shape[2]
    B, T, _ = x.shape
    bb = min(SAMPLE_MIXER_SEQS, B)
    cos, sin = _rope_tables(past, T, bb, T)
    mab, gc, q, k16, v16, k32, v32, co, po = _mixer_in(
        x, _pad_hist(state_conv), _pad_hist(state_pool), cos, sin, layer, p, kv_bufs,
        bb=bb, tt=T, pos0=past, tk=None)
    rows = cache_k.shape[:2] + (past * HEADS, HEAD_W)
    x = _attn_sample(q, cache_k.reshape(rows), cache_v.reshape(rows), k16, v16, mab, gc, x, layer,
                     p, seqs=min(SAMPLE_ATTN_SEQS, B))
    return (x,) + _new_state(co, po) + ((k32, v32),)


def _channel_mixer(x, layer, p, *, final):
    tokens = x.shape[0] * x.shape[1]
    if layer % 2 == 0:
        return _ffn_dense(x, layer, p, tm=min(DENSE_TOKENS, tokens), final=final)
    tm = min(MOE_TOKENS, tokens)
    return _ffn_moe(x, layer, p, tm=tm, tb=min(MOE_SUB_TOKENS, tm), cms=MOE_CHUNK_ROWS,
                    final=final)


def kernel(x_prompt, x_sample, cache_k, cache_v, state_conv, state_pool, norm_mix, w_in, conv_w,
           w_conv_out, w_pool, pool_scale, lambda_q1, lambda_k1, lambda_q2, lambda_k2, subln_g,
           w_attn_out, w_o, norm_ffn, w_gate_d, w_up_d, w_down_d, w_router, b_router, w_gate_e,
           w_up_e, w_down_e, norm_final):
    depth = w_in.shape[0]
    p = _params(dict(
        norm_mix=norm_mix, w_in=w_in, conv_w=conv_w, w_conv_out=w_conv_out, w_pool=w_pool,
        pool_scale=pool_scale, lambda_q1=lambda_q1, lambda_k1=lambda_k1, lambda_q2=lambda_q2,
        lambda_k2=lambda_k2, subln_g=subln_g, w_attn_out=w_attn_out, w_o=w_o, norm_ffn=norm_ffn,
        w_gate_d=w_gate_d, w_up_d=w_up_d, w_down_d=w_down_d, w_router=w_router,
        b_router=b_router, w_gate_e=w_gate_e, w_up_e=w_up_e, w_down_e=w_down_e,
        norm_final=norm_final))

    xp, xs = x_prompt, x_sample
    kv_p = kv_s = None
    states = [[] for _ in range(4)]
    for l in range(depth):
        final = l == depth - 1
        xp, cp, pp, kv_p = _token_mixer_prompt(xp, l, p, kv_p)
        xs, cs, ps, kv_s = _token_mixer_sample(xs, cache_k, cache_v, state_conv, state_pool, l, p,
                                               kv_s)
        for lst, a in zip(states, (cp, pp, cs, ps)):
            lst.append(a)
        xp = _channel_mixer(xp, l, p, final=final)
        xs = _channel_mixer(xs, l, p, final=final)

    def heads_view(a):
        return a.reshape(a.shape[0], a.shape[1], a.shape[2] // HEADS, HEADS, HEAD_W)

    return (xp, xs, heads_view(kv_p[0]), heads_view(kv_p[1]), jnp.stack(states[0]),
            jnp.stack(states[1]), heads_view(kv_s[0]), heads_view(kv_s[1]),
            jnp.stack(states[2]), jnp.stack(states[3]))
```

```python
import functools
import math

import jax
import jax.numpy as jnp
from jax import lax
from jax.experimental import pallas as pl
from jax.experimental.pallas import tpu as pltpu

D_MODEL = 1024
CHUNK = 64
CONV_W = 512
CONV_K = 3
POOL_W = 512
POOL_GC = 128
POOL_WINDOWS = (2, 4, 8, 16)
POOL_HIST = 15
HEADS = 4
HEAD_DIM = 64
HEAD_W = 2 * HEAD_DIM
ATT_W = HEADS * HEAD_W
ROPE_THETA = 10000.0
N_EXPERTS = 8
EPS = 1e-6
SUBLN_EPS = 1e-5

C_XA, C_BA, C_CA, C_U, C_Q, C_K, C_V, C_GA, C_GB, C_GC, C_END = (
    0, 512, 1024, 1536, 2048, 2560, 3072, 3584, 4608, 5632, 6656)

HIST_PAD = 16
Q_SCALE = HEAD_DIM ** -0.5 * math.log2(math.e)
DENOM_ROWS = 16

VMEM_LIMIT = 56 * 1024 * 1024
PROMPT_TILE = 512
KEY_TILE = 512
SAMPLE_MIXER_SEQS = 16
SAMPLE_ATTN_SEQS = 4
DENSE_TOKENS = 1024
MOE_TOKENS = 1024
MOE_SUB_TOKENS = 512
MOE_CHUNK_ROWS = tuple(range(128, 257, 16))

F32 = jnp.float32
BF16 = jnp.bfloat16
NT_DIMS = (((1,), (1,)), ((), ()))
TN_DIMS = (((0,), (0,)), ((), ()))


def _const_spec(shape):
    nd = len(shape)
    return pl.BlockSpec(tuple(shape), lambda *_: (0,) * nd, pipeline_mode=pl.Buffered(1))


def _layer_spec(arr, layer):
    nd = arr.ndim
    return pl.BlockSpec((None,) + tuple(arr.shape[1:]), lambda *_: (layer,) + (0,) * (nd - 1),
                        pipeline_mode=pl.Buffered(1))


def _rms(x, g, eps):
    return x * lax.rsqrt(jnp.mean(x * x, axis=-1, keepdims=True) + eps) * g


def _chunks(n, step):
    return [(lo, min(lo + step, n)) for lo in range(0, n, step)]


def _mixer_in_kernel(*refs, bb, tt, pos0, tk, aliased):
    (x_ref, hc_ref, hp_ref, cos_ref, sin_ref, ng_ref, win_ref, cw_ref, wa_ref, wp_ref,
     ps_ref) = refs[:11]
    refs = refs[11 + (2 if aliased else 0):]
    (mab_ref, gc_ref, q_ref, k16_ref, v16_ref, k32_ref, v32_ref, co_ref, po_ref, cbuf,
     ubuf) = refs
    t = pl.program_id(1)
    m = bb * tt
    x = x_ref[...].reshape(m, D_MODEL)
    h = _rms(x, ng_ref[...], EPS).astype(BF16)

    def seg(lo, hi):
        return jnp.dot(h, win_ref[:, lo:hi], preferred_element_type=F32)

    @pl.when(t == 0)
    def _():
        cbuf[:, 0:HIST_PAD, :] = hc_ref[...]
        ubuf[:, 0:HIST_PAD, :] = hp_ref[...]

    z_ca, z_xa, z_ba, z_u = seg(C_CA, C_U), seg(C_XA, C_BA), seg(C_BA, C_CA), seg(C_U, C_Q)
    z_q, z_k, z_v = seg(C_Q, C_K), seg(C_K, C_V), seg(C_V, C_GA)

    cin = (z_ca * z_xa).reshape(bb, tt, CONV_W)
    cbuf[:, HIST_PAD:, :] = cin
    conv = cbuf[:, HIST_PAD - 2:HIST_PAD - 2 + tt, :] * cw_ref[0:1, :]
    conv = conv + cbuf[:, HIST_PAD - 1:HIST_PAD - 1 + tt, :] * cw_ref[1:2, :]
    conv = conv + cin * cw_ref[2:3, :]
    ya = jnp.dot((z_ba * conv.reshape(m, CONV_W)).astype(BF16), wa_ref[...],
                 preferred_element_type=F32)

    ubuf[:, HIST_PAD:, :] = z_u.reshape(bb, tt, POOL_W)
    pos = pos0 + t * tt + lax.broadcasted_iota(jnp.int32, (bb, tt, POOL_GC), 1)
    yb_parts = []
    for gi, win in enumerate(POOL_WINDOWS):
        lo, hi = gi * POOL_GC, (gi + 1) * POOL_GC
        cur = ubuf[:, HIST_PAD:HIST_PAD + tt, lo:hi]
        s = cur
        for k in range(1, win):
            s = s + ubuf[:, HIST_PAD - k:HIST_PAD - k + tt, lo:hi]
        cnt = jnp.minimum(pos + 1, win).astype(F32)
        pooled = s / cnt - cur
        yb_parts.append(jnp.dot(pooled.reshape(m, POOL_GC).astype(BF16), wp_ref[gi],
                                preferred_element_type=F32))
    yb = jnp.concatenate(yb_parts, axis=-1) * ps_ref[...]

    cos4 = jnp.concatenate([cos_ref[...]] * HEADS, axis=-1)
    sin4 = jnp.concatenate([sin_ref[...]] * HEADS, axis=-1)
    lane = lax.broadcasted_iota(jnp.int32, (m, ATT_W), 1)
    first_half = (lane & (HEAD_DIM // 2)) == 0

    def rope(z):
        swapped = jnp.where(first_half, pltpu.roll(z, ATT_W - HEAD_DIM // 2, axis=1),
                            pltpu.roll(z, HEAD_DIM // 2, axis=1))
        return z * cos4 + swapped * sin4

    q = rope(z_q) * Q_SCALE
    k = rope(z_k)
    v = z_v
    for hd in range(HEADS):
        lo, hi = hd * HEAD_W, (hd + 1) * HEAD_W
        k32_ref[:, pl.ds(hd, tt, stride=HEADS), :] = k[:, lo:hi].reshape(bb, tt, HEAD_W)
        v32_ref[:, pl.ds(hd, tt, stride=HEADS), :] = v[:, lo:hi].reshape(bb, tt, HEAD_W)
    k16_ref[...] = k.astype(BF16).reshape(bb, tt, ATT_W)
    if tk is None:
        q_ref[...] = q.astype(BF16).reshape(bb, tt, ATT_W)
        v16_ref[...] = v.astype(BF16).reshape(bb, tt, ATT_W)
    else:
        q_ref[0] = q.T.astype(BF16)
        for c in range(tt // tk):
            v16_ref[0, c] = v[c * tk:(c + 1) * tk, :].T.astype(BF16)

    ga = jax.nn.sigmoid(seg(C_GA, C_GB))
    gb = jax.nn.sigmoid(seg(C_GB, C_GC))
    mab_ref[...] = (ga * ya + gb * yb).astype(BF16).reshape(bb, tt, D_MODEL)
    gc_ref[...] = jax.nn.sigmoid(seg(C_GC, C_END)).astype(BF16).reshape(bb, tt, D_MODEL)

    last_c = cbuf[:, tt:tt + HIST_PAD, :]
    last_p = ubuf[:, tt:tt + HIST_PAD, :]
    co_ref[...] = last_c
    po_ref[...] = last_p
    cbuf[:, 0:HIST_PAD, :] = last_c
    ubuf[:, 0:HIST_PAD, :] = last_p


def _mixer_in(x, hist_c, hist_p, cos, sin, layer, p, kv_bufs, *, bb, tt, pos0, tk):
    B, T, _ = x.shape
    depth = p['w_in'].shape[0]
    assert B % bb == 0 and T % tt == 0 and tt >= HIST_PAD and tt % 8 == 0
    assert tk is None or (bb == 1 and tt % tk == 0)
    grid = (B // bb, T // tt)
    m = bb * tt

    def seq_spec(w):
        return pl.BlockSpec((bb, tt, w), lambda b, t: (b, t, 0))

    if hist_c.ndim == 4:
        hist_spec = pl.BlockSpec((None, bb, HIST_PAD, CONV_W), lambda b, t: (layer, b, 0, 0))
    else:
        hist_spec = pl.BlockSpec((bb, HIST_PAD, CONV_W), lambda b, t: (b, 0, 0))
    state_spec = pl.BlockSpec((bb, HIST_PAD, CONV_W), lambda b, t: (b, 0, 0))
    rope_spec = pl.BlockSpec((None, m, HEAD_W), lambda b, t: (t, 0, 0))
    kv_spec = pl.BlockSpec((None, bb, tt * HEADS, HEAD_W), lambda b, t: (layer, b, t, 0))
    kv_shape = jax.ShapeDtypeStruct((depth, B, T * HEADS, HEAD_W), F32)
    if tk is None:
        q_shape, q_spec = jax.ShapeDtypeStruct((B, T, ATT_W), BF16), seq_spec(ATT_W)
        v_shape, v_spec = q_shape, q_spec
    else:
        q_shape = jax.ShapeDtypeStruct((B, ATT_W, T), BF16)
        q_spec = pl.BlockSpec((1, ATT_W, tt), lambda b, t: (b, 0, t))
        v_shape = jax.ShapeDtypeStruct((B, T // tk, ATT_W, tk), BF16)
        v_spec = pl.BlockSpec((1, tt // tk, ATT_W, tk), lambda b, t: (b, t, 0, 0))

    out_shape = (
        jax.ShapeDtypeStruct((B, T, D_MODEL), BF16),
        jax.ShapeDtypeStruct((B, T, D_MODEL), BF16),
        q_shape,
        jax.ShapeDtypeStruct((B, T, ATT_W), BF16),
        v_shape,
        kv_shape, kv_shape,
        jax.ShapeDtypeStruct((B, HIST_PAD, CONV_W), F32),
        jax.ShapeDtypeStruct((B, HIST_PAD, POOL_W), F32),
    )
    out_specs = (seq_spec(D_MODEL), seq_spec(D_MODEL), q_spec, seq_spec(ATT_W), v_spec,
                 kv_spec, kv_spec, state_spec, state_spec)
    in_specs = [seq_spec(D_MODEL), hist_spec, hist_spec, rope_spec, rope_spec,
                _layer_spec(p['norm_mix'], layer), _layer_spec(p['w_in'], layer),
                _layer_spec(p['conv_w'], layer), _layer_spec(p['w_conv_out'], layer),
                _layer_spec(p['w_pool'], layer), _layer_spec(p['pool_scale'], layer)]
    args = [x, hist_c, hist_p, cos, sin, p['norm_mix'], p['w_in'], p['conv_w'], p['w_conv_out'],
            p['w_pool'], p['pool_scale']]
    aliases = {}
    if kv_bufs is not None:
        in_specs += [pl.BlockSpec(memory_space=pl.ANY)] * 2
        aliases = {len(args): 5, len(args) + 1: 6}
        args += list(kv_bufs)
    return pl.pallas_call(
        functools.partial(_mixer_in_kernel, bb=bb, tt=tt, pos0=pos0, tk=tk,
                          aliased=kv_bufs is not None),
        grid=grid, in_specs=in_specs, out_specs=out_specs, out_shape=out_shape,
        scratch_shapes=[pltpu.VMEM((bb, HIST_PAD + tt, CONV_W), F32),
                        pltpu.VMEM((bb, HIST_PAD + tt, POOL_W), F32)],
        input_output_aliases=aliases,
        compiler_params=pltpu.CompilerParams(dimension_semantics=("parallel", "arbitrary"),
                                             vmem_limit_bytes=VMEM_LIMIT),
        name="mixer_in",
    )(*args)


def _lambda(lq1_ref, lk1_ref, lq2_ref, lk2_ref, lam_init):
    a = jnp.exp(jnp.sum(lq1_ref[...] * lk1_ref[...], axis=-1, keepdims=True))
    b = jnp.exp(jnp.sum(lq2_ref[...] * lk2_ref[...], axis=-1, keepdims=True))
    return a - b + lam_init


def _lam_init(layer):
    return 0.8 - 0.6 * math.exp(-0.3 * layer)


def _merge_out(o, mab, gc, x, wc_ref, wo_ref):
    yc = jnp.dot(o.astype(BF16), wc_ref[...], preferred_element_type=F32)
    merged = mab.astype(F32) + gc.astype(F32) * yc
    return x + jnp.dot(merged.astype(BF16), wo_ref[...], preferred_element_type=F32)


def _attn_weight_specs(p, layer):
    names = ('lambda_q1', 'lambda_k1', 'lambda_q2', 'lambda_k2', 'subln_g', 'w_attn_out', 'w_o')
    return [_layer_spec(p[n], layer) for n in names], [p[n] for n in names]


def _attn_prompt_kernel(qt_ref, k_ref, vt_ref, mab_ref, gc_ref, x_ref, lq1_ref, lk1_ref, lq2_ref,
                        lk2_ref, sg_ref, wc_ref, wo_ref, o_ref, p_scr, acc_scr, *, tq, tk,
                        lam_init):
    i = pl.program_id(1)
    n_diag = tq // tk
    lam = _lambda(lq1_ref, lk1_ref, lq2_ref, lk2_ref, lam_init)
    feat = lax.broadcasted_iota(jnp.int32, (HEAD_W, tq), 0)
    qs = []
    for hd in range(HEADS):
        qt = qt_ref[0, hd * HEAD_W:(hd + 1) * HEAD_W, :]
        zero = jnp.zeros_like(qt)
        qs.append(jnp.concatenate([jnp.where(feat < HEAD_DIM, qt, zero),
                                   jnp.where(feat < HEAD_DIM, zero, qt)], axis=1))

    def scores(j):
        off = pl.multiple_of(j * tk, tk)
        return [jnp.dot(k_ref[0, pl.ds(off, tk), hd * HEAD_W:(hd + 1) * HEAD_W], qs[hd],
                        preferred_element_type=F32) for hd in range(HEADS)]

    def softmax(s, m):
        m_out, alpha_out = [], []
        for hd in range(HEADS):
            m_new = jnp.maximum(m[hd], jnp.max(s[hd], axis=0, keepdims=True))
            p_scr[hd] = jnp.exp2(s[hd] - m_new).astype(BF16)
            m_out.append(m_new)
            alpha_out.append(jnp.exp2(m[hd] - m_new))
        return m_out, alpha_out

    ones_rows = jnp.ones((DENOM_ROWS, tk), BF16)

    def flush(j, alpha):
        for hd in range(HEADS):
            vt1 = jnp.concatenate([vt_ref[0, j, hd * HEAD_W:(hd + 1) * HEAD_W, :], ones_rows],
                                  axis=0)
            acc_scr[hd] = alpha[hd] * acc_scr[hd] + jnp.dot(vt1, p_scr[hd],
                                                            preferred_element_type=F32)

    acc_scr[...] = jnp.zeros_like(acc_scr)
    m = [jnp.full((1, 2 * tq), -jnp.inf, F32)] * HEADS
    alpha = [jnp.zeros((1, 2 * tq), F32)] * HEADS
    q_chunk = (lax.broadcasted_iota(jnp.int32, (tk, 2 * tq), 1) % tq) // CHUNK
    k_row = lax.broadcasted_iota(jnp.int32, (tk, 2 * tq), 0)
    for d in range(n_diag):
        j = i * n_diag + d
        s = scores(j)
        if d > 0:
            flush(j - 1, alpha)
        visible = (k_row + d * tk) // CHUNK <= q_chunk
        m, alpha = softmax([jnp.where(visible, sh, -jnp.inf) for sh in s], m)

    def body(j, carry):
        pend, alpha, m = carry
        s = scores(j)
        flush(pend, alpha)
        m, alpha = softmax(s, m)
        return j, alpha, m

    pend, alpha, m = lax.fori_loop(0, i * n_diag, body, (i * n_diag + n_diag - 1, alpha, m))
    flush(pend, alpha)
    heads = []
    for hd in range(HEADS):
        ot = acc_scr[hd, 0:HEAD_W, :] / acc_scr[hd, HEAD_W:HEAD_W + 1, :]
        ot = ot[:, :tq] - lam * ot[:, tq:]
        ot = ot * lax.rsqrt(jnp.mean(ot * ot, axis=0, keepdims=True) + SUBLN_EPS)
        heads.append(ot.T * sg_ref[...] * (1.0 - lam_init))
    o = jnp.concatenate(heads, axis=-1)
    o_ref[0] = _merge_out(o, mab_ref[0], gc_ref[0], x_ref[0], wc_ref, wo_ref)


def _attn_prompt(qt, k16, vt, mab, gc, x, layer, p, *, tq, tk):
    B, T, _ = x.shape
    assert T % tq == 0 and tq % tk == 0 and tk % CHUNK == 0 and vt.shape[3] == tk

    def tile_spec(w):
        return pl.BlockSpec((1, tq, w), lambda b, i: (b, i, 0))

    weight_specs, weights = _attn_weight_specs(p, layer)
    return pl.pallas_call(
        functools.partial(_attn_prompt_kernel, tq=tq, tk=tk, lam_init=_lam_init(layer)),
        grid=(B, T // tq),
        in_specs=[pl.BlockSpec((1, ATT_W, tq), lambda b, i: (b, 0, i)),
                  pl.BlockSpec((1, T, ATT_W), lambda b, i: (b, 0, 0)),
                  pl.BlockSpec((1, T // tk, ATT_W, tk), lambda b, i: (b, 0, 0, 0)),
                  tile_spec(D_MODEL), tile_spec(D_MODEL), tile_spec(D_MODEL)] + weight_specs,
        out_specs=tile_spec(D_MODEL),
        out_shape=jax.ShapeDtypeStruct((B, T, D_MODEL), F32),
        scratch_shapes=[pltpu.VMEM((HEADS, tk, 2 * tq), BF16),
                        pltpu.VMEM((HEADS, HEAD_W + DENOM_ROWS, 2 * tq), F32)],
        compiler_params=pltpu.CompilerParams(dimension_semantics=("parallel", "arbitrary"),
                                             vmem_limit_bytes=VMEM_LIMIT),
        name="attn_prompt",
    )(qt, k16, vt, mab, gc, x, *weights)


def _stack_components(qh):
    lane = lax.broadcasted_iota(jnp.int32, qh.shape, 1)
    zero = jnp.zeros_like(qh)
    return jnp.concatenate([jnp.where(lane < HEAD_DIM, qh, zero),
                            jnp.where(lane < HEAD_DIM, zero, qh)], axis=0)


def _softmax_step(carry, s, vj):
    m, l, acc = carry
    m_new = jnp.maximum(m, jnp.max(s, axis=-1, keepdims=True))
    pr = jnp.exp2(s - m_new)
    alpha = jnp.exp2(m - m_new)
    l = alpha * l + jnp.sum(pr, axis=-1, keepdims=True)
    acc = alpha * acc + jnp.dot(pr.astype(BF16), vj, preferred_element_type=F32)
    return m_new, l, acc


def _attn_sample_kernel(q_ref, kp_ref, vp_ref, kn_ref, vn_ref, mab_ref, gc_ref, x_ref, lq1_ref,
                        lk1_ref, lq2_ref, lk2_ref, sg_ref, wc_ref, wo_ref, o_ref, *, past,
                        lam_init):
    n_seq, tq, _ = q_ref.shape
    lam = _lambda(lq1_ref, lk1_ref, lq2_ref, lk2_ref, lam_init)
    q_chunk = (past + lax.broadcasted_iota(jnp.int32, (2 * tq, 1), 0) % tq) // CHUNK
    vis_past = (lax.broadcasted_iota(jnp.int32, (2 * tq, past), 1) // CHUNK) <= q_chunk
    vis_new = ((past + lax.broadcasted_iota(jnp.int32, (2 * tq, tq), 1)) // CHUNK) <= q_chunk
    rows = []
    for sq in range(n_seq):
        heads = []
        for hd in range(HEADS):
            lo, hi = hd * HEAD_W, (hd + 1) * HEAD_W
            qs = _stack_components(q_ref[sq, :, lo:hi])
            carry = (jnp.full((2 * tq, 1), -jnp.inf, F32), jnp.zeros((2 * tq, 1), F32),
                     jnp.zeros((2 * tq, HEAD_W), F32))
            s = lax.dot_general(qs, kn_ref[sq, :, lo:hi], NT_DIMS, preferred_element_type=F32)
            carry = _softmax_step(carry, jnp.where(vis_new, s, -jnp.inf), vn_ref[sq, :, lo:hi])
            kp = kp_ref[sq, pl.ds(hd, past, stride=HEADS), :].astype(BF16)
            vp = vp_ref[sq, pl.ds(hd, past, stride=HEADS), :].astype(BF16)
            s = lax.dot_general(qs, kp, NT_DIMS, preferred_element_type=F32)
            _, l, acc = _softmax_step(carry, jnp.where(vis_past, s, -jnp.inf), vp)
            o = acc / l
            o = o[:tq] - lam * o[tq:]
            heads.append(_rms(o, sg_ref[...], SUBLN_EPS) * (1.0 - lam_init))
        rows.append(jnp.concatenate(heads, axis=-1))
    o = jnp.concatenate(rows, axis=0)
    flat = (n_seq * tq, D_MODEL)
    out = _merge_out(o, mab_ref[...].reshape(flat), gc_ref[...].reshape(flat),
                     x_ref[...].reshape(flat), wc_ref, wo_ref)
    o_ref[...] = out.reshape(n_seq, tq, D_MODEL)


def _attn_sample(q, cache_k, cache_v, k16, v16, mab, gc, x, layer, p, *, seqs):
    B, T, _ = x.shape
    past = cache_k.shape[2] // HEADS
    assert B % seqs == 0

    def tile_spec(w):
        return pl.BlockSpec((seqs, T, w), lambda b: (b, 0, 0))

    past_spec = pl.BlockSpec((None, seqs, past * HEADS, HEAD_W), lambda b: (layer, b, 0, 0))
    weight_specs, weights = _attn_weight_specs(p, layer)
    return pl.pallas_call(
        functools.partial(_attn_sample_kernel, past=past, lam_init=_lam_init(layer)),
        grid=(B // seqs,),
        in_specs=[tile_spec(ATT_W), past_spec, past_spec, tile_spec(ATT_W), tile_spec(ATT_W),
                  tile_spec(D_MODEL), tile_spec(D_MODEL), tile_spec(D_MODEL)] + weight_specs,
        out_specs=tile_spec(D_MODEL),
        out_shape=jax.ShapeDtypeStruct((B, T, D_MODEL), F32),
        compiler_params=pltpu.CompilerParams(dimension_semantics=("parallel",),
                                             vmem_limit_bytes=VMEM_LIMIT),
        name="attn_sample",
    )(q, cache_k, cache_v, k16, v16, mab, gc, x, *weights)


def _ffn_dense_kernel(x_ref, ng_ref, wg_ref, wu_ref, wd_ref, nf_ref, o_ref, *, final):
    bb, tt, _ = x_ref.shape
    x = x_ref[...].reshape(bb * tt, D_MODEL)
    h = _rms(x, ng_ref[...], EPS).astype(BF16)
    y = x
    for lo, hi in _chunks(wg_ref.shape[1], 1024):
        g = jnp.dot(h, wg_ref[:, lo:hi], preferred_element_type=F32)
        u = jnp.dot(h, wu_ref[:, lo:hi], preferred_element_type=F32)
        a = (g * jax.nn.sigmoid(g) * u).astype(BF16)
        y = y + jnp.dot(a, wd_ref[lo:hi, :], preferred_element_type=F32)
    if final:
        y = _rms(y, nf_ref[...], EPS)
    o_ref[...] = y.reshape(bb, tt, D_MODEL)


def _token_blocks(B, T, tm):
    if T >= tm:
        assert T % tm == 0
        return 1, tm
    assert tm % T == 0 and B % (tm // T) == 0
    return tm // T, T


def _ffn_dense(x, layer, p, *, tm, final):
    B, T, _ = x.shape
    bb, tt = _token_blocks(B, T, tm)
    d = layer // 2
    row_spec = pl.BlockSpec((bb, tt, D_MODEL), lambda b, t: (b, t, 0))
    return pl.pallas_call(
        functools.partial(_ffn_dense_kernel, final=final),
        grid=(B // bb, T // tt),
        in_specs=[row_spec, _layer_spec(p['norm_ffn'], layer), _layer_spec(p['w_gate_d'], d),
                  _layer_spec(p['w_up_d'], d), _layer_spec(p['w_down_d'], d),
                  _const_spec(p['norm_final'].shape)],
        out_specs=row_spec,
        out_shape=jax.ShapeDtypeStruct(x.shape, F32),
        compiler_params=pltpu.CompilerParams(dimension_semantics=("parallel", "parallel"),
                                             vmem_limit_bytes=VMEM_LIMIT),
        name="ffn_dense",
    )(x, p['norm_ffn'], p['w_gate_d'], p['w_up_d'], p['w_down_d'], p['norm_final'])


def _route_top2(logits):
    row = lax.broadcasted_iota(jnp.int32, logits.shape, 0).astype(F32)
    big = float(N_EXPERTS)
    m1 = jnp.max(logits, axis=0, keepdims=True)
    i1 = jnp.min(jnp.where(logits == m1, row, big), axis=0, keepdims=True)
    rest = jnp.where(row == i1, -jnp.inf, logits)
    m2 = jnp.max(rest, axis=0, keepdims=True)
    i2 = jnp.min(jnp.where(rest == m2, row, big), axis=0, keepdims=True)
    e2 = jnp.exp(m2 - m1)
    w1 = 1.0 / (1.0 + e2)
    w2 = e2 / (1.0 + e2)
    return jnp.where(row == i1, w1, 0.0) + jnp.where(row == i2, w2, 0.0)


def _ffn_moe_kernel(x_ref, ng_ref, wrt_ref, brt_ref, wg_ref, wu_ref, wd_ref, nf_ref, o_ref,
                    h_scr, wts_scr, rank_scr, *, final, tb, cms):
    e = pl.program_id(2)
    bb, tt, _ = x_ref.shape
    n_tok = bb * tt
    sub_blocks = _chunks(n_tok, tb)

    @pl.when(e == 0)
    def _():
        x = x_ref[...].reshape(n_tok, D_MODEL)
        h = _rms(x, ng_ref[...], EPS)
        def split(a):
            hi = a.astype(BF16)
            return hi, (a - hi.astype(F32)).astype(BF16)

        (w_hi, w_lo), (h_hi, h_lo) = split(wrt_ref[...]), split(h)
        logits = (lax.dot_general(w_hi, h_hi, NT_DIMS, preferred_element_type=F32)
                  + lax.dot_general(w_lo, h_hi, NT_DIMS, preferred_element_type=F32)
                  + lax.dot_general(w_hi, h_lo, NT_DIMS, preferred_element_type=F32)
                  + brt_ref[...])
        wts = _route_top2(logits)
        sel = (wts > 0.0).astype(BF16)
        before = (lax.broadcasted_iota(jnp.int32, (tb, tb), 0)
                  < lax.broadcasted_iota(jnp.int32, (tb, tb), 1)).astype(BF16)
        rank = jnp.concatenate(
            [jnp.dot(sel[:, lo:hi], before, preferred_element_type=F32) for lo, hi in sub_blocks],
            axis=1)
        for ee in range(N_EXPERTS):
            for sb, (lo, hi) in enumerate(sub_blocks):
                wts_scr[ee, sb] = wts[ee:ee + 1, lo:hi]
                rank_scr[ee, sb] = rank[ee:ee + 1, lo:hi]
        h_scr[...] = h_hi
        o_ref[...] = x_ref[...]

    cm_max = cms[-1]

    def sub_block(sb, carry):
        w_row = wts_scr[e, sb]
        r_row = rank_scr[e, sb]
        sel = w_row > 0.0
        n_sel = jnp.sum(sel.astype(F32)).astype(jnp.int32)
        row0 = pl.multiple_of(sb * tb, tb)

        def run_chunk(cm, base):
            rows = (lax.broadcasted_iota(jnp.int32, (cm, tb), 0) + base).astype(F32)
            hit = (r_row == rows) & sel
            gather = jnp.where(hit, 1.0, 0.0).astype(BF16)
            xg = jnp.dot(gather, h_scr[pl.ds(row0, tb), :],
                         preferred_element_type=F32).astype(BF16)
            g = jnp.dot(xg, wg_ref[...], preferred_element_type=F32)
            u = jnp.dot(xg, wu_ref[...], preferred_element_type=F32)
            a = (g * jax.nn.sigmoid(g) * u).astype(BF16)
            y = jnp.dot(a, wd_ref[...], preferred_element_type=F32).astype(BF16)
            scatter = jnp.where(hit, w_row, 0.0).astype(BF16)
            upd = lax.dot_general(scatter, y, TN_DIMS, preferred_element_type=F32)
            if bb == 1:
                o_ref[0, pl.ds(row0, tb), :] += upd
            else:
                o_ref[pl.ds(sb * (tb // tt), tb // tt)] += upd.reshape(tb // tt, tt, D_MODEL)

        def chunk(c, carry):
            left = n_sel - c * cm_max
            size_idx = sum((left > cm).astype(jnp.int32) for cm in cms[:-1])
            lax.switch(size_idx, [functools.partial(run_chunk, cm) for cm in cms], c * cm_max)
            return carry

        n_chunks = sum((n_sel > c * cm_max).astype(jnp.int32) for c in range(-(-tb // cm_max)))
        lax.fori_loop(0, n_chunks, chunk, 0)
        return carry

    lax.fori_loop(0, n_tok // tb, sub_block, 0)

    if final:
        @pl.when(e == N_EXPERTS - 1)
        def _():
            y = o_ref[...].reshape(n_tok, D_MODEL)
            o_ref[...] = _rms(y, nf_ref[...], EPS).reshape(bb, tt, D_MODEL)


def _ffn_moe(x, layer, p, *, tm, tb, cms, final):
    B, T, _ = x.shape
    bb, tt = _token_blocks(B, T, tm)
    assert tm % tb == 0 and (bb == 1 or tb % tt == 0)
    mo = layer // 2
    row_spec = pl.BlockSpec((bb, tt, D_MODEL), lambda b, t, e: (b, t, 0))

    def expert_spec(arr):
        return pl.BlockSpec((None, None) + tuple(arr.shape[2:]), lambda b, t, e: (mo, e, 0, 0))

    return pl.pallas_call(
        functools.partial(_ffn_moe_kernel, final=final, tb=tb, cms=cms),
        grid=(B // bb, T // tt, N_EXPERTS),
        in_specs=[row_spec, _layer_spec(p['norm_ffn'], layer), _layer_spec(p['w_router_t'], mo),
                  _layer_spec(p['b_router_t'], mo), expert_spec(p['w_gate_e']),
                  expert_spec(p['w_up_e']), expert_spec(p['w_down_e']),
                  _const_spec(p['norm_final'].shape)],
        out_specs=row_spec,
        out_shape=jax.ShapeDtypeStruct(x.shape, F32),
        scratch_shapes=[pltpu.VMEM((tm, D_MODEL), BF16),
                        pltpu.VMEM((N_EXPERTS, tm // tb, 1, tb), F32),
                        pltpu.VMEM((N_EXPERTS, tm // tb, 1, tb), F32)],
        compiler_params=pltpu.CompilerParams(
            dimension_semantics=("parallel", "parallel", "arbitrary"),
            vmem_limit_bytes=VMEM_LIMIT),
        name="ffn_moe",
    )(x, p['norm_ffn'], p['w_router_t'], p['b_router_t'], p['w_gate_e'], p['w_up_e'],
      p['w_down_e'], p['norm_final'])


def _rope_tables(pos0, T, bb, tt):
    half = HEAD_DIM // 2
    inv_freq = ROPE_THETA ** (-jnp.arange(half, dtype=F32) / half)
    ang = (pos0 + jnp.arange(T)).astype(F32)[:, None] * inv_freq[None, :]
    cos, sin = jnp.cos(ang), jnp.sin(ang)
    cos = jnp.concatenate([cos, cos, cos, cos], axis=-1)
    sin = jnp.concatenate([-sin, sin, -sin, sin], axis=-1)

    def lay(a):
        a = a.reshape(T // tt, 1, tt, HEAD_W)
        return jnp.broadcast_to(a, (T // tt, bb, tt, HEAD_W)).reshape(T // tt, bb * tt, HEAD_W)

    return lay(cos), lay(sin)


def _pad_hist(state):
    return jnp.pad(state, ((0, 0), (0, 0), (HIST_PAD - state.shape[2], 0), (0, 0)))


_MATMUL_WEIGHTS = ('w_in', 'w_conv_out', 'w_pool', 'w_attn_out', 'w_o', 'w_gate_d', 'w_up_d',
                   'w_down_d', 'w_gate_e', 'w_up_e', 'w_down_e')
_ROW_VECTORS = ('norm_mix', 'pool_scale', 'lambda_q1', 'lambda_k1', 'lambda_q2', 'lambda_k2',
                'subln_g', 'norm_ffn')


def _params(w):
    p = dict(w)
    for n in _MATMUL_WEIGHTS:
        p[n] = w[n].astype(BF16)
    for n in _ROW_VECTORS:
        p[n] = w[n][:, None, :]
    p['w_router_t'] = jnp.swapaxes(w['w_router'], 1, 2)
    p['b_router_t'] = w['b_router'][:, :, None]
    p['norm_final'] = w['norm_final'][None, :]
    return p


def _new_state(co, po):
    return co[:, HIST_PAD - (CONV_K - 1):], po[:, HIST_PAD - POOL_HIST:]


def _token_mixer_prompt(x, layer, p, kv_bufs):
    B, T, _ = x.shape
    tt = min(PROMPT_TILE, T)
    tk = min(KEY_TILE, tt)
    cos, sin = _rope_tables(0, T, 1, tt)
    zero_hist = jnp.zeros((B, HIST_PAD, CONV_W), F32)
    mab, gc, qt, k16, vt, k32, v32, co, po = _mixer_in(
        x, zero_hist, zero_hist, cos, sin, layer, p, kv_bufs, bb=1, tt=tt, pos0=0, tk=tk)
    x = _attn_prompt(qt, k16, vt, mab, gc, x, layer, p, tq=tt, tk=tk)
    return (x,) + _new_state(co, po) + ((k32, v32),)


def _token_mixer_sample(x, cache_k, cache_v, state_conv, state_pool, layer, p, kv_bufs):
    past = cache_k.shape[2]
    B, T, _ = x.shape
    bb = min(SAMPLE_MIXER_SEQS, B)
    cos, sin = _rope_tables(past, T, bb, T)
    mab, gc, q, k16, v16, k32, v32, co, po = _mixer_in(
        x, _pad_hist(state_conv), _pad_hist(state_pool), cos, sin, layer, p, kv_bufs,
        bb=bb, tt=T, pos0=past, tk=None)
    rows = cache_k.shape[:2] + (past * HEADS, HEAD_W)
    x = _attn_sample(q, cache_k.reshape(rows), cache_v.reshape(rows), k16, v16, mab, gc, x, layer,
                     p, seqs=min(SAMPLE_ATTN_SEQS, B))
    return (x,) + _new_state(co, po) + ((k32, v32),)


def _channel_mixer(x, layer, p, *, final):
    tokens = x.shape[0] * x.shape[1]
    if layer % 2 == 0:
        return _ffn_dense(x, layer, p, tm=min(DENSE_TOKENS, tokens), final=final)
    tm = min(MOE_TOKENS, tokens)
    return _ffn_moe(x, layer, p, tm=tm, tb=min(MOE_SUB_TOKENS, tm), cms=MOE_CHUNK_ROWS,
                    final=final)


def kernel(x_prompt, x_sample, cache_k, cache_v, state_conv, state_pool, norm_mix, w_in, conv_w,
           w_conv_out, w_pool, pool_scale, lambda_q1, lambda_k1, lambda_q2, lambda_k2, subln_g,
           w_attn_out, w_o, norm_ffn, w_gate_d, w_up_d, w_down_d, w_router, b_router, w_gate_e,
           w_up_e, w_down_e, norm_final):
    depth = w_in.shape[0]
    p = _params(dict(
        norm_mix=norm_mix, w_in=w_in, conv_w=conv_w, w_conv_out=w_conv_out, w_pool=w_pool,
        pool_scale=pool_scale, lambda_q1=lambda_q1, lambda_k1=lambda_k1, lambda_q2=lambda_q2,
        lambda_k2=lambda_k2, subln_g=subln_g, w_attn_out=w_attn_out, w_o=w_o, norm_ffn=norm_ffn,
        w_gate_d=w_gate_d, w_up_d=w_up_d, w_down_d=w_down_d, w_router=w_router,
        b_router=b_router, w_gate_e=w_gate_e, w_up_e=w_up_e, w_down_e=w_down_e,
        norm_final=norm_final))

    xp, xs = x_prompt, x_sample
    kv_p = kv_s = None
    states = [[] for _ in range(4)]
    for l in range(depth):
        final = l == depth - 1
        xp, cp, pp, kv_p = _token_mixer_prompt(xp, l, p, kv_p)
        xs, cs, ps, kv_s = _token_mixer_sample(xs, cache_k, cache_v, state_conv, state_pool, l, p,
                                               kv_s)
        for lst, a in zip(states, (cp, pp, cs, ps)):
            lst.append(a)
        xp = _channel_mixer(xp, l, p, final=final)
        xs = _channel_mixer(xs, l, p, final=final)

    def heads_view(a):
        return a.reshape(a.shape[0], a.shape[1], a.shape[2] // HEADS, HEADS, HEAD_W)

    return (xp, xs, heads_view(kv_p[0]), heads_view(kv_p[1]), jnp.stack(states[0]),
            jnp.stack(states[1]), heads_view(kv_s[0]), heads_view(kv_s[1]),
            jnp.stack(states[2]), jnp.stack(states[3]))
```

```python
import functools
import math

import jax
import jax.numpy as jnp
from jax import lax
from jax.experimental import pallas as pl
from jax.experimental.pallas import tpu as pltpu

D_MODEL = 1024
CHUNK = 64
CONV_W = 512
CONV_K = 3
POOL_W = 512
POOL_GC = 128
POOL_WINDOWS = (2, 4, 8, 16)
POOL_HIST = 15
HEADS = 4
HEAD_DIM = 64
HEAD_W = 2 * HEAD_DIM
ATT_W = HEADS * HEAD_W
ROPE_THETA = 10000.0
N_EXPERTS = 8
EPS = 1e-6
SUBLN_EPS = 1e-5

C_XA, C_BA, C_CA, C_U, C_Q, C_K, C_V, C_GA, C_GB, C_GC, C_END = (
    0, 512, 1024, 1536, 2048, 2560, 3072, 3584, 4608, 5632, 6656)

HIST_PAD = 16
Q_SCALE = HEAD_DIM ** -0.5 * math.log2(math.e)
DENOM_ROWS = 16

VMEM_LIMIT = 56 * 1024 * 1024
PROMPT_TILE = 512
KEY_TILE = 512
SAMPLE_MIXER_SEQS = 16
SAMPLE_ATTN_SEQS = 4
DENSE_TOKENS = 1024
MOE_TOKENS = 1024
MOE_SUB_TOKENS = 512
MOE_CHUNK_ROWS = tuple(range(128, 257, 16))

F32 = jnp.float32
BF16 = jnp.bfloat16
NT_DIMS = (((1,), (1,)), ((), ()))
TN_DIMS = (((0,), (0,)), ((), ()))


def _const_spec(shape):
    nd = len(shape)
    return pl.BlockSpec(tuple(shape), lambda *_: (0,) * nd, pipeline_mode=pl.Buffered(1))


def _layer_spec(arr, layer):
    nd = arr.ndim
    return pl.BlockSpec((None,) + tuple(arr.shape[1:]), lambda *_: (layer,) + (0,) * (nd - 1),
                        pipeline_mode=pl.Buffered(1))


def _rms(x, g, eps):
    return x * lax.rsqrt(jnp.mean(x * x, axis=-1, keepdims=True) + eps) * g


def _chunks(n, step):
    return [(lo, min(lo + step, n)) for lo in range(0, n, step)]


def _mixer_in_kernel(*refs, bb, tt, pos0, tk, aliased):
    (x_ref, hc_ref, hp_ref, cos_ref, sin_ref, ng_ref, win_ref, cw_ref, wa_ref, wp_ref,
     ps_ref) = refs[:11]
    refs = refs[11 + (2 if aliased else 0):]
    (mab_ref, gc_ref, q_ref, k16_ref, v16_ref, k32_ref, v32_ref, co_ref, po_ref, cbuf,
     ubuf) = refs
    t = pl.program_id(1)
    m = bb * tt
    x = x_ref[...].reshape(m, D_MODEL)
    halves = [_rms(x[r0:r0 + m // 2], ng_ref[...], EPS).astype(BF16) for r0 in (0, m // 2)]

    def seg(lo, hi):
        return jnp.concatenate([jnp.dot(h, win_ref[:, lo:hi], preferred_element_type=F32)
                                for h in halves], axis=0)

    @pl.when(t == 0)
    def _():
        cbuf[:, 0:HIST_PAD, :] = hc_ref[...]
        ubuf[:, 0:HIST_PAD, :] = hp_ref[...]

    z_ca, z_xa, z_ba, z_u = seg(C_CA, C_U), seg(C_XA, C_BA), seg(C_BA, C_CA), seg(C_U, C_Q)
    z_q, z_k, z_v = seg(C_Q, C_K), seg(C_K, C_V), seg(C_V, C_GA)

    cin = (z_ca * z_xa).reshape(bb, tt, CONV_W)
    cbuf[:, HIST_PAD:, :] = cin
    conv = cbuf[:, HIST_PAD - 2:HIST_PAD - 2 + tt, :] * cw_ref[0:1, :]
    conv = conv + cbuf[:, HIST_PAD - 1:HIST_PAD - 1 + tt, :] * cw_ref[1:2, :]
    conv = conv + cin * cw_ref[2:3, :]
    ya = jnp.dot((z_ba * conv.reshape(m, CONV_W)).astype(BF16), wa_ref[...],
                 preferred_element_type=F32)

    ubuf[:, HIST_PAD:, :] = z_u.reshape(bb, tt, POOL_W)
    pos = pos0 + t * tt + lax.broadcasted_iota(jnp.int32, (bb, tt, POOL_GC), 1)
    yb_parts = []
    for gi, win in enumerate(POOL_WINDOWS):
        lo, hi = gi * POOL_GC, (gi + 1) * POOL_GC
        cur = ubuf[:, HIST_PAD:HIST_PAD + tt, lo:hi]
        s = cur
        for k in range(1, win):
            s = s + ubuf[:, HIST_PAD - k:HIST_PAD - k + tt, lo:hi]
        cnt = jnp.minimum(pos + 1, win).astype(F32)
        pooled = s / cnt - cur
        yb_parts.append(jnp.dot(pooled.reshape(m, POOL_GC).astype(BF16), wp_ref[gi],
                                preferred_element_type=F32))
    yb = jnp.concatenate(yb_parts, axis=-1) * ps_ref[...]

    cos4 = jnp.concatenate([cos_ref[...]] * HEADS, axis=-1)
    sin4 = jnp.concatenate([sin_ref[...]] * HEADS, axis=-1)
    lane = lax.broadcasted_iota(jnp.int32, (m, ATT_W), 1)
    first_half = (lane & (HEAD_DIM // 2)) == 0

    def rope(z):
        swapped = jnp.where(first_half, pltpu.roll(z, ATT_W - HEAD_DIM // 2, axis=1),
                            pltpu.roll(z, HEAD_DIM // 2, axis=1))
        return z * cos4 + swapped * sin4

    q = rope(z_q) * Q_SCALE
    k = rope(z_k)
    v = z_v
    for hd in range(HEADS):
        lo, hi = hd * HEAD_W, (hd + 1) * HEAD_W
        k32_ref[:, pl.ds(hd, tt, stride=HEADS), :] = k[:, lo:hi].reshape(bb, tt, HEAD_W)
        v32_ref[:, pl.ds(hd, tt, stride=HEADS), :] = v[:, lo:hi].reshape(bb, tt, HEAD_W)
    k16_ref[...] = k.astype(BF16).reshape(bb, tt, ATT_W)
    if tk is None:
        q_ref[...] = q.astype(BF16).reshape(bb, tt, ATT_W)
        v16_ref[...] = v.astype(BF16).reshape(bb, tt, ATT_W)
    else:
        q_ref[0] = q.T.astype(BF16)
        for c in range(tt // tk):
            v16_ref[0, c] = v[c * tk:(c + 1) * tk, :].T.astype(BF16)

    ga = jax.nn.sigmoid(seg(C_GA, C_GB))
    gb = jax.nn.sigmoid(seg(C_GB, C_GC))
    mab_ref[...] = (ga * ya + gb * yb).astype(BF16).reshape(bb, tt, D_MODEL)
    gc_ref[...] = jax.nn.sigmoid(seg(C_GC, C_END)).astype(BF16).reshape(bb, tt, D_MODEL)

    last_c = cbuf[:, tt:tt + HIST_PAD, :]
    last_p = ubuf[:, tt:tt + HIST_PAD, :]
    co_ref[...] = last_c
    po_ref[...] = last_p
    cbuf[:, 0:HIST_PAD, :] = last_c
    ubuf[:, 0:HIST_PAD, :] = last_p


def _mixer_in(x, hist_c, hist_p, cos, sin, layer, p, kv_bufs, *, bb, tt, pos0, tk):
    B, T, _ = x.shape
    depth = p['w_in'].shape[0]
    assert B % bb == 0 and T % tt == 0 and tt >= HIST_PAD and tt % 8 == 0
    assert tk is None or (bb == 1 and tt % tk == 0)
    grid = (B // bb, T // tt)
    m = bb * tt

    def seq_spec(w):
        return pl.BlockSpec((bb, tt, w), lambda b, t: (b, t, 0))

    if hist_c.ndim == 4:
        hist_spec = pl.BlockSpec((None, bb, HIST_PAD, CONV_W), lambda b, t: (layer, b, 0, 0))
    else:
        hist_spec = pl.BlockSpec((bb, HIST_PAD, CONV_W), lambda b, t: (b, 0, 0))
    state_spec = pl.BlockSpec((bb, HIST_PAD, CONV_W), lambda b, t: (b, 0, 0))
    rope_spec = pl.BlockSpec((None, m, HEAD_W), lambda b, t: (t, 0, 0))
    kv_spec = pl.BlockSpec((None, bb, tt * HEADS, HEAD_W), lambda b, t: (layer, b, t, 0))
    kv_shape = jax.ShapeDtypeStruct((depth, B, T * HEADS, HEAD_W), F32)
    if tk is None:
        q_shape, q_spec = jax.ShapeDtypeStruct((B, T, ATT_W), BF16), seq_spec(ATT_W)
        v_shape, v_spec = q_shape, q_spec
    else:
        q_shape = jax.ShapeDtypeStruct((B, ATT_W, T), BF16)
        q_spec = pl.BlockSpec((1, ATT_W, tt), lambda b, t: (b, 0, t))
        v_shape = jax.ShapeDtypeStruct((B, T // tk, ATT_W, tk), BF16)
        v_spec = pl.BlockSpec((1, tt // tk, ATT_W, tk), lambda b, t: (b, t, 0, 0))

    out_shape = (
        jax.ShapeDtypeStruct((B, T, D_MODEL), BF16),
        jax.ShapeDtypeStruct((B, T, D_MODEL), BF16),
        q_shape,
        jax.ShapeDtypeStruct((B, T, ATT_W), BF16),
        v_shape,
        kv_shape, kv_shape,
        jax.ShapeDtypeStruct((B, HIST_PAD, CONV_W), F32),
        jax.ShapeDtypeStruct((B, HIST_PAD, POOL_W), F32),
    )
    out_specs = (seq_spec(D_MODEL), seq_spec(D_MODEL), q_spec, seq_spec(ATT_W), v_spec,
                 kv_spec, kv_spec, state_spec, state_spec)
    in_specs = [seq_spec(D_MODEL), hist_spec, hist_spec, rope_spec, rope_spec,
                _layer_spec(p['norm_mix'], layer), _layer_spec(p['w_in'], layer),
                _layer_spec(p['conv_w'], layer), _layer_spec(p['w_conv_out'], layer),
                _layer_spec(p['w_pool'], layer), _layer_spec(p['pool_scale'], layer)]
    args = [x, hist_c, hist_p, cos, sin, p['norm_mix'], p['w_in'], p['conv_w'], p['w_conv_out'],
            p['w_pool'], p['pool_scale']]
    aliases = {}
    if kv_bufs is not None:
        in_specs += [pl.BlockSpec(memory_space=pl.ANY)] * 2
        aliases = {len(args): 5, len(args) + 1: 6}
        args += list(kv_bufs)
    return pl.pallas_call(
        functools.partial(_mixer_in_kernel, bb=bb, tt=tt, pos0=pos0, tk=tk,
                          aliased=kv_bufs is not None),
        grid=grid, in_specs=in_specs, out_specs=out_specs, out_shape=out_shape,
        scratch_shapes=[pltpu.VMEM((bb, HIST_PAD + tt, CONV_W), F32),
                        pltpu.VMEM((bb, HIST_PAD + tt, POOL_W), F32)],
        input_output_aliases=aliases,
        compiler_params=pltpu.CompilerParams(dimension_semantics=("parallel", "arbitrary"),
                                             vmem_limit_bytes=VMEM_LIMIT),
        name="mixer_in",
    )(*args)


def _lambda(lq1_ref, lk1_ref, lq2_ref, lk2_ref, lam_init):
    a = jnp.exp(jnp.sum(lq1_ref[...] * lk1_ref[...], axis=-1, keepdims=True))
    b = jnp.exp(jnp.sum(lq2_ref[...] * lk2_ref[...], axis=-1, keepdims=True))
    return a - b + lam_init


def _lam_init(layer):
    return 0.8 - 0.6 * math.exp(-0.3 * layer)


def _merge_out(o, mab, gc, x, wc_ref, wo_ref):
    yc = jnp.dot(o.astype(BF16), wc_ref[...], preferred_element_type=F32)
    merged = mab.astype(F32) + gc.astype(F32) * yc
    return x + jnp.dot(merged.astype(BF16), wo_ref[...], preferred_element_type=F32)


def _attn_weight_specs(p, layer):
    names = ('lambda_q1', 'lambda_k1', 'lambda_q2', 'lambda_k2', 'subln_g', 'w_attn_out', 'w_o')
    return [_layer_spec(p[n], layer) for n in names], [p[n] for n in names]


def _attn_prompt_kernel(qt_ref, k_ref, vt_ref, mab_ref, gc_ref, x_ref, lq1_ref, lk1_ref, lq2_ref,
                        lk2_ref, sg_ref, wc_ref, wo_ref, o_ref, p_scr, acc_scr, *, tq, tk,
                        lam_init):
    i = pl.program_id(1)
    n_diag = tq // tk
    lam = _lambda(lq1_ref, lk1_ref, lq2_ref, lk2_ref, lam_init)
    feat = lax.broadcasted_iota(jnp.int32, (HEAD_W, tq), 0)
    qs = []
    for hd in range(HEADS):
        qt = qt_ref[0, hd * HEAD_W:(hd + 1) * HEAD_W, :]
        zero = jnp.zeros_like(qt)
        qs.append(jnp.concatenate([jnp.where(feat < HEAD_DIM, qt, zero),
                                   jnp.where(feat < HEAD_DIM, zero, qt)], axis=1))

    def scores(j):
        off = pl.multiple_of(j * tk, tk)
        return [jnp.dot(k_ref[0, pl.ds(off, tk), hd * HEAD_W:(hd + 1) * HEAD_W], qs[hd],
                        preferred_element_type=F32) for hd in range(HEADS)]

    def softmax(s, m):
        m_out, alpha_out = [], []
        for hd in range(HEADS):
            m_new = jnp.maximum(m[hd], jnp.max(s[hd], axis=0, keepdims=True))
            p_scr[hd] = jnp.exp2(s[hd] - m_new).astype(BF16)
            m_out.append(m_new)
            alpha_out.append(jnp.exp2(m[hd] - m_new))
        return m_out, alpha_out

    ones_rows = jnp.ones((DENOM_ROWS, tk), BF16)

    def flush(j, alpha):
        for hd in range(HEADS):
            vt1 = jnp.concatenate([vt_ref[0, j, hd * HEAD_W:(hd + 1) * HEAD_W, :], ones_rows],
                                  axis=0)
            acc_scr[hd] = alpha[hd] * acc_scr[hd] + jnp.dot(vt1, p_scr[hd],
                                                            preferred_element_type=F32)

    acc_scr[...] = jnp.zeros_like(acc_scr)
    m = [jnp.full((1, 2 * tq), -jnp.inf, F32)] * HEADS
    alpha = [jnp.zeros((1, 2 * tq), F32)] * HEADS
    q_chunk = (lax.broadcasted_iota(jnp.int32, (tk, 2 * tq), 1) % tq) // CHUNK
    k_row = lax.broadcasted_iota(jnp.int32, (tk, 2 * tq), 0)
    for d in range(n_diag):
        j = i * n_diag + d
        s = scores(j)
        if d > 0:
            flush(j - 1, alpha)
        visible = (k_row + d * tk) // CHUNK <= q_chunk
        m, alpha = softmax([jnp.where(visible, sh, -jnp.inf) for sh in s], m)

    def body(j, carry):
        pend, alpha, m = carry
        s = scores(j)
        flush(pend, alpha)
        m, alpha = softmax(s, m)
        return j, alpha, m

    pend, alpha, m = lax.fori_loop(0, i * n_diag, body, (i * n_diag + n_diag - 1, alpha, m))
    flush(pend, alpha)
    heads = []
    for hd in range(HEADS):
        ot = acc_scr[hd, 0:HEAD_W, :] / acc_scr[hd, HEAD_W:HEAD_W + 1, :]
        ot = ot[:, :tq] - lam * ot[:, tq:]
        ot = ot * lax.rsqrt(jnp.mean(ot * ot, axis=0, keepdims=True) + SUBLN_EPS)
        heads.append(ot.T * sg_ref[...] * (1.0 - lam_init))
    o = jnp.concatenate(heads, axis=-1)
    o_ref[0] = _merge_out(o, mab_ref[0], gc_ref[0], x_ref[0], wc_ref, wo_ref)


def _attn_prompt(qt, k16, vt, mab, gc, x, layer, p, *, tq, tk):
    B, T, _ = x.shape
    assert T % tq == 0 and tq % tk == 0 and tk % CHUNK == 0 and vt.shape[3] == tk

    def tile_spec(w):
        return pl.BlockSpec((1, tq, w), lambda b, i: (b, i, 0))

    weight_specs, weights = _attn_weight_specs(p, layer)
    return pl.pallas_call(
        functools.partial(_attn_prompt_kernel, tq=tq, tk=tk, lam_init=_lam_init(layer)),
        grid=(B, T // tq),
        in_specs=[pl.BlockSpec((1, ATT_W, tq), lambda b, i: (b, 0, i)),
                  pl.BlockSpec((1, T, ATT_W), lambda b, i: (b, 0, 0)),
                  pl.BlockSpec((1, T // tk, ATT_W, tk), lambda b, i: (b, 0, 0, 0)),
                  tile_spec(D_MODEL), tile_spec(D_MODEL), tile_spec(D_MODEL)] + weight_specs,
        out_specs=tile_spec(D_MODEL),
        out_shape=jax.ShapeDtypeStruct((B, T, D_MODEL), F32),
        scratch_shapes=[pltpu.VMEM((HEADS, tk, 2 * tq), BF16),
                        pltpu.VMEM((HEADS, HEAD_W + DENOM_ROWS, 2 * tq), F32)],
        compiler_params=pltpu.CompilerParams(dimension_semantics=("parallel", "arbitrary"),
                                             vmem_limit_bytes=VMEM_LIMIT),
        name="attn_prompt",
    )(qt, k16, vt, mab, gc, x, *weights)


def _stack_components(qh):
    lane = lax.broadcasted_iota(jnp.int32, qh.shape, 1)
    zero = jnp.zeros_like(qh)
    return jnp.concatenate([jnp.where(lane < HEAD_DIM, qh, zero),
                            jnp.where(lane < HEAD_DIM, zero, qh)], axis=0)


def _softmax_step(carry, s, vj):
    m, l, acc = carry
    m_new = jnp.maximum(m, jnp.max(s, axis=-1, keepdims=True))
    pr = jnp.exp2(s - m_new)
    alpha = jnp.exp2(m - m_new)
    l = alpha * l + jnp.sum(pr, axis=-1, keepdims=True)
    acc = alpha * acc + jnp.dot(pr.astype(BF16), vj, preferred_element_type=F32)
    return m_new, l, acc


def _attn_sample_kernel(q_ref, kp_ref, vp_ref, kn_ref, vn_ref, mab_ref, gc_ref, x_ref, lq1_ref,
                        lk1_ref, lq2_ref, lk2_ref, sg_ref, wc_ref, wo_ref, o_ref, *, past,
                        lam_init):
    n_seq, tq, _ = q_ref.shape
    lam = _lambda(lq1_ref, lk1_ref, lq2_ref, lk2_ref, lam_init)
    q_chunk = (past + lax.broadcasted_iota(jnp.int32, (2 * tq, 1), 0) % tq) // CHUNK
    vis_past = (lax.broadcasted_iota(jnp.int32, (2 * tq, past), 1) // CHUNK) <= q_chunk
    vis_new = ((past + lax.broadcasted_iota(jnp.int32, (2 * tq, tq), 1)) // CHUNK) <= q_chunk
    rows = []
    for sq in range(n_seq):
        heads = []
        for hd in range(HEADS):
            lo, hi = hd * HEAD_W, (hd + 1) * HEAD_W
            qs = _stack_components(q_ref[sq, :, lo:hi])
            carry = (jnp.full((2 * tq, 1), -jnp.inf, F32), jnp.zeros((2 * tq, 1), F32),
                     jnp.zeros((2 * tq, HEAD_W), F32))
            s = lax.dot_general(qs, kn_ref[sq, :, lo:hi], NT_DIMS, preferred_element_type=F32)
            carry = _softmax_step(carry, jnp.where(vis_new, s, -jnp.inf), vn_ref[sq, :, lo:hi])
            kp = kp_ref[sq, pl.ds(hd, past, stride=HEADS), :].astype(BF16)
            vp = vp_ref[sq, pl.ds(hd, past, stride=HEADS), :].astype(BF16)
            s = lax.dot_general(qs, kp, NT_DIMS, preferred_element_type=F32)
            _, l, acc = _softmax_step(carry, jnp.where(vis_past, s, -jnp.inf), vp)
            o = acc / l
            o = o[:tq] - lam * o[tq:]
            heads.append(_rms(o, sg_ref[...], SUBLN_EPS) * (1.0 - lam_init))
        rows.append(jnp.concatenate(heads, axis=-1))
    o = jnp.concatenate(rows, axis=0)
    flat = (n_seq * tq, D_MODEL)
    out = _merge_out(o, mab_ref[...].reshape(flat), gc_ref[...].reshape(flat),
                     x_ref[...].reshape(flat), wc_ref, wo_ref)
    o_ref[...] = out.reshape(n_seq, tq, D_MODEL)


def _attn_sample(q, cache_k, cache_v, k16, v16, mab, gc, x, layer, p, *, seqs):
    B, T, _ = x.shape
    past = cache_k.shape[2] // HEADS
    assert B % seqs == 0

    def tile_spec(w):
        return pl.BlockSpec((seqs, T, w), lambda b: (b, 0, 0))

    past_spec = pl.BlockSpec((None, seqs, past * HEADS, HEAD_W), lambda b: (layer, b, 0, 0))
    weight_specs, weights = _attn_weight_specs(p, layer)
    return pl.pallas_call(
        functools.partial(_attn_sample_kernel, past=past, lam_init=_lam_init(layer)),
        grid=(B // seqs,),
        in_specs=[tile_spec(ATT_W), past_spec, past_spec, tile_spec(ATT_W), tile_spec(ATT_W),
                  tile_spec(D_MODEL), tile_spec(D_MODEL), tile_spec(D_MODEL)] + weight_specs,
        out_specs=tile_spec(D_MODEL),
        out_shape=jax.ShapeDtypeStruct((B, T, D_MODEL), F32),
        compiler_params=pltpu.CompilerParams(dimension_semantics=("parallel",),
                                             vmem_limit_bytes=VMEM_LIMIT),
        name="attn_sample",
    )(q, cache_k, cache_v, k16, v16, mab, gc, x, *weights)


def _ffn_dense_kernel(x_ref, ng_ref, wg_ref, wu_ref, wd_ref, nf_ref, o_ref, *, final):
    bb, tt, _ = x_ref.shape
    x = x_ref[...].reshape(bb * tt, D_MODEL)
    h = _rms(x, ng_ref[...], EPS).astype(BF16)
    y = x
    for lo, hi in _chunks(wg_ref.shape[1], 1024):
        g = jnp.dot(h, wg_ref[:, lo:hi], preferred_element_type=F32)
        u = jnp.dot(h, wu_ref[:, lo:hi], preferred_element_type=F32)
        a = (g * jax.nn.sigmoid(g) * u).astype(BF16)
        y = y + jnp.dot(a, wd_ref[lo:hi, :], preferred_element_type=F32)
    if final:
        y = _rms(y, nf_ref[...], EPS)
    o_ref[...] = y.reshape(bb, tt, D_MODEL)


def _token_blocks(B, T, tm):
    if T >= tm:
        assert T % tm == 0
        return 1, tm
    assert tm % T == 0 and B % (tm // T) == 0
    return tm // T, T


def _ffn_dense(x, layer, p, *, tm, final):
    B, T, _ = x.shape
    bb, tt = _token_blocks(B, T, tm)
    d = layer // 2
    row_spec = pl.BlockSpec((bb, tt, D_MODEL), lambda b, t: (b, t, 0))
    return pl.pallas_call(
        functools.partial(_ffn_dense_kernel, final=final),
        grid=(B // bb, T // tt),
        in_specs=[row_spec, _layer_spec(p['norm_ffn'], layer), _layer_spec(p['w_gate_d'], d),
                  _layer_spec(p['w_up_d'], d), _layer_spec(p['w_down_d'], d),
                  _const_spec(p['norm_final'].shape)],
        out_specs=row_spec,
        out_shape=jax.ShapeDtypeStruct(x.shape, F32),
        compiler_params=pltpu.CompilerParams(dimension_semantics=("parallel", "parallel"),
                                             vmem_limit_bytes=VMEM_LIMIT),
        name="ffn_dense",
    )(x, p['norm_ffn'], p['w_gate_d'], p['w_up_d'], p['w_down_d'], p['norm_final'])


def _route_top2(logits):
    row = lax.broadcasted_iota(jnp.int32, logits.shape, 0).astype(F32)
    big = float(N_EXPERTS)
    m1 = jnp.max(logits, axis=0, keepdims=True)
    i1 = jnp.min(jnp.where(logits == m1, row, big), axis=0, keepdims=True)
    rest = jnp.where(row == i1, -jnp.inf, logits)
    m2 = jnp.max(rest, axis=0, keepdims=True)
    i2 = jnp.min(jnp.where(rest == m2, row, big), axis=0, keepdims=True)
    e2 = jnp.exp(m2 - m1)
    w1 = 1.0 / (1.0 + e2)
    w2 = e2 / (1.0 + e2)
    return jnp.where(row == i1, w1, 0.0) + jnp.where(row == i2, w2, 0.0)


def _ffn_moe_kernel(x_ref, ng_ref, wrt_ref, brt_ref, wg_ref, wu_ref, wd_ref, nf_ref, o_ref,
                    h_scr, wts_scr, rank_scr, *, final, tb, cms):
    e = pl.program_id(2)
    bb, tt, _ = x_ref.shape
    n_tok = bb * tt
    sub_blocks = _chunks(n_tok, tb)

    @pl.when(e == 0)
    def _():
        x = x_ref[...].reshape(n_tok, D_MODEL)
        h = _rms(x, ng_ref[...], EPS)
        def split(a):
            hi = a.astype(BF16)
            return hi, (a - hi.astype(F32)).astype(BF16)

        (w_hi, w_lo), (h_hi, h_lo) = split(wrt_ref[...]), split(h)
        logits = (lax.dot_general(w_hi, h_hi, NT_DIMS, preferred_element_type=F32)
                  + lax.dot_general(w_lo, h_hi, NT_DIMS, preferred_element_type=F32)
                  + lax.dot_general(w_hi, h_lo, NT_DIMS, preferred_element_type=F32)
                  + brt_ref[...])
        wts = _route_top2(logits)
        sel = (wts > 0.0).astype(BF16)
        before = (lax.broadcasted_iota(jnp.int32, (tb, tb), 0)
                  < lax.broadcasted_iota(jnp.int32, (tb, tb), 1)).astype(BF16)
        rank = jnp.concatenate(
            [jnp.dot(sel[:, lo:hi], before, preferred_element_type=F32) for lo, hi in sub_blocks],
            axis=1)
        for ee in range(N_EXPERTS):
            for sb, (lo, hi) in enumerate(sub_blocks):
                wts_scr[ee, sb] = wts[ee:ee + 1, lo:hi]
                rank_scr[ee, sb] = rank[ee:ee + 1, lo:hi]
        h_scr[...] = h_hi
        o_ref[...] = x_ref[...]

    cm_max = cms[-1]

    def sub_block(sb, carry):
        w_row = wts_scr[e, sb]
        r_row = rank_scr[e, sb]
        sel = w_row > 0.0
        n_sel = jnp.sum(sel.astype(F32)).astype(jnp.int32)
        row0 = pl.multiple_of(sb * tb, tb)

        def run_chunk(cm, base):
            rows = (lax.broadcasted_iota(jnp.int32, (cm, tb), 0) + base).astype(F32)
            hit = (r_row == rows) & sel
            gather = jnp.where(hit, 1.0, 0.0).astype(BF16)
            xg = jnp.dot(gather, h_scr[pl.ds(row0, tb), :],
                         preferred_element_type=F32).astype(BF16)
            g = jnp.dot(xg, wg_ref[...], preferred_element_type=F32)
            u = jnp.dot(xg, wu_ref[...], preferred_element_type=F32)
            a = (g * jax.nn.sigmoid(g) * u).astype(BF16)
            y = jnp.dot(a, wd_ref[...], preferred_element_type=F32).astype(BF16)
            scatter = jnp.where(hit, w_row, 0.0).astype(BF16)
            upd = lax.dot_general(scatter, y, TN_DIMS, preferred_element_type=F32)
            if bb == 1:
                o_ref[0, pl.ds(row0, tb), :] += upd
            else:
                o_ref[pl.ds(sb * (tb // tt), tb // tt)] += upd.reshape(tb // tt, tt, D_MODEL)

        def chunk(c, carry):
            left = n_sel - c * cm_max
            size_idx = sum((left > cm).astype(jnp.int32) for cm in cms[:-1])
            lax.switch(size_idx, [functools.partial(run_chunk, cm) for cm in cms], c * cm_max)
            return carry

        n_chunks = sum((n_sel > c * cm_max).astype(jnp.int32) for c in range(-(-tb // cm_max)))
        lax.fori_loop(0, n_chunks, chunk, 0)
        return carry

    lax.fori_loop(0, n_tok // tb, sub_block, 0)

    if final:
        @pl.when(e == N_EXPERTS - 1)
        def _():
            y = o_ref[...].reshape(n_tok, D_MODEL)
            o_ref[...] = _rms(y, nf_ref[...], EPS).reshape(bb, tt, D_MODEL)


def _ffn_moe(x, layer, p, *, tm, tb, cms, final):
    B, T, _ = x.shape
    bb, tt = _token_blocks(B, T, tm)
    assert tm % tb == 0 and (bb == 1 or tb % tt == 0)
    mo = layer // 2
    row_spec = pl.BlockSpec((bb, tt, D_MODEL), lambda b, t, e: (b, t, 0))

    def expert_spec(arr):
        return pl.BlockSpec((None, None) + tuple(arr.shape[2:]), lambda b, t, e: (mo, e, 0, 0))

    return pl.pallas_call(
        functools.partial(_ffn_moe_kernel, final=final, tb=tb, cms=cms),
        grid=(B // bb, T // tt, N_EXPERTS),
        in_specs=[row_spec, _layer_spec(p['norm_ffn'], layer), _layer_spec(p['w_router_t'], mo),
                  _layer_spec(p['b_router_t'], mo), expert_spec(p['w_gate_e']),
                  expert_spec(p['w_up_e']), expert_spec(p['w_down_e']),
                  _const_spec(p['norm_final'].shape)],
        out_specs=row_spec,
        out_shape=jax.ShapeDtypeStruct(x.shape, F32),
        scratch_shapes=[pltpu.VMEM((tm, D_MODEL), BF16),
                        pltpu.VMEM((N_EXPERTS, tm // tb, 1, tb), F32),
                        pltpu.VMEM((N_EXPERTS, tm // tb, 1, tb), F32)],
        compiler_params=pltpu.CompilerParams(
            dimension_semantics=("parallel", "parallel", "arbitrary"),
            vmem_limit_bytes=VMEM_LIMIT),
        name="ffn_moe",
    )(x, p['norm_ffn'], p['w_router_t'], p['b_router_t'], p['w_gate_e'], p['w_up_e'],
      p['w_down_e'], p['norm_final'])


def _rope_tables(pos0, T, bb, tt):
    half = HEAD_DIM // 2
    inv_freq = ROPE_THETA ** (-jnp.arange(half, dtype=F32) / half)
    ang = (pos0 + jnp.arange(T)).astype(F32)[:, None] * inv_freq[None, :]
    cos, sin = jnp.cos(ang), jnp.sin(ang)
    cos = jnp.concatenate([cos, cos, cos, cos], axis=-1)
    sin = jnp.concatenate([-sin, sin, -sin, sin], axis=-1)

    def lay(a):
        a = a.reshape(T // tt, 1, tt, HEAD_W)
        return jnp.broadcast_to(a, (T // tt, bb, tt, HEAD_W)).reshape(T // tt, bb * tt, HEAD_W)

    return lay(cos), lay(sin)


def _pad_hist(state):
    return jnp.pad(state, ((0, 0), (0, 0), (HIST_PAD - state.shape[2], 0), (0, 0)))


_MATMUL_WEIGHTS = ('w_in', 'w_conv_out', 'w_pool', 'w_attn_out', 'w_o', 'w_gate_d', 'w_up_d',
                   'w_down_d', 'w_gate_e', 'w_up_e', 'w_down_e')
_ROW_VECTORS = ('norm_mix', 'pool_scale', 'lambda_q1', 'lambda_k1', 'lambda_q2', 'lambda_k2',
                'subln_g', 'norm_ffn')


def _params(w):
    p = dict(w)
    for n in _MATMUL_WEIGHTS:
        p[n] = w[n].astype(BF16)
    for n in _ROW_VECTORS:
        p[n] = w[n][:, None, :]
    p['w_router_t'] = jnp.swapaxes(w['w_router'], 1, 2)
    p['b_router_t'] = w['b_router'][:, :, None]
    p['norm_final'] = w['norm_final'][None, :]
    return p


def _new_state(co, po):
    return co[:, HIST_PAD - (CONV_K - 1):], po[:, HIST_PAD - POOL_HIST:]


def _token_mixer_prompt(x, layer, p, kv_bufs):
    B, T, _ = x.shape
    tt = min(PROMPT_TILE, T)
    tk = min(KEY_TILE, tt)
    cos, sin = _rope_tables(0, T, 1, tt)
    zero_hist = jnp.zeros((B, HIST_PAD, CONV_W), F32)
    mab, gc, qt, k16, vt, k32, v32, co, po = _mixer_in(
        x, zero_hist, zero_hist, cos, sin, layer, p, kv_bufs, bb=1, tt=tt, pos0=0, tk=tk)
    x = _attn_prompt(qt, k16, vt, mab, gc, x, layer, p, tq=tt, tk=tk)
    return (x,) + _new_state(co, po) + ((k32, v32),)


def _token_mixer_sample(x, cache_k, cache_v, state_conv, state_pool, layer, p, kv_bufs):
    past = cache_k.shape[2]
    B, T, _ = x.shape
    bb = min(SAMPLE_MIXER_SEQS, B)
    cos, sin = _rope_tables(past, T, bb, T)
    mab, gc, q, k16, v16, k32, v32, co, po = _mixer_in(
        x, _pad_hist(state_conv), _pad_hist(state_pool), cos, sin, layer, p, kv_bufs,
        bb=bb, tt=T, pos0=past, tk=None)
    rows = cache_k.shape[:2] + (past * HEADS, HEAD_W)
    x = _attn_sample(q, cache_k.reshape(rows), cache_v.reshape(rows), k16, v16, mab, gc, x, layer,
                     p, seqs=min(SAMPLE_ATTN_SEQS, B))
    return (x,) + _new_state(co, po) + ((k32, v32),)


def _channel_mixer(x, layer, p, *, final):
    tokens = x.shape[0] * x.shape[1]
    if layer % 2 == 0:
        return _ffn_dense(x, layer, p, tm=min(DENSE_TOKENS, tokens), final=final)
    tm = min(MOE_TOKENS, tokens)
    return _ffn_moe(x, layer, p, tm=tm, tb=min(MOE_SUB_TOKENS, tm), cms=MOE_CHUNK_ROWS,
                    final=final)


def kernel(x_prompt, x_sample, cache_k, cache_v, state_conv, state_pool, norm_mix, w_in, conv_w,
           w_conv_out, w_pool, pool_scale, lambda_q1, lambda_k1, lambda_q2, lambda_k2, subln_g,
           w_attn_out, w_o, norm_ffn, w_gate_d, w_up_d, w_down_d, w_router, b_router, w_gate_e,
           w_up_e, w_down_e, norm_final):
    depth = w_in.shape[0]
    p = _params(dict(
        norm_mix=norm_mix, w_in=w_in, conv_w=conv_w, w_conv_out=w_conv_out, w_pool=w_pool,
        pool_scale=pool_scale, lambda_q1=lambda_q1, lambda_k1=lambda_k1, lambda_q2=lambda_q2,
        lambda_k2=lambda_k2, subln_g=subln_g, w_attn_out=w_attn_out, w_o=w_o, norm_ffn=norm_ffn,
        w_gate_d=w_gate_d, w_up_d=w_up_d, w_down_d=w_down_d, w_router=w_router,
        b_router=b_router, w_gate_e=w_gate_e, w_up_e=w_up_e, w_down_e=w_down_e,
        norm_final=norm_final))

    xp, xs = x_prompt, x_sample
    kv_p = kv_s = None
    states = [[] for _ in range(4)]
    for l in range(depth):
        final = l == depth - 1
        xp, cp, pp, kv_p = _token_mixer_prompt(xp, l, p, kv_p)
        xs, cs, ps, kv_s = _token_mixer_sample(xs, cache_k, cache_v, state_conv, state_pool, l, p,
                                               kv_s)
        for lst, a in zip(states, (cp, pp, cs, ps)):
            lst.append(a)
        xp = _channel_mixer(xp, l, p, final=final)
        xs = _channel_mixer(xs, l, p, final=final)

    def heads_view(a):
        return a.reshape(a.shape[0], a.shape[1], a.shape[2] // HEADS, HEADS, HEAD_W)

    return (xp, xs, heads_view(kv_p[0]), heads_view(kv_p[1]), jnp.stack(states[0]),
            jnp.stack(states[1]), heads_view(kv_s[0]), heads_view(kv_s[1]),
            jnp.stack(states[2]), jnp.stack(states[3]))
```
